```python
import jax, jax.numpy as jnp
from jax import lax
import numpy as np

D_MODEL = 2048
BATCH = 2
SEQ = 8192
DEPTH = 2

GRID_W = 64
CTX_LEN = 256
F_WIDTH = D_MODEL // 2
F_GROUPS = 4
F_GW = F_WIDTH // F_GROUPS
M_HEADS = 8
M_WIDTH = D_MODEL // 2
M_DV = M_WIDTH // M_HEADS
M_DK = M_DV // 2
QK_WIDTH = 2 * M_HEADS * M_DK
N_GATES = 4 * M_HEADS
PROJ_WIDTH = F_WIDTH + QK_WIDTH + M_WIDTH + M_WIDTH + N_GATES
MIX_WIDTH = F_WIDTH + M_WIDTH
CHUNK = 64
CONV_K = 3
N_KEYS = 128
N_EXPERTS = N_KEYS * N_KEYS
PEER_HEADS = 8
PEER_TOPK = 16
PEER_QDIM = 256
PEER_BLOCK = 128
N_MOD = 6
EPS = 1e-6

kernel_name = 'hybrid_fnet_mlstm_peer_dit'


def rmsnorm(x, g):
    xf = x.astype(jnp.float32)
    y = xf * lax.rsqrt(jnp.mean(xf * xf, axis=-1, keepdims=True) + EPS)
    return (y * g.astype(jnp.float32)).astype(x.dtype)


def modulate(h, shift, scale):
    return h * (1 + scale) + shift


def split_proj(p, b_gate):
    o1 = F_WIDTH
    o2 = o1 + QK_WIDTH
    o3 = o2 + M_WIDTH
    o4 = o3 + M_WIDTH
    return p[..., :o1], p[..., o1:o2], p[..., o2:o3], p[..., o3:o4], p[..., o4:] + b_gate


def fourier_mix(u, g):
    B_, T, _ = u.shape
    uf = u.astype(jnp.float32).reshape(B_, T, F_GROUPS, F_GW)
    y = jnp.fft.fft2(uf, axes=(1, 3), norm='ortho').real
    return rmsnorm(y, g.reshape(F_GROUPS, F_GW)).reshape(B_, T, F_WIDTH).astype(u.dtype)


def conv_latent(u, w, rows):
    B_ = u.shape[0]
    img = u.reshape(B_, rows, GRID_W, QK_WIDTH)
    y = lax.conv_general_dilated(img, w[:, :, None, :].astype(u.dtype), (1, 1), 'SAME',
                                 dimension_numbers=('NHWC', 'HWIO', 'NHWC'),
                                 feature_group_count=QK_WIDTH)
    return y.reshape(u.shape)


def conv_context(u, w):
    y = lax.conv_general_dilated(u, w[CONV_K // 2][:, None, :].astype(u.dtype), (1,), 'SAME',
                                 dimension_numbers=('NWC', 'WIO', 'NWC'),
                                 feature_group_count=QK_WIDTH)
    return y


def mlstm_chunkwise(q, k, v, ig, lf, state):
    B_, H_, T, dk = q.shape
    dv = v.shape[-1]
    nc = T // CHUNK
    q = q.reshape(B_, H_, nc, CHUNK, dk)
    k = k.reshape(B_, H_, nc, CHUNK, dk)
    v = v.reshape(B_, H_, nc, CHUNK, dv)
    ig = ig.reshape(B_, H_, nc, CHUNK)
    lf = lf.reshape(B_, H_, nc, CHUNK)
    b = jnp.cumsum(lf, axis=-1)
    b_end = b[..., -1]
    a = b_end[..., None] - b + ig
    m_loc = jnp.max(a, axis=-1)
    kw = k * jnp.exp(a - m_loc[..., None])[..., None]
    C_chunk = jnp.einsum('bhcsk,bhcsv->bhckv', kw, v)
    n_chunk = jnp.sum(kw, axis=3)

    def step(carry, inp):
        C, n, m = carry
        Cc, ncnk, mc, be = inp
        m_new = jnp.maximum(be + m, mc)
        f_old = jnp.exp(be + m - m_new)
        f_new = jnp.exp(mc - m_new)
        C_new = f_old[..., None, None] * C + f_new[..., None, None] * Cc
        n_new = f_old[..., None] * n + f_new[..., None] * ncnk
        return (C_new, n_new, m_new), (C, n, m)

    xs = (jnp.moveaxis(C_chunk, 2, 0), jnp.moveaxis(n_chunk, 2, 0),
          jnp.moveaxis(m_loc, 2, 0), jnp.moveaxis(b_end, 2, 0))
    final, (C_prev, n_prev, m_prev) = lax.scan(step, state, xs)
    C_prev = jnp.moveaxis(C_prev, 0, 2)
    n_prev = jnp.moveaxis(n_prev, 0, 2)
    m_prev = jnp.moveaxis(m_prev, 0, 2)

    tri = jnp.tril(jnp.ones((CHUNK, CHUNK), dtype=bool))
    D = jnp.where(tri, b[..., :, None] - b[..., None, :] + ig[..., None, :], -jnp.inf)
    g = b + m_prev[..., None]
    m_j = jnp.maximum(g, jnp.max(D, axis=-1))
    S = jnp.einsum('bhcjk,bhcsk->bhcjs', q, k) * jnp.exp(D - m_j[..., None])
    inter = jnp.exp(g - m_j)
    num = inter[..., None] * jnp.einsum('bhcjk,bhckv->bhcjv', q, C_prev) + jnp.einsum('bhcjs,bhcsv->bhcjv', S, v)
    den = inter * jnp.einsum('bhcjk,bhck->bhcj', q, n_prev) + jnp.sum(S, axis=-1)
    h = num / jnp.maximum(jnp.abs(den), jnp.exp(-m_j))[..., None]
    return h.reshape(B_, H_, T, dv), final


def mlstm_mix(qk, v, o, gates, g, states):
    B_, T, _ = qk.shape
    heads = lambda t, d: t.astype(jnp.float32).reshape(B_, T, M_HEADS, d).transpose(0, 2, 1, 3)
    q = heads(qk[..., :QK_WIDTH // 2], M_DK)
    k = heads(qk[..., QK_WIDTH // 2:], M_DK) * (M_DK ** -0.5)
    vh = heads(v, M_DV)
    gt = gates.astype(jnp.float32).reshape(B_, T, 4, M_HEADS).transpose(2, 0, 3, 1)
    h_f, st_f = mlstm_chunkwise(q, k, vh, gt[0], jax.nn.log_sigmoid(gt[1]), states[0])
    rev = lambda t: jnp.flip(t, axis=2)
    h_b, st_b = mlstm_chunkwise(rev(q), rev(k), rev(vh), rev(gt[2]),
                                rev(jax.nn.log_sigmoid(gt[3])), states[1])
    h = (h_f + rev(h_b)).transpose(0, 2, 1, 3)
    h = rmsnorm(h, g.reshape(M_HEADS, M_DV)).reshape(B_, T, M_WIDTH)
    y = h * jax.nn.sigmoid(o.astype(jnp.float32))
    return y.astype(v.dtype), (st_f, st_b)


def peer(h, w_query, sub_keys, expert_u, expert_v):
    shape = h.shape
    blocks = h.reshape(-1, PEER_BLOCK, D_MODEL)

    def block(xb):
        q = (xb @ w_query).reshape(PEER_BLOCK, PEER_HEADS, 2, PEER_QDIM // 2)
        s = jnp.einsum('thpd,hpnd->thpn', q, sub_keys)
        sv, si = lax.top_k(s, PEER_TOPK)
        cand = (sv[:, :, 0, :, None] + sv[:, :, 1, None, :]).reshape(PEER_BLOCK, PEER_HEADS, PEER_TOPK * PEER_TOPK)
        cidx = (si[:, :, 0, :, None] * N_KEYS + si[:, :, 1, None, :]).reshape(PEER_BLOCK, PEER_HEADS, PEER_TOPK * PEER_TOPK)
        top_s, pos = lax.top_k(cand, PEER_TOPK)
        idx = jnp.take_along_axis(cidx, pos, axis=-1)
        gate = jax.nn.softmax(top_s.astype(jnp.float32), axis=-1)
        act = jax.nn.gelu(jnp.einsum('thkd,td->thk', expert_u[idx], xb).astype(jnp.float32))
        return jnp.einsum('thk,thkd->td', (gate * act).astype(xb.dtype), expert_v[idx])

    return lax.map(block, blocks).reshape(shape)


def setup_inputs(seed: int = 0) -> dict:
    key = jax.random.key(seed)
    ks = jax.random.split(key, 20)
    nrm = lambda k, shape, s: jax.random.normal(k, shape, jnp.float32) * s
    L, D = DEPTH, D_MODEL
    f_bias = jnp.linspace(3.0, 6.0, M_HEADS, dtype=jnp.float32)
    i_bias = jnp.zeros((M_HEADS,), jnp.float32)
    gate_base = jnp.concatenate([i_bias, f_bias, i_bias, f_bias])
    return {
        'x': nrm(ks[0], (BATCH, SEQ, D), 1.0),
        'c': nrm(ks[1], (BATCH, D), 1.0),
        'ctx': nrm(ks[2], (BATCH, CTX_LEN, D), 1.0),
        'c_ctx': nrm(ks[3], (D,), 1.0),
        'w_mod': nrm(ks[4], (L, D, N_MOD * D), 0.5 * D ** -0.5),
        'b_mod': nrm(ks[5], (L, N_MOD * D), 0.02),
        'g_norm_mix': 1.0 + nrm(ks[6], (L, D), 0.02),
        'g_norm_ffn': 1.0 + nrm(ks[7], (L, D), 0.02),
        'w_in': nrm(ks[8], (L, D, PROJ_WIDTH), D ** -0.5),
        'b_gate': gate_base + nrm(ks[9], (L, N_GATES), 0.1),
        'conv_qk': nrm(ks[10], (L, CONV_K, CONV_K, QK_WIDTH), 1.0 / CONV_K),
        'g_fourier': 1.0 + nrm(ks[11], (L, F_WIDTH), 0.02),
        'g_mlstm': 1.0 + nrm(ks[12], (L, M_WIDTH), 0.02),
        'w_out': nrm(ks[13], (L, MIX_WIDTH, D), MIX_WIDTH ** -0.5),
        'w_query': nrm(ks[14], (L, D, PEER_HEADS * PEER_QDIM), D ** -0.5),
        'sub_keys': nrm(ks[15], (L, PEER_HEADS, 2, N_KEYS, PEER_QDIM // 2), (PEER_QDIM // 2) ** -0.5),
        'expert_u': nrm(ks[16], (L, N_EXPERTS, D), D ** -0.5),
        'expert_v': nrm(ks[17], (L, N_EXPERTS, D), 1.0),
        'g_final': 1.0 + nrm(ks[18], (D,), 0.02),
    }


def reference(x, c, ctx, c_ctx, w_mod, b_mod, g_norm_mix, g_norm_ffn, w_in, b_gate, conv_qk,
              g_fourier, g_mlstm, w_out, w_query, sub_keys, expert_u, expert_v, g_final):
    B_, T, _ = x.shape
    rows = T // GRID_W
    zero = (jnp.zeros((B_, M_HEADS, M_DK, M_DV), jnp.float32),
            jnp.zeros((B_, M_HEADS, M_DK), jnp.float32),
            jnp.zeros((B_, M_HEADS), jnp.float32))
    for l in range(DEPTH):
        last = l == DEPTH - 1
        mod = jax.nn.silu(c) @ w_mod[l] + b_mod[l]
        mod_c = jax.nn.silu(c_ctx) @ w_mod[l] + b_mod[l]
        sh1, sc1, ga1, sh2, sc2, ga2 = jnp.split(mod[:, None, :], N_MOD, axis=-1)
        csh1, csc1, cga1, csh2, csc2, cga2 = jnp.split(mod_c, N_MOD)

        pc = modulate(rmsnorm(ctx, g_norm_mix[l]), csh1, csc1) @ w_in[l]
        four_c, qk_c, v_c, o_c, gates_c = split_proj(pc, b_gate[l])
        qk_c = jax.nn.silu(conv_context(qk_c, conv_qk[l]))
        y_mc, ctx_states = mlstm_mix(qk_c, v_c, o_c, gates_c, g_mlstm[l], (zero, zero))

        p = modulate(rmsnorm(x, g_norm_mix[l]), sh1, sc1) @ w_in[l]
        four, qk, v, o, gates = split_proj(p, b_gate[l])
        qk = jax.nn.silu(conv_latent(qk, conv_qk[l], rows))
        y_m, _ = mlstm_mix(qk, v, o, gates, g_mlstm[l], ctx_states)
        y = jnp.concatenate([fourier_mix(four, g_fourier[l]), y_m], axis=-1) @ w_out[l]
        x = x + ga1 * y
        h2 = modulate(rmsnorm(x, g_norm_ffn[l]), sh2, sc2)
        x = x + ga2 * peer(h2, w_query[l], sub_keys[l], expert_u[l], expert_v[l])

        if not last:
            yc = jnp.concatenate([fourier_mix(four_c, g_fourier[l]), y_mc], axis=-1) @ w_out[l]
            ctx = ctx + cga1 * yc
            hc2 = modulate(rmsnorm(ctx, g_norm_ffn[l]), csh2, csc2)
            ctx = ctx + cga2 * peer(hc2, w_query[l], sub_keys[l], expert_u[l], expert_v[l])
    return rmsnorm(x, g_final)
```

```python
import functools
import math

import jax
import jax.numpy as jnp
from jax import lax
from jax.experimental import pallas as pl
from jax.experimental.pallas import tpu as pltpu

F32 = jnp.float32
BF16 = jnp.bfloat16

F_GROUPS = 4
M_HEADS = 8
GRID_W = 64
N_KEYS = 128
PEER_HEADS = 8
PEER_TOPK = 16
N_MOD = 6
EPS = 1e-6

LANES = 128
SUBLANES = 8
VMEM_LIMIT = 56 * 1024 * 1024

MLSTM_CHUNK = 128
DFT_N2 = 128
DFT_KB = 8
NEG = -3.0e38
RANK_NONE = 99.0


def _cparams(sem):
    return pltpu.CompilerParams(dimension_semantics=sem, vmem_limit_bytes=VMEM_LIMIT)


def _sigmoid(x):
    return 1.0 / (1.0 + jnp.exp(-x))


def _split3(x):
    p0 = x.astype(BF16)
    r1 = x - p0.astype(F32)
    p1 = r1.astype(BF16)
    p2 = (r1 - p1.astype(F32)).astype(BF16)
    return p0, p1, p2


def _dot01_left(a01, x):
    return sum(jnp.dot(a01, p, preferred_element_type=F32) for p in _split3(x))


def _dot01_right(x, b01):
    return sum(jnp.dot(p, b01, preferred_element_type=F32) for p in _split3(x))


def _mod_kernel(c_ref, w_ref, b_ref, o_ref):
    c = c_ref[...]
    s = (c * _sigmoid(c)).astype(BF16)
    o_ref[0] = jnp.dot(s, w_ref[0].astype(BF16), preferred_element_type=F32) + b_ref[0]


def _modulation(cond, w_mod, b_mod):
    n_layers, d, n_out = w_mod.shape
    rows = cond.shape[0]
    tn = 1024
    return pl.pallas_call(
        _mod_kernel,
        grid=(n_layers, n_out // tn),
        in_specs=[
            pl.BlockSpec((rows, d), lambda l, j: (0, 0)),
            pl.BlockSpec((1, d, tn), lambda l, j: (l, 0, j)),
            pl.BlockSpec((1, 1, tn), lambda l, j: (l, 0, j)),
        ],
        out_specs=pl.BlockSpec((1, rows, tn), lambda l, j: (l, 0, j)),
        out_shape=jax.ShapeDtypeStruct((n_layers, rows, n_out), F32),
        compiler_params=_cparams(("arbitrary", "arbitrary")),
        name="modulation",
    )(cond, w_mod, b_mod.reshape(n_layers, 1, n_out))


def _rms_modulate(x, g, shift, scale):
    ms = jnp.mean(x * x, axis=-1, keepdims=True)
    return (x * lax.rsqrt(ms + EPS) * g) * (1.0 + scale) + shift


def _inproj_kernel(x_ref, g_ref, sh_ref, sc_ref, w_ref, wg_ref, bg_ref, p_ref, gate_ref, h_scr):
    @pl.when(pl.program_id(1) == 0)
    def _():
        h = _rms_modulate(x_ref[...], g_ref[...], sh_ref[0], sc_ref[0]).astype(BF16)
        h_scr[...] = h
        gate_ref[...] = jnp.dot(h, wg_ref[...], preferred_element_type=F32) + bg_ref[...]

    p_ref[...] = jnp.dot(h_scr[...], w_ref[...], preferred_element_type=F32).astype(BF16)


def _in_projection(x2, g, mod3, mod_row, w_main, w_gate, b_gate):
    rows, d = x2.shape
    n_main = w_main.shape[1]
    tm = min(1024, rows)
    tn = 1024
    return pl.pallas_call(
        _inproj_kernel,
        grid=(rows // tm, n_main // tn),
        in_specs=[
            pl.BlockSpec((tm, d), lambda i, j: (i, 0)),
            pl.BlockSpec((1, d), lambda i, j: (0, 0)),
            pl.BlockSpec((1, 1, d), lambda i, j: (mod_row(i, tm), 0, 0)),
            pl.BlockSpec((1, 1, d), lambda i, j: (mod_row(i, tm), 0, 1)),
            pl.BlockSpec((d, tn), lambda i, j: (0, j)),
            pl.BlockSpec((d, LANES), lambda i, j: (0, 0)),
            pl.BlockSpec((1, LANES), lambda i, j: (0, 0)),
        ],
        out_specs=[
            pl.BlockSpec((tm, tn), lambda i, j: (i, j)),
            pl.BlockSpec((tm, LANES), lambda i, j: (i, 0)),
        ],
        out_shape=[
            jax.ShapeDtypeStruct((rows, n_main), BF16),
            jax.ShapeDtypeStruct((rows, LANES), F32),
        ],
        scratch_shapes=[pltpu.VMEM((tm, d), BF16)],
        compiler_params=_cparams(("arbitrary", "arbitrary")),
        name="in_projection",
    )(x2, g.reshape(1, d), mod3, mod3, w_main, w_gate, b_gate)


def _conv_kernel(*refs, tb, width, vertical, blocks_per_image):
    if vertical:
        cur_ref, top_ref, bot_ref, w_ref, o_ref = refs
    else:
        cur_ref, w_ref, o_ref = refs
    cur = cur_ref[...].astype(F32)
    ch = cur.shape[1]
    wpos = lax.rem(lax.broadcasted_iota(jnp.int32, (tb, ch), 0), width)
    first_col = wpos == 0
    last_col = wpos == width - 1
    if vertical:
        r = lax.rem(pl.program_id(0), blocks_per_image)
        top = jnp.where(r == 0, 0.0, top_ref[...].astype(F32))
        bot = jnp.where(r == blocks_per_image - 1, 0.0, bot_ref[...].astype(F32))
        ext = jnp.concatenate([top, cur, bot], axis=0)
        bases = [(dr, ext[dr * width:dr * width + tb]) for dr in range(3)]
    else:
        bases = [(1, cur)]
    acc = jnp.zeros((tb, ch), F32)
    for dr, base in bases:
        left = jnp.where(first_col, 0.0, pltpu.roll(base, 1, axis=0))
        right = jnp.where(last_col, 0.0, pltpu.roll(base, tb - 1, axis=0))
        for dw, shifted in enumerate((left, base, right)):
            k = dr * 3 + dw
            acc = acc + shifted * w_ref[k:k + 1, :]
    o_ref[...] = (acc * _sigmoid(acc)).astype(BF16)


def _qk_conv(p, conv_w, qk_width, col_block, tokens_per_image, vertical):
    rows = p.shape[0]
    w9 = conv_w.reshape(9, qk_width)
    if vertical:
        width = GRID_W
        tb = min(512, tokens_per_image)
        bpi = tokens_per_image // tb
        halo = tb // width
        n_halo = rows // width
        in_specs = [
            pl.BlockSpec((tb, qk_width), lambda i: (i, col_block)),
            pl.BlockSpec((width, qk_width), lambda i: (jnp.maximum(i * halo - 1, 0), col_block)),
            pl.BlockSpec((width, qk_width), lambda i: (jnp.minimum((i + 1) * halo, n_halo - 1), col_block)),
            pl.BlockSpec((9, qk_width), lambda i: (0, 0)),
        ]
        args = (p, p, p, w9)
    else:
        width = tb = tokens_per_image
        bpi = 1
        in_specs = [
            pl.BlockSpec((tb, qk_width), lambda i: (i, col_block)),
            pl.BlockSpec((9, qk_width), lambda i: (0, 0)),
        ]
        args = (p, w9)
    return pl.pallas_call(
        functools.partial(_conv_kernel, tb=tb, width=width, vertical=vertical, blocks_per_image=bpi),
        grid=(rows // tb,),
        in_specs=in_specs,
        out_specs=pl.BlockSpec((tb, qk_width), lambda i: (i, 0)),
        out_shape=jax.ShapeDtypeStruct((rows, qk_width), BF16),
        compiler_params=_cparams(("arbitrary",)),
        name="qk_conv_latent" if vertical else "qk_conv_context",
    )(*args)


def _mlstm_chunk(q, k, vaug, gch, sel_i, sel_f, cum, causal, first_row, state, m_prev):
    L = q.shape[0]
    logsig = jnp.minimum(gch, 0.0) - jnp.log(1.0 + jnp.exp(-jnp.abs(gch)))
    ig = _dot01_right(gch, sel_i)
    lf = _dot01_right(logsig, sel_f)
    b = _dot01_left(cum, lf)
    b_end = b[L - 1:L, :] if first_row is False else b[0:1, :]
    dmat = b + (ig - b).T
    dmat = jnp.where(causal, dmat, NEG)
    g = b + m_prev
    m_j = jnp.maximum(g, jnp.max(dmat, axis=-1, keepdims=True))
    pmat = jnp.exp(dmat - m_j)
    s_raw = lax.dot_general(q, k, (((1,), (1,)), ((), ())), preferred_element_type=F32)
    s = (s_raw * pmat).astype(BF16)
    intra = jnp.dot(s, vaug, preferred_element_type=F32)
    carried = jnp.dot(q, state.astype(BF16), preferred_element_type=F32)
    inter = jnp.exp(g - m_j)
    tot = intra + jnp.concatenate([inter, inter], axis=1) * carried
    num = tot[:, :LANES]
    den = tot[:, LANES:]
    h = num / jnp.maximum(jnp.abs(den), jnp.exp(-m_j))
    a = b_end - b + ig
    m_new = jnp.maximum(b_end + m_prev, jnp.max(a, axis=0, keepdims=True))
    kw = (k.astype(F32) * jnp.exp(a - m_new)).astype(BF16)
    f_old = jnp.exp(b_end + m_prev - m_new)
    state_new = jnp.concatenate([f_old, f_old], axis=1) * state + pl.dot(kw, vaug, trans_a=True)
    return h, state_new, m_new


def _mlstm_kernel(q_ref, k_ref, v_ref, o_ref, gt_ref, g_ref, sin_ref, min_ref,
                  y_ref, sout_ref, mout_ref, hf_scr, hb_scr, *, seq, k_scale):
    L = MLSTM_CHUNK
    nc = seq // L
    head = pl.program_id(1)
    lane = lax.broadcasted_iota(jnp.int32, (1, LANES), 1)
    own = (lane // (LANES // 2)) == lax.rem(head, 2)
    qmask = jnp.where(own, 1.0, 0.0).astype(BF16)
    kmask = jnp.where(own, k_scale, 0.0).astype(BF16)
    rr = lax.broadcasted_iota(jnp.int32, (L, L), 0)
    cc = lax.broadcasted_iota(jnp.int32, (L, L), 1)
    causal_f = rr >= cc
    causal_b = rr <= cc
    cum_f = jnp.where(causal_f, 1.0, 0.0).astype(BF16)
    cum_b = jnp.where(causal_b, 1.0, 0.0).astype(BF16)

    def selector(col):
        return jnp.where(rr == col, 1.0, 0.0).astype(BF16)

    sel = [selector(kind * M_HEADS + head) for kind in range(4)]
    ones = jnp.ones((L, LANES), BF16)

    def load(c):
        rows = pl.ds(pl.multiple_of(c * L, L), L)
        q = q_ref[rows, :] * qmask
        k = k_ref[rows, :] * kmask
        vaug = jnp.concatenate([v_ref[rows, :], ones], axis=1)
        return rows, q, k, vaug, gt_ref[rows, :]

    def body(c, carry):
        sf, mf, sb, mb = carry
        rows, q, k, vaug, gch = load(c)
        h, sf, mf = _mlstm_chunk(q, k, vaug, gch, sel[0], sel[1], cum_f, causal_f, False, sf, mf)
        hf_scr[rows, :] = h
        rows, q, k, vaug, gch = load(nc - 1 - c)
        h, sb, mb = _mlstm_chunk(q, k, vaug, gch, sel[2], sel[3], cum_b, causal_b, True, sb, mb)
        hb_scr[rows, :] = h
        return sf, mf, sb, mb

    init = (sin_ref[0, 0, 0], min_ref[0, 0, 0][0:1, :], sin_ref[0, 0, 1], min_ref[0, 0, 1][0:1, :])
    sf, mf, sb, mb = lax.fori_loop(0, nc, body, init)
    sout_ref[0, 0, 0] = sf
    sout_ref[0, 0, 1] = sb
    mout_ref[0, 0, 0] = jnp.broadcast_to(mf, (SUBLANES, LANES))
    mout_ref[0, 0, 1] = jnp.broadcast_to(mb, (SUBLANES, LANES))

    def finish(c, _):
        rows = pl.ds(pl.multiple_of(c * L, L), L)
        h = hf_scr[rows, :] + hb_scr[rows, :]
        ms = jnp.mean(h * h, axis=-1, keepdims=True)
        y = h * lax.rsqrt(ms + EPS) * g_ref[...]
        y_ref[rows, :] = (y * _sigmoid(o_ref[rows, :].astype(F32))).astype(BF16)
        return 0

    lax.fori_loop(0, nc, finish, 0)


def _mlstm(qk, p, gates, g_mlstm, state_in, m_in, batch, seq, v_col0, o_col0):
    rows = qk.shape[0]
    dv = LANES
    k_col0 = qk.shape[1] // (2 * LANES)
    dk = qk.shape[1] // (2 * M_HEADS)
    tok = lambda b, h: b
    return pl.pallas_call(
        functools.partial(_mlstm_kernel, seq=seq, k_scale=dk ** -0.5),
        grid=(batch, M_HEADS),
        in_specs=[
            pl.BlockSpec((seq, LANES), lambda b, h: (b, h // 2)),
            pl.BlockSpec((seq, LANES), lambda b, h: (b, k_col0 + h // 2)),
            pl.BlockSpec((seq, dv), lambda b, h: (b, v_col0 + h)),
            pl.BlockSpec((seq, dv), lambda b, h: (b, o_col0 + h)),
            pl.BlockSpec((seq, LANES), lambda b, h: (b, 0)),
            pl.BlockSpec((1, dv), lambda b, h: (0, h)),
            pl.BlockSpec((1, 1, 2, LANES, 2 * LANES), lambda b, h: (b, h, 0, 0, 0)),
            pl.BlockSpec((1, 1, 2, SUBLANES, LANES), lambda b, h: (b, h, 0, 0, 0)),
        ],
        out_specs=[
            pl.BlockSpec((seq, dv), lambda b, h: (b, h)),
            pl.BlockSpec((1, 1, 2, LANES, 2 * LANES), lambda b, h: (b, h, 0, 0, 0)),
            pl.BlockSpec((1, 1, 2, SUBLANES, LANES), lambda b, h: (b, h, 0, 0, 0)),
        ],
        out_shape=[
            jax.ShapeDtypeStruct((rows, M_HEADS * dv), BF16),
            jax.ShapeDtypeStruct(state_in.shape, F32),
            jax.ShapeDtypeStruct(m_in.shape, F32),
        ],
        scratch_shapes=[pltpu.VMEM((seq, dv), F32), pltpu.VMEM((seq, dv), F32)],
        compiler_params=_cparams(("arbitrary", "arbitrary")),
        name="mlstm",
    )(qk, qk, p, p, gates, g_mlstm.reshape(1, -1), state_in, m_in)


def _dft_mats(n, scale):
    idx = jnp.arange(n, dtype=jnp.int32)
    ang = (2.0 * math.pi / n) * ((idx[:, None] * idx[None, :]) % n).astype(F32)
    return jnp.cos(ang) * scale, jnp.sin(ang) * scale


def _channel_dft(u, cc_ref, sc_ref):
    gw = cc_ref.shape[0]
    zr, zi = [], []
    for g in range(u.shape[1] // gw):
        ug = u[:, g * gw:(g + 1) * gw]
        zr.append(jnp.dot(ug, cc_ref[...], preferred_element_type=F32))
        zi.append(jnp.dot(ug, sc_ref[...], preferred_element_type=F32))
    return jnp.concatenate(zr, axis=1), jnp.concatenate(zi, axis=1)


def _group_rmsnorm(y, g, gw):
    outs = []
    for k in range(y.shape[1] // gw):
        yk = y[:, k * gw:(k + 1) * gw]
        ms = jnp.mean(yk * yk, axis=-1, keepdims=True)
        outs.append(yk * lax.rsqrt(ms + EPS) * g[:, k * gw:(k + 1) * gw])
    return jnp.concatenate(outs, axis=1)


def _fnet_a_kernel(u_ref, cc_ref, sc_ref, f_ref, cw_ref, sw_ref, yr_ref, yi_ref):
    zr, zi = _channel_dft(u_ref[0], cc_ref, sc_ref)
    z = jnp.concatenate([zr, zi], axis=0).astype(BF16)
    y = jnp.dot(f_ref[...], z, preferred_element_type=F32)
    n2 = y.shape[0] // 2
    yr, yi = y[:n2], y[n2:]
    cw, sw = cw_ref[0], sw_ref[0]
    for j in range(y.shape[1] // LANES):
        cols = slice(j * LANES, (j + 1) * LANES)
        yr_ref[0, :, cols] = (yr[:, cols] * cw + yi[:, cols] * sw).astype(BF16)
        yi_ref[0, :, cols] = (yi[:, cols] * cw - yr[:, cols] * sw).astype(BF16)


def _fnet_b_kernel(yr_ref, yi_ref, gc_ref, gs_ref, g_ref, o_ref, *, gw):
    y = (jnp.dot(gc_ref[...], yr_ref[0], preferred_element_type=F32)
         + jnp.dot(gs_ref[...], yi_ref[0], preferred_element_type=F32))
    out = _group_rmsnorm(y, g_ref[...], gw).astype(BF16)
    o_ref[0] = out.reshape(o_ref.shape[1:])


def _fnet_direct_kernel(u_ref, cc_ref, sc_ref, ct_ref, st_ref, g_ref, o_ref, *, gw):
    zr, zi = _channel_dft(u_ref[...], cc_ref, sc_ref)
    y = (jnp.dot(ct_ref[...], zr.astype(BF16), preferred_element_type=F32)
         + jnp.dot(st_ref[...], zi.astype(BF16), preferred_element_type=F32))
    o_ref[...] = _group_rmsnorm(y, g_ref[...], gw).astype(BF16)


def _fourier_mix(p, g_fourier, batch, seq, p_width, f_width):
    gw = f_width // F_GROUPS
    cc, sc = _dft_mats(gw, gw ** -0.5)
    cc, msc = cc.astype(BF16), (-sc).astype(BF16)
    g2 = g_fourier.reshape(1, f_width)
    rows = batch * seq
    if seq <= 512:
        ct, st = _dft_mats(seq, seq ** -0.5)
        return pl.pallas_call(
            functools.partial(_fnet_direct_kernel, gw=gw),
            grid=(batch,),
            in_specs=[
                pl.BlockSpec((seq, f_width), lambda b: (b, 0)),
                pl.BlockSpec((gw, gw), lambda b: (0, 0)),
                pl.BlockSpec((gw, gw), lambda b: (0, 0)),
                pl.BlockSpec((seq, seq), lambda b: (0, 0)),
                pl.BlockSpec((seq, seq), lambda b: (0, 0)),
                pl.BlockSpec((1, f_width), lambda b: (0, 0)),
            ],
            out_specs=pl.BlockSpec((seq, f_width), lambda b: (b, 0)),
            out_shape=jax.ShapeDtypeStruct((rows, f_width), BF16),
            compiler_params=_cparams(("arbitrary",)),
            name="fourier_direct",
        )(p, cc, msc, ct.astype(BF16), st.astype(BF16), g2)

    n2 = DFT_N2
    n1 = seq // n2
    c2, s2 = _dft_mats(n2, n2 ** -0.5)
    fmat = jnp.concatenate([jnp.concatenate([c2, s2], axis=1),
                            jnp.concatenate([-s2, c2], axis=1)], axis=0).astype(BF16)
    i1 = jnp.arange(n1, dtype=jnp.int32)
    i2 = jnp.arange(n2, dtype=jnp.int32)
    tw = (2.0 * math.pi / seq) * (i1[:, None] * i2[None, :]).astype(F32)
    cw = jnp.broadcast_to(jnp.cos(tw)[:, :, None], (n1, n2, LANES))
    sw = jnp.broadcast_to(jnp.sin(tw)[:, :, None], (n1, n2, LANES))
    pb = p_width // f_width
    yr, yi = pl.pallas_call(
        _fnet_a_kernel,
        grid=(batch, n1),
        in_specs=[
            pl.BlockSpec((1, n2, f_width), lambda b, j: (b, 0, j * pb)),
            pl.BlockSpec((gw, gw), lambda b, j: (0, 0)),
            pl.BlockSpec((gw, gw), lambda b, j: (0, 0)),
            pl.BlockSpec((2 * n2, 2 * n2), lambda b, j: (0, 0)),
            pl.BlockSpec((1, n2, LANES), lambda b, j: (j, 0, 0)),
            pl.BlockSpec((1, n2, LANES), lambda b, j: (j, 0, 0)),
        ],
        out_specs=[
            pl.BlockSpec((1, n2, f_width), lambda b, j: (b, 0, j)),
            pl.BlockSpec((1, n2, f_width), lambda b, j: (b, 0, j)),
        ],
        out_shape=[jax.ShapeDtypeStruct((batch, n2, n1 * f_width), BF16)] * 2,
        compiler_params=_cparams(("arbitrary", "arbitrary")),
        name="fourier_stage_a",
    )(p.reshape(batch, n2, n1 * p_width), cc, msc, fmat, cw, sw)

    kb = DFT_KB
    c1, s1 = _dft_mats(n1, n1 ** -0.5)
    eye = jnp.eye(kb, dtype=F32)
    gc = jnp.einsum("kn,ab->kabn", c1, eye).reshape(n1 * kb, kb * n1).astype(BF16)
    gs = jnp.einsum("kn,ab->kabn", s1, eye).reshape(n1 * kb, kb * n1).astype(BF16)
    out = pl.pallas_call(
        functools.partial(_fnet_b_kernel, gw=gw),
        grid=(batch, n2 // kb),
        in_specs=[
            pl.BlockSpec((1, kb * n1, f_width), lambda b, j: (b, j, 0)),
            pl.BlockSpec((1, kb * n1, f_width), lambda b, j: (b, j, 0)),
            pl.BlockSpec((n1 * kb, kb * n1), lambda b, j: (0, 0)),
            pl.BlockSpec((n1 * kb, kb * n1), lambda b, j: (0, 0)),
            pl.BlockSpec((1, f_width), lambda b, j: (0, 0)),
        ],
        out_specs=pl.BlockSpec((1, n1, kb, f_width), lambda b, j: (b, 0, j, 0)),
        out_shape=jax.ShapeDtypeStruct((batch, n1, n2, f_width), BF16),
        compiler_params=_cparams(("arbitrary", "arbitrary")),
        name="fourier_stage_b",
    )(yr.reshape(batch, n2 * n1, f_width), yi.reshape(batch, n2 * n1, f_width), gc, gs, g2)
    return out.reshape(rows, f_width)


def _outproj_kernel(f_ref, ym_ref, w_ref, x_ref, ga_ref, o_ref):
    fw = f_ref.shape[1]
    y = (jnp.dot(f_ref[...], w_ref[:fw, :], preferred_element_type=F32)
         + jnp.dot(ym_ref[...], w_ref[fw:, :], preferred_element_type=F32))
    o_ref[...] = x_ref[...] + ga_ref[0] * y


def _out_projection(f, ym, w_out, x2, mod3, mod_row):
    rows, d = x2.shape
    tm = min(512, rows)
    return pl.pallas_call(
        _outproj_kernel,
        grid=(rows // tm,),
        in_specs=[
            pl.BlockSpec((tm, f.shape[1]), lambda i: (i, 0)),
            pl.BlockSpec((tm, ym.shape[1]), lambda i: (i, 0)),
            pl.BlockSpec(w_out.shape, lambda i: (0, 0)),
            pl.BlockSpec((tm, d), lambda i: (i, 0)),
            pl.BlockSpec((1, 1, d), lambda i: (mod_row(i, tm), 0, 2)),
        ],
        out_specs=pl.BlockSpec((tm, d), lambda i: (i, 0)),
        out_shape=jax.ShapeDtypeStruct((rows, d), F32),
        compiler_params=_cparams(("arbitrary",)),
        name="out_projection",
    )(f, ym, w_out, x2, mod3)


def _topk_rows(s, k):
    n = s.shape[0]
    row = lax.broadcasted_iota(jnp.int32, s.shape, 0)
    rank = jnp.full(s.shape, RANK_NONE, F32)
    vals = []
    for it in range(k):
        m = jnp.max(s, axis=0, keepdims=True)
        first = jnp.min(jnp.where(s == m, row, n), axis=0, keepdims=True)
        hit = row == first
        rank = jnp.where(hit, float(it), rank)
        s = jnp.where(hit, NEG, s)
        vals.append(m)
    return vals, rank


def _peer_route_kernel(x_ref, g_ref, sh_ref, sc_ref, wq_ref, keys_ref,
                       ht_ref, e0_ref, nn_ref, e1_ref, r1_ref):
    k = PEER_TOPK
    h2 = _rms_modulate(x_ref[...], g_ref[...], sh_ref[0], sc_ref[0])
    ht_ref[...] = h2.T.astype(BF16)
    q = jnp.dot(h2.astype(BF16), wq_ref[...], preferred_element_type=F32)
    tb = q.shape[0]
    kd = keys_ref.shape[-1]
    for h in range(PEER_HEADS):
        sv, rk, sc = [], [], []
        for half in range(2):
            c0 = (h * 2 + half) * kd
            s = lax.dot_general(keys_ref[h, half], q[:, c0:c0 + kd], (((1,), (1,)), ((), ())),
                                precision=lax.Precision.HIGHEST, preferred_element_type=F32)
            vals, rank = _topk_rows(s, k)
            sv.append(vals)
            rk.append(rank)
            sc.append(s)
        sv1 = jnp.concatenate(sv[1], axis=0)
        blocks = []
        for a in range(k):
            nb = k // (a + 1)
            width = k if nb > SUBLANES else SUBLANES
            blk = sv[0][a] + sv1[:width]
            if nb < width:
                brow = lax.broadcasted_iota(jnp.int32, blk.shape, 0)
                blk = jnp.where(brow < nb, blk, NEG)
            blocks.append(blk)
        cand = jnp.concatenate(blocks, axis=0)
        n_c = cand.shape[0]
        crow = lax.broadcasted_iota(jnp.int32, cand.shape, 0)
        work = cand
        cnt = jnp.zeros(cand.shape, F32)
        for _ in range(k):
            m = jnp.max(work, axis=0, keepdims=True)
            first = jnp.min(jnp.where(work == m, crow, n_c), axis=0, keepdims=True)
            hit = crow == first
            cnt = jnp.where(hit, 1.0, cnt)
            work = jnp.where(hit, NEG, work)
        top = sv[0][0] + sv[1][0]
        z = jnp.sum(cnt * jnp.exp(jnp.maximum(cand, NEG) - top), axis=0, keepdims=True)
        n_of_rank = jnp.zeros(rk[0].shape, F32)
        off = 0
        for a in range(k):
            width = blocks[a].shape[0]
            n_a = jnp.sum(cnt[off:off + width], axis=0, keepdims=True)
            off += width
            n_of_rank = n_of_rank + jnp.where(rk[0] == float(a), n_a, 0.0)
        e0_ref[h] = jnp.exp(sc[0] - sv[0][0]) / z
        nn_ref[h] = n_of_rank
        e1_ref[h] = jnp.exp(sc[1] - sv[1][0])
        r1_ref[h] = rk[1]


def _peer_route(x2, g, mod3, mod_row, wq, sub_keys):
    rows, d = x2.shape
    tb = min(256, rows)
    nk = sub_keys.shape[2]
    tab = jax.ShapeDtypeStruct((PEER_HEADS, nk, rows), F32)
    tab_spec = pl.BlockSpec((PEER_HEADS, nk, tb), lambda i: (0, 0, i))
    return pl.pallas_call(
        _peer_route_kernel,
        grid=(rows // tb,),
        in_specs=[
            pl.BlockSpec((tb, d), lambda i: (i, 0)),
            pl.BlockSpec((1, d), lambda i: (0, 0)),
            pl.BlockSpec((1, 1, d), lambda i: (mod_row(i, tb), 0, 3)),
            pl.BlockSpec((1, 1, d), lambda i: (mod_row(i, tb), 0, 4)),
            pl.BlockSpec(wq.shape, lambda i: (0, 0)),
            pl.BlockSpec(sub_keys.shape, lambda i: (0, 0, 0, 0)),
        ],
        out_specs=[pl.BlockSpec((d, tb), lambda i: (0, i)), tab_spec, tab_spec, tab_spec, tab_spec],
        out_shape=[jax.ShapeDtypeStruct((d, rows), BF16), tab, tab, tab, tab],
        compiler_params=_cparams(("arbitrary",)),
        name="peer_route",
    )(x2, g.reshape(1, d), mod3, mod3, wq, sub_keys)


def _gelu_tanh(x):
    return 0.5 * x * (1.0 + jnp.tanh(math.sqrt(2.0 / math.pi) * (x + 0.044715 * (x * x * x))))


def _peer_expert_kernel(ht_ref, u_ref, v_ref, e0_ref, nn_ref, e1_ref, r1_ref, x_ref, ga_ref, gf_ref,
                        o_ref, act_scr, wa_scr, *, final_norm):
    e = pl.program_id(1)
    nk = e1_ref.shape[1]
    ib = u_ref.shape[0] // nk

    @pl.when(e == 0)
    def _():
        o_ref[...] = jnp.zeros(o_ref.shape, F32)

    act_scr[...] = _gelu_tanh(jnp.dot(u_ref[...], ht_ref[...], preferred_element_type=F32))

    def build(il, _):
        i = e * ib + il
        w = jnp.zeros((nk, ht_ref.shape[1]), F32)
        for h in range(PEER_HEADS):
            n_i = nn_ref[h, pl.ds(i, 1), :]
            e0_i = e0_ref[h, pl.ds(i, 1), :]
            w = w + jnp.where(r1_ref[h] < n_i, e1_ref[h], 0.0) * e0_i
        rows = pl.ds(pl.multiple_of(il * nk, nk), nk)
        wa_scr[rows, :] = (w * act_scr[rows, :]).astype(BF16)
        return 0

    lax.fori_loop(0, ib, build, 0)
    o_ref[...] += pl.dot(wa_scr[...], v_ref[...], trans_a=True)

    @pl.when(e == pl.num_programs(1) - 1)
    def _():
        y = x_ref[...] + ga_ref[0] * o_ref[...]
        if final_norm:
            ms = jnp.mean(y * y, axis=-1, keepdims=True)
            y = y * lax.rsqrt(ms + EPS) * gf_ref[...]
        o_ref[...] = y


def _peer_experts(ht, u_bf, v_bf, tables, x2, mod3, mod_row, g_final, final_norm):
    rows, d = x2.shape
    n_exp = u_bf.shape[0]
    nk = tables[0].shape[1]
    tb = min(512, rows)
    eb = 512
    tab_spec = pl.BlockSpec((PEER_HEADS, nk, tb), lambda i, e: (0, 0, i))
    return pl.pallas_call(
        functools.partial(_peer_expert_kernel, final_norm=final_norm),
        grid=(rows // tb, n_exp // eb),
        in_specs=[
            pl.BlockSpec((d, tb), lambda i, e: (0, i)),
            pl.BlockSpec((eb, d), lambda i, e: (e, 0)),
            pl.BlockSpec((eb, d), lambda i, e: (e, 0)),
            tab_spec, tab_spec, tab_spec, tab_spec,
            pl.BlockSpec((tb, d), lambda i, e: (i, 0)),
            pl.BlockSpec((1, 1, d), lambda i, e: (mod_row(i, tb), 0, 5)),
            pl.BlockSpec((1, d), lambda i, e: (0, 0)),
        ],
        out_specs=pl.BlockSpec((tb, d), lambda i, e: (i, 0)),
        out_shape=jax.ShapeDtypeStruct((rows, d), F32),
        scratch_shapes=[pltpu.VMEM((eb, tb), F32), pltpu.VMEM((eb, tb), BF16)],
        compiler_params=_cparams(("arbitrary", "arbitrary")),
        name="peer_experts",
    )(ht, u_bf, v_bf, *tables, x2, mod3, g_final.reshape(1, d))


def kernel(x, c, ctx, c_ctx, w_mod, b_mod, g_norm_mix, g_norm_ffn, w_in, b_gate, conv_qk, g_fourier,
           g_mlstm, w_out, w_query, sub_keys, expert_u, expert_v, g_final):
    batch, seq, d = x.shape
    ctx_len = ctx.shape[1]
    depth = w_mod.shape[0]
    f_width = g_fourier.shape[1]
    m_width = g_mlstm.shape[1]
    qk_width = conv_qk.shape[-1]
    n_gates = b_gate.shape[1]
    n_main = f_width + qk_width + 2 * m_width
    assert w_in.shape[2] == n_main + n_gates and n_gates <= LANES
    assert f_width == qk_width == m_width and m_width == M_HEADS * LANES
    assert seq % MLSTM_CHUNK == 0 and ctx_len % MLSTM_CHUNK == 0 and seq % GRID_W == 0

    cond_rows = -(-(batch + 1) // SUBLANES) * SUBLANES
    cond = jnp.zeros((cond_rows, d), F32).at[:batch].set(c).at[batch].set(c_ctx)
    mod_all = _modulation(cond, w_mod, b_mod)

    latent_row = lambda i, tm: (i * tm) // seq
    context_row = lambda i, tm: batch

    x2 = x.reshape(batch * seq, d)
    c2 = ctx.reshape(batch * ctx_len, d)
    v_col0 = (f_width + qk_width) // LANES
    o_col0 = (f_width + qk_width + m_width) // LANES
    qk_block = f_width // qk_width
    zero_state = jnp.zeros((batch, M_HEADS, 2, LANES, 2 * LANES), F32)
    zero_m = jnp.zeros((batch, M_HEADS, 2, SUBLANES, LANES), F32)

    for l in range(depth):
        last = l == depth - 1
        mod3 = mod_all[l].reshape(cond_rows, 1, N_MOD * d)
        w_main = w_in[l, :, :n_main].astype(BF16)
        w_gate = jnp.zeros((d, LANES), F32).at[:, :n_gates].set(w_in[l, :, n_main:]).astype(BF16)
        bg = jnp.zeros((1, LANES), F32).at[0, :n_gates].set(b_gate[l])
        w_out_bf = w_out[l].astype(BF16)
        wq_bf = w_query[l].astype(BF16)
        u_bf = expert_u[l].astype(BF16)
        v_bf = expert_v[l].astype(BF16)

        def mixer(tokens, mod_row, n_tok, vertical, state, m_state):
            p, gates = _in_projection(tokens, g_norm_mix[l], mod3, mod_row, w_main, w_gate, bg)
            qk = _qk_conv(p, conv_qk[l], qk_width, qk_block, n_tok, vertical)
            ym, state, m_state = _mlstm(qk, p, gates, g_mlstm[l], state, m_state, batch, n_tok,
                                        v_col0, o_col0)
            return p, ym, state, m_state

        def ffn_and_residuals(tokens, p, ym, mod_row, n_tok, final_norm):
            f = _fourier_mix(p, g_fourier[l], batch, n_tok, n_main, f_width)
            tokens = _out_projection(f, ym, w_out_bf, tokens, mod3, mod_row)
            routed = _peer_route(tokens, g_norm_ffn[l], mod3, mod_row, wq_bf, sub_keys[l])
            return _peer_experts(routed[0], u_bf, v_bf, routed[1:], tokens, mod3, mod_row,
                                 g_final, final_norm)

        pc, ymc, st, m_st = mixer(c2, context_row, ctx_len, False, zero_state, zero_m)
        p, ym, _, _ = mixer(x2, latent_row, seq, True, st, m_st)
        x2 = ffn_and_residuals(x2, p, ym, latent_row, seq, last)
        if not last:
            c2 = ffn_and_residuals(c2, pc, ymc, context_row, ctx_len, False)
    return x2.reshape(batch, seq, d)
```

```python
import functools
import math

import jax
import jax.numpy as jnp
from jax import lax
from jax.experimental import pallas as pl
from jax.experimental.pallas import tpu as pltpu

F32 = jnp.float32
BF16 = jnp.bfloat16

F_GROUPS = 4
M_HEADS = 8
GRID_W = 64
N_KEYS = 128
PEER_HEADS = 8
PEER_TOPK = 16
N_MOD = 6
EPS = 1e-6

LANES = 128
SUBLANES = 8
VMEM_LIMIT = 56 * 1024 * 1024

MLSTM_CHUNK = 128
DFT_N2 = 128
DFT_KB = 8
NEG = -3.0e38
RANK_NONE = 99.0


def _cparams(sem):
    return pltpu.CompilerParams(dimension_semantics=sem, vmem_limit_bytes=VMEM_LIMIT)


def _sigmoid(x):
    return 1.0 / (1.0 + jnp.exp(-x))


def _split3(x):
    p0 = x.astype(BF16)
    r1 = x - p0.astype(F32)
    p1 = r1.astype(BF16)
    p2 = (r1 - p1.astype(F32)).astype(BF16)
    return p0, p1, p2


def _dot01_left(a01, x):
    return sum(jnp.dot(a01, p, preferred_element_type=F32) for p in _split3(x))


def _dot01_right(x, b01):
    return sum(jnp.dot(p, b01, preferred_element_type=F32) for p in _split3(x))


def _mod_kernel(c_ref, w_ref, b_ref, o_ref):
    c = c_ref[...]
    s = (c * _sigmoid(c)).astype(BF16)
    o_ref[0] = jnp.dot(s, w_ref[0].astype(BF16), preferred_element_type=F32) + b_ref[0]


def _modulation(cond, w_mod, b_mod):
    n_layers, d, n_out = w_mod.shape
    rows = cond.shape[0]
    tn = 1024
    return pl.pallas_call(
        _mod_kernel,
        grid=(n_layers, n_out // tn),
        in_specs=[
            pl.BlockSpec((rows, d), lambda l, j: (0, 0)),
            pl.BlockSpec((1, d, tn), lambda l, j: (l, 0, j)),
            pl.BlockSpec((1, 1, tn), lambda l, j: (l, 0, j)),
        ],
        out_specs=pl.BlockSpec((1, rows, tn), lambda l, j: (l, 0, j)),
        out_shape=jax.ShapeDtypeStruct((n_layers, rows, n_out), F32),
        compiler_params=_cparams(("arbitrary", "arbitrary")),
        name="modulation",
    )(cond, w_mod, b_mod.reshape(n_layers, 1, n_out))


def _rms_modulate(x, g, shift, scale):
    ms = jnp.mean(x * x, axis=-1, keepdims=True)
    return (x * lax.rsqrt(ms + EPS) * g) * (1.0 + scale) + shift


def _inproj_kernel(x_ref, g_ref, sh_ref, sc_ref, w_ref, wg_ref, bg_ref, four_ref, p_ref, gate_ref, h_scr):
    j = pl.program_id(1)

    @pl.when(j == 0)
    def _():
        h = _rms_modulate(x_ref[...], g_ref[...], sh_ref[0], sc_ref[0]).astype(BF16)
        h_scr[...] = h
        gate_ref[...] = jnp.dot(h, wg_ref[...], preferred_element_type=F32) + bg_ref[...]

    acc = jnp.dot(h_scr[...], w_ref[...], preferred_element_type=F32)

    @pl.when(j == 0)
    def _():
        four_ref[...] = acc

    @pl.when(j > 0)
    def _():
        p_ref[...] = acc.astype(BF16)


def _in_projection(x2, g, mod3, mod_row, w_main, w_gate, b_gate):
    rows, d = x2.shape
    n_main = w_main.shape[1]
    tm = min(1024, rows)
    tn = 1024
    return pl.pallas_call(
        _inproj_kernel,
        grid=(rows // tm, n_main // tn),
        in_specs=[
            pl.BlockSpec((tm, d), lambda i, j: (i, 0)),
            pl.BlockSpec((1, d), lambda i, j: (0, 0)),
            pl.BlockSpec((1, 1, d), lambda i, j: (mod_row(i, tm), 0, 0)),
            pl.BlockSpec((1, 1, d), lambda i, j: (mod_row(i, tm), 0, 1)),
            pl.BlockSpec((d, tn), lambda i, j: (0, j)),
            pl.BlockSpec((d, LANES), lambda i, j: (0, 0)),
            pl.BlockSpec((1, LANES), lambda i, j: (0, 0)),
        ],
        out_specs=[
            pl.BlockSpec((tm, tn), lambda i, j: (i, 0)),
            pl.BlockSpec((tm, tn), lambda i, j: (i, jnp.maximum(j - 1, 0))),
            pl.BlockSpec((tm, LANES), lambda i, j: (i, 0)),
        ],
        out_shape=[
            jax.ShapeDtypeStruct((rows, tn), F32),
            jax.ShapeDtypeStruct((rows, n_main - tn), BF16),
            jax.ShapeDtypeStruct((rows, LANES), F32),
        ],
        scratch_shapes=[pltpu.VMEM((tm, d), BF16)],
        compiler_params=_cparams(("arbitrary", "arbitrary")),
        name="in_projection",
    )(x2, g.reshape(1, d), mod3, mod3, w_main, w_gate, b_gate)


def _conv_kernel(*refs, tb, width, vertical, blocks_per_image):
    if vertical:
        cur_ref, top_ref, bot_ref, w_ref, o_ref = refs
    else:
        cur_ref, w_ref, o_ref = refs
    cur = cur_ref[...].astype(F32)
    ch = cur.shape[1]
    wpos = lax.rem(lax.broadcasted_iota(jnp.int32, (tb, ch), 0), width)
    first_col = wpos == 0
    last_col = wpos == width - 1
    if vertical:
        r = lax.rem(pl.program_id(0), blocks_per_image)
        top = jnp.where(r == 0, 0.0, top_ref[...].astype(F32))
        bot = jnp.where(r == blocks_per_image - 1, 0.0, bot_ref[...].astype(F32))
        ext = jnp.concatenate([top, cur, bot], axis=0)
        bases = [(dr, ext[dr * width:dr * width + tb]) for dr in range(3)]
    else:
        bases = [(1, cur)]
    acc = jnp.zeros((tb, ch), F32)
    for dr, base in bases:
        left = jnp.where(first_col, 0.0, pltpu.roll(base, 1, axis=0))
        right = jnp.where(last_col, 0.0, pltpu.roll(base, tb - 1, axis=0))
        for dw, shifted in enumerate((left, base, right)):
            k = dr * 3 + dw
            acc = acc + shifted * w_ref[k:k + 1, :]
    o_ref[...] = (acc * _sigmoid(acc)).astype(BF16)


def _qk_conv(p, conv_w, qk_width, col_block, tokens_per_image, vertical):
    rows = p.shape[0]
    w9 = conv_w.reshape(9, qk_width)
    if vertical:
        width = GRID_W
        tb = min(512, tokens_per_image)
        bpi = tokens_per_image // tb
        halo = tb // width
        n_halo = rows // width
        in_specs = [
            pl.BlockSpec((tb, qk_width), lambda i: (i, col_block)),
            pl.BlockSpec((width, qk_width), lambda i: (jnp.maximum(i * halo - 1, 0), col_block)),
            pl.BlockSpec((width, qk_width), lambda i: (jnp.minimum((i + 1) * halo, n_halo - 1), col_block)),
            pl.BlockSpec((9, qk_width), lambda i: (0, 0)),
        ]
        args = (p, p, p, w9)
    else:
        width = tb = tokens_per_image
        bpi = 1
        in_specs = [
            pl.BlockSpec((tb, qk_width), lambda i: (i, col_block)),
            pl.BlockSpec((9, qk_width), lambda i: (0, 0)),
        ]
        args = (p, w9)
    return pl.pallas_call(
        functools.partial(_conv_kernel, tb=tb, width=width, vertical=vertical, blocks_per_image=bpi),
        grid=(rows // tb,),
        in_specs=in_specs,
        out_specs=pl.BlockSpec((tb, qk_width), lambda i: (i, 0)),
        out_shape=jax.ShapeDtypeStruct((rows, qk_width), BF16),
        compiler_params=_cparams(("arbitrary",)),
        name="qk_conv_latent" if vertical else "qk_conv_context",
    )(*args)


def _mlstm_chunk(q, k, vaug, gch, sel_i, sel_f, cum, causal, first_row, state, m_prev):
    L = q.shape[0]
    logsig = jnp.minimum(gch, 0.0) - jnp.log(1.0 + jnp.exp(-jnp.abs(gch)))
    ig = _dot01_right(gch, sel_i)
    lf = _dot01_right(logsig, sel_f)
    b = _dot01_left(cum, lf)
    b_end = b[L - 1:L, :] if first_row is False else b[0:1, :]
    dmat = b + (ig - b).T
    dmat = jnp.where(causal, dmat, NEG)
    g = b + m_prev
    m_j = jnp.maximum(g, jnp.max(dmat, axis=-1, keepdims=True))
    pmat = jnp.exp(dmat - m_j)
    s_raw = lax.dot_general(q, k, (((1,), (1,)), ((), ())), preferred_element_type=F32)
    s = (s_raw * pmat).astype(BF16)
    intra = jnp.dot(s, vaug, preferred_element_type=F32)
    carried = jnp.dot(q, state.astype(BF16), preferred_element_type=F32)
    inter = jnp.exp(g - m_j)
    tot = intra + jnp.concatenate([inter, inter], axis=1) * carried
    num = tot[:, :LANES]
    den = tot[:, LANES:]
    h = num / jnp.maximum(jnp.abs(den), jnp.exp(-m_j))
    a = b_end - b + ig
    m_new = jnp.maximum(b_end + m_prev, jnp.max(a, axis=0, keepdims=True))
    kw = (k.astype(F32) * jnp.exp(a - m_new)).astype(BF16)
    f_old = jnp.exp(b_end + m_prev - m_new)
    state_new = jnp.concatenate([f_old, f_old], axis=1) * state + pl.dot(kw, vaug, trans_a=True)
    return h, state_new, m_new


def _mlstm_kernel(q_ref, k_ref, v_ref, o_ref, gt_ref, g_ref, sin_ref, min_ref,
                  y_ref, sout_ref, mout_ref, hf_scr, hb_scr, *, seq, k_scale):
    L = MLSTM_CHUNK
    nc = seq // L
    head = pl.program_id(1)
    lane = lax.broadcasted_iota(jnp.int32, (1, LANES), 1)
    own = (lane // (LANES // 2)) == lax.rem(head, 2)
    qmask = jnp.where(own, 1.0, 0.0).astype(BF16)
    kmask = jnp.where(own, k_scale, 0.0).astype(BF16)
    rr = lax.broadcasted_iota(jnp.int32, (L, L), 0)
    cc = lax.broadcasted_iota(jnp.int32, (L, L), 1)
    causal_f = rr >= cc
    causal_b = rr <= cc
    cum_f = jnp.where(causal_f, 1.0, 0.0).astype(BF16)
    cum_b = jnp.where(causal_b, 1.0, 0.0).astype(BF16)

    def selector(col):
        return jnp.where(rr == col, 1.0, 0.0).astype(BF16)

    sel = [selector(kind * M_HEADS + head) for kind in range(4)]
    ones = jnp.ones((L, LANES), BF16)

    def load(c):
        rows = pl.ds(pl.multiple_of(c * L, L), L)
        q = q_ref[rows, :] * qmask
        k = k_ref[rows, :] * kmask
        vaug = jnp.concatenate([v_ref[rows, :], ones], axis=1)
        return rows, q, k, vaug, gt_ref[rows, :]

    def body(c, carry):
        sf, mf, sb, mb = carry
        rows, q, k, vaug, gch = load(c)
        h, sf, mf = _mlstm_chunk(q, k, vaug, gch, sel[0], sel[1], cum_f, causal_f, False, sf, mf)
        hf_scr[rows, :] = h
        rows, q, k, vaug, gch = load(nc - 1 - c)
        h, sb, mb = _mlstm_chunk(q, k, vaug, gch, sel[2], sel[3], cum_b, causal_b, True, sb, mb)
        hb_scr[rows, :] = h
        return sf, mf, sb, mb

    init = (sin_ref[0, 0, 0], min_ref[0, 0, 0][0:1, :], sin_ref[0, 0, 1], min_ref[0, 0, 1][0:1, :])
    sf, mf, sb, mb = lax.fori_loop(0, nc, body, init)
    sout_ref[0, 0, 0] = sf
    sout_ref[0, 0, 1] = sb
    mout_ref[0, 0, 0] = jnp.broadcast_to(mf, (SUBLANES, LANES))
    mout_ref[0, 0, 1] = jnp.broadcast_to(mb, (SUBLANES, LANES))

    def finish(c, _):
        rows = pl.ds(pl.multiple_of(c * L, L), L)
        h = hf_scr[rows, :] + hb_scr[rows, :]
        ms = jnp.mean(h * h, axis=-1, keepdims=True)
        y = h * lax.rsqrt(ms + EPS) * g_ref[...]
        y_ref[rows, :] = (y * _sigmoid(o_ref[rows, :].astype(F32))).astype(BF16)
        return 0

    lax.fori_loop(0, nc, finish, 0)


def _mlstm(qk, p, gates, g_mlstm, state_in, m_in, batch, seq, v_col0, o_col0):
    rows = qk.shape[0]
    dv = LANES
    k_col0 = qk.shape[1] // (2 * LANES)
    dk = qk.shape[1] // (2 * M_HEADS)
    tok = lambda b, h: b
    return pl.pallas_call(
        functools.partial(_mlstm_kernel, seq=seq, k_scale=dk ** -0.5),
        grid=(batch, M_HEADS),
        in_specs=[
            pl.BlockSpec((seq, LANES), lambda b, h: (b, h // 2)),
            pl.BlockSpec((seq, LANES), lambda b, h: (b, k_col0 + h // 2)),
            pl.BlockSpec((seq, dv), lambda b, h: (b, v_col0 + h)),
            pl.BlockSpec((seq, dv), lambda b, h: (b, o_col0 + h)),
            pl.BlockSpec((seq, LANES), lambda b, h: (b, 0)),
            pl.BlockSpec((1, dv), lambda b, h: (0, h)),
            pl.BlockSpec((1, 1, 2, LANES, 2 * LANES), lambda b, h: (b, h, 0, 0, 0)),
            pl.BlockSpec((1, 1, 2, SUBLANES, LANES), lambda b, h: (b, h, 0, 0, 0)),
        ],
        out_specs=[
            pl.BlockSpec((seq, dv), lambda b, h: (b, h)),
            pl.BlockSpec((1, 1, 2, LANES, 2 * LANES), lambda b, h: (b, h, 0, 0, 0)),
            pl.BlockSpec((1, 1, 2, SUBLANES, LANES), lambda b, h: (b, h, 0, 0, 0)),
        ],
        out_shape=[
            jax.ShapeDtypeStruct((rows, M_HEADS * dv), BF16),
            jax.ShapeDtypeStruct(state_in.shape, F32),
            jax.ShapeDtypeStruct(m_in.shape, F32),
        ],
        scratch_shapes=[pltpu.VMEM((seq, dv), F32), pltpu.VMEM((seq, dv), F32)],
        compiler_params=_cparams(("arbitrary", "arbitrary")),
        name="mlstm",
    )(qk, qk, p, p, gates, g_mlstm.reshape(1, -1), state_in, m_in)


def _dft_mats(n, scale):
    idx = jnp.arange(n, dtype=jnp.int32)
    ang = (2.0 * math.pi / n) * ((idx[:, None] * idx[None, :]) % n).astype(F32)
    return jnp.cos(ang) * scale, jnp.sin(ang) * scale


def _channel_dft(u, cc_ref, sc_ref):
    gw = cc_ref.shape[0]
    zr, zi = [], []
    for g in range(u.shape[1] // gw):
        ug = u[:, g * gw:(g + 1) * gw]
        zr.append(jnp.dot(ug, cc_ref[...], preferred_element_type=F32))
        zi.append(jnp.dot(ug, sc_ref[...], preferred_element_type=F32))
    return jnp.concatenate(zr, axis=1), jnp.concatenate(zi, axis=1)


def _group_rmsnorm(y, g, gw):
    outs = []
    for k in range(y.shape[1] // gw):
        yk = y[:, k * gw:(k + 1) * gw]
        ms = jnp.mean(yk * yk, axis=-1, keepdims=True)
        outs.append(yk * lax.rsqrt(ms + EPS) * g[:, k * gw:(k + 1) * gw])
    return jnp.concatenate(outs, axis=1)


def _fnet_a_kernel(u_ref, cc_ref, sc_ref, f_ref, cw_ref, sw_ref, yr_ref, yi_ref):
    fw = u_ref.shape[3]
    for r in range(u_ref.shape[2]):
        zr, zi = _channel_dft(u_ref[0, :, r, :].astype(BF16), cc_ref, sc_ref)
        z = jnp.concatenate([zr, zi], axis=0).astype(BF16)
        y = jnp.dot(f_ref[...], z, preferred_element_type=F32)
        n2 = y.shape[0] // 2
        yr, yi = y[:n2], y[n2:]
        cw, sw = cw_ref[r], sw_ref[r]
        for j in range(fw // LANES):
            cols = slice(j * LANES, (j + 1) * LANES)
            ocols = slice(r * fw + j * LANES, r * fw + (j + 1) * LANES)
            yr_ref[0, :, ocols] = (yr[:, cols] * cw + yi[:, cols] * sw).astype(BF16)
            yi_ref[0, :, ocols] = (yi[:, cols] * cw - yr[:, cols] * sw).astype(BF16)


def _fnet_b_kernel(yr_ref, yi_ref, gc_ref, gs_ref, g_ref, o_ref, *, gw):
    y = (jnp.dot(gc_ref[...], yr_ref[0], preferred_element_type=F32)
         + jnp.dot(gs_ref[...], yi_ref[0], preferred_element_type=F32))
    out = _group_rmsnorm(y, g_ref[...], gw).astype(BF16)
    o_ref[0] = out.reshape(o_ref.shape[1:])


def _fnet_direct_kernel(u_ref, cc_ref, sc_ref, ct_ref, st_ref, g_ref, o_ref, *, gw):
    zr, zi = _channel_dft(u_ref[...].astype(BF16), cc_ref, sc_ref)
    y = (jnp.dot(ct_ref[...], zr.astype(BF16), preferred_element_type=F32)
         + jnp.dot(st_ref[...], zi.astype(BF16), preferred_element_type=F32))
    o_ref[...] = _group_rmsnorm(y, g_ref[...], gw).astype(BF16)


def _fourier_mix(p, g_fourier, batch, seq):
    f_width = p.shape[1]
    gw = f_width // F_GROUPS
    cc, sc = _dft_mats(gw, gw ** -0.5)
    cc, msc = cc.astype(BF16), (-sc).astype(BF16)
    g2 = g_fourier.reshape(1, f_width)
    rows = batch * seq
    if seq <= 512:
        ct, st = _dft_mats(seq, seq ** -0.5)
        return pl.pallas_call(
            functools.partial(_fnet_direct_kernel, gw=gw),
            grid=(batch,),
            in_specs=[
                pl.BlockSpec((seq, f_width), lambda b: (b, 0)),
                pl.BlockSpec((gw, gw), lambda b: (0, 0)),
                pl.BlockSpec((gw, gw), lambda b: (0, 0)),
                pl.BlockSpec((seq, seq), lambda b: (0, 0)),
                pl.BlockSpec((seq, seq), lambda b: (0, 0)),
                pl.BlockSpec((1, f_width), lambda b: (0, 0)),
            ],
            out_specs=pl.BlockSpec((seq, f_width), lambda b: (b, 0)),
            out_shape=jax.ShapeDtypeStruct((rows, f_width), BF16),
            compiler_params=_cparams(("arbitrary",)),
            name="fourier_direct",
        )(p, cc, msc, ct.astype(BF16), st.astype(BF16), g2)

    n2 = DFT_N2
    n1 = seq // n2
    c2, s2 = _dft_mats(n2, n2 ** -0.5)
    fmat = jnp.concatenate([jnp.concatenate([c2, s2], axis=1),
                            jnp.concatenate([-s2, c2], axis=1)], axis=0).astype(BF16)
    i1 = jnp.arange(n1, dtype=jnp.int32)
    i2 = jnp.arange(n2, dtype=jnp.int32)
    tw = (2.0 * math.pi / seq) * (i1[:, None] * i2[None, :]).astype(F32)
    cw = jnp.broadcast_to(jnp.cos(tw)[:, :, None], (n1, n2, LANES))
    sw = jnp.broadcast_to(jnp.sin(tw)[:, :, None], (n1, n2, LANES))
    nr = SUBLANES
    yr, yi = pl.pallas_call(
        _fnet_a_kernel,
        grid=(batch, n1 // nr),
        in_specs=[
            pl.BlockSpec((1, n2, nr, f_width), lambda b, j: (b, 0, j, 0)),
            pl.BlockSpec((gw, gw), lambda b, j: (0, 0)),
            pl.BlockSpec((gw, gw), lambda b, j: (0, 0)),
            pl.BlockSpec((2 * n2, 2 * n2), lambda b, j: (0, 0)),
            pl.BlockSpec((nr, n2, LANES), lambda b, j: (j, 0, 0)),
            pl.BlockSpec((nr, n2, LANES), lambda b, j: (j, 0, 0)),
        ],
        out_specs=[
            pl.BlockSpec((1, n2, nr * f_width), lambda b, j: (b, 0, j)),
            pl.BlockSpec((1, n2, nr * f_width), lambda b, j: (b, 0, j)),
        ],
        out_shape=[jax.ShapeDtypeStruct((batch, n2, n1 * f_width), BF16)] * 2,
        compiler_params=_cparams(("arbitrary", "arbitrary")),
        name="fourier_stage_a",
    )(p.reshape(batch, n2, n1, f_width), cc, msc, fmat, cw, sw)

    kb = DFT_KB
    c1, s1 = _dft_mats(n1, n1 ** -0.5)
    eye = jnp.eye(kb, dtype=F32)
    gc = jnp.einsum("kn,ab->kabn", c1, eye).reshape(n1 * kb, kb * n1).astype(BF16)
    gs = jnp.einsum("kn,ab->kabn", s1, eye).reshape(n1 * kb, kb * n1).astype(BF16)
    out = pl.pallas_call(
        functools.partial(_fnet_b_kernel, gw=gw),
        grid=(batch, n2 // kb),
        in_specs=[
            pl.BlockSpec((1, kb * n1, f_width), lambda b, j: (b, j, 0)),
            pl.BlockSpec((1, kb * n1, f_width), lambda b, j: (b, j, 0)),
            pl.BlockSpec((n1 * kb, kb * n1), lambda b, j: (0, 0)),
            pl.BlockSpec((n1 * kb, kb * n1), lambda b, j: (0, 0)),
            pl.BlockSpec((1, f_width), lambda b, j: (0, 0)),
        ],
        out_specs=pl.BlockSpec((1, n1, kb, f_width), lambda b, j: (b, 0, j, 0)),
        out_shape=jax.ShapeDtypeStruct((batch, n1, n2, f_width), BF16),
        compiler_params=_cparams(("arbitrary", "arbitrary")),
        name="fourier_stage_b",
    )(yr.reshape(batch, n2 * n1, f_width), yi.reshape(batch, n2 * n1, f_width), gc, gs, g2)
    return out.reshape(rows, f_width)


def _outproj_kernel(f_ref, ym_ref, w_ref, x_ref, ga_ref, o_ref):
    fw = f_ref.shape[1]
    y = (jnp.dot(f_ref[...], w_ref[:fw, :], preferred_element_type=F32)
         + jnp.dot(ym_ref[...], w_ref[fw:, :], preferred_element_type=F32))
    o_ref[...] = x_ref[...] + ga_ref[0] * y


def _out_projection(f, ym, w_out, x2, mod3, mod_row):
    rows, d = x2.shape
    tm = min(512, rows)
    return pl.pallas_call(
        _outproj_kernel,
        grid=(rows // tm,),
        in_specs=[
            pl.BlockSpec((tm, f.shape[1]), lambda i: (i, 0)),
            pl.BlockSpec((tm, ym.shape[1]), lambda i: (i, 0)),
            pl.BlockSpec(w_out.shape, lambda i: (0, 0)),
            pl.BlockSpec((tm, d), lambda i: (i, 0)),
            pl.BlockSpec((1, 1, d), lambda i: (mod_row(i, tm), 0, 2)),
        ],
        out_specs=pl.BlockSpec((tm, d), lambda i: (i, 0)),
        out_shape=jax.ShapeDtypeStruct((rows, d), F32),
        compiler_params=_cparams(("arbitrary",)),
        name="out_projection",
    )(f, ym, w_out, x2, mod3)


def _extract_top(s, k, exact):
    n = s.shape[0]
    row = lax.broadcasted_iota(jnp.int32, s.shape, 0) if exact else None
    rank = jnp.full(s.shape, RANK_NONE, F32)
    vals = []
    for it in range(k):
        m = jnp.max(s, axis=0, keepdims=True)
        hit = s == m
        if exact:
            first = jnp.min(jnp.where(hit, row, n), axis=0, keepdims=True)
            hit = row == first
        rank = jnp.where(hit, float(it), rank)
        s = jnp.where(hit, NEG, s)
        vals.append(m)
    removed = jnp.sum(jnp.where(rank < RANK_NONE, 1.0, 0.0), axis=0, keepdims=True)
    return vals, rank, removed


def _route_chunk(s0, s1, exact):
    k = PEER_TOPK
    v0, rank0, rem0 = _extract_top(s0, k, exact)
    v1, rank1, rem1 = _extract_top(s1, k, exact)
    sv1 = jnp.concatenate(v1, axis=0)
    blocks = []
    for a in range(k):
        nb = k // (a + 1)
        width = k if nb > SUBLANES else SUBLANES
        blk = v0[a] + sv1[:width]
        if nb < width:
            brow = lax.broadcasted_iota(jnp.int32, blk.shape, 0)
            blk = jnp.where(brow < nb, blk, NEG)
        blocks.append(blk)
    cand = jnp.concatenate(blocks, axis=0)
    _, crank, rem2 = _extract_top(cand, k, exact)
    cnt = jnp.where(crank < RANK_NONE, 1.0, 0.0)
    z = jnp.sum(cnt * jnp.exp(cand - (v0[0] + v1[0])), axis=0, keepdims=True)
    n_of_key = jnp.zeros(rank0.shape, F32)
    off = 0
    for a in range(k):
        width = blocks[a].shape[0]
        n_a = jnp.sum(cnt[off:off + width], axis=0, keepdims=True)
        off += width
        n_of_key = n_of_key + jnp.where(rank0 == float(a), n_a, 0.0)
    tie = (jnp.where(rem0 == k, 0.0, 1.0) + jnp.where(rem1 == k, 0.0, 1.0)
           + jnp.where(rem2 == k, 0.0, 1.0))
    return jnp.exp(s0 - v0[0]) / z, n_of_key, jnp.exp(s1 - v1[0]), rank1, tie


def _peer_route_kernel(x_ref, g_ref, sh_ref, sc_ref, wq_ref, keys_ref,
                       ht_ref, e0_ref, nn_ref, e1_ref, r1_ref, s_scr):
    h2 = _rms_modulate(x_ref[...], g_ref[...], sh_ref[0], sc_ref[0])
    ht_ref[...] = h2.T.astype(BF16)
    q = jnp.dot(h2.astype(BF16), wq_ref[...], preferred_element_type=F32)
    tb = q.shape[0]
    kd = keys_ref.shape[-1]
    for hp in range(2 * PEER_HEADS):
        s_scr[hp] = lax.dot_general(keys_ref[hp // 2, hp % 2], q[:, hp * kd:(hp + 1) * kd],
                                    (((1,), (1,)), ((), ())),
                                    precision=lax.Precision.HIGHEST, preferred_element_type=F32)

    def route_head(h, tie, exact):
        for c in range(tb // LANES):
            cols = slice(c * LANES, (c + 1) * LANES)
            e0, n_of_key, e1, rank1, tie_c = _route_chunk(s_scr[2 * h, :, cols],
                                                          s_scr[2 * h + 1, :, cols], exact)
            e0_ref[h, :, cols] = e0
            nn_ref[h, :, cols] = n_of_key
            e1_ref[h, :, cols] = e1.astype(BF16)
            r1_ref[h, :, cols] = rank1.astype(BF16)
            tie = jnp.maximum(tie, tie_c)
        return tie

    no_tie = jnp.zeros((1, LANES), F32)
    tie = lax.fori_loop(0, PEER_HEADS, functools.partial(route_head, exact=False), no_tie)

    @pl.when(jnp.max(tie) > 0.0)
    def _():
        lax.fori_loop(0, PEER_HEADS, functools.partial(route_head, exact=True), no_tie)


def _peer_route(x2, g, mod3, mod_row, wq, sub_keys):
    rows, d = x2.shape
    tb = min(256, rows)
    nk = sub_keys.shape[2]
    tab_spec = pl.BlockSpec((PEER_HEADS, nk, tb), lambda i: (0, 0, i))
    tab_f32 = jax.ShapeDtypeStruct((PEER_HEADS, nk, rows), F32)
    tab_bf16 = jax.ShapeDtypeStruct((PEER_HEADS, nk, rows), BF16)
    return pl.pallas_call(
        _peer_route_kernel,
        grid=(rows // tb,),
        in_specs=[
            pl.BlockSpec((tb, d), lambda i: (i, 0)),
            pl.BlockSpec((1, d), lambda i: (0, 0)),
            pl.BlockSpec((1, 1, d), lambda i: (mod_row(i, tb), 0, 3)),
            pl.BlockSpec((1, 1, d), lambda i: (mod_row(i, tb), 0, 4)),
            pl.BlockSpec(wq.shape, lambda i: (0, 0)),
            pl.BlockSpec(sub_keys.shape, lambda i: (0, 0, 0, 0)),
        ],
        out_specs=[pl.BlockSpec((d, tb), lambda i: (0, i)), tab_spec, tab_spec, tab_spec, tab_spec],
        out_shape=[jax.ShapeDtypeStruct((d, rows), BF16), tab_f32, tab_f32, tab_bf16, tab_bf16],
        scratch_shapes=[pltpu.VMEM((2 * PEER_HEADS, nk, tb), F32)],
        compiler_params=_cparams(("arbitrary",)),
        name="peer_route",
    )(x2, g.reshape(1, d), mod3, mod3, wq, sub_keys)


def _gelu_tanh(x):
    return 0.5 * x * (1.0 + jnp.tanh(math.sqrt(2.0 / math.pi) * (x + 0.044715 * (x * x * x))))


def _peer_expert_kernel(ht_ref, u_ref, v_ref, e0_ref, nn_ref, e1_ref, r1_ref, x_ref, ga_ref, gf_ref,
                        o_ref, wa_scr, *, final_norm):
    e = pl.program_id(1)
    nk = e1_ref.shape[1]
    tb = ht_ref.shape[1]
    ib = u_ref.shape[0] // nk
    pk = 2 * SUBLANES

    @pl.when(e == 0)
    def _():
        o_ref[...] = jnp.zeros(o_ref.shape, F32)

    act = jnp.dot(u_ref[...], ht_ref[...], preferred_element_type=F32)
    assert ib == SUBLANES
    i_rows = pl.ds(pl.multiple_of(e * ib, ib), ib)
    for il in range(ib):
        for c in range(tb // LANES):
            cols = slice(c * LANES, (c + 1) * LANES)
            w = [None] * (nk // pk)
            for h in range(PEER_HEADS):
                n_i = jnp.broadcast_to(nn_ref[h, i_rows, cols][il:il + 1], (pk, LANES)).astype(BF16)
                e0_i = jnp.broadcast_to(e0_ref[h, i_rows, cols][il:il + 1], (pk, LANES)).astype(BF16)
                for s in range(nk // pk):
                    rows = slice(s * pk, (s + 1) * pk)
                    e1 = e1_ref[h, rows, cols]
                    t = jnp.where(r1_ref[h, rows, cols] < n_i, e1, jnp.zeros_like(e1)) * e0_i
                    w[s] = t if w[s] is None else w[s] + t
            rows = slice(il * nk, (il + 1) * nk)
            g = _gelu_tanh(act[rows, cols]).astype(BF16)
            wa_scr[rows, cols] = jnp.concatenate(w, axis=0) * g
    o_ref[...] += pl.dot(wa_scr[...], v_ref[...], trans_a=True)

    @pl.when(e == pl.num_programs(1) - 1)
    def _():
        y = x_ref[...] + ga_ref[0] * o_ref[...]
        if final_norm:
            ms = jnp.mean(y * y, axis=-1, keepdims=True)
            y = y * lax.rsqrt(ms + EPS) * gf_ref[...]
        o_ref[...] = y


def _peer_experts(ht, u_bf, v_bf, tables, x2, mod3, mod_row, g_final, final_norm):
    rows, d = x2.shape
    n_exp = u_bf.shape[0]
    nk = tables[0].shape[1]
    tb = min(512, rows)
    eb = SUBLANES * nk
    tab_spec = pl.BlockSpec((PEER_HEADS, nk, tb), lambda i, e: (0, 0, i))
    return pl.pallas_call(
        functools.partial(_peer_expert_kernel, final_norm=final_norm),
        grid=(rows // tb, n_exp // eb),
        in_specs=[
            pl.BlockSpec((d, tb), lambda i, e: (0, i)),
            pl.BlockSpec((eb, d), lambda i, e: (e, 0)),
            pl.BlockSpec((eb, d), lambda i, e: (e, 0)),
            tab_spec, tab_spec, tab_spec, tab_spec,
            pl.BlockSpec((tb, d), lambda i, e: (i, 0)),
            pl.BlockSpec((1, 1, d), lambda i, e: (mod_row(i, tb), 0, 5)),
            pl.BlockSpec((1, d), lambda i, e: (0, 0)),
        ],
        out_specs=pl.BlockSpec((tb, d), lambda i, e: (i, 0)),
        out_shape=jax.ShapeDtypeStruct((rows, d), F32),
        scratch_shapes=[pltpu.VMEM((eb, tb), BF16)],
        compiler_params=_cparams(("arbitrary", "arbitrary")),
        name="peer_experts",
    )(ht, u_bf, v_bf, *tables, x2, mod3, g_final.reshape(1, d))


def kernel(x, c, ctx, c_ctx, w_mod, b_mod, g_norm_mix, g_norm_ffn, w_in, b_gate, conv_qk, g_fourier,
           g_mlstm, w_out, w_query, sub_keys, expert_u, expert_v, g_final):
    batch, seq, d = x.shape
    ctx_len = ctx.shape[1]
    depth = w_mod.shape[0]
    f_width = g_fourier.shape[1]
    m_width = g_mlstm.shape[1]
    qk_width = conv_qk.shape[-1]
    n_gates = b_gate.shape[1]
    n_main = f_width + qk_width + 2 * m_width
    assert w_in.shape[2] == n_main + n_gates and n_gates <= LANES
    assert f_width == qk_width == m_width and m_width == M_HEADS * LANES
    assert seq % MLSTM_CHUNK == 0 and ctx_len % MLSTM_CHUNK == 0 and seq % GRID_W == 0

    cond_rows = -(-(batch + 1) // SUBLANES) * SUBLANES
    cond = jnp.zeros((cond_rows, d), F32).at[:batch].set(c).at[batch].set(c_ctx)
    mod_all = _modulation(cond, w_mod, b_mod)

    latent_row = lambda i, tm: (i * tm) // seq
    context_row = lambda i, tm: batch

    x2 = x.reshape(batch * seq, d)
    c2 = ctx.reshape(batch * ctx_len, d)
    v_col0 = qk_width // LANES
    o_col0 = (qk_width + m_width) // LANES
    zero_state = jnp.zeros((batch, M_HEADS, 2, LANES, 2 * LANES), F32)
    zero_m = jnp.zeros((batch, M_HEADS, 2, SUBLANES, LANES), F32)

    for l in range(depth):
        last = l == depth - 1
        mod3 = mod_all[l].reshape(cond_rows, 1, N_MOD * d)
        w_main = w_in[l, :, :n_main].astype(BF16)
        w_gate = jnp.zeros((d, LANES), F32).at[:, :n_gates].set(w_in[l, :, n_main:]).astype(BF16)
        bg = jnp.zeros((1, LANES), F32).at[0, :n_gates].set(b_gate[l])
        w_out_bf = w_out[l].astype(BF16)
        wq_bf = w_query[l].astype(BF16)
        u_bf = expert_u[l].astype(BF16)
        v_bf = expert_v[l].astype(BF16)

        def mixer(tokens, mod_row, n_tok, vertical, state, m_state):
            four, p, gates = _in_projection(tokens, g_norm_mix[l], mod3, mod_row, w_main, w_gate, bg)
            qk = _qk_conv(p, conv_qk[l], qk_width, 0, n_tok, vertical)
            ym, state, m_state = _mlstm(qk, p, gates, g_mlstm[l], state, m_state, batch, n_tok,
                                        v_col0, o_col0)
            return four, ym, state, m_state

        def ffn_and_residuals(tokens, four, ym, mod_row, n_tok, final_norm):
            f = _fourier_mix(four, g_fourier[l], batch, n_tok)
            tokens = _out_projection(f, ym, w_out_bf, tokens, mod3, mod_row)
            routed = _peer_route(tokens, g_norm_ffn[l], mod3, mod_row, wq_bf, sub_keys[l])
            return _peer_experts(routed[0], u_bf, v_bf, routed[1:], tokens, mod3, mod_row,
                                 g_final, final_norm)

        pc, ymc, st, m_st = mixer(c2, context_row, ctx_len, False, zero_state, zero_m)
        p, ym, _, _ = mixer(x2, latent_row, seq, True, st, m_st)
        x2 = ffn_and_residuals(x2, p, ym, latent_row, seq, last)
        if not last:
            c2 = ffn_and_residuals(c2, pc, ymc, context_row, ctx_len, False)
    return x2.reshape(batch, seq, d)
```

```python
import functools
import math

import jax
import jax.numpy as jnp
from jax import lax
from jax.experimental import pallas as pl
from jax.experimental.pallas import tpu as pltpu

F32 = jnp.float32
BF16 = jnp.bfloat16

F_GROUPS = 4
M_HEADS = 8
GRID_W = 64
N_KEYS = 128
PEER_HEADS = 8
PEER_TOPK = 16
N_MOD = 6
EPS = 1e-6

LANES = 128
SUBLANES = 8
VMEM_LIMIT = 56 * 1024 * 1024

MLSTM_CHUNK = 128
DFT_N2 = 128
DFT_KB = 8
NEG = -3.0e38
RANK_NONE = 99.0


def _cparams(sem):
    return pltpu.CompilerParams(dimension_semantics=sem, vmem_limit_bytes=VMEM_LIMIT)


def _sigmoid(x):
    return 1.0 / (1.0 + jnp.exp(-x))


def _split3(x):
    p0 = x.astype(BF16)
    r1 = x - p0.astype(F32)
    p1 = r1.astype(BF16)
    p2 = (r1 - p1.astype(F32)).astype(BF16)
    return p0, p1, p2


def _dot01_left(a01, x):
    return sum(jnp.dot(a01, p, preferred_element_type=F32) for p in _split3(x))


def _dot01_right(x, b01):
    return sum(jnp.dot(p, b01, preferred_element_type=F32) for p in _split3(x))


def _mod_kernel(c_ref, w_ref, b_ref, o_ref):
    c = c_ref[...]
    s = (c * _sigmoid(c)).astype(BF16)
    o_ref[0] = jnp.dot(s, w_ref[0].astype(BF16), preferred_element_type=F32) + b_ref[0]


def _modulation(cond, w_mod, b_mod):
    n_layers, d, n_out = w_mod.shape
    rows = cond.shape[0]
    tn = 1024
    return pl.pallas_call(
        _mod_kernel,
        grid=(n_layers, n_out // tn),
        in_specs=[
            pl.BlockSpec((rows, d), lambda l, j: (0, 0)),
            pl.BlockSpec((1, d, tn), lambda l, j: (l, 0, j)),
            pl.BlockSpec((1, 1, tn), lambda l, j: (l, 0, j)),
        ],
        out_specs=pl.BlockSpec((1, rows, tn), lambda l, j: (l, 0, j)),
        out_shape=jax.ShapeDtypeStruct((n_layers, rows, n_out), F32),
        compiler_params=_cparams(("arbitrary", "arbitrary")),
        name="modulation",
    )(cond, w_mod, b_mod.reshape(n_layers, 1, n_out))


def _rms_modulate(x, g, shift, scale):
    ms = jnp.mean(x * x, axis=-1, keepdims=True)
    return (x * lax.rsqrt(ms + EPS) * g) * (1.0 + scale) + shift


def _inproj_kernel(x_ref, g_ref, sh_ref, sc_ref, w_ref, wg_ref, bg_ref, four_ref, p_ref, gate_ref, h_scr):
    j = pl.program_id(1)

    @pl.when(j == 0)
    def _():
        h = _rms_modulate(x_ref[...], g_ref[...], sh_ref[0], sc_ref[0]).astype(BF16)
        h_scr[...] = h
        gate_ref[...] = jnp.dot(h, wg_ref[...], preferred_element_type=F32) + bg_ref[...]

    acc = jnp.dot(h_scr[...], w_ref[...], preferred_element_type=F32)

    @pl.when(j == 0)
    def _():
        four_ref[...] = acc

    @pl.when(j > 0)
    def _():
        p_ref[...] = acc.astype(BF16)


def _in_projection(x2, g, mod3, mod_row, w_main, w_gate, b_gate):
    rows, d = x2.shape
    n_main = w_main.shape[1]
    tm = min(1024, rows)
    tn = 1024
    return pl.pallas_call(
        _inproj_kernel,
        grid=(rows // tm, n_main // tn),
        in_specs=[
            pl.BlockSpec((tm, d), lambda i, j: (i, 0)),
            pl.BlockSpec((1, d), lambda i, j: (0, 0)),
            pl.BlockSpec((1, 1, d), lambda i, j: (mod_row(i, tm), 0, 0)),
            pl.BlockSpec((1, 1, d), lambda i, j: (mod_row(i, tm), 0, 1)),
            pl.BlockSpec((d, tn), lambda i, j: (0, j)),
            pl.BlockSpec((d, LANES), lambda i, j: (0, 0)),
            pl.BlockSpec((1, LANES), lambda i, j: (0, 0)),
        ],
        out_specs=[
            pl.BlockSpec((tm, tn), lambda i, j: (i, 0)),
            pl.BlockSpec((tm, tn), lambda i, j: (i, jnp.maximum(j - 1, 0))),
            pl.BlockSpec((tm, LANES), lambda i, j: (i, 0)),
        ],
        out_shape=[
            jax.ShapeDtypeStruct((rows, tn), F32),
            jax.ShapeDtypeStruct((rows, n_main - tn), BF16),
            jax.ShapeDtypeStruct((rows, LANES), F32),
        ],
        scratch_shapes=[pltpu.VMEM((tm, d), BF16)],
        compiler_params=_cparams(("arbitrary", "arbitrary")),
        name="in_projection",
    )(x2, g.reshape(1, d), mod3, mod3, w_main, w_gate, b_gate)


def _conv_kernel(*refs, tb, width, vertical, blocks_per_image):
    if vertical:
        cur_ref, top_ref, bot_ref, w_ref, o_ref = refs
    else:
        cur_ref, w_ref, o_ref = refs
    cur = cur_ref[...].astype(F32)
    ch = cur.shape[1]
    wpos = lax.rem(lax.broadcasted_iota(jnp.int32, (tb, ch), 0), width)
    first_col = wpos == 0
    last_col = wpos == width - 1
    if vertical:
        r = lax.rem(pl.program_id(0), blocks_per_image)
        top = jnp.where(r == 0, 0.0, top_ref[...].astype(F32))
        bot = jnp.where(r == blocks_per_image - 1, 0.0, bot_ref[...].astype(F32))
        ext = jnp.concatenate([top, cur, bot], axis=0)
        bases = [(dr, ext[dr * width:dr * width + tb]) for dr in range(3)]
    else:
        bases = [(1, cur)]
    acc = jnp.zeros((tb, ch), F32)
    for dr, base in bases:
        left = jnp.where(first_col, 0.0, pltpu.roll(base, 1, axis=0))
        right = jnp.where(last_col, 0.0, pltpu.roll(base, tb - 1, axis=0))
        for dw, shifted in enumerate((left, base, right)):
            k = dr * 3 + dw
            acc = acc + shifted * w_ref[k:k + 1, :]
    o_ref[...] = (acc * _sigmoid(acc)).astype(BF16)


def _qk_conv(p, conv_w, qk_width, col_block, tokens_per_image, vertical):
    rows = p.shape[0]
    w9 = conv_w.reshape(9, qk_width)
    if vertical:
        width = GRID_W
        tb = min(512, tokens_per_image)
        bpi = tokens_per_image // tb
        halo = tb // width
        n_halo = rows // width
        in_specs = [
            pl.BlockSpec((tb, qk_width), lambda i: (i, col_block)),
            pl.BlockSpec((width, qk_width), lambda i: (jnp.maximum(i * halo - 1, 0), col_block)),
            pl.BlockSpec((width, qk_width), lambda i: (jnp.minimum((i + 1) * halo, n_halo - 1), col_block)),
            pl.BlockSpec((9, qk_width), lambda i: (0, 0)),
        ]
        args = (p, p, p, w9)
    else:
        width = tb = tokens_per_image
        bpi = 1
        in_specs = [
            pl.BlockSpec((tb, qk_width), lambda i: (i, col_block)),
            pl.BlockSpec((9, qk_width), lambda i: (0, 0)),
        ]
        args = (p, w9)
    return pl.pallas_call(
        functools.partial(_conv_kernel, tb=tb, width=width, vertical=vertical, blocks_per_image=bpi),
        grid=(rows // tb,),
        in_specs=in_specs,
        out_specs=pl.BlockSpec((tb, qk_width), lambda i: (i, 0)),
        out_shape=jax.ShapeDtypeStruct((rows, qk_width), BF16),
        compiler_params=_cparams(("arbitrary",)),
        name="qk_conv_latent" if vertical else "qk_conv_context",
    )(*args)


def _mlstm_chunks(chains):
    n = range(len(chains))
    ch = chains
    L = ch[0]["q"].shape[0]
    logsig = [jnp.minimum(c["gch"], 0.0) - jnp.log(1.0 + jnp.exp(-jnp.abs(c["gch"]))) for c in ch]
    ig_p = [_split3(c["gch"]) for c in ch]
    lf_p = [_split3(logsig[i]) for i in n]
    ig = [sum(jnp.dot(p, ch[i]["sel_i"], preferred_element_type=F32) for p in ig_p[i]) for i in n]
    lf = [sum(jnp.dot(p, ch[i]["sel_f"], preferred_element_type=F32) for p in lf_p[i]) for i in n]
    b_p = [_split3(lf[i]) for i in n]
    b = [sum(jnp.dot(ch[i]["cum"], p, preferred_element_type=F32) for p in b_p[i]) for i in n]
    b_end = [b[i][0:1, :] if ch[i]["reverse"] else b[i][L - 1:L, :] for i in n]
    a_t = [(ig[i] - b[i]).T for i in n]
    dmat = [jnp.where(ch[i]["causal"], b[i] + a_t[i], NEG) for i in n]
    g = [b[i] + ch[i]["m"] for i in n]
    m_j = [jnp.maximum(g[i], jnp.max(dmat[i], axis=-1, keepdims=True)) for i in n]
    pmat = [jnp.exp(dmat[i] - m_j[i]) for i in n]
    s_raw = [lax.dot_general(c["q"], c["k"], (((1,), (1,)), ((), ())), preferred_element_type=F32)
             for c in ch]
    s = [(s_raw[i] * pmat[i]).astype(BF16) for i in n]
    intra = [jnp.dot(s[i], ch[i]["vaug"], preferred_element_type=F32) for i in n]
    carried = [jnp.dot(c["q"], c["state"].astype(BF16), preferred_element_type=F32) for c in ch]
    inter = [jnp.exp(g[i] - m_j[i]) for i in n]
    tot = [intra[i] + jnp.concatenate([inter[i], inter[i]], axis=1) * carried[i] for i in n]
    h = [tot[i][:, :LANES] / jnp.maximum(jnp.abs(tot[i][:, LANES:]), jnp.exp(-m_j[i])) for i in n]
    a = [b_end[i] - b[i] + ig[i] for i in n]
    m_new = [jnp.maximum(b_end[i] + ch[i]["m"], jnp.max(a[i], axis=0, keepdims=True)) for i in n]
    kw = [(ch[i]["k"].astype(F32) * jnp.exp(a[i] - m_new[i])).astype(BF16) for i in n]
    f_old = [jnp.exp(b_end[i] + ch[i]["m"] - m_new[i]) for i in n]
    upd = [pl.dot(kw[i], ch[i]["vaug"], trans_a=True) for i in n]
    state_new = [jnp.concatenate([f_old[i], f_old[i]], axis=1) * ch[i]["state"] + upd[i] for i in n]
    return [(h[i], state_new[i], m_new[i]) for i in n]


def _mlstm_kernel(q_ref, k_ref, v_ref, o_ref, gt_ref, g_ref, sin_ref, min_ref,
                  y_ref, sout_ref, mout_ref, hf_scr, hb_scr, *, seq, k_scale):
    L = MLSTM_CHUNK
    nc = seq // L
    pair = pl.program_id(1)
    lane = lax.broadcasted_iota(jnp.int32, (1, LANES), 1)
    rr = lax.broadcasted_iota(jnp.int32, (L, L), 0)
    cc = lax.broadcasted_iota(jnp.int32, (L, L), 1)
    causal = (rr >= cc, rr <= cc)
    cum = tuple(jnp.where(m, 1.0, 0.0).astype(BF16) for m in causal)
    ones = jnp.ones((L, LANES), BF16)
    qmask, kmask, sel = [], [], []
    for j in range(2):
        own = (lane // (LANES // 2)) == j
        qmask.append(jnp.where(own, 1.0, 0.0).astype(BF16))
        kmask.append(jnp.where(own, k_scale, 0.0).astype(BF16))
        head = 2 * pair + j
        sel.append([jnp.where(rr == kind * M_HEADS + head, 1.0, 0.0).astype(BF16) for kind in range(4)])

    def body(c, carry):
        chains, dest = [], []
        for d in range(2):
            rows = pl.ds(pl.multiple_of((c if d == 0 else nc - 1 - c) * L, L), L)
            q_all, k_all, gch = q_ref[rows, :], k_ref[rows, :], gt_ref[rows, :]
            for j in range(2):
                hcols = slice(j * LANES, (j + 1) * LANES)
                idx = 2 * (2 * j + d)
                chains.append(dict(
                    q=q_all * qmask[j], k=k_all * kmask[j],
                    vaug=jnp.concatenate([v_ref[rows, hcols], ones], axis=1), gch=gch,
                    sel_i=sel[j][2 * d], sel_f=sel[j][2 * d + 1], cum=cum[d], causal=causal[d],
                    reverse=d == 1, state=carry[idx], m=carry[idx + 1]))
                dest.append((hf_scr if d == 0 else hb_scr, rows, hcols, idx))
        carry = list(carry)
        for (scr, rows, hcols, idx), (h, state, m) in zip(dest, _mlstm_chunks(chains)):
            scr[rows, hcols] = h.astype(BF16)
            carry[idx], carry[idx + 1] = state, m
        return tuple(carry)

    init = []
    for j in range(2):
        for d in range(2):
            init += [sin_ref[0, j, d], min_ref[0, j, d][0:1, :]]
    final = lax.fori_loop(0, nc, body, tuple(init))
    for j in range(2):
        for d in range(2):
            idx = 2 * (2 * j + d)
            sout_ref[0, j, d] = final[idx]
            mout_ref[0, j, d] = jnp.broadcast_to(final[idx + 1], (SUBLANES, LANES))

    def finish(c, _):
        rows = pl.ds(pl.multiple_of(c * L, L), L)
        for j in range(2):
            hcols = slice(j * LANES, (j + 1) * LANES)
            h = hf_scr[rows, hcols].astype(F32) + hb_scr[rows, hcols].astype(F32)
            ms = jnp.mean(h * h, axis=-1, keepdims=True)
            y = h * lax.rsqrt(ms + EPS) * g_ref[:, hcols]
            y_ref[rows, hcols] = (y * _sigmoid(o_ref[rows, hcols].astype(F32))).astype(BF16)
        return 0

    lax.fori_loop(0, nc, finish, 0)


def _mlstm(qk, p, gates, g_mlstm, state_in, m_in, batch, seq, v_col0, o_col0):
    rows = qk.shape[0]
    pw = 2 * LANES
    k_col0 = qk.shape[1] // (2 * LANES)
    dk = qk.shape[1] // (2 * M_HEADS)
    assert v_col0 % 2 == 0 and o_col0 % 2 == 0
    return pl.pallas_call(
        functools.partial(_mlstm_kernel, seq=seq, k_scale=dk ** -0.5),
        grid=(batch, M_HEADS // 2),
        in_specs=[
            pl.BlockSpec((seq, LANES), lambda b, h: (b, h)),
            pl.BlockSpec((seq, LANES), lambda b, h: (b, k_col0 + h)),
            pl.BlockSpec((seq, pw), lambda b, h: (b, v_col0 // 2 + h)),
            pl.BlockSpec((seq, pw), lambda b, h: (b, o_col0 // 2 + h)),
            pl.BlockSpec((seq, LANES), lambda b, h: (b, 0)),
            pl.BlockSpec((1, pw), lambda b, h: (0, h)),
            pl.BlockSpec((1, 2, 2, LANES, 2 * LANES), lambda b, h: (b, h, 0, 0, 0)),
            pl.BlockSpec((1, 2, 2, SUBLANES, LANES), lambda b, h: (b, h, 0, 0, 0)),
        ],
        out_specs=[
            pl.BlockSpec((seq, pw), lambda b, h: (b, h)),
            pl.BlockSpec((1, 2, 2, LANES, 2 * LANES), lambda b, h: (b, h, 0, 0, 0)),
            pl.BlockSpec((1, 2, 2, SUBLANES, LANES), lambda b, h: (b, h, 0, 0, 0)),
        ],
        out_shape=[
            jax.ShapeDtypeStruct((rows, M_HEADS * LANES), BF16),
            jax.ShapeDtypeStruct(state_in.shape, F32),
            jax.ShapeDtypeStruct(m_in.shape, F32),
        ],
        scratch_shapes=[pltpu.VMEM((seq, pw), BF16), pltpu.VMEM((seq, pw), BF16)],
        compiler_params=_cparams(("arbitrary", "arbitrary")),
        name="mlstm",
    )(qk, qk, p, p, gates, g_mlstm.reshape(1, -1), state_in, m_in)


def _dft_mats(n, scale):
    idx = jnp.arange(n, dtype=jnp.int32)
    ang = (2.0 * math.pi / n) * ((idx[:, None] * idx[None, :]) % n).astype(F32)
    return jnp.cos(ang) * scale, jnp.sin(ang) * scale


def _channel_dft(u, cc_ref, sc_ref):
    gw = cc_ref.shape[0]
    zr, zi = [], []
    for g in range(u.shape[1] // gw):
        ug = u[:, g * gw:(g + 1) * gw]
        zr.append(jnp.dot(ug, cc_ref[...], preferred_element_type=F32))
        zi.append(jnp.dot(ug, sc_ref[...], preferred_element_type=F32))
    return jnp.concatenate(zr, axis=1), jnp.concatenate(zi, axis=1)


def _group_rmsnorm(y, g, gw):
    outs = []
    for k in range(y.shape[1] // gw):
        yk = y[:, k * gw:(k + 1) * gw]
        ms = jnp.mean(yk * yk, axis=-1, keepdims=True)
        outs.append(yk * lax.rsqrt(ms + EPS) * g[:, k * gw:(k + 1) * gw])
    return jnp.concatenate(outs, axis=1)


def _fnet_a_kernel(u_ref, cc_ref, sc_ref, f_ref, cw_ref, sw_ref, yr_ref, yi_ref):
    fw = u_ref.shape[3]
    for r in range(u_ref.shape[2]):
        zr, zi = _channel_dft(u_ref[0, :, r, :].astype(BF16), cc_ref, sc_ref)
        z = jnp.concatenate([zr, zi], axis=0).astype(BF16)
        y = jnp.dot(f_ref[...], z, preferred_element_type=F32)
        n2 = y.shape[0] // 2
        yr, yi = y[:n2], y[n2:]
        cw, sw = cw_ref[r], sw_ref[r]
        for j in range(fw // LANES):
            cols = slice(j * LANES, (j + 1) * LANES)
            ocols = slice(r * fw + j * LANES, r * fw + (j + 1) * LANES)
            yr_ref[0, :, ocols] = (yr[:, cols] * cw + yi[:, cols] * sw).astype(BF16)
            yi_ref[0, :, ocols] = (yi[:, cols] * cw - yr[:, cols] * sw).astype(BF16)


def _fnet_b_kernel(yr_ref, yi_ref, gc_ref, gs_ref, g_ref, o_ref, *, gw):
    y = (jnp.dot(gc_ref[...], yr_ref[0], preferred_element_type=F32)
         + jnp.dot(gs_ref[...], yi_ref[0], preferred_element_type=F32))
    out = _group_rmsnorm(y, g_ref[...], gw).astype(BF16)
    o_ref[0] = out.reshape(o_ref.shape[1:])


def _fnet_direct_kernel(u_ref, cc_ref, sc_ref, ct_ref, st_ref, g_ref, o_ref, *, gw):
    zr, zi = _channel_dft(u_ref[...].astype(BF16), cc_ref, sc_ref)
    y = (jnp.dot(ct_ref[...], zr.astype(BF16), preferred_element_type=F32)
         + jnp.dot(st_ref[...], zi.astype(BF16), preferred_element_type=F32))
    o_ref[...] = _group_rmsnorm(y, g_ref[...], gw).astype(BF16)


def _fourier_mix(p, g_fourier, batch, seq):
    f_width = p.shape[1]
    gw = f_width // F_GROUPS
    cc, sc = _dft_mats(gw, gw ** -0.5)
    cc, msc = cc.astype(BF16), (-sc).astype(BF16)
    g2 = g_fourier.reshape(1, f_width)
    rows = batch * seq
    if seq <= 512:
        ct, st = _dft_mats(seq, seq ** -0.5)
        return pl.pallas_call(
            functools.partial(_fnet_direct_kernel, gw=gw),
            grid=(batch,),
            in_specs=[
                pl.BlockSpec((seq, f_width), lambda b: (b, 0)),
                pl.BlockSpec((gw, gw), lambda b: (0, 0)),
                pl.BlockSpec((gw, gw), lambda b: (0, 0)),
                pl.BlockSpec((seq, seq), lambda b: (0, 0)),
                pl.BlockSpec((seq, seq), lambda b: (0, 0)),
                pl.BlockSpec((1, f_width), lambda b: (0, 0)),
            ],
            out_specs=pl.BlockSpec((seq, f_width), lambda b: (b, 0)),
            out_shape=jax.ShapeDtypeStruct((rows, f_width), BF16),
            compiler_params=_cparams(("arbitrary",)),
            name="fourier_direct",
        )(p, cc, msc, ct.astype(BF16), st.astype(BF16), g2)

    n2 = DFT_N2
    n1 = seq // n2
    c2, s2 = _dft_mats(n2, n2 ** -0.5)
    fmat = jnp.concatenate([jnp.concatenate([c2, s2], axis=1),
                            jnp.concatenate([-s2, c2], axis=1)], axis=0).astype(BF16)
    i1 = jnp.arange(n1, dtype=jnp.int32)
    i2 = jnp.arange(n2, dtype=jnp.int32)
    tw = (2.0 * math.pi / seq) * (i1[:, None] * i2[None, :]).astype(F32)
    cw = jnp.broadcast_to(jnp.cos(tw)[:, :, None], (n1, n2, LANES))
    sw = jnp.broadcast_to(jnp.sin(tw)[:, :, None], (n1, n2, LANES))
    nr = SUBLANES
    yr, yi = pl.pallas_call(
        _fnet_a_kernel,
        grid=(batch, n1 // nr),
        in_specs=[
            pl.BlockSpec((1, n2, nr, f_width), lambda b, j: (b, 0, j, 0)),
            pl.BlockSpec((gw, gw), lambda b, j: (0, 0)),
            pl.BlockSpec((gw, gw), lambda b, j: (0, 0)),
            pl.BlockSpec((2 * n2, 2 * n2), lambda b, j: (0, 0)),
            pl.BlockSpec((nr, n2, LANES), lambda b, j: (j, 0, 0)),
            pl.BlockSpec((nr, n2, LANES), lambda b, j: (j, 0, 0)),
        ],
        out_specs=[
            pl.BlockSpec((1, n2, nr * f_width), lambda b, j: (b, 0, j)),
            pl.BlockSpec((1, n2, nr * f_width), lambda b, j: (b, 0, j)),
        ],
        out_shape=[jax.ShapeDtypeStruct((batch, n2, n1 * f_width), BF16)] * 2,
        compiler_params=_cparams(("arbitrary", "arbitrary")),
        name="fourier_stage_a",
    )(p.reshape(batch, n2, n1, f_width), cc, msc, fmat, cw, sw)

    kb = DFT_KB
    c1, s1 = _dft_mats(n1, n1 ** -0.5)
    eye = jnp.eye(kb, dtype=F32)
    gc = jnp.einsum("kn,ab->kabn", c1, eye).reshape(n1 * kb, kb * n1).astype(BF16)
    gs = jnp.einsum("kn,ab->kabn", s1, eye).reshape(n1 * kb, kb * n1).astype(BF16)
    out = pl.pallas_call(
        functools.partial(_fnet_b_kernel, gw=gw),
        grid=(batch, n2 // kb),
        in_specs=[
            pl.BlockSpec((1, kb * n1, f_width), lambda b, j: (b, j, 0)),
            pl.BlockSpec((1, kb * n1, f_width), lambda b, j: (b, j, 0)),
            pl.BlockSpec((n1 * kb, kb * n1), lambda b, j: (0, 0)),
            pl.BlockSpec((n1 * kb, kb * n1), lambda b, j: (0, 0)),
            pl.BlockSpec((1, f_width), lambda b, j: (0, 0)),
        ],
        out_specs=pl.BlockSpec((1, n1, kb, f_width), lambda b, j: (b, 0, j, 0)),
        out_shape=jax.ShapeDtypeStruct((batch, n1, n2, f_width), BF16),
        compiler_params=_cparams(("arbitrary", "arbitrary")),
        name="fourier_stage_b",
    )(yr.reshape(batch, n2 * n1, f_width), yi.reshape(batch, n2 * n1, f_width), gc, gs, g2)
    return out.reshape(rows, f_width)


def _outproj_kernel(f_ref, ym_ref, w_ref, x_ref, ga_ref, o_ref):
    fw = f_ref.shape[1]
    y = (jnp.dot(f_ref[...], w_ref[:fw, :], preferred_element_type=F32)
         + jnp.dot(ym_ref[...], w_ref[fw:, :], preferred_element_type=F32))
    o_ref[...] = x_ref[...] + ga_ref[0] * y


def _out_projection(f, ym, w_out, x2, mod3, mod_row):
    rows, d = x2.shape
    tm = min(512, rows)
    return pl.pallas_call(
        _outproj_kernel,
        grid=(rows // tm,),
        in_specs=[
            pl.BlockSpec((tm, f.shape[1]), lambda i: (i, 0)),
            pl.BlockSpec((tm, ym.shape[1]), lambda i: (i, 0)),
            pl.BlockSpec(w_out.shape, lambda i: (0, 0)),
            pl.BlockSpec((tm, d), lambda i: (i, 0)),
            pl.BlockSpec((1, 1, d), lambda i: (mod_row(i, tm), 0, 2)),
        ],
        out_specs=pl.BlockSpec((tm, d), lambda i: (i, 0)),
        out_shape=jax.ShapeDtypeStruct((rows, d), F32),
        compiler_params=_cparams(("arbitrary",)),
        name="out_projection",
    )(f, ym, w_out, x2, mod3)


def _extract_top(s, k, exact):
    n = s.shape[0]
    row = lax.broadcasted_iota(jnp.int32, s.shape, 0) if exact else None
    rank = jnp.full(s.shape, RANK_NONE, F32)
    vals = []
    for it in range(k):
        m = jnp.max(s, axis=0, keepdims=True)
        hit = s == m
        if exact:
            first = jnp.min(jnp.where(hit, row, n), axis=0, keepdims=True)
            hit = row == first
        rank = jnp.where(hit, float(it), rank)
        s = jnp.where(hit, NEG, s)
        vals.append(m)
    removed = jnp.sum(jnp.where(rank < RANK_NONE, 1.0, 0.0), axis=0, keepdims=True)
    return vals, rank, removed


def _route_chunk(s0, s1, exact):
    k = PEER_TOPK
    sub = SUBLANES
    v0, rank0, rem0 = _extract_top(s0, k, exact)
    v1, rank1, rem1 = _extract_top(s1, k, exact)
    sv0 = jnp.concatenate(v0, axis=0)
    sv1 = jnp.concatenate(v1, axis=0)
    if exact:
        layout = [("row", a, b0) for a in range(k) for b0 in range(0, max(k // (a + 1), 1), sub)]
    else:
        layout = ([("row", 0, 0), ("row", 0, sub), ("row", 1, 0), ("col", 0, sub)]
                  + [("row", a, 0) for a in range(2, k // 3)]
                  + [("col", b, 0) for b in range(k // sub)])
    seen = set()
    blocks = []
    for kind, fixed, start in layout:
        cells = [(fixed, start + r) if kind == "row" else (start + r, fixed) for r in range(sub)]
        keep = [(a + 1) * (b + 1) <= k and (a, b) not in seen for a, b in cells]
        seen.update(cell for cell, kp in zip(cells, keep) if kp)
        blk = (v0[fixed] + sv1[start:start + sub]) if kind == "row" else (sv0[start:start + sub] + v1[fixed])
        if not all(keep):
            brow = lax.broadcasted_iota(jnp.int32, blk.shape, 0)
            mask = functools.reduce(jnp.logical_or, [brow == r for r, kp in enumerate(keep) if kp])
            blk = jnp.where(mask, blk, NEG)
        blocks.append(blk)
    assert len(seen) == sum(k // (a + 1) for a in range(k))
    cand = jnp.concatenate(blocks, axis=0)
    _, crank, rem2 = _extract_top(cand, k, exact)
    cnt = jnp.where(crank < RANK_NONE, 1.0, 0.0)
    z = jnp.sum(cnt * jnp.exp(cand - (v0[0] + v1[0])), axis=0, keepdims=True)
    arow = lax.broadcasted_iota(jnp.int32, sv0.shape, 0)
    n_rank = jnp.zeros(sv0.shape, F32)
    for idx, (kind, fixed, start) in enumerate(layout):
        c_blk = cnt[idx * sub:(idx + 1) * sub]
        if kind == "row":
            n_rank = n_rank + jnp.where(arow == fixed, jnp.sum(c_blk, axis=0, keepdims=True), 0.0)
        else:
            pieces = [c_blk if a0 == start else jnp.zeros_like(c_blk) for a0 in range(0, k, sub)]
            n_rank = n_rank + jnp.concatenate(pieces, axis=0)
    n_of_key = jnp.zeros(rank0.shape, F32)
    for a in range(k):
        n_of_key = n_of_key + jnp.where(rank0 == float(a), n_rank[a:a + 1], 0.0)
    tie = (jnp.where(rem0 == k, 0.0, 1.0) + jnp.where(rem1 == k, 0.0, 1.0)
           + jnp.where(rem2 == k, 0.0, 1.0))
    return jnp.exp(s0 - v0[0]) / z, n_of_key, jnp.exp(s1 - v1[0]), rank1, tie


def _peer_route_kernel(x_ref, g_ref, sh_ref, sc_ref, wq_ref, keys_ref,
                       ht_ref, e0_ref, nn_ref, e1_ref, r1_ref, s_scr):
    h2 = _rms_modulate(x_ref[...], g_ref[...], sh_ref[0], sc_ref[0])
    ht_ref[...] = h2.T.astype(BF16)
    q = jnp.dot(h2.astype(BF16), wq_ref[...], preferred_element_type=F32)
    tb = q.shape[0]
    kd = keys_ref.shape[-1]
    for hp in range(2 * PEER_HEADS):
        s_scr[hp] = lax.dot_general(keys_ref[hp // 2, hp % 2], q[:, hp * kd:(hp + 1) * kd],
                                    (((1,), (1,)), ((), ())),
                                    precision=lax.Precision.HIGHEST, preferred_element_type=F32)

    def route_head(h, _):
        for c in range(tb // LANES):
            cols = slice(c * LANES, (c + 1) * LANES)

            def route(exact):
                e0, n_of_key, e1, rank1, tie = _route_chunk(s_scr[2 * h, :, cols],
                                                            s_scr[2 * h + 1, :, cols], exact)
                e0_ref[h, :, cols] = e0
                nn_ref[h, :, cols] = n_of_key
                e1_ref[h, :, cols] = e1.astype(BF16)
                r1_ref[h, :, cols] = rank1.astype(BF16)
                return tie

            tie = route(False)

            @pl.when(jnp.max(tie) > 0.0)
            def _():
                route(True)
        return 0

    lax.fori_loop(0, PEER_HEADS, route_head, 0)


def _peer_route(x2, g, mod3, mod_row, wq, sub_keys):
    rows, d = x2.shape
    tb = min(256, rows)
    nk = sub_keys.shape[2]
    tab_spec = pl.BlockSpec((PEER_HEADS, nk, tb), lambda i: (0, 0, i))
    tab_f32 = jax.ShapeDtypeStruct((PEER_HEADS, nk, rows), F32)
    tab_bf16 = jax.ShapeDtypeStruct((PEER_HEADS, nk, rows), BF16)
    return pl.pallas_call(
        _peer_route_kernel,
        grid=(rows // tb,),
        in_specs=[
            pl.BlockSpec((tb, d), lambda i: (i, 0)),
            pl.BlockSpec((1, d), lambda i: (0, 0)),
            pl.BlockSpec((1, 1, d), lambda i: (mod_row(i, tb), 0, 3)),
            pl.BlockSpec((1, 1, d), lambda i: (mod_row(i, tb), 0, 4)),
            pl.BlockSpec(wq.shape, lambda i: (0, 0)),
            pl.BlockSpec(sub_keys.shape, lambda i: (0, 0, 0, 0)),
        ],
        out_specs=[pl.BlockSpec((d, tb), lambda i: (0, i)), tab_spec, tab_spec, tab_spec, tab_spec],
        out_shape=[jax.ShapeDtypeStruct((d, rows), BF16), tab_f32, tab_f32, tab_bf16, tab_bf16],
        scratch_shapes=[pltpu.VMEM((2 * PEER_HEADS, nk, tb), F32)],
        compiler_params=_cparams(("arbitrary",)),
        name="peer_route",
    )(x2, g.reshape(1, d), mod3, mod3, wq, sub_keys)


def _gelu_tanh(x):
    return 0.5 * x * (1.0 + jnp.tanh(math.sqrt(2.0 / math.pi) * (x + 0.044715 * (x * x * x))))


def _peer_expert_kernel(ht_ref, u_ref, v_ref, e0_ref, nn_ref, e1_ref, r1_ref, x_ref, ga_ref, gf_ref,
                        o_ref, wa_scr, *, final_norm):
    e = pl.program_id(1)
    nk = e1_ref.shape[1]
    tb = ht_ref.shape[1]
    ib = u_ref.shape[0] // nk
    pk = 2 * SUBLANES

    @pl.when(e == 0)
    def _():
        o_ref[...] = jnp.zeros(o_ref.shape, F32)

    act = jnp.dot(u_ref[...], ht_ref[...], preferred_element_type=F32)
    assert ib == SUBLANES
    i_rows = pl.ds(pl.multiple_of(e * ib, ib), ib)
    for il in range(ib):
        for c in range(tb // LANES):
            cols = slice(c * LANES, (c + 1) * LANES)
            w = [None] * (nk // pk)
            for h in range(PEER_HEADS):
                n_i = jnp.broadcast_to(nn_ref[h, i_rows, cols][il:il + 1], (pk, LANES)).astype(BF16)
                e0_i = jnp.broadcast_to(e0_ref[h, i_rows, cols][il:il + 1], (pk, LANES)).astype(BF16)
                for s in range(nk // pk):
                    rows = slice(s * pk, (s + 1) * pk)
                    e1 = e1_ref[h, rows, cols]
                    t = jnp.where(r1_ref[h, rows, cols] < n_i, e1, jnp.zeros_like(e1)) * e0_i
                    w[s] = t if w[s] is None else w[s] + t
            rows = slice(il * nk, (il + 1) * nk)
            g = _gelu_tanh(act[rows, cols]).astype(BF16)
            wa_scr[rows, cols] = jnp.concatenate(w, axis=0) * g
    o_ref[...] += pl.dot(wa_scr[...], v_ref[...], trans_a=True)

    @pl.when(e == pl.num_programs(1) - 1)
    def _():
        y = x_ref[...] + ga_ref[0] * o_ref[...]
        if final_norm:
            ms = jnp.mean(y * y, axis=-1, keepdims=True)
            y = y * lax.rsqrt(ms + EPS) * gf_ref[...]
        o_ref[...] = y


def _peer_experts(ht, u_bf, v_bf, tables, x2, mod3, mod_row, g_final, final_norm):
    rows, d = x2.shape
    n_exp = u_bf.shape[0]
    nk = tables[0].shape[1]
    tb = min(512, rows)
    eb = SUBLANES * nk
    tab_spec = pl.BlockSpec((PEER_HEADS, nk, tb), lambda i, e: (0, 0, i))
    return pl.pallas_call(
        functools.partial(_peer_expert_kernel, final_norm=final_norm),
        grid=(rows // tb, n_exp // eb),
        in_specs=[
            pl.BlockSpec((d, tb), lambda i, e: (0, i)),
            pl.BlockSpec((eb, d), lambda i, e: (e, 0)),
            pl.BlockSpec((eb, d), lambda i, e: (e, 0)),
            tab_spec, tab_spec, tab_spec, tab_spec,
            pl.BlockSpec((tb, d), lambda i, e: (i, 0)),
            pl.BlockSpec((1, 1, d), lambda i, e: (mod_row(i, tb), 0, 5)),
            pl.BlockSpec((1, d), lambda i, e: (0, 0)),
        ],
        out_specs=pl.BlockSpec((tb, d), lambda i, e: (i, 0)),
        out_shape=jax.ShapeDtypeStruct((rows, d), F32),
        scratch_shapes=[pltpu.VMEM((eb, tb), BF16)],
        compiler_params=_cparams(("arbitrary", "arbitrary")),
        name="peer_experts",
    )(ht, u_bf, v_bf, *tables, x2, mod3, g_final.reshape(1, d))


def kernel(x, c, ctx, c_ctx, w_mod, b_mod, g_norm_mix, g_norm_ffn, w_in, b_gate, conv_qk, g_fourier,
           g_mlstm, w_out, w_query, sub_keys, expert_u, expert_v, g_final):
    batch, seq, d = x.shape
    ctx_len = ctx.shape[1]
    depth = w_mod.shape[0]
    f_width = g_fourier.shape[1]
    m_width = g_mlstm.shape[1]
    qk_width = conv_qk.shape[-1]
    n_gates = b_gate.shape[1]
    n_main = f_width + qk_width + 2 * m_width
    assert w_in.shape[2] == n_main + n_gates and n_gates <= LANES
    assert f_width == qk_width == m_width and m_width == M_HEADS * LANES
    assert seq % MLSTM_CHUNK == 0 and ctx_len % MLSTM_CHUNK == 0 and seq % GRID_W == 0

    cond_rows = -(-(batch + 1) // SUBLANES) * SUBLANES
    cond = jnp.zeros((cond_rows, d), F32).at[:batch].set(c).at[batch].set(c_ctx)
    mod_all = _modulation(cond, w_mod, b_mod)

    latent_row = lambda i, tm: (i * tm) // seq
    context_row = lambda i, tm: batch

    x2 = x.reshape(batch * seq, d)
    c2 = ctx.reshape(batch * ctx_len, d)
    v_col0 = qk_width // LANES
    o_col0 = (qk_width + m_width) // LANES
    zero_state = jnp.zeros((batch, M_HEADS, 2, LANES, 2 * LANES), F32)
    zero_m = jnp.zeros((batch, M_HEADS, 2, SUBLANES, LANES), F32)

    for l in range(depth):
        last = l == depth - 1
        mod3 = mod_all[l].reshape(cond_rows, 1, N_MOD * d)
        w_main = w_in[l, :, :n_main].astype(BF16)
        w_gate = jnp.zeros((d, LANES), F32).at[:, :n_gates].set(w_in[l, :, n_main:]).astype(BF16)
        bg = jnp.zeros((1, LANES), F32).at[0, :n_gates].set(b_gate[l])
        w_out_bf = w_out[l].astype(BF16)
        wq_bf = w_query[l].astype(BF16)
        u_bf = expert_u[l].astype(BF16)
        v_bf = expert_v[l].astype(BF16)

        def mixer(tokens, mod_row, n_tok, vertical, state, m_state):
            four, p, gates = _in_projection(tokens, g_norm_mix[l], mod3, mod_row, w_main, w_gate, bg)
            qk = _qk_conv(p, conv_qk[l], qk_width, 0, n_tok, vertical)
            ym, state, m_state = _mlstm(qk, p, gates, g_mlstm[l], state, m_state, batch, n_tok,
                                        v_col0, o_col0)
            return four, ym, state, m_state

        def ffn_and_residuals(tokens, four, ym, mod_row, n_tok, final_norm):
            f = _fourier_mix(four, g_fourier[l], batch, n_tok)
            tokens = _out_projection(f, ym, w_out_bf, tokens, mod3, mod_row)
            routed = _peer_route(tokens, g_norm_ffn[l], mod3, mod_row, wq_bf, sub_keys[l])
            return _peer_experts(routed[0], u_bf, v_bf, routed[1:], tokens, mod3, mod_row,
                                 g_final, final_norm)

        pc, ymc, st, m_st = mixer(c2, context_row, ctx_len, False, zero_state, zero_m)
        p, ym, _, _ = mixer(x2, latent_row, seq, True, st, m_st)
        x2 = ffn_and_residuals(x2, p, ym, latent_row, seq, last)
        if not last:
            c2 = ffn_and_residuals(c2, pc, ymc, context_row, ctx_len, False)
    return x2.reshape(batch, seq, d)
```

```python
import functools
import math

import jax
import jax.numpy as jnp
from jax import lax
from jax.experimental import pallas as pl
from jax.experimental.pallas import tpu as pltpu

F32 = jnp.float32
BF16 = jnp.bfloat16

F_GROUPS = 4
M_HEADS = 8
GRID_W = 64
N_KEYS = 128
PEER_HEADS = 8
PEER_TOPK = 16
N_MOD = 6
EPS = 1e-6

LANES = 128
SUBLANES = 8
VMEM_LIMIT = 56 * 1024 * 1024

MLSTM_CHUNK = 128
DFT_N2 = 128
DFT_KB = 8
NEG = -3.0e38
RANK_NONE = 99.0


def _cparams(sem):
    return pltpu.CompilerParams(dimension_semantics=sem, vmem_limit_bytes=VMEM_LIMIT)


def _sigmoid(x):
    return 1.0 / (1.0 + jnp.exp(-x))


def _split3(x):
    p0 = x.astype(BF16)
    r1 = x - p0.astype(F32)
    p1 = r1.astype(BF16)
    p2 = (r1 - p1.astype(F32)).astype(BF16)
    return p0, p1, p2


def _dot01_left(a01, x):
    return sum(jnp.dot(a01, p, preferred_element_type=F32) for p in _split3(x))


def _dot01_right(x, b01):
    return sum(jnp.dot(p, b01, preferred_element_type=F32) for p in _split3(x))


def _mod_kernel(c_ref, w_ref, b_ref, o_ref):
    c = c_ref[...]
    s = (c * _sigmoid(c)).astype(BF16)
    o_ref[0] = jnp.dot(s, w_ref[0].astype(BF16), preferred_element_type=F32) + b_ref[0]


def _modulation(cond, w_mod, b_mod):
    n_layers, d, n_out = w_mod.shape
    rows = cond.shape[0]
    tn = 1024
    return pl.pallas_call(
        _mod_kernel,
        grid=(n_layers, n_out // tn),
        in_specs=[
            pl.BlockSpec((rows, d), lambda l, j: (0, 0)),
            pl.BlockSpec((1, d, tn), lambda l, j: (l, 0, j)),
            pl.BlockSpec((1, 1, tn), lambda l, j: (l, 0, j)),
        ],
        out_specs=pl.BlockSpec((1, rows, tn), lambda l, j: (l, 0, j)),
        out_shape=jax.ShapeDtypeStruct((n_layers, rows, n_out), F32),
        compiler_params=_cparams(("arbitrary", "arbitrary")),
        name="modulation",
    )(cond, w_mod, b_mod.reshape(n_layers, 1, n_out))


def _rms_modulate(x, g, shift, scale):
    ms = jnp.mean(x * x, axis=-1, keepdims=True)
    return (x * lax.rsqrt(ms + EPS) * g) * (1.0 + scale) + shift


def _inproj_kernel(x_ref, g_ref, sh_ref, sc_ref, w_ref, wg_ref, bg_ref, four_ref, p_ref, gate_ref, h_scr):
    j = pl.program_id(1)

    @pl.when(j == 0)
    def _():
        h = _rms_modulate(x_ref[...], g_ref[...], sh_ref[0], sc_ref[0]).astype(BF16)
        h_scr[...] = h
        gate_ref[...] = jnp.dot(h, wg_ref[...], preferred_element_type=F32) + bg_ref[...]

    acc = jnp.dot(h_scr[...], w_ref[...], preferred_element_type=F32)

    @pl.when(j == 0)
    def _():
        four_ref[...] = acc

    @pl.when(j > 0)
    def _():
        p_ref[...] = acc.astype(BF16)


def _in_projection(x2, g, mod3, mod_row, w_main, w_gate, b_gate):
    rows, d = x2.shape
    n_main = w_main.shape[1]
    tm = min(1024, rows)
    tn = 1024
    return pl.pallas_call(
        _inproj_kernel,
        grid=(rows // tm, n_main // tn),
        in_specs=[
            pl.BlockSpec((tm, d), lambda i, j: (i, 0)),
            pl.BlockSpec((1, d), lambda i, j: (0, 0)),
            pl.BlockSpec((1, 1, d), lambda i, j: (mod_row(i, tm), 0, 0)),
            pl.BlockSpec((1, 1, d), lambda i, j: (mod_row(i, tm), 0, 1)),
            pl.BlockSpec((d, tn), lambda i, j: (0, j)),
            pl.BlockSpec((d, LANES), lambda i, j: (0, 0)),
            pl.BlockSpec((1, LANES), lambda i, j: (0, 0)),
        ],
        out_specs=[
            pl.BlockSpec((tm, tn), lambda i, j: (i, 0)),
            pl.BlockSpec((tm, tn), lambda i, j: (i, jnp.maximum(j - 1, 0))),
            pl.BlockSpec((tm, LANES), lambda i, j: (i, 0)),
        ],
        out_shape=[
            jax.ShapeDtypeStruct((rows, tn), F32),
            jax.ShapeDtypeStruct((rows, n_main - tn), BF16),
            jax.ShapeDtypeStruct((rows, LANES), F32),
        ],
        scratch_shapes=[pltpu.VMEM((tm, d), BF16)],
        compiler_params=_cparams(("arbitrary", "arbitrary")),
        name="in_projection",
    )(x2, g.reshape(1, d), mod3, mod3, w_main, w_gate, b_gate)


def _conv_kernel(*refs, tb, width, vertical, blocks_per_image):
    if vertical:
        cur_ref, top_ref, bot_ref, w_ref, o_ref = refs
    else:
        cur_ref, w_ref, o_ref = refs
    cur = cur_ref[...].astype(F32)
    ch = cur.shape[1]
    wpos = lax.rem(lax.broadcasted_iota(jnp.int32, (tb, ch), 0), width)
    first_col = wpos == 0
    last_col = wpos == width - 1
    if vertical:
        r = lax.rem(pl.program_id(0), blocks_per_image)
        top = jnp.where(r == 0, 0.0, top_ref[...].astype(F32))
        bot = jnp.where(r == blocks_per_image - 1, 0.0, bot_ref[...].astype(F32))
        ext = jnp.concatenate([top, cur, bot], axis=0)
        bases = [(dr, ext[dr * width:dr * width + tb]) for dr in range(3)]
    else:
        bases = [(1, cur)]
    acc = jnp.zeros((tb, ch), F32)
    for dr, base in bases:
        left = jnp.where(first_col, 0.0, pltpu.roll(base, 1, axis=0))
        right = jnp.where(last_col, 0.0, pltpu.roll(base, tb - 1, axis=0))
        for dw, shifted in enumerate((left, base, right)):
            k = dr * 3 + dw
            acc = acc + shifted * w_ref[k:k + 1, :]
    o_ref[...] = (acc * _sigmoid(acc)).astype(BF16)


def _qk_conv(p, conv_w, qk_width, col_block, tokens_per_image, vertical):
    rows = p.shape[0]
    w9 = conv_w.reshape(9, qk_width)
    if vertical:
        width = GRID_W
        tb = min(512, tokens_per_image)
        bpi = tokens_per_image // tb
        halo = tb // width
        n_halo = rows // width
        in_specs = [
            pl.BlockSpec((tb, qk_width), lambda i: (i, col_block)),
            pl.BlockSpec((width, qk_width), lambda i: (jnp.maximum(i * halo - 1, 0), col_block)),
            pl.BlockSpec((width, qk_width), lambda i: (jnp.minimum((i + 1) * halo, n_halo - 1), col_block)),
            pl.BlockSpec((9, qk_width), lambda i: (0, 0)),
        ]
        args = (p, p, p, w9)
    else:
        width = tb = tokens_per_image
        bpi = 1
        in_specs = [
            pl.BlockSpec((tb, qk_width), lambda i: (i, col_block)),
            pl.BlockSpec((9, qk_width), lambda i: (0, 0)),
        ]
        args = (p, w9)
    return pl.pallas_call(
        functools.partial(_conv_kernel, tb=tb, width=width, vertical=vertical, blocks_per_image=bpi),
        grid=(rows // tb,),
        in_specs=in_specs,
        out_specs=pl.BlockSpec((tb, qk_width), lambda i: (i, 0)),
        out_shape=jax.ShapeDtypeStruct((rows, qk_width), BF16),
        compiler_params=_cparams(("arbitrary",)),
        name="qk_conv_latent" if vertical else "qk_conv_context",
    )(*args)


def _mlstm_chunks(chains):
    n = range(len(chains))
    ch = chains
    L = ch[0]["q"].shape[0]
    logsig = [jnp.minimum(c["gch"], 0.0) - jnp.log(1.0 + jnp.exp(-jnp.abs(c["gch"]))) for c in ch]
    ig_p = [_split3(c["gch"]) for c in ch]
    lf_p = [_split3(logsig[i]) for i in n]
    ig = [sum(jnp.dot(p, ch[i]["sel_i"], preferred_element_type=F32) for p in ig_p[i]) for i in n]
    lf = [sum(jnp.dot(p, ch[i]["sel_f"], preferred_element_type=F32) for p in lf_p[i]) for i in n]
    b_p = [_split3(lf[i]) for i in n]
    b = [sum(jnp.dot(ch[i]["cum"], p, preferred_element_type=F32) for p in b_p[i]) for i in n]
    b_end = [b[i][0:1, :] if ch[i]["reverse"] else b[i][L - 1:L, :] for i in n]
    a_t = [(ig[i] - b[i]).T for i in n]
    dmat = [jnp.where(ch[i]["causal"], b[i] + a_t[i], NEG) for i in n]
    g = [b[i] + ch[i]["m"] for i in n]
    m_j = [jnp.maximum(g[i], jnp.max(dmat[i], axis=-1, keepdims=True)) for i in n]
    pmat = [jnp.exp(dmat[i] - m_j[i]) for i in n]
    s_raw = [lax.dot_general(c["q"], c["k"], (((1,), (1,)), ((), ())), preferred_element_type=F32)
             for c in ch]
    s = [(s_raw[i] * pmat[i]).astype(BF16) for i in n]
    intra = [jnp.dot(s[i], ch[i]["vaug"], preferred_element_type=F32) for i in n]
    carried = [jnp.dot(c["q"], c["state"].astype(BF16), preferred_element_type=F32) for c in ch]
    inter = [jnp.exp(g[i] - m_j[i]) for i in n]
    tot = [intra[i] + jnp.concatenate([inter[i], inter[i]], axis=1) * carried[i] for i in n]
    h = [tot[i][:, :LANES] / jnp.maximum(jnp.abs(tot[i][:, LANES:]), jnp.exp(-m_j[i])) for i in n]
    a = [b_end[i] - b[i] + ig[i] for i in n]
    m_new = [jnp.maximum(b_end[i] + ch[i]["m"], jnp.max(a[i], axis=0, keepdims=True)) for i in n]
    kw = [(ch[i]["k"].astype(F32) * jnp.exp(a[i] - m_new[i])).astype(BF16) for i in n]
    f_old = [jnp.exp(b_end[i] + ch[i]["m"] - m_new[i]) for i in n]
    upd = [pl.dot(kw[i], ch[i]["vaug"], trans_a=True) for i in n]
    state_new = [jnp.concatenate([f_old[i], f_old[i]], axis=1) * ch[i]["state"] + upd[i] for i in n]
    return [(h[i], state_new[i], m_new[i]) for i in n]


def _mlstm_kernel(q_ref, k_ref, v_ref, o_ref, gt_ref, g_ref, sin_ref, min_ref,
                  y_ref, sout_ref, mout_ref, hf_scr, hb_scr, *, seq, k_scale):
    L = MLSTM_CHUNK
    nc = seq // L
    pair = pl.program_id(1)
    lane = lax.broadcasted_iota(jnp.int32, (1, LANES), 1)
    rr = lax.broadcasted_iota(jnp.int32, (L, L), 0)
    cc = lax.broadcasted_iota(jnp.int32, (L, L), 1)
    causal = (rr >= cc, rr <= cc)
    cum = tuple(jnp.where(m, 1.0, 0.0).astype(BF16) for m in causal)
    ones = jnp.ones((L, LANES), BF16)
    qmask, kmask, sel = [], [], []
    for j in range(2):
        own = (lane // (LANES // 2)) == j
        qmask.append(jnp.where(own, 1.0, 0.0).astype(BF16))
        kmask.append(jnp.where(own, k_scale, 0.0).astype(BF16))
        head = 2 * pair + j
        sel.append([jnp.where(rr == kind * M_HEADS + head, 1.0, 0.0).astype(BF16) for kind in range(4)])

    def body(c, carry):
        chains, dest = [], []
        for d in range(2):
            rows = pl.ds(pl.multiple_of((c if d == 0 else nc - 1 - c) * L, L), L)
            q_all, k_all, gch = q_ref[rows, :], k_ref[rows, :], gt_ref[rows, :]
            for j in range(2):
                hcols = slice(j * LANES, (j + 1) * LANES)
                idx = 2 * (2 * j + d)
                chains.append(dict(
                    q=q_all * qmask[j], k=k_all * kmask[j],
                    vaug=jnp.concatenate([v_ref[rows, hcols], ones], axis=1), gch=gch,
                    sel_i=sel[j][2 * d], sel_f=sel[j][2 * d + 1], cum=cum[d], causal=causal[d],
                    reverse=d == 1, state=carry[idx], m=carry[idx + 1]))
                dest.append((hf_scr if d == 0 else hb_scr, rows, hcols, idx))
        carry = list(carry)
        for (scr, rows, hcols, idx), (h, state, m) in zip(dest, _mlstm_chunks(chains)):
            scr[rows, hcols] = h.astype(BF16)
            carry[idx], carry[idx + 1] = state, m
        return tuple(carry)

    init = []
    for j in range(2):
        for d in range(2):
            init += [sin_ref[0, j, d], min_ref[0, j, d][0:1, :]]
    final = lax.fori_loop(0, nc, body, tuple(init))
    for j in range(2):
        for d in range(2):
            idx = 2 * (2 * j + d)
            sout_ref[0, j, d] = final[idx]
            mout_ref[0, j, d] = jnp.broadcast_to(final[idx + 1], (SUBLANES, LANES))

    def finish(c, _):
        rows = pl.ds(pl.multiple_of(c * L, L), L)
        for j in range(2):
            hcols = slice(j * LANES, (j + 1) * LANES)
            h = hf_scr[rows, hcols].astype(F32) + hb_scr[rows, hcols].astype(F32)
            ms = jnp.mean(h * h, axis=-1, keepdims=True)
            y = h * lax.rsqrt(ms + EPS) * g_ref[:, hcols]
            y_ref[rows, hcols] = (y * _sigmoid(o_ref[rows, hcols].astype(F32))).astype(BF16)
        return 0

    lax.fori_loop(0, nc, finish, 0)


def _mlstm(qk, p, gates, g_mlstm, state_in, m_in, batch, seq, v_col0, o_col0):
    rows = qk.shape[0]
    pw = 2 * LANES
    k_col0 = qk.shape[1] // (2 * LANES)
    dk = qk.shape[1] // (2 * M_HEADS)
    assert v_col0 % 2 == 0 and o_col0 % 2 == 0
    return pl.pallas_call(
        functools.partial(_mlstm_kernel, seq=seq, k_scale=dk ** -0.5),
        grid=(batch, M_HEADS // 2),
        in_specs=[
            pl.BlockSpec((seq, LANES), lambda b, h: (b, h)),
            pl.BlockSpec((seq, LANES), lambda b, h: (b, k_col0 + h)),
            pl.BlockSpec((seq, pw), lambda b, h: (b, v_col0 // 2 + h)),
            pl.BlockSpec((seq, pw), lambda b, h: (b, o_col0 // 2 + h)),
            pl.BlockSpec((seq, LANES), lambda b, h: (b, 0)),
            pl.BlockSpec((1, pw), lambda b, h: (0, h)),
            pl.BlockSpec((1, 2, 2, LANES, 2 * LANES), lambda b, h: (b, h, 0, 0, 0)),
            pl.BlockSpec((1, 2, 2, SUBLANES, LANES), lambda b, h: (b, h, 0, 0, 0)),
        ],
        out_specs=[
            pl.BlockSpec((seq, pw), lambda b, h: (b, h)),
            pl.BlockSpec((1, 2, 2, LANES, 2 * LANES), lambda b, h: (b, h, 0, 0, 0)),
            pl.BlockSpec((1, 2, 2, SUBLANES, LANES), lambda b, h: (b, h, 0, 0, 0)),
        ],
        out_shape=[
            jax.ShapeDtypeStruct((rows, M_HEADS * LANES), BF16),
            jax.ShapeDtypeStruct(state_in.shape, F32),
            jax.ShapeDtypeStruct(m_in.shape, F32),
        ],
        scratch_shapes=[pltpu.VMEM((seq, pw), BF16), pltpu.VMEM((seq, pw), BF16)],
        compiler_params=_cparams(("arbitrary", "arbitrary")),
        name="mlstm",
    )(qk, qk, p, p, gates, g_mlstm.reshape(1, -1), state_in, m_in)


def _dft_mats(n, scale):
    idx = jnp.arange(n, dtype=jnp.int32)
    ang = (2.0 * math.pi / n) * ((idx[:, None] * idx[None, :]) % n).astype(F32)
    return jnp.cos(ang) * scale, jnp.sin(ang) * scale


def _channel_dft(u, cc_ref, sc_ref):
    gw = cc_ref.shape[0]
    zr, zi = [], []
    for g in range(u.shape[1] // gw):
        ug = u[:, g * gw:(g + 1) * gw]
        zr.append(jnp.dot(ug, cc_ref[...], preferred_element_type=F32))
        zi.append(jnp.dot(ug, sc_ref[...], preferred_element_type=F32))
    return jnp.concatenate(zr, axis=1), jnp.concatenate(zi, axis=1)


def _group_rmsnorm(y, g, gw):
    outs = []
    for k in range(y.shape[1] // gw):
        yk = y[:, k * gw:(k + 1) * gw]
        ms = jnp.mean(yk * yk, axis=-1, keepdims=True)
        outs.append(yk * lax.rsqrt(ms + EPS) * g[:, k * gw:(k + 1) * gw])
    return jnp.concatenate(outs, axis=1)


def _fnet_a_kernel(u_ref, cc_ref, sc_ref, f_ref, cw_ref, sw_ref, yr_ref, yi_ref):
    fw = u_ref.shape[3]
    for r in range(u_ref.shape[2]):
        zr, zi = _channel_dft(u_ref[0, :, r, :].astype(BF16), cc_ref, sc_ref)
        z = jnp.concatenate([zr, zi], axis=0).astype(BF16)
        y = jnp.dot(f_ref[...], z, preferred_element_type=F32)
        n2 = y.shape[0] // 2
        yr, yi = y[:n2], y[n2:]
        cw, sw = cw_ref[r], sw_ref[r]
        for j in range(fw // LANES):
            cols = slice(j * LANES, (j + 1) * LANES)
            ocols = slice(r * fw + j * LANES, r * fw + (j + 1) * LANES)
            yr_ref[0, :, ocols] = (yr[:, cols] * cw + yi[:, cols] * sw).astype(BF16)
            yi_ref[0, :, ocols] = (yi[:, cols] * cw - yr[:, cols] * sw).astype(BF16)


def _fnet_b_kernel(yr_ref, yi_ref, gc_ref, gs_ref, g_ref, o_ref, *, gw):
    y = (jnp.dot(gc_ref[...], yr_ref[0], preferred_element_type=F32)
         + jnp.dot(gs_ref[...], yi_ref[0], preferred_element_type=F32))
    out = _group_rmsnorm(y, g_ref[...], gw).astype(BF16)
    o_ref[0] = out.reshape(o_ref.shape[1:])


def _fnet_direct_kernel(u_ref, cc_ref, sc_ref, ct_ref, st_ref, g_ref, o_ref, *, gw):
    zr, zi = _channel_dft(u_ref[...].astype(BF16), cc_ref, sc_ref)
    y = (jnp.dot(ct_ref[...], zr.astype(BF16), preferred_element_type=F32)
         + jnp.dot(st_ref[...], zi.astype(BF16), preferred_element_type=F32))
    o_ref[...] = _group_rmsnorm(y, g_ref[...], gw).astype(BF16)


def _fourier_mix(p, g_fourier, batch, seq):
    f_width = p.shape[1]
    gw = f_width // F_GROUPS
    cc, sc = _dft_mats(gw, gw ** -0.5)
    cc, msc = cc.astype(BF16), (-sc).astype(BF16)
    g2 = g_fourier.reshape(1, f_width)
    rows = batch * seq
    if seq <= 512:
        ct, st = _dft_mats(seq, seq ** -0.5)
        return pl.pallas_call(
            functools.partial(_fnet_direct_kernel, gw=gw),
            grid=(batch,),
            in_specs=[
                pl.BlockSpec((seq, f_width), lambda b: (b, 0)),
                pl.BlockSpec((gw, gw), lambda b: (0, 0)),
                pl.BlockSpec((gw, gw), lambda b: (0, 0)),
                pl.BlockSpec((seq, seq), lambda b: (0, 0)),
                pl.BlockSpec((seq, seq), lambda b: (0, 0)),
                pl.BlockSpec((1, f_width), lambda b: (0, 0)),
            ],
            out_specs=pl.BlockSpec((seq, f_width), lambda b: (b, 0)),
            out_shape=jax.ShapeDtypeStruct((rows, f_width), BF16),
            compiler_params=_cparams(("arbitrary",)),
            name="fourier_direct",
        )(p, cc, msc, ct.astype(BF16), st.astype(BF16), g2)

    n2 = DFT_N2
    n1 = seq // n2
    c2, s2 = _dft_mats(n2, n2 ** -0.5)
    fmat = jnp.concatenate([jnp.concatenate([c2, s2], axis=1),
                            jnp.concatenate([-s2, c2], axis=1)], axis=0).astype(BF16)
    i1 = jnp.arange(n1, dtype=jnp.int32)
    i2 = jnp.arange(n2, dtype=jnp.int32)
    tw = (2.0 * math.pi / seq) * (i1[:, None] * i2[None, :]).astype(F32)
    cw = jnp.broadcast_to(jnp.cos(tw)[:, :, None], (n1, n2, LANES))
    sw = jnp.broadcast_to(jnp.sin(tw)[:, :, None], (n1, n2, LANES))
    nr = SUBLANES
    yr, yi = pl.pallas_call(
        _fnet_a_kernel,
        grid=(batch, n1 // nr),
        in_specs=[
            pl.BlockSpec((1, n2, nr, f_width), lambda b, j: (b, 0, j, 0)),
            pl.BlockSpec((gw, gw), lambda b, j: (0, 0)),
            pl.BlockSpec((gw, gw), lambda b, j: (0, 0)),
            pl.BlockSpec((2 * n2, 2 * n2), lambda b, j: (0, 0)),
            pl.BlockSpec((nr, n2, LANES), lambda b, j: (j, 0, 0)),
            pl.BlockSpec((nr, n2, LANES), lambda b, j: (j, 0, 0)),
        ],
        out_specs=[
            pl.BlockSpec((1, n2, nr * f_width), lambda b, j: (b, 0, j)),
            pl.BlockSpec((1, n2, nr * f_width), lambda b, j: (b, 0, j)),
        ],
        out_shape=[jax.ShapeDtypeStruct((batch, n2, n1 * f_width), BF16)] * 2,
        compiler_params=_cparams(("arbitrary", "arbitrary")),
        name="fourier_stage_a",
    )(p.reshape(batch, n2, n1, f_width), cc, msc, fmat, cw, sw)

    kb = DFT_KB
    c1, s1 = _dft_mats(n1, n1 ** -0.5)
    eye = jnp.eye(kb, dtype=F32)
    gc = jnp.einsum("kn,ab->kabn", c1, eye).reshape(n1 * kb, kb * n1).astype(BF16)
    gs = jnp.einsum("kn,ab->kabn", s1, eye).reshape(n1 * kb, kb * n1).astype(BF16)
    out = pl.pallas_call(
        functools.partial(_fnet_b_kernel, gw=gw),
        grid=(batch, n2 // kb),
        in_specs=[
            pl.BlockSpec((1, kb * n1, f_width), lambda b, j: (b, j, 0)),
            pl.BlockSpec((1, kb * n1, f_width), lambda b, j: (b, j, 0)),
            pl.BlockSpec((n1 * kb, kb * n1), lambda b, j: (0, 0)),
            pl.BlockSpec((n1 * kb, kb * n1), lambda b, j: (0, 0)),
            pl.BlockSpec((1, f_width), lambda b, j: (0, 0)),
        ],
        out_specs=pl.BlockSpec((1, n1, kb, f_width), lambda b, j: (b, 0, j, 0)),
        out_shape=jax.ShapeDtypeStruct((batch, n1, n2, f_width), BF16),
        compiler_params=_cparams(("arbitrary", "arbitrary")),
        name="fourier_stage_b",
    )(yr.reshape(batch, n2 * n1, f_width), yi.reshape(batch, n2 * n1, f_width), gc, gs, g2)
    return out.reshape(rows, f_width)


def _outproj_kernel(f_ref, ym_ref, w_ref, x_ref, ga_ref, o_ref):
    fw = f_ref.shape[1]
    y = (jnp.dot(f_ref[...], w_ref[:fw, :], preferred_element_type=F32)
         + jnp.dot(ym_ref[...], w_ref[fw:, :], preferred_element_type=F32))
    o_ref[...] = x_ref[...] + ga_ref[0] * y


def _out_projection(f, ym, w_out, x2, mod3, mod_row):
    rows, d = x2.shape
    tm = min(512, rows)
    return pl.pallas_call(
        _outproj_kernel,
        grid=(rows // tm,),
        in_specs=[
            pl.BlockSpec((tm, f.shape[1]), lambda i: (i, 0)),
            pl.BlockSpec((tm, ym.shape[1]), lambda i: (i, 0)),
            pl.BlockSpec(w_out.shape, lambda i: (0, 0)),
            pl.BlockSpec((tm, d), lambda i: (i, 0)),
            pl.BlockSpec((1, 1, d), lambda i: (mod_row(i, tm), 0, 2)),
        ],
        out_specs=pl.BlockSpec((tm, d), lambda i: (i, 0)),
        out_shape=jax.ShapeDtypeStruct((rows, d), F32),
        compiler_params=_cparams(("arbitrary",)),
        name="out_projection",
    )(f, ym, w_out, x2, mod3)


def _extract_top(s, k, exact):
    n = s.shape[0]
    row = lax.broadcasted_iota(jnp.int32, s.shape, 0) if exact else None
    rank = jnp.full(s.shape, RANK_NONE, F32)
    vals = []
    for it in range(k):
        m = jnp.max(s, axis=0, keepdims=True)
        hit = s == m
        if exact:
            first = jnp.min(jnp.where(hit, row, n), axis=0, keepdims=True)
            hit = row == first
        rank = jnp.where(hit, float(it), rank)
        s = jnp.where(hit, NEG, s)
        vals.append(m)
    removed = jnp.sum(jnp.where(rank < RANK_NONE, 1.0, 0.0), axis=0, keepdims=True)
    return vals, rank, removed


def _route_chunk(s0, s1, exact):
    k = PEER_TOPK
    sub = SUBLANES
    v0, rank0, rem0 = _extract_top(s0, k, exact)
    v1, rank1, rem1 = _extract_top(s1, k, exact)
    sv0 = jnp.concatenate(v0, axis=0)
    sv1 = jnp.concatenate(v1, axis=0)
    if exact:
        layout = [("row", a, b0) for a in range(k) for b0 in range(0, max(k // (a + 1), 1), sub)]
    else:
        layout = ([("row", 0, 0), ("row", 0, sub), ("row", 1, 0), ("col", 0, sub)]
                  + [("row", a, 0) for a in range(2, k // 3)]
                  + [("col", b, 0) for b in range(k // sub)])
    seen = set()
    blocks = []
    for kind, fixed, start in layout:
        cells = [(fixed, start + r) if kind == "row" else (start + r, fixed) for r in range(sub)]
        keep = [(a + 1) * (b + 1) <= k and (a, b) not in seen for a, b in cells]
        seen.update(cell for cell, kp in zip(cells, keep) if kp)
        blk = (v0[fixed] + sv1[start:start + sub]) if kind == "row" else (sv0[start:start + sub] + v1[fixed])
        if not all(keep):
            brow = lax.broadcasted_iota(jnp.int32, blk.shape, 0)
            mask = functools.reduce(jnp.logical_or, [brow == r for r, kp in enumerate(keep) if kp])
            blk = jnp.where(mask, blk, NEG)
        blocks.append(blk)
    assert len(seen) == sum(k // (a + 1) for a in range(k))
    cand = jnp.concatenate(blocks, axis=0)
    _, crank, rem2 = _extract_top(cand, k, exact)
    cnt = jnp.where(crank < RANK_NONE, 1.0, 0.0)
    z = jnp.sum(cnt * jnp.exp(cand - (v0[0] + v1[0])), axis=0, keepdims=True)
    arow = lax.broadcasted_iota(jnp.int32, sv0.shape, 0)
    n_rank = jnp.zeros(sv0.shape, F32)
    for idx, (kind, fixed, start) in enumerate(layout):
        c_blk = cnt[idx * sub:(idx + 1) * sub]
        if kind == "row":
            n_rank = n_rank + jnp.where(arow == fixed, jnp.sum(c_blk, axis=0, keepdims=True), 0.0)
        else:
            pieces = [c_blk if a0 == start else jnp.zeros_like(c_blk) for a0 in range(0, k, sub)]
            n_rank = n_rank + jnp.concatenate(pieces, axis=0)
    n_of_key = jnp.zeros(rank0.shape, F32)
    for a in range(k):
        n_of_key = n_of_key + jnp.where(rank0 == float(a), n_rank[a:a + 1], 0.0)
    tie = (jnp.where(rem0 == k, 0.0, 1.0) + jnp.where(rem1 == k, 0.0, 1.0)
           + jnp.where(rem2 == k, 0.0, 1.0))
    return jnp.exp(s0 - v0[0]) / z, n_of_key, jnp.exp(s1 - v1[0]), rank1, tie


def _peer_route_kernel(x_ref, g_ref, sh_ref, sc_ref, wq_ref, keys_ref,
                       ht_ref, e0_ref, nn_ref, e1_ref, r1_ref, s_scr):
    h2 = _rms_modulate(x_ref[...], g_ref[...], sh_ref[0], sc_ref[0])
    ht_ref[...] = h2.T.astype(BF16)
    q = jnp.dot(h2.astype(BF16), wq_ref[...], preferred_element_type=F32)
    tb = q.shape[0]
    kd = keys_ref.shape[-1]
    for hp in range(2 * PEER_HEADS):
        s_scr[hp] = lax.dot_general(keys_ref[hp // 2, hp % 2], q[:, hp * kd:(hp + 1) * kd],
                                    (((1,), (1,)), ((), ())),
                                    precision=lax.Precision.HIGHEST, preferred_element_type=F32)

    def route_head(h, _):
        for c in range(tb // LANES):
            cols = slice(c * LANES, (c + 1) * LANES)

            def route(exact):
                e0, n_of_key, e1, rank1, tie = _route_chunk(s_scr[2 * h, :, cols],
                                                            s_scr[2 * h + 1, :, cols], exact)
                e0_ref[h, :, cols] = e0
                nn_ref[h, :, cols] = n_of_key
                e1_ref[h, :, cols] = e1.astype(BF16)
                r1_ref[h, :, cols] = rank1.astype(BF16)
                return tie

            tie = route(False)

            @pl.when(jnp.max(tie) > 0.0)
            def _():
                route(True)
        return 0

    lax.fori_loop(0, PEER_HEADS, route_head, 0)


def _peer_route(x2, g, mod3, mod_row, wq, sub_keys):
    rows, d = x2.shape
    tb = min(256, rows)
    nk = sub_keys.shape[2]
    tab_spec = pl.BlockSpec((PEER_HEADS, nk, tb), lambda i: (0, 0, i))
    tab_f32 = jax.ShapeDtypeStruct((PEER_HEADS, nk, rows), F32)
    tab_bf16 = jax.ShapeDtypeStruct((PEER_HEADS, nk, rows), BF16)
    return pl.pallas_call(
        _peer_route_kernel,
        grid=(rows // tb,),
        in_specs=[
            pl.BlockSpec((tb, d), lambda i: (i, 0)),
            pl.BlockSpec((1, d), lambda i: (0, 0)),
            pl.BlockSpec((1, 1, d), lambda i: (mod_row(i, tb), 0, 3)),
            pl.BlockSpec((1, 1, d), lambda i: (mod_row(i, tb), 0, 4)),
            pl.BlockSpec(wq.shape, lambda i: (0, 0)),
            pl.BlockSpec(sub_keys.shape, lambda i: (0, 0, 0, 0)),
        ],
        out_specs=[pl.BlockSpec((d, tb), lambda i: (0, i)), tab_spec, tab_spec, tab_spec, tab_spec],
        out_shape=[jax.ShapeDtypeStruct((d, rows), BF16), tab_f32, tab_f32, tab_bf16, tab_bf16],
        scratch_shapes=[pltpu.VMEM((2 * PEER_HEADS, nk, tb), F32)],
        compiler_params=_cparams(("arbitrary",)),
        name="peer_route",
    )(x2, g.reshape(1, d), mod3, mod3, wq, sub_keys)


def _gelu_tanh(x):
    return 0.5 * x * (1.0 + jnp.tanh(math.sqrt(2.0 / math.pi) * (x + 0.044715 * (x * x * x))))


def _peer_expert_kernel(ht_ref, u_ref, v_ref, e0_ref, nn_ref, e1_ref, r1_ref, x_ref, ga_ref, gf_ref,
                        o_ref, wat_scr, *, final_norm, n_blocks):
    e = pl.program_id(1)
    nk = e1_ref.shape[1]
    d, tb = ht_ref.shape
    ib = u_ref.shape[0] // nk
    pk = 2 * SUBLANES
    tile = 2 * LANES
    tw = min(tile, tb)
    n_chunks = tb // LANES
    assert ib == SUBLANES
    slot = lax.rem(e, 2)
    i_rows = pl.ds(pl.multiple_of(jnp.minimum(e, n_blocks - 1) * ib, ib), ib)

    @pl.when(e == 0)
    def _():
        o_ref[...] = jnp.zeros(o_ref.shape, F32)
        wat_scr[1] = jnp.zeros(wat_scr.shape[1:], BF16)

    def back_pieces():
        prev = 1 - slot
        kw = 2 * tile
        for kt in range(u_ref.shape[0] // kw):
            for nt in range(d // tile):
                def piece(kt=kt, nt=nt):
                    krows = slice(kt * kw, (kt + 1) * kw)
                    ncols = slice(nt * tile, (nt + 1) * tile)
                    o_ref[:, ncols] += jnp.dot(wat_scr[prev, :, krows], v_ref[krows, ncols],
                                               preferred_element_type=F32)
                yield piece

    def gate_chunk(il, c):
        cols = slice(c * LANES, (c + 1) * LANES)
        w = [None] * (nk // pk)
        for h in range(PEER_HEADS):
            n_i = jnp.broadcast_to(nn_ref[h, i_rows, cols][il:il + 1], (pk, LANES)).astype(BF16)
            e0_i = jnp.broadcast_to(e0_ref[h, i_rows, cols][il:il + 1], (pk, LANES)).astype(BF16)
            for s in range(nk // pk):
                rows = slice(s * pk, (s + 1) * pk)
                e1 = e1_ref[h, rows, cols]
                t = jnp.where(r1_ref[h, rows, cols] < n_i, e1, jnp.zeros_like(e1)) * e0_i
                w[s] = t if w[s] is None else w[s] + t
        return jnp.concatenate(w, axis=0)

    def step(front, back):
        pending = list(back_pieces()) if back else []
        n_pieces, done = len(pending), 0
        for il in range(ib):
            rows = slice(il * nk, (il + 1) * nk)
            act, w = {}, {}
            for c in range(n_chunks):
                if front:
                    t0 = (c * LANES) // tw
                    if t0 not in act:
                        act[t0] = jnp.dot(u_ref[rows, :], ht_ref[:, t0 * tw:(t0 + 1) * tw],
                                          preferred_element_type=F32)
                    w[c] = gate_chunk(il, c)
                due = ((il * n_chunks + c + 1) * n_pieces) // (ib * n_chunks)
                while done < due:
                    pending.pop(0)()
                    done += 1
            if front:
                for c in range(n_chunks):
                    t0, off = divmod(c * LANES, tw)
                    wa = w[c].astype(F32) * _gelu_tanh(act[t0][:, off:off + LANES])
                    wat_scr[slot, c * LANES:(c + 1) * LANES, rows] = wa.T.astype(BF16)
        for piece in pending:
            piece()

    @pl.when(e < n_blocks)
    def _():
        step(True, True)

    @pl.when(e == n_blocks)
    def _():
        step(False, True)
        y = x_ref[...] + ga_ref[0] * o_ref[...]
        if final_norm:
            ms = jnp.mean(y * y, axis=-1, keepdims=True)
            y = y * lax.rsqrt(ms + EPS) * gf_ref[...]
        o_ref[...] = y


def _peer_experts(ht, u_bf, v_bf, tables, x2, mod3, mod_row, g_final, final_norm):
    rows, d = x2.shape
    n_exp = u_bf.shape[0]
    nk = tables[0].shape[1]
    tb = min(512, rows)
    eb = SUBLANES * nk
    nb = n_exp // eb
    tab_spec = pl.BlockSpec((PEER_HEADS, nk, tb), lambda i, e: (0, 0, i))
    return pl.pallas_call(
        functools.partial(_peer_expert_kernel, final_norm=final_norm, n_blocks=nb),
        grid=(rows // tb, nb + 1),
        in_specs=[
            pl.BlockSpec((d, tb), lambda i, e: (0, i)),
            pl.BlockSpec((eb, d), lambda i, e: (jnp.minimum(e, nb - 1), 0)),
            pl.BlockSpec((eb, d), lambda i, e: (jnp.maximum(e - 1, 0), 0)),
            tab_spec, tab_spec, tab_spec, tab_spec,
            pl.BlockSpec((tb, d), lambda i, e: (i, 0)),
            pl.BlockSpec((1, 1, d), lambda i, e: (mod_row(i, tb), 0, 5)),
            pl.BlockSpec((1, d), lambda i, e: (0, 0)),
        ],
        out_specs=pl.BlockSpec((tb, d), lambda i, e: (i, 0)),
        out_shape=jax.ShapeDtypeStruct((rows, d), F32),
        scratch_shapes=[pltpu.VMEM((2, tb, eb), BF16)],
        compiler_params=_cparams(("arbitrary", "arbitrary")),
        name="peer_experts",
    )(ht, u_bf, v_bf, *tables, x2, mod3, g_final.reshape(1, d))


def kernel(x, c, ctx, c_ctx, w_mod, b_mod, g_norm_mix, g_norm_ffn, w_in, b_gate, conv_qk, g_fourier,
           g_mlstm, w_out, w_query, sub_keys, expert_u, expert_v, g_final):
    batch, seq, d = x.shape
    ctx_len = ctx.shape[1]
    depth = w_mod.shape[0]
    f_width = g_fourier.shape[1]
    m_width = g_mlstm.shape[1]
    qk_width = conv_qk.shape[-1]
    n_gates = b_gate.shape[1]
    n_main = f_width + qk_width + 2 * m_width
    assert w_in.shape[2] == n_main + n_gates and n_gates <= LANES
    assert f_width == qk_width == m_width and m_width == M_HEADS * LANES
    assert seq % MLSTM_CHUNK == 0 and ctx_len % MLSTM_CHUNK == 0 and seq % GRID_W == 0

    cond_rows = -(-(batch + 1) // SUBLANES) * SUBLANES
    cond = jnp.zeros((cond_rows, d), F32).at[:batch].set(c).at[batch].set(c_ctx)
    mod_all = _modulation(cond, w_mod, b_mod)

    latent_row = lambda i, tm: (i * tm) // seq
    context_row = lambda i, tm: batch

    x2 = x.reshape(batch * seq, d)
    c2 = ctx.reshape(batch * ctx_len, d)
    v_col0 = qk_width // LANES
    o_col0 = (qk_width + m_width) // LANES
    zero_state = jnp.zeros((batch, M_HEADS, 2, LANES, 2 * LANES), F32)
    zero_m = jnp.zeros((batch, M_HEADS, 2, SUBLANES, LANES), F32)

    for l in range(depth):
        last = l == depth - 1
        mod3 = mod_all[l].reshape(cond_rows, 1, N_MOD * d)
        w_main = w_in[l, :, :n_main].astype(BF16)
        w_gate = jnp.zeros((d, LANES), F32).at[:, :n_gates].set(w_in[l, :, n_main:]).astype(BF16)
        bg = jnp.zeros((1, LANES), F32).at[0, :n_gates].set(b_gate[l])
        w_out_bf = w_out[l].astype(BF16)
        wq_bf = w_query[l].astype(BF16)
        u_bf = expert_u[l].astype(BF16)
        v_bf = expert_v[l].astype(BF16)

        def mixer(tokens, mod_row, n_tok, vertical, state, m_state):
            four, p, gates = _in_projection(tokens, g_norm_mix[l], mod3, mod_row, w_main, w_gate, bg)
            qk = _qk_conv(p, conv_qk[l], qk_width, 0, n_tok, vertical)
            ym, state, m_state = _mlstm(qk, p, gates, g_mlstm[l], state, m_state, batch, n_tok,
                                        v_col0, o_col0)
            return four, ym, state, m_state

        def ffn_and_residuals(tokens, four, ym, mod_row, n_tok, final_norm):
            f = _fourier_mix(four, g_fourier[l], batch, n_tok)
            tokens = _out_projection(f, ym, w_out_bf, tokens, mod3, mod_row)
            routed = _peer_route(tokens, g_norm_ffn[l], mod3, mod_row, wq_bf, sub_keys[l])
            return _peer_experts(routed[0], u_bf, v_bf, routed[1:], tokens, mod3, mod_row,
                                 g_final, final_norm)

        pc, ymc, st, m_st = mixer(c2, context_row, ctx_len, False, zero_state, zero_m)
        p, ym, _, _ = mixer(x2, latent_row, seq, True, st, m_st)
        x2 = ffn_and_residuals(x2, p, ym, latent_row, seq, last)
        if not last:
            c2 = ffn_and_residuals(c2, pc, ymc, context_row, ctx_len, False)
    return x2.reshape(batch, seq, d)
```

```python
import functools
import math

import jax
import jax.numpy as jnp
from jax import lax
from jax.experimental import pallas as pl
from jax.experimental.pallas import tpu as pltpu

F32 = jnp.float32
BF16 = jnp.bfloat16

F_GROUPS = 4
M_HEADS = 8
GRID_W = 64
N_KEYS = 128
PEER_HEADS = 8
PEER_TOPK = 16
N_MOD = 6
EPS = 1e-6

LANES = 128
SUBLANES = 8
VMEM_LIMIT = 56 * 1024 * 1024

MLSTM_CHUNK = 128
DFT_N2 = 128
DFT_KB = 8
NEG = -3.0e38
RANK_NONE = 99.0
F8 = jnp.float8_e4m3fn
F8_TARGET = 224.0
TINY = 1e-30


def _cparams(sem):
    return pltpu.CompilerParams(dimension_semantics=sem, vmem_limit_bytes=VMEM_LIMIT)


def _sigmoid(x):
    return 1.0 / (1.0 + jnp.exp(-x))


def _split3(x):
    p0 = x.astype(BF16)
    r1 = x - p0.astype(F32)
    p1 = r1.astype(BF16)
    p2 = (r1 - p1.astype(F32)).astype(BF16)
    return p0, p1, p2


def _dot01_left(a01, x):
    return sum(jnp.dot(a01, p, preferred_element_type=F32) for p in _split3(x))


def _dot01_right(x, b01):
    return sum(jnp.dot(p, b01, preferred_element_type=F32) for p in _split3(x))


def _mod_kernel(c_ref, w_ref, b_ref, o_ref):
    c = c_ref[...]
    s = (c * _sigmoid(c)).astype(BF16)
    o_ref[0] = jnp.dot(s, w_ref[0].astype(BF16), preferred_element_type=F32) + b_ref[0]


def _modulation(cond, w_mod, b_mod):
    n_layers, d, n_out = w_mod.shape
    rows = cond.shape[0]
    tn = 1024
    return pl.pallas_call(
        _mod_kernel,
        grid=(n_layers, n_out // tn),
        in_specs=[
            pl.BlockSpec((rows, d), lambda l, j: (0, 0)),
            pl.BlockSpec((1, d, tn), lambda l, j: (l, 0, j)),
            pl.BlockSpec((1, 1, tn), lambda l, j: (l, 0, j)),
        ],
        out_specs=pl.BlockSpec((1, rows, tn), lambda l, j: (l, 0, j)),
        out_shape=jax.ShapeDtypeStruct((n_layers, rows, n_out), F32),
        compiler_params=_cparams(("arbitrary", "arbitrary")),
        name="modulation",
    )(cond, w_mod, b_mod.reshape(n_layers, 1, n_out))


def _rms_modulate(x, g, shift, scale):
    ms = jnp.mean(x * x, axis=-1, keepdims=True)
    return (x * lax.rsqrt(ms + EPS) * g) * (1.0 + scale) + shift


def _inproj_kernel(x_ref, g_ref, sh_ref, sc_ref, w_ref, wg_ref, bg_ref, four_ref, p_ref, gate_ref, h_scr):
    j = pl.program_id(1)

    @pl.when(j == 0)
    def _():
        h = _rms_modulate(x_ref[...], g_ref[...], sh_ref[0], sc_ref[0]).astype(BF16)
        h_scr[...] = h
        gate_ref[...] = jnp.dot(h, wg_ref[...], preferred_element_type=F32) + bg_ref[...]

    acc = jnp.dot(h_scr[...], w_ref[...], preferred_element_type=F32)

    @pl.when(j == 0)
    def _():
        four_ref[...] = acc

    @pl.when(j > 0)
    def _():
        p_ref[...] = acc.astype(BF16)


def _in_projection(x2, g, mod3, mod_row, w_main, w_gate, b_gate):
    rows, d = x2.shape
    n_main = w_main.shape[1]
    tm = min(1024, rows)
    tn = 1024
    return pl.pallas_call(
        _inproj_kernel,
        grid=(rows // tm, n_main // tn),
        in_specs=[
            pl.BlockSpec((tm, d), lambda i, j: (i, 0)),
            pl.BlockSpec((1, d), lambda i, j: (0, 0)),
            pl.BlockSpec((1, 1, d), lambda i, j: (mod_row(i, tm), 0, 0)),
            pl.BlockSpec((1, 1, d), lambda i, j: (mod_row(i, tm), 0, 1)),
            pl.BlockSpec((d, tn), lambda i, j: (0, j)),
            pl.BlockSpec((d, LANES), lambda i, j: (0, 0)),
            pl.BlockSpec((1, LANES), lambda i, j: (0, 0)),
        ],
        out_specs=[
            pl.BlockSpec((tm, tn), lambda i, j: (i, 0)),
            pl.BlockSpec((tm, tn), lambda i, j: (i, jnp.maximum(j - 1, 0))),
            pl.BlockSpec((tm, LANES), lambda i, j: (i, 0)),
        ],
        out_shape=[
            jax.ShapeDtypeStruct((rows, tn), F32),
            jax.ShapeDtypeStruct((rows, n_main - tn), BF16),
            jax.ShapeDtypeStruct((rows, LANES), F32),
        ],
        scratch_shapes=[pltpu.VMEM((tm, d), BF16)],
        compiler_params=_cparams(("arbitrary", "arbitrary")),
        name="in_projection",
    )(x2, g.reshape(1, d), mod3, mod3, w_main, w_gate, b_gate)


def _conv_kernel(*refs, tb, width, vertical, blocks_per_image):
    if vertical:
        cur_ref, top_ref, bot_ref, w_ref, o_ref = refs
    else:
        cur_ref, w_ref, o_ref = refs
    cur = cur_ref[...].astype(F32)
    ch = cur.shape[1]
    wpos = lax.rem(lax.broadcasted_iota(jnp.int32, (tb, ch), 0), width)
    first_col = wpos == 0
    last_col = wpos == width - 1
    if vertical:
        r = lax.rem(pl.program_id(0), blocks_per_image)
        top = jnp.where(r == 0, 0.0, top_ref[...].astype(F32))
        bot = jnp.where(r == blocks_per_image - 1, 0.0, bot_ref[...].astype(F32))
        ext = jnp.concatenate([top, cur, bot], axis=0)
        bases = [(dr, ext[dr * width:dr * width + tb]) for dr in range(3)]
    else:
        bases = [(1, cur)]
    acc = jnp.zeros((tb, ch), F32)
    for dr, base in bases:
        left = jnp.where(first_col, 0.0, pltpu.roll(base, 1, axis=0))
        right = jnp.where(last_col, 0.0, pltpu.roll(base, tb - 1, axis=0))
        for dw, shifted in enumerate((left, base, right)):
            k = dr * 3 + dw
            acc = acc + shifted * w_ref[k:k + 1, :]
    o_ref[...] = (acc * _sigmoid(acc)).astype(BF16)


def _qk_conv(p, conv_w, qk_width, col_block, tokens_per_image, vertical):
    rows = p.shape[0]
    w9 = conv_w.reshape(9, qk_width)
    if vertical:
        width = GRID_W
        tb = min(512, tokens_per_image)
        bpi = tokens_per_image // tb
        halo = tb // width
        n_halo = rows // width
        in_specs = [
            pl.BlockSpec((tb, qk_width), lambda i: (i, col_block)),
            pl.BlockSpec((width, qk_width), lambda i: (jnp.maximum(i * halo - 1, 0), col_block)),
            pl.BlockSpec((width, qk_width), lambda i: (jnp.minimum((i + 1) * halo, n_halo - 1), col_block)),
            pl.BlockSpec((9, qk_width), lambda i: (0, 0)),
        ]
        args = (p, p, p, w9)
    else:
        width = tb = tokens_per_image
        bpi = 1
        in_specs = [
            pl.BlockSpec((tb, qk_width), lambda i: (i, col_block)),
            pl.BlockSpec((9, qk_width), lambda i: (0, 0)),
        ]
        args = (p, w9)
    return pl.pallas_call(
        functools.partial(_conv_kernel, tb=tb, width=width, vertical=vertical, blocks_per_image=bpi),
        grid=(rows // tb,),
        in_specs=in_specs,
        out_specs=pl.BlockSpec((tb, qk_width), lambda i: (i, 0)),
        out_shape=jax.ShapeDtypeStruct((rows, qk_width), BF16),
        compiler_params=_cparams(("arbitrary",)),
        name="qk_conv_latent" if vertical else "qk_conv_context",
    )(*args)


def _mlstm_chunks(chains):
    n = range(len(chains))
    ch = chains
    L = ch[0]["q"].shape[0]
    logsig = [jnp.minimum(c["gch"], 0.0) - jnp.log(1.0 + jnp.exp(-jnp.abs(c["gch"]))) for c in ch]
    ig_p = [_split3(c["gch"]) for c in ch]
    lf_p = [_split3(logsig[i]) for i in n]
    ig = [sum(jnp.dot(p, ch[i]["sel_i"], preferred_element_type=F32) for p in ig_p[i]) for i in n]
    lf = [sum(jnp.dot(p, ch[i]["sel_f"], preferred_element_type=F32) for p in lf_p[i]) for i in n]
    b_p = [_split3(lf[i]) for i in n]
    b = [sum(jnp.dot(ch[i]["cum"], p, preferred_element_type=F32) for p in b_p[i]) for i in n]
    b_end = [b[i][0:1, :] if ch[i]["reverse"] else b[i][L - 1:L, :] for i in n]
    a_t = [(ig[i] - b[i]).T for i in n]
    dmat = [jnp.where(ch[i]["causal"], b[i] + a_t[i], NEG) for i in n]
    g = [b[i] + ch[i]["m"] for i in n]
    m_j = [jnp.maximum(g[i], jnp.max(dmat[i], axis=-1, keepdims=True)) for i in n]
    pmat = [jnp.exp(dmat[i] - m_j[i]) for i in n]
    s_raw = [lax.dot_general(c["q"], c["k"], (((1,), (1,)), ((), ())), preferred_element_type=F32)
             for c in ch]
    s = [(s_raw[i] * pmat[i]).astype(BF16) for i in n]
    intra = [jnp.dot(s[i], ch[i]["vaug"], preferred_element_type=F32) for i in n]
    carried = [jnp.dot(c["q"], c["state"].astype(BF16), preferred_element_type=F32) for c in ch]
    inter = [jnp.exp(g[i] - m_j[i]) for i in n]
    tot = [intra[i] + jnp.concatenate([inter[i], inter[i]], axis=1) * carried[i] for i in n]
    h = [tot[i][:, :LANES] / jnp.maximum(jnp.abs(tot[i][:, LANES:]), jnp.exp(-m_j[i])) for i in n]
    a = [b_end[i] - b[i] + ig[i] for i in n]
    m_new = [jnp.maximum(b_end[i] + ch[i]["m"], jnp.max(a[i], axis=0, keepdims=True)) for i in n]
    kw = [(ch[i]["k"].astype(F32) * jnp.exp(a[i] - m_new[i])).astype(BF16) for i in n]
    f_old = [jnp.exp(b_end[i] + ch[i]["m"] - m_new[i]) for i in n]
    upd = [pl.dot(kw[i], ch[i]["vaug"], trans_a=True) for i in n]
    state_new = [jnp.concatenate([f_old[i], f_old[i]], axis=1) * ch[i]["state"] + upd[i] for i in n]
    return [(h[i], state_new[i], m_new[i]) for i in n]


def _mlstm_kernel(q_ref, k_ref, v_ref, o_ref, gt_ref, g_ref, sin_ref, min_ref,
                  y_ref, sout_ref, mout_ref, hf_scr, hb_scr, *, seq, k_scale):
    L = MLSTM_CHUNK
    nc = seq // L
    pair = pl.program_id(1)
    lane = lax.broadcasted_iota(jnp.int32, (1, LANES), 1)
    rr = lax.broadcasted_iota(jnp.int32, (L, L), 0)
    cc = lax.broadcasted_iota(jnp.int32, (L, L), 1)
    causal = (rr >= cc, rr <= cc)
    cum = tuple(jnp.where(m, 1.0, 0.0).astype(BF16) for m in causal)
    ones = jnp.ones((L, LANES), BF16)
    qmask, kmask, sel = [], [], []
    for j in range(2):
        own = (lane // (LANES // 2)) == j
        qmask.append(jnp.where(own, 1.0, 0.0).astype(BF16))
        kmask.append(jnp.where(own, k_scale, 0.0).astype(BF16))
        head = 2 * pair + j
        sel.append([jnp.where(rr == kind * M_HEADS + head, 1.0, 0.0).astype(BF16) for kind in range(4)])

    def body(c, carry):
        chains, dest = [], []
        for d in range(2):
            rows = pl.ds(pl.multiple_of((c if d == 0 else nc - 1 - c) * L, L), L)
            q_all, k_all, gch = q_ref[rows, :], k_ref[rows, :], gt_ref[rows, :]
            for j in range(2):
                hcols = slice(j * LANES, (j + 1) * LANES)
                idx = 2 * (2 * j + d)
                chains.append(dict(
                    q=q_all * qmask[j], k=k_all * kmask[j],
                    vaug=jnp.concatenate([v_ref[rows, hcols], ones], axis=1), gch=gch,
                    sel_i=sel[j][2 * d], sel_f=sel[j][2 * d + 1], cum=cum[d], causal=causal[d],
                    reverse=d == 1, state=carry[idx], m=carry[idx + 1]))
                dest.append((hf_scr if d == 0 else hb_scr, rows, hcols, idx))
        carry = list(carry)
        for (scr, rows, hcols, idx), (h, state, m) in zip(dest, _mlstm_chunks(chains)):
            scr[rows, hcols] = h.astype(BF16)
            carry[idx], carry[idx + 1] = state, m
        return tuple(carry)

    init = []
    for j in range(2):
        for d in range(2):
            init += [sin_ref[0, j, d], min_ref[0, j, d][0:1, :]]
    final = lax.fori_loop(0, nc, body, tuple(init))
    for j in range(2):
        for d in range(2):
            idx = 2 * (2 * j + d)
            sout_ref[0, j, d] = final[idx]
            mout_ref[0, j, d] = jnp.broadcast_to(final[idx + 1], (SUBLANES, LANES))

    def finish(c, _):
        rows = pl.ds(pl.multiple_of(c * L, L), L)
        for j in range(2):
            hcols = slice(j * LANES, (j + 1) * LANES)
            h = hf_scr[rows, hcols].astype(F32) + hb_scr[rows, hcols].astype(F32)
            ms = jnp.mean(h * h, axis=-1, keepdims=True)
            y = h * lax.rsqrt(ms + EPS) * g_ref[:, hcols]
            y_ref[rows, hcols] = (y * _sigmoid(o_ref[rows, hcols].astype(F32))).astype(BF16)
        return 0

    lax.fori_loop(0, nc, finish, 0)


def _mlstm(qk, p, gates, g_mlstm, state_in, m_in, batch, seq, v_col0, o_col0):
    rows = qk.shape[0]
    pw = 2 * LANES
    k_col0 = qk.shape[1] // (2 * LANES)
    dk = qk.shape[1] // (2 * M_HEADS)
    assert v_col0 % 2 == 0 and o_col0 % 2 == 0
    return pl.pallas_call(
        functools.partial(_mlstm_kernel, seq=seq, k_scale=dk ** -0.5),
        grid=(batch, M_HEADS // 2),
        in_specs=[
            pl.BlockSpec((seq, LANES), lambda b, h: (b, h)),
            pl.BlockSpec((seq, LANES), lambda b, h: (b, k_col0 + h)),
            pl.BlockSpec((seq, pw), lambda b, h: (b, v_col0 // 2 + h)),
            pl.BlockSpec((seq, pw), lambda b, h: (b, o_col0 // 2 + h)),
            pl.BlockSpec((seq, LANES), lambda b, h: (b, 0)),
            pl.BlockSpec((1, pw), lambda b, h: (0, h)),
            pl.BlockSpec((1, 2, 2, LANES, 2 * LANES), lambda b, h: (b, h, 0, 0, 0)),
            pl.BlockSpec((1, 2, 2, SUBLANES, LANES), lambda b, h: (b, h, 0, 0, 0)),
        ],
        out_specs=[
            pl.BlockSpec((seq, pw), lambda b, h: (b, h)),
            pl.BlockSpec((1, 2, 2, LANES, 2 * LANES), lambda b, h: (b, h, 0, 0, 0)),
            pl.BlockSpec((1, 2, 2, SUBLANES, LANES), lambda b, h: (b, h, 0, 0, 0)),
        ],
        out_shape=[
            jax.ShapeDtypeStruct((rows, M_HEADS * LANES), BF16),
            jax.ShapeDtypeStruct(state_in.shape, F32),
            jax.ShapeDtypeStruct(m_in.shape, F32),
        ],
        scratch_shapes=[pltpu.VMEM((seq, pw), BF16), pltpu.VMEM((seq, pw), BF16)],
        compiler_params=_cparams(("arbitrary", "arbitrary")),
        name="mlstm",
    )(qk, qk, p, p, gates, g_mlstm.reshape(1, -1), state_in, m_in)


def _dft_mats(n, scale):
    idx = jnp.arange(n, dtype=jnp.int32)
    ang = (2.0 * math.pi / n) * ((idx[:, None] * idx[None, :]) % n).astype(F32)
    return jnp.cos(ang) * scale, jnp.sin(ang) * scale


def _channel_dft(u, cc_ref, sc_ref):
    gw = cc_ref.shape[0]
    zr, zi = [], []
    for g in range(u.shape[1] // gw):
        ug = u[:, g * gw:(g + 1) * gw]
        zr.append(jnp.dot(ug, cc_ref[...], preferred_element_type=F32))
        zi.append(jnp.dot(ug, sc_ref[...], preferred_element_type=F32))
    return jnp.concatenate(zr, axis=1), jnp.concatenate(zi, axis=1)


def _group_rmsnorm(y, g, gw):
    outs = []
    for k in range(y.shape[1] // gw):
        yk = y[:, k * gw:(k + 1) * gw]
        ms = jnp.mean(yk * yk, axis=-1, keepdims=True)
        outs.append(yk * lax.rsqrt(ms + EPS) * g[:, k * gw:(k + 1) * gw])
    return jnp.concatenate(outs, axis=1)


def _fnet_a_kernel(u_ref, cc_ref, sc_ref, f_ref, cw_ref, sw_ref, yr_ref, yi_ref):
    fw = u_ref.shape[3]
    for r in range(u_ref.shape[2]):
        zr, zi = _channel_dft(u_ref[0, :, r, :].astype(BF16), cc_ref, sc_ref)
        z = jnp.concatenate([zr, zi], axis=0).astype(BF16)
        y = jnp.dot(f_ref[...], z, preferred_element_type=F32)
        n2 = y.shape[0] // 2
        yr, yi = y[:n2], y[n2:]
        cw, sw = cw_ref[r], sw_ref[r]
        for j in range(fw // LANES):
            cols = slice(j * LANES, (j + 1) * LANES)
            ocols = slice(r * fw + j * LANES, r * fw + (j + 1) * LANES)
            yr_ref[0, :, ocols] = (yr[:, cols] * cw + yi[:, cols] * sw).astype(BF16)
            yi_ref[0, :, ocols] = (yi[:, cols] * cw - yr[:, cols] * sw).astype(BF16)


def _fnet_b_kernel(yr_ref, yi_ref, gc_ref, gs_ref, g_ref, o_ref, *, gw):
    y = (jnp.dot(gc_ref[...], yr_ref[0], preferred_element_type=F32)
         + jnp.dot(gs_ref[...], yi_ref[0], preferred_element_type=F32))
    out = _group_rmsnorm(y, g_ref[...], gw).astype(BF16)
    o_ref[0] = out.reshape(o_ref.shape[1:])


def _fnet_direct_kernel(u_ref, cc_ref, sc_ref, ct_ref, st_ref, g_ref, o_ref, *, gw):
    zr, zi = _channel_dft(u_ref[...].astype(BF16), cc_ref, sc_ref)
    y = (jnp.dot(ct_ref[...], zr.astype(BF16), preferred_element_type=F32)
         + jnp.dot(st_ref[...], zi.astype(BF16), preferred_element_type=F32))
    o_ref[...] = _group_rmsnorm(y, g_ref[...], gw).astype(BF16)


def _fourier_mix(p, g_fourier, batch, seq):
    f_width = p.shape[1]
    gw = f_width // F_GROUPS
    cc, sc = _dft_mats(gw, gw ** -0.5)
    cc, msc = cc.astype(BF16), (-sc).astype(BF16)
    g2 = g_fourier.reshape(1, f_width)
    rows = batch * seq
    if seq <= 512:
        ct, st = _dft_mats(seq, seq ** -0.5)
        return pl.pallas_call(
            functools.partial(_fnet_direct_kernel, gw=gw),
            grid=(batch,),
            in_specs=[
                pl.BlockSpec((seq, f_width), lambda b: (b, 0)),
                pl.BlockSpec((gw, gw), lambda b: (0, 0)),
                pl.BlockSpec((gw, gw), lambda b: (0, 0)),
                pl.BlockSpec((seq, seq), lambda b: (0, 0)),
                pl.BlockSpec((seq, seq), lambda b: (0, 0)),
                pl.BlockSpec((1, f_width), lambda b: (0, 0)),
            ],
            out_specs=pl.BlockSpec((seq, f_width), lambda b: (b, 0)),
            out_shape=jax.ShapeDtypeStruct((rows, f_width), BF16),
            compiler_params=_cparams(("arbitrary",)),
            name="fourier_direct",
        )(p, cc, msc, ct.astype(BF16), st.astype(BF16), g2)

    n2 = DFT_N2
    n1 = seq // n2
    c2, s2 = _dft_mats(n2, n2 ** -0.5)
    fmat = jnp.concatenate([jnp.concatenate([c2, s2], axis=1),
                            jnp.concatenate([-s2, c2], axis=1)], axis=0).astype(BF16)
    i1 = jnp.arange(n1, dtype=jnp.int32)
    i2 = jnp.arange(n2, dtype=jnp.int32)
    tw = (2.0 * math.pi / seq) * (i1[:, None] * i2[None, :]).astype(F32)
    cw = jnp.broadcast_to(jnp.cos(tw)[:, :, None], (n1, n2, LANES))
    sw = jnp.broadcast_to(jnp.sin(tw)[:, :, None], (n1, n2, LANES))
    nr = SUBLANES
    yr, yi = pl.pallas_call(
        _fnet_a_kernel,
        grid=(batch, n1 // nr),
        in_specs=[
            pl.BlockSpec((1, n2, nr, f_width), lambda b, j: (b, 0, j, 0)),
            pl.BlockSpec((gw, gw), lambda b, j: (0, 0)),
            pl.BlockSpec((gw, gw), lambda b, j: (0, 0)),
            pl.BlockSpec((2 * n2, 2 * n2), lambda b, j: (0, 0)),
            pl.BlockSpec((nr, n2, LANES), lambda b, j: (j, 0, 0)),
            pl.BlockSpec((nr, n2, LANES), lambda b, j: (j, 0, 0)),
        ],
        out_specs=[
            pl.BlockSpec((1, n2, nr * f_width), lambda b, j: (b, 0, j)),
            pl.BlockSpec((1, n2, nr * f_width), lambda b, j: (b, 0, j)),
        ],
        out_shape=[jax.ShapeDtypeStruct((batch, n2, n1 * f_width), BF16)] * 2,
        compiler_params=_cparams(("arbitrary", "arbitrary")),
        name="fourier_stage_a",
    )(p.reshape(batch, n2, n1, f_width), cc, msc, fmat, cw, sw)

    kb = DFT_KB
    c1, s1 = _dft_mats(n1, n1 ** -0.5)
    eye = jnp.eye(kb, dtype=F32)
    gc = jnp.einsum("kn,ab->kabn", c1, eye).reshape(n1 * kb, kb * n1).astype(BF16)
    gs = jnp.einsum("kn,ab->kabn", s1, eye).reshape(n1 * kb, kb * n1).astype(BF16)
    out = pl.pallas_call(
        functools.partial(_fnet_b_kernel, gw=gw),
        grid=(batch, n2 // kb),
        in_specs=[
            pl.BlockSpec((1, kb * n1, f_width), lambda b, j: (b, j, 0)),
            pl.BlockSpec((1, kb * n1, f_width), lambda b, j: (b, j, 0)),
            pl.BlockSpec((n1 * kb, kb * n1), lambda b, j: (0, 0)),
            pl.BlockSpec((n1 * kb, kb * n1), lambda b, j: (0, 0)),
            pl.BlockSpec((1, f_width), lambda b, j: (0, 0)),
        ],
        out_specs=pl.BlockSpec((1, n1, kb, f_width), lambda b, j: (b, 0, j, 0)),
        out_shape=jax.ShapeDtypeStruct((batch, n1, n2, f_width), BF16),
        compiler_params=_cparams(("arbitrary", "arbitrary")),
        name="fourier_stage_b",
    )(yr.reshape(batch, n2 * n1, f_width), yi.reshape(batch, n2 * n1, f_width), gc, gs, g2)
    return out.reshape(rows, f_width)


def _outproj_kernel(f_ref, ym_ref, w_ref, x_ref, ga_ref, o_ref):
    fw = f_ref.shape[1]
    y = (jnp.dot(f_ref[...], w_ref[:fw, :], preferred_element_type=F32)
         + jnp.dot(ym_ref[...], w_ref[fw:, :], preferred_element_type=F32))
    o_ref[...] = x_ref[...] + ga_ref[0] * y


def _out_projection(f, ym, w_out, x2, mod3, mod_row):
    rows, d = x2.shape
    tm = min(512, rows)
    return pl.pallas_call(
        _outproj_kernel,
        grid=(rows // tm,),
        in_specs=[
            pl.BlockSpec((tm, f.shape[1]), lambda i: (i, 0)),
            pl.BlockSpec((tm, ym.shape[1]), lambda i: (i, 0)),
            pl.BlockSpec(w_out.shape, lambda i: (0, 0)),
            pl.BlockSpec((tm, d), lambda i: (i, 0)),
            pl.BlockSpec((1, 1, d), lambda i: (mod_row(i, tm), 0, 2)),
        ],
        out_specs=pl.BlockSpec((tm, d), lambda i: (i, 0)),
        out_shape=jax.ShapeDtypeStruct((rows, d), F32),
        compiler_params=_cparams(("arbitrary",)),
        name="out_projection",
    )(f, ym, w_out, x2, mod3)


def _extract_top(s, k, exact):
    n = s.shape[0]
    row = lax.broadcasted_iota(jnp.int32, s.shape, 0) if exact else None
    rank = jnp.full(s.shape, RANK_NONE, F32)
    vals = []
    for it in range(k):
        m = jnp.max(s, axis=0, keepdims=True)
        hit = s == m
        if exact:
            first = jnp.min(jnp.where(hit, row, n), axis=0, keepdims=True)
            hit = row == first
        rank = jnp.where(hit, float(it), rank)
        s = jnp.where(hit, NEG, s)
        vals.append(m)
    removed = jnp.sum(jnp.where(rank < RANK_NONE, 1.0, 0.0), axis=0, keepdims=True)
    return vals, rank, removed


def _route_chunk(s0, s1, exact):
    k = PEER_TOPK
    sub = SUBLANES
    v0, rank0, rem0 = _extract_top(s0, k, exact)
    v1, rank1, rem1 = _extract_top(s1, k, exact)
    sv0 = jnp.concatenate(v0, axis=0)
    sv1 = jnp.concatenate(v1, axis=0)
    if exact:
        layout = [("row", a, b0) for a in range(k) for b0 in range(0, max(k // (a + 1), 1), sub)]
    else:
        layout = ([("row", 0, 0), ("row", 0, sub), ("row", 1, 0), ("col", 0, sub)]
                  + [("row", a, 0) for a in range(2, k // 3)]
                  + [("col", b, 0) for b in range(k // sub)])
    seen = set()
    blocks = []
    for kind, fixed, start in layout:
        cells = [(fixed, start + r) if kind == "row" else (start + r, fixed) for r in range(sub)]
        keep = [(a + 1) * (b + 1) <= k and (a, b) not in seen for a, b in cells]
        seen.update(cell for cell, kp in zip(cells, keep) if kp)
        blk = (v0[fixed] + sv1[start:start + sub]) if kind == "row" else (sv0[start:start + sub] + v1[fixed])
        if not all(keep):
            brow = lax.broadcasted_iota(jnp.int32, blk.shape, 0)
            mask = functools.reduce(jnp.logical_or, [brow == r for r, kp in enumerate(keep) if kp])
            blk = jnp.where(mask, blk, NEG)
        blocks.append(blk)
    assert len(seen) == sum(k // (a + 1) for a in range(k))
    cand = jnp.concatenate(blocks, axis=0)
    _, crank, rem2 = _extract_top(cand, k, exact)
    cnt = jnp.where(crank < RANK_NONE, 1.0, 0.0)
    z = jnp.sum(cnt * jnp.exp(cand - (v0[0] + v1[0])), axis=0, keepdims=True)
    arow = lax.broadcasted_iota(jnp.int32, sv0.shape, 0)
    n_rank = jnp.zeros(sv0.shape, F32)
    for idx, (kind, fixed, start) in enumerate(layout):
        c_blk = cnt[idx * sub:(idx + 1) * sub]
        if kind == "row":
            n_rank = n_rank + jnp.where(arow == fixed, jnp.sum(c_blk, axis=0, keepdims=True), 0.0)
        else:
            pieces = [c_blk if a0 == start else jnp.zeros_like(c_blk) for a0 in range(0, k, sub)]
            n_rank = n_rank + jnp.concatenate(pieces, axis=0)
    n_of_key = jnp.zeros(rank0.shape, F32)
    for a in range(k):
        n_of_key = n_of_key + jnp.where(rank0 == float(a), n_rank[a:a + 1], 0.0)
    tie = (jnp.where(rem0 == k, 0.0, 1.0) + jnp.where(rem1 == k, 0.0, 1.0)
           + jnp.where(rem2 == k, 0.0, 1.0))
    return jnp.exp(s0 - v0[0]) / z, n_of_key, jnp.exp(s1 - v1[0]), rank1, tie


def _peer_route_kernel(x_ref, g_ref, sh_ref, sc_ref, wq_ref, keys_ref,
                       ht_ref, hs_ref, e0_ref, nn_ref, e1_ref, r1_ref, s_scr):
    h2 = _rms_modulate(x_ref[...], g_ref[...], sh_ref[0], sc_ref[0])
    ht = h2.T
    amax = jnp.maximum(jnp.max(jnp.abs(ht), axis=0, keepdims=True), TINY)
    ht_ref[...] = (ht * (F8_TARGET / amax)).astype(F8)
    hs_ref[...] = amax * (1.0 / F8_TARGET)
    q = jnp.dot(h2.astype(BF16), wq_ref[...], preferred_element_type=F32)
    tb = q.shape[0]
    kd = keys_ref.shape[-1]
    for hp in range(2 * PEER_HEADS):
        s_scr[hp] = lax.dot_general(keys_ref[hp // 2, hp % 2], q[:, hp * kd:(hp + 1) * kd],
                                    (((1,), (1,)), ((), ())),
                                    precision=lax.Precision.HIGHEST, preferred_element_type=F32)

    def route_head(h, _):
        for c in range(tb // LANES):
            cols = slice(c * LANES, (c + 1) * LANES)

            def route(exact):
                e0, n_of_key, e1, rank1, tie = _route_chunk(s_scr[2 * h, :, cols],
                                                            s_scr[2 * h + 1, :, cols], exact)
                e0_ref[h, :, cols] = e0
                nn_ref[h, :, cols] = n_of_key
                e1_ref[h, :, cols] = e1.astype(BF16)
                r1_ref[h, :, cols] = rank1.astype(BF16)
                return tie

            tie = route(False)

            @pl.when(jnp.max(tie) > 0.0)
            def _():
                route(True)
        return 0

    lax.fori_loop(0, PEER_HEADS, route_head, 0)


def _peer_route(x2, g, mod3, mod_row, wq, sub_keys):
    rows, d = x2.shape
    tb = min(256, rows)
    nk = sub_keys.shape[2]
    tab_spec = pl.BlockSpec((PEER_HEADS, nk, tb), lambda i: (0, 0, i))
    tab_f32 = jax.ShapeDtypeStruct((PEER_HEADS, nk, rows), F32)
    tab_bf16 = jax.ShapeDtypeStruct((PEER_HEADS, nk, rows), BF16)
    return pl.pallas_call(
        _peer_route_kernel,
        grid=(rows // tb,),
        in_specs=[
            pl.BlockSpec((tb, d), lambda i: (i, 0)),
            pl.BlockSpec((1, d), lambda i: (0, 0)),
            pl.BlockSpec((1, 1, d), lambda i: (mod_row(i, tb), 0, 3)),
            pl.BlockSpec((1, 1, d), lambda i: (mod_row(i, tb), 0, 4)),
            pl.BlockSpec(wq.shape, lambda i: (0, 0)),
            pl.BlockSpec(sub_keys.shape, lambda i: (0, 0, 0, 0)),
        ],
        out_specs=[pl.BlockSpec((d, tb), lambda i: (0, i)), pl.BlockSpec((1, tb), lambda i: (0, i)),
                   tab_spec, tab_spec, tab_spec, tab_spec],
        out_shape=[jax.ShapeDtypeStruct((d, rows), F8), jax.ShapeDtypeStruct((1, rows), F32),
                   tab_f32, tab_f32, tab_bf16, tab_bf16],
        scratch_shapes=[pltpu.VMEM((2 * PEER_HEADS, nk, tb), F32)],
        compiler_params=_cparams(("arbitrary",)),
        name="peer_route",
    )(x2, g.reshape(1, d), mod3, mod3, wq, sub_keys)


def _gelu_tanh(x):
    return 0.5 * x * (1.0 + jnp.tanh(math.sqrt(2.0 / math.pi) * (x + 0.044715 * (x * x * x))))


def _peer_expert_kernel(ht_ref, hs_ref, u_ref, v_ref, ws_ref, e0_ref, nn_ref, e1_ref, r1_ref,
                        x_ref, ga_ref, gf_ref, o_ref, wa_scr, act_scr, *, final_norm):
    e = pl.program_id(1)
    nk = e1_ref.shape[1]
    tb = ht_ref.shape[1]
    ib = u_ref.shape[0] // nk
    pk = 2 * SUBLANES
    assert ib == SUBLANES
    i_rows = pl.ds(pl.multiple_of(e * ib, ib), ib)

    @pl.when(e == 0)
    def _():
        o_ref[...] = jnp.zeros(o_ref.shape, F32)

    act = (jnp.dot(u_ref[...], ht_ref[...], preferred_element_type=F32)
           * (hs_ref[...] * ws_ref[0:1, 0:1]))
    act_scr[...] = act
    amax = jnp.max(jnp.max(jnp.abs(act), axis=0, keepdims=True), axis=1, keepdims=True)
    bound = PEER_HEADS * jnp.maximum(amax, TINY)
    to_f8 = F8_TARGET / bound
    for il in range(ib):
        for c in range(tb // LANES):
            cols = slice(c * LANES, (c + 1) * LANES)
            w = [None] * (nk // pk)
            for h in range(PEER_HEADS):
                n_i = jnp.broadcast_to(nn_ref[h, i_rows, cols][il:il + 1], (pk, LANES)).astype(BF16)
                e0_i = jnp.broadcast_to(e0_ref[h, i_rows, cols][il:il + 1], (pk, LANES)).astype(BF16)
                for s in range(nk // pk):
                    rows = slice(s * pk, (s + 1) * pk)
                    e1 = e1_ref[h, rows, cols]
                    t = jnp.where(r1_ref[h, rows, cols] < n_i, e1, jnp.zeros_like(e1)) * e0_i
                    w[s] = t if w[s] is None else w[s] + t
            rows = slice(il * nk, (il + 1) * nk)
            wa = jnp.concatenate(w, axis=0).astype(F32) * _gelu_tanh(act_scr[rows, cols])
            wa_scr[rows, cols] = (wa * to_f8).astype(F8)
    o_ref[...] += (pl.dot(wa_scr[...], v_ref[...], trans_a=True)
                   * (bound * (ws_ref[0:1, 1:2] * (1.0 / F8_TARGET))))

    @pl.when(e == pl.num_programs(1) - 1)
    def _():
        y = x_ref[...] + ga_ref[0] * o_ref[...]
        if final_norm:
            ms = jnp.mean(y * y, axis=-1, keepdims=True)
            y = y * lax.rsqrt(ms + EPS) * gf_ref[...]
        o_ref[...] = y


def _fp8_tensor(w):
    amax = jnp.maximum(jnp.max(jnp.abs(w)), TINY)
    return (w * (F8_TARGET / amax)).astype(F8), amax * (1.0 / F8_TARGET)


def _peer_experts(ht, hs, u8, v8, w_scales, tables, x2, mod3, mod_row, g_final, final_norm):
    rows, d = x2.shape
    n_exp = u8.shape[0]
    nk = tables[0].shape[1]
    tb = min(512, rows)
    eb = SUBLANES * nk
    tab_spec = pl.BlockSpec((PEER_HEADS, nk, tb), lambda i, e: (0, 0, i))
    return pl.pallas_call(
        functools.partial(_peer_expert_kernel, final_norm=final_norm),
        grid=(rows // tb, n_exp // eb),
        in_specs=[
            pl.BlockSpec((d, tb), lambda i, e: (0, i)),
            pl.BlockSpec((1, tb), lambda i, e: (0, i)),
            pl.BlockSpec((eb, d), lambda i, e: (e, 0)),
            pl.BlockSpec((eb, d), lambda i, e: (e, 0)),
            pl.BlockSpec((1, LANES), lambda i, e: (0, 0)),
            tab_spec, tab_spec, tab_spec, tab_spec,
            pl.BlockSpec((tb, d), lambda i, e: (i, 0)),
            pl.BlockSpec((1, 1, d), lambda i, e: (mod_row(i, tb), 0, 5)),
            pl.BlockSpec((1, d), lambda i, e: (0, 0)),
        ],
        out_specs=pl.BlockSpec((tb, d), lambda i, e: (i, 0)),
        out_shape=jax.ShapeDtypeStruct((rows, d), F32),
        scratch_shapes=[pltpu.VMEM((eb, tb), F8), pltpu.VMEM((eb, tb), F32)],
        compiler_params=_cparams(("arbitrary", "arbitrary")),
        name="peer_experts",
    )(ht, hs, u8, v8, w_scales, *tables, x2, mod3, g_final.reshape(1, d))


def kernel(x, c, ctx, c_ctx, w_mod, b_mod, g_norm_mix, g_norm_ffn, w_in, b_gate, conv_qk, g_fourier,
           g_mlstm, w_out, w_query, sub_keys, expert_u, expert_v, g_final):
    batch, seq, d = x.shape
    ctx_len = ctx.shape[1]
    depth = w_mod.shape[0]
    f_width = g_fourier.shape[1]
    m_width = g_mlstm.shape[1]
    qk_width = conv_qk.shape[-1]
    n_gates = b_gate.shape[1]
    n_main = f_width + qk_width + 2 * m_width
    assert w_in.shape[2] == n_main + n_gates and n_gates <= LANES
    assert f_width == qk_width == m_width and m_width == M_HEADS * LANES
    assert seq % MLSTM_CHUNK == 0 and ctx_len % MLSTM_CHUNK == 0 and seq % GRID_W == 0

    cond_rows = -(-(batch + 1) // SUBLANES) * SUBLANES
    cond = jnp.zeros((cond_rows, d), F32).at[:batch].set(c).at[batch].set(c_ctx)
    mod_all = _modulation(cond, w_mod, b_mod)

    latent_row = lambda i, tm: (i * tm) // seq
    context_row = lambda i, tm: batch

    x2 = x.reshape(batch * seq, d)
    c2 = ctx.reshape(batch * ctx_len, d)
    v_col0 = qk_width // LANES
    o_col0 = (qk_width + m_width) // LANES
    zero_state = jnp.zeros((batch, M_HEADS, 2, LANES, 2 * LANES), F32)
    zero_m = jnp.zeros((batch, M_HEADS, 2, SUBLANES, LANES), F32)

    for l in range(depth):
        last = l == depth - 1
        mod3 = mod_all[l].reshape(cond_rows, 1, N_MOD * d)
        w_main = w_in[l, :, :n_main].astype(BF16)
        w_gate = jnp.zeros((d, LANES), F32).at[:, :n_gates].set(w_in[l, :, n_main:]).astype(BF16)
        bg = jnp.zeros((1, LANES), F32).at[0, :n_gates].set(b_gate[l])
        w_out_bf = w_out[l].astype(BF16)
        wq_bf = w_query[l].astype(BF16)
        u8, u_inv = _fp8_tensor(expert_u[l])
        v8, v_inv = _fp8_tensor(expert_v[l])
        w_scales = jnp.zeros((1, LANES), F32).at[0, 0].set(u_inv).at[0, 1].set(v_inv)

        def mixer(tokens, mod_row, n_tok, vertical, state, m_state):
            four, p, gates = _in_projection(tokens, g_norm_mix[l], mod3, mod_row, w_main, w_gate, bg)
            qk = _qk_conv(p, conv_qk[l], qk_width, 0, n_tok, vertical)
            ym, state, m_state = _mlstm(qk, p, gates, g_mlstm[l], state, m_state, batch, n_tok,
                                        v_col0, o_col0)
            return four, ym, state, m_state

        def ffn_and_residuals(tokens, four, ym, mod_row, n_tok, final_norm):
            f = _fourier_mix(four, g_fourier[l], batch, n_tok)
            tokens = _out_projection(f, ym, w_out_bf, tokens, mod3, mod_row)
            routed = _peer_route(tokens, g_norm_ffn[l], mod3, mod_row, wq_bf, sub_keys[l])
            return _peer_experts(routed[0], routed[1], u8, v8, w_scales, routed[2:], tokens, mod3, mod_row,
                                 g_final, final_norm)

        pc, ymc, st, m_st = mixer(c2, context_row, ctx_len, False, zero_state, zero_m)
        p, ym, _, _ = mixer(x2, latent_row, seq, True, st, m_st)
        x2 = ffn_and_residuals(x2, p, ym, latent_row, seq, last)
        if not last:
            c2 = ffn_and_residuals(c2, pc, ymc, context_row, ctx_len, False)
    return x2.reshape(batch, seq, d)
```

```python
import functools
import math

import jax
import jax.numpy as jnp
from jax import lax
from jax.experimental import pallas as pl
from jax.experimental.pallas import tpu as pltpu

F32 = jnp.float32
BF16 = jnp.bfloat16

F_GROUPS = 4
M_HEADS = 8
GRID_W = 64
N_KEYS = 128
PEER_HEADS = 8
PEER_TOPK = 16
N_MOD = 6
EPS = 1e-6

LANES = 128
SUBLANES = 8
VMEM_LIMIT = 56 * 1024 * 1024

MLSTM_CHUNK = 128
DFT_N2 = 128
DFT_KB = 8
NEG = -3.0e38
RANK_NONE = 99.0
F8 = jnp.float8_e4m3fn
F8_TARGET = 224.0
TINY = 1e-30


def _cparams(sem):
    return pltpu.CompilerParams(dimension_semantics=sem, vmem_limit_bytes=VMEM_LIMIT)


def _sigmoid(x):
    return 1.0 / (1.0 + jnp.exp(-x))


def _split3(x):
    p0 = x.astype(BF16)
    r1 = x - p0.astype(F32)
    p1 = r1.astype(BF16)
    p2 = (r1 - p1.astype(F32)).astype(BF16)
    return p0, p1, p2


def _dot01_left(a01, x):
    return sum(jnp.dot(a01, p, preferred_element_type=F32) for p in _split3(x))


def _dot01_right(x, b01):
    return sum(jnp.dot(p, b01, preferred_element_type=F32) for p in _split3(x))


def _mod_kernel(c_ref, w_ref, b_ref, o_ref):
    c = c_ref[...]
    s = (c * _sigmoid(c)).astype(BF16)
    o_ref[0] = jnp.dot(s, w_ref[0].astype(BF16), preferred_element_type=F32) + b_ref[0]


def _modulation(cond, w_mod, b_mod):
    n_layers, d, n_out = w_mod.shape
    rows = cond.shape[0]
    tn = 1024
    return pl.pallas_call(
        _mod_kernel,
        grid=(n_layers, n_out // tn),
        in_specs=[
            pl.BlockSpec((rows, d), lambda l, j: (0, 0)),
            pl.BlockSpec((1, d, tn), lambda l, j: (l, 0, j)),
            pl.BlockSpec((1, 1, tn), lambda l, j: (l, 0, j)),
        ],
        out_specs=pl.BlockSpec((1, rows, tn), lambda l, j: (l, 0, j)),
        out_shape=jax.ShapeDtypeStruct((n_layers, rows, n_out), F32),
        compiler_params=_cparams(("arbitrary", "arbitrary")),
        name="modulation",
    )(cond, w_mod, b_mod.reshape(n_layers, 1, n_out))


def _rms_modulate(x, g, shift, scale):
    ms = jnp.mean(x * x, axis=-1, keepdims=True)
    return (x * lax.rsqrt(ms + EPS) * g) * (1.0 + scale) + shift


def _inproj_kernel(x_ref, g_ref, sh_ref, sc_ref, w_ref, wg_ref, bg_ref, four_ref, p_ref, gate_ref, h_scr):
    j = pl.program_id(1)

    @pl.when(j == 0)
    def _():
        h = _rms_modulate(x_ref[...], g_ref[...], sh_ref[0], sc_ref[0]).astype(BF16)
        h_scr[...] = h
        gate_ref[...] = jnp.dot(h, wg_ref[...], preferred_element_type=F32) + bg_ref[...]

    acc = jnp.dot(h_scr[...], w_ref[...], preferred_element_type=F32)

    @pl.when(j == 0)
    def _():
        four_ref[...] = acc

    @pl.when(j > 0)
    def _():
        p_ref[...] = acc.astype(BF16)


def _in_projection(x2, g, mod3, mod_row, w_main, w_gate, b_gate):
    rows, d = x2.shape
    n_main = w_main.shape[1]
    tm = min(1024, rows)
    tn = 1024
    return pl.pallas_call(
        _inproj_kernel,
        grid=(rows // tm, n_main // tn),
        in_specs=[
            pl.BlockSpec((tm, d), lambda i, j: (i, 0)),
            pl.BlockSpec((1, d), lambda i, j: (0, 0)),
            pl.BlockSpec((1, 1, d), lambda i, j: (mod_row(i, tm), 0, 0)),
            pl.BlockSpec((1, 1, d), lambda i, j: (mod_row(i, tm), 0, 1)),
            pl.BlockSpec((d, tn), lambda i, j: (0, j)),
            pl.BlockSpec((d, LANES), lambda i, j: (0, 0)),
            pl.BlockSpec((1, LANES), lambda i, j: (0, 0)),
        ],
        out_specs=[
            pl.BlockSpec((tm, tn), lambda i, j: (i, 0)),
            pl.BlockSpec((tm, tn), lambda i, j: (i, jnp.maximum(j - 1, 0))),
            pl.BlockSpec((tm, LANES), lambda i, j: (i, 0)),
        ],
        out_shape=[
            jax.ShapeDtypeStruct((rows, tn), F32),
            jax.ShapeDtypeStruct((rows, n_main - tn), BF16),
            jax.ShapeDtypeStruct((rows, LANES), F32),
        ],
        scratch_shapes=[pltpu.VMEM((tm, d), BF16)],
        compiler_params=_cparams(("arbitrary", "arbitrary")),
        name="in_projection",
    )(x2, g.reshape(1, d), mod3, mod3, w_main, w_gate, b_gate)


def _conv_kernel(*refs, tb, width, vertical, blocks_per_image):
    if vertical:
        cur_ref, top_ref, bot_ref, w_ref, o_ref = refs
    else:
        cur_ref, w_ref, o_ref = refs
    cur = cur_ref[...].astype(F32)
    ch = cur.shape[1]
    wpos = lax.rem(lax.broadcasted_iota(jnp.int32, (tb, ch), 0), width)
    first_col = wpos == 0
    last_col = wpos == width - 1
    if vertical:
        r = lax.rem(pl.program_id(0), blocks_per_image)
        top = jnp.where(r == 0, 0.0, top_ref[...].astype(F32))
        bot = jnp.where(r == blocks_per_image - 1, 0.0, bot_ref[...].astype(F32))
        ext = jnp.concatenate([top, cur, bot], axis=0)
        bases = [(dr, ext[dr * width:dr * width + tb]) for dr in range(3)]
    else:
        bases = [(1, cur)]
    acc = jnp.zeros((tb, ch), F32)
    for dr, base in bases:
        left = jnp.where(first_col, 0.0, pltpu.roll(base, 1, axis=0))
        right = jnp.where(last_col, 0.0, pltpu.roll(base, tb - 1, axis=0))
        for dw, shifted in enumerate((left, base, right)):
            k = dr * 3 + dw
            acc = acc + shifted * w_ref[k:k + 1, :]
    o_ref[...] = (acc * _sigmoid(acc)).astype(BF16)


def _qk_conv(p, conv_w, qk_width, col_block, tokens_per_image, vertical):
    rows = p.shape[0]
    w9 = conv_w.reshape(9, qk_width)
    if vertical:
        width = GRID_W
        tb = min(512, tokens_per_image)
        bpi = tokens_per_image // tb
        halo = tb // width
        n_halo = rows // width
        in_specs = [
            pl.BlockSpec((tb, qk_width), lambda i: (i, col_block)),
            pl.BlockSpec((width, qk_width), lambda i: (jnp.maximum(i * halo - 1, 0), col_block)),
            pl.BlockSpec((width, qk_width), lambda i: (jnp.minimum((i + 1) * halo, n_halo - 1), col_block)),
            pl.BlockSpec((9, qk_width), lambda i: (0, 0)),
        ]
        args = (p, p, p, w9)
    else:
        width = tb = tokens_per_image
        bpi = 1
        in_specs = [
            pl.BlockSpec((tb, qk_width), lambda i: (i, col_block)),
            pl.BlockSpec((9, qk_width), lambda i: (0, 0)),
        ]
        args = (p, w9)
    return pl.pallas_call(
        functools.partial(_conv_kernel, tb=tb, width=width, vertical=vertical, blocks_per_image=bpi),
        grid=(rows // tb,),
        in_specs=in_specs,
        out_specs=pl.BlockSpec((tb, qk_width), lambda i: (i, 0)),
        out_shape=jax.ShapeDtypeStruct((rows, qk_width), BF16),
        compiler_params=_cparams(("arbitrary",)),
        name="qk_conv_latent" if vertical else "qk_conv_context",
    )(*args)


def _mlstm_chunks(chains):
    n = range(len(chains))
    ch = chains
    L = ch[0]["q"].shape[0]
    logsig = [jnp.minimum(c["gch"], 0.0) - jnp.log(1.0 + jnp.exp(-jnp.abs(c["gch"]))) for c in ch]
    ig_p = [_split3(c["gch"]) for c in ch]
    lf_p = [_split3(logsig[i]) for i in n]
    ig = [sum(jnp.dot(p, ch[i]["sel_i"], preferred_element_type=F32) for p in ig_p[i]) for i in n]
    lf = [sum(jnp.dot(p, ch[i]["sel_f"], preferred_element_type=F32) for p in lf_p[i]) for i in n]
    b_p = [_split3(lf[i]) for i in n]
    b = [sum(jnp.dot(ch[i]["cum"], p, preferred_element_type=F32) for p in b_p[i]) for i in n]
    b_end = [b[i][0:1, :] if ch[i]["reverse"] else b[i][L - 1:L, :] for i in n]
    a_t = [(ig[i] - b[i]).T for i in n]
    dmat = [jnp.where(ch[i]["causal"], b[i] + a_t[i], NEG) for i in n]
    g = [b[i] + ch[i]["m"] for i in n]
    m_j = [jnp.maximum(g[i], jnp.max(dmat[i], axis=-1, keepdims=True)) for i in n]
    pmat = [jnp.exp(dmat[i] - m_j[i]) for i in n]
    s_raw = [lax.dot_general(c["q"], c["k"], (((1,), (1,)), ((), ())), preferred_element_type=F32)
             for c in ch]
    s = [(s_raw[i] * pmat[i]).astype(BF16) for i in n]
    intra = [jnp.dot(s[i], ch[i]["vaug"], preferred_element_type=F32) for i in n]
    carried = [jnp.dot(c["q"], c["state"].astype(BF16), preferred_element_type=F32) for c in ch]
    inter = [jnp.exp(g[i] - m_j[i]) for i in n]
    tot = [intra[i] + jnp.concatenate([inter[i], inter[i]], axis=1) * carried[i] for i in n]
    h = [tot[i][:, :LANES] / jnp.maximum(jnp.abs(tot[i][:, LANES:]), jnp.exp(-m_j[i])) for i in n]
    a = [b_end[i] - b[i] + ig[i] for i in n]
    m_new = [jnp.maximum(b_end[i] + ch[i]["m"], jnp.max(a[i], axis=0, keepdims=True)) for i in n]
    kw = [(ch[i]["k"].astype(F32) * jnp.exp(a[i] - m_new[i])).astype(BF16) for i in n]
    f_old = [jnp.exp(b_end[i] + ch[i]["m"] - m_new[i]) for i in n]
    upd = [pl.dot(kw[i], ch[i]["vaug"], trans_a=True) for i in n]
    state_new = [jnp.concatenate([f_old[i], f_old[i]], axis=1) * ch[i]["state"] + upd[i] for i in n]
    return [(h[i], state_new[i], m_new[i]) for i in n]


def _mlstm_kernel(q_ref, k_ref, v_ref, o_ref, gt_ref, g_ref, sin_ref, min_ref,
                  y_ref, sout_ref, mout_ref, hf_scr, hb_scr, *, seq, k_scale):
    L = MLSTM_CHUNK
    nc = seq // L
    pair = pl.program_id(1)
    lane = lax.broadcasted_iota(jnp.int32, (1, LANES), 1)
    rr = lax.broadcasted_iota(jnp.int32, (L, L), 0)
    cc = lax.broadcasted_iota(jnp.int32, (L, L), 1)
    causal = (rr >= cc, rr <= cc)
    cum = tuple(jnp.where(m, 1.0, 0.0).astype(BF16) for m in causal)
    ones = jnp.ones((L, LANES), BF16)
    qmask, kmask, sel = [], [], []
    for j in range(2):
        own = (lane // (LANES // 2)) == j
        qmask.append(jnp.where(own, 1.0, 0.0).astype(BF16))
        kmask.append(jnp.where(own, k_scale, 0.0).astype(BF16))
        head = 2 * pair + j
        sel.append([jnp.where(rr == kind * M_HEADS + head, 1.0, 0.0).astype(BF16) for kind in range(4)])

    def body(c, carry):
        chains, dest = [], []
        for d in range(2):
            rows = pl.ds(pl.multiple_of((c if d == 0 else nc - 1 - c) * L, L), L)
            q_all, k_all, gch = q_ref[rows, :], k_ref[rows, :], gt_ref[rows, :]
            for j in range(2):
                hcols = slice(j * LANES, (j + 1) * LANES)
                idx = 2 * (2 * j + d)
                chains.append(dict(
                    q=q_all * qmask[j], k=k_all * kmask[j],
                    vaug=jnp.concatenate([v_ref[rows, hcols], ones], axis=1), gch=gch,
                    sel_i=sel[j][2 * d], sel_f=sel[j][2 * d + 1], cum=cum[d], causal=causal[d],
                    reverse=d == 1, state=carry[idx], m=carry[idx + 1]))
                dest.append((hf_scr if d == 0 else hb_scr, rows, hcols, idx))
        carry = list(carry)
        for (scr, rows, hcols, idx), (h, state, m) in zip(dest, _mlstm_chunks(chains)):
            scr[rows, hcols] = h.astype(BF16)
            carry[idx], carry[idx + 1] = state, m
        return tuple(carry)

    init = []
    for j in range(2):
        for d in range(2):
            init += [sin_ref[0, j, d], min_ref[0, j, d][0:1, :]]
    final = lax.fori_loop(0, nc, body, tuple(init))
    for j in range(2):
        for d in range(2):
            idx = 2 * (2 * j + d)
            sout_ref[0, j, d] = final[idx]
            mout_ref[0, j, d] = jnp.broadcast_to(final[idx + 1], (SUBLANES, LANES))

    def finish(c, _):
        rows = pl.ds(pl.multiple_of(c * L, L), L)
        for j in range(2):
            hcols = slice(j * LANES, (j + 1) * LANES)
            h = hf_scr[rows, hcols].astype(F32) + hb_scr[rows, hcols].astype(F32)
            ms = jnp.mean(h * h, axis=-1, keepdims=True)
            y = h * lax.rsqrt(ms + EPS) * g_ref[:, hcols]
            y_ref[rows, hcols] = (y * _sigmoid(o_ref[rows, hcols].astype(F32))).astype(BF16)
        return 0

    lax.fori_loop(0, nc, finish, 0)


def _mlstm(qk, p, gates, g_mlstm, state_in, m_in, batch, seq, v_col0, o_col0):
    rows = qk.shape[0]
    pw = 2 * LANES
    k_col0 = qk.shape[1] // (2 * LANES)
    dk = qk.shape[1] // (2 * M_HEADS)
    assert v_col0 % 2 == 0 and o_col0 % 2 == 0
    return pl.pallas_call(
        functools.partial(_mlstm_kernel, seq=seq, k_scale=dk ** -0.5),
        grid=(batch, M_HEADS // 2),
        in_specs=[
            pl.BlockSpec((seq, LANES), lambda b, h: (b, h)),
            pl.BlockSpec((seq, LANES), lambda b, h: (b, k_col0 + h)),
            pl.BlockSpec((seq, pw), lambda b, h: (b, v_col0 // 2 + h)),
            pl.BlockSpec((seq, pw), lambda b, h: (b, o_col0 // 2 + h)),
            pl.BlockSpec((seq, LANES), lambda b, h: (b, 0)),
            pl.BlockSpec((1, pw), lambda b, h: (0, h)),
            pl.BlockSpec((1, 2, 2, LANES, 2 * LANES), lambda b, h: (b, h, 0, 0, 0)),
            pl.BlockSpec((1, 2, 2, SUBLANES, LANES), lambda b, h: (b, h, 0, 0, 0)),
        ],
        out_specs=[
            pl.BlockSpec((seq, pw), lambda b, h: (b, h)),
            pl.BlockSpec((1, 2, 2, LANES, 2 * LANES), lambda b, h: (b, h, 0, 0, 0)),
            pl.BlockSpec((1, 2, 2, SUBLANES, LANES), lambda b, h: (b, h, 0, 0, 0)),
        ],
        out_shape=[
            jax.ShapeDtypeStruct((rows, M_HEADS * LANES), BF16),
            jax.ShapeDtypeStruct(state_in.shape, F32),
            jax.ShapeDtypeStruct(m_in.shape, F32),
        ],
        scratch_shapes=[pltpu.VMEM((seq, pw), BF16), pltpu.VMEM((seq, pw), BF16)],
        compiler_params=_cparams(("arbitrary", "arbitrary")),
        name="mlstm",
    )(qk, qk, p, p, gates, g_mlstm.reshape(1, -1), state_in, m_in)


def _dft_mats(n, scale):
    idx = jnp.arange(n, dtype=jnp.int32)
    ang = (2.0 * math.pi / n) * ((idx[:, None] * idx[None, :]) % n).astype(F32)
    return jnp.cos(ang) * scale, jnp.sin(ang) * scale


def _channel_dft(u, cc_ref, sc_ref):
    gw = cc_ref.shape[0]
    zr, zi = [], []
    for g in range(u.shape[1] // gw):
        ug = u[:, g * gw:(g + 1) * gw]
        zr.append(jnp.dot(ug, cc_ref[...], preferred_element_type=F32))
        zi.append(jnp.dot(ug, sc_ref[...], preferred_element_type=F32))
    return jnp.concatenate(zr, axis=1), jnp.concatenate(zi, axis=1)


def _group_rmsnorm(y, g, gw):
    outs = []
    for k in range(y.shape[1] // gw):
        yk = y[:, k * gw:(k + 1) * gw]
        ms = jnp.mean(yk * yk, axis=-1, keepdims=True)
        outs.append(yk * lax.rsqrt(ms + EPS) * g[:, k * gw:(k + 1) * gw])
    return jnp.concatenate(outs, axis=1)


def _fnet_a_kernel(u_ref, cc_ref, sc_ref, f_ref, cw_ref, sw_ref, yr_ref, yi_ref):
    fw = u_ref.shape[3]
    for r in range(u_ref.shape[2]):
        zr, zi = _channel_dft(u_ref[0, :, r, :].astype(BF16), cc_ref, sc_ref)
        z = jnp.concatenate([zr, zi], axis=0).astype(BF16)
        y = jnp.dot(f_ref[...], z, preferred_element_type=F32)
        n2 = y.shape[0] // 2
        yr, yi = y[:n2], y[n2:]
        cw, sw = cw_ref[r], sw_ref[r]
        for j in range(fw // LANES):
            cols = slice(j * LANES, (j + 1) * LANES)
            ocols = slice(r * fw + j * LANES, r * fw + (j + 1) * LANES)
            yr_ref[0, :, ocols] = (yr[:, cols] * cw + yi[:, cols] * sw).astype(BF16)
            yi_ref[0, :, ocols] = (yi[:, cols] * cw - yr[:, cols] * sw).astype(BF16)


def _fnet_b_kernel(yr_ref, yi_ref, gc_ref, gs_ref, g_ref, o_ref, *, gw):
    y = (jnp.dot(gc_ref[...], yr_ref[0], preferred_element_type=F32)
         + jnp.dot(gs_ref[...], yi_ref[0], preferred_element_type=F32))
    out = _group_rmsnorm(y, g_ref[...], gw).astype(BF16)
    o_ref[0] = out.reshape(o_ref.shape[1:])


def _fnet_direct_kernel(u_ref, cc_ref, sc_ref, ct_ref, st_ref, g_ref, o_ref, *, gw):
    zr, zi = _channel_dft(u_ref[...].astype(BF16), cc_ref, sc_ref)
    y = (jnp.dot(ct_ref[...], zr.astype(BF16), preferred_element_type=F32)
         + jnp.dot(st_ref[...], zi.astype(BF16), preferred_element_type=F32))
    o_ref[...] = _group_rmsnorm(y, g_ref[...], gw).astype(BF16)


def _fourier_mix(p, g_fourier, batch, seq):
    f_width = p.shape[1]
    gw = f_width // F_GROUPS
    cc, sc = _dft_mats(gw, gw ** -0.5)
    cc, msc = cc.astype(BF16), (-sc).astype(BF16)
    g2 = g_fourier.reshape(1, f_width)
    rows = batch * seq
    if seq <= 512:
        ct, st = _dft_mats(seq, seq ** -0.5)
        return pl.pallas_call(
            functools.partial(_fnet_direct_kernel, gw=gw),
            grid=(batch,),
            in_specs=[
                pl.BlockSpec((seq, f_width), lambda b: (b, 0)),
                pl.BlockSpec((gw, gw), lambda b: (0, 0)),
                pl.BlockSpec((gw, gw), lambda b: (0, 0)),
                pl.BlockSpec((seq, seq), lambda b: (0, 0)),
                pl.BlockSpec((seq, seq), lambda b: (0, 0)),
                pl.BlockSpec((1, f_width), lambda b: (0, 0)),
            ],
            out_specs=pl.BlockSpec((seq, f_width), lambda b: (b, 0)),
            out_shape=jax.ShapeDtypeStruct((rows, f_width), BF16),
            compiler_params=_cparams(("arbitrary",)),
            name="fourier_direct",
        )(p, cc, msc, ct.astype(BF16), st.astype(BF16), g2)

    n2 = DFT_N2
    n1 = seq // n2
    c2, s2 = _dft_mats(n2, n2 ** -0.5)
    fmat = jnp.concatenate([jnp.concatenate([c2, s2], axis=1),
                            jnp.concatenate([-s2, c2], axis=1)], axis=0).astype(BF16)
    i1 = jnp.arange(n1, dtype=jnp.int32)
    i2 = jnp.arange(n2, dtype=jnp.int32)
    tw = (2.0 * math.pi / seq) * (i1[:, None] * i2[None, :]).astype(F32)
    cw = jnp.broadcast_to(jnp.cos(tw)[:, :, None], (n1, n2, LANES))
    sw = jnp.broadcast_to(jnp.sin(tw)[:, :, None], (n1, n2, LANES))
    nr = SUBLANES
    yr, yi = pl.pallas_call(
        _fnet_a_kernel,
        grid=(batch, n1 // nr),
        in_specs=[
            pl.BlockSpec((1, n2, nr, f_width), lambda b, j: (b, 0, j, 0)),
            pl.BlockSpec((gw, gw), lambda b, j: (0, 0)),
            pl.BlockSpec((gw, gw), lambda b, j: (0, 0)),
            pl.BlockSpec((2 * n2, 2 * n2), lambda b, j: (0, 0)),
            pl.BlockSpec((nr, n2, LANES), lambda b, j: (j, 0, 0)),
            pl.BlockSpec((nr, n2, LANES), lambda b, j: (j, 0, 0)),
        ],
        out_specs=[
            pl.BlockSpec((1, n2, nr * f_width), lambda b, j: (b, 0, j)),
            pl.BlockSpec((1, n2, nr * f_width), lambda b, j: (b, 0, j)),
        ],
        out_shape=[jax.ShapeDtypeStruct((batch, n2, n1 * f_width), BF16)] * 2,
        compiler_params=_cparams(("arbitrary", "arbitrary")),
        name="fourier_stage_a",
    )(p.reshape(batch, n2, n1, f_width), cc, msc, fmat, cw, sw)

    kb = DFT_KB
    c1, s1 = _dft_mats(n1, n1 ** -0.5)
    eye = jnp.eye(kb, dtype=F32)
    gc = jnp.einsum("kn,ab->kabn", c1, eye).reshape(n1 * kb, kb * n1).astype(BF16)
    gs = jnp.einsum("kn,ab->kabn", s1, eye).reshape(n1 * kb, kb * n1).astype(BF16)
    out = pl.pallas_call(
        functools.partial(_fnet_b_kernel, gw=gw),
        grid=(batch, n2 // kb),
        in_specs=[
            pl.BlockSpec((1, kb * n1, f_width), lambda b, j: (b, j, 0)),
            pl.BlockSpec((1, kb * n1, f_width), lambda b, j: (b, j, 0)),
            pl.BlockSpec((n1 * kb, kb * n1), lambda b, j: (0, 0)),
            pl.BlockSpec((n1 * kb, kb * n1), lambda b, j: (0, 0)),
            pl.BlockSpec((1, f_width), lambda b, j: (0, 0)),
        ],
        out_specs=pl.BlockSpec((1, n1, kb, f_width), lambda b, j: (b, 0, j, 0)),
        out_shape=jax.ShapeDtypeStruct((batch, n1, n2, f_width), BF16),
        compiler_params=_cparams(("arbitrary", "arbitrary")),
        name="fourier_stage_b",
    )(yr.reshape(batch, n2 * n1, f_width), yi.reshape(batch, n2 * n1, f_width), gc, gs, g2)
    return out.reshape(rows, f_width)


def _outproj_kernel(f_ref, ym_ref, w_ref, x_ref, ga_ref, o_ref):
    fw = f_ref.shape[1]
    y = (jnp.dot(f_ref[...], w_ref[:fw, :], preferred_element_type=F32)
         + jnp.dot(ym_ref[...], w_ref[fw:, :], preferred_element_type=F32))
    o_ref[...] = x_ref[...] + ga_ref[0] * y


def _out_projection(f, ym, w_out, x2, mod3, mod_row):
    rows, d = x2.shape
    tm = min(512, rows)
    return pl.pallas_call(
        _outproj_kernel,
        grid=(rows // tm,),
        in_specs=[
            pl.BlockSpec((tm, f.shape[1]), lambda i: (i, 0)),
            pl.BlockSpec((tm, ym.shape[1]), lambda i: (i, 0)),
            pl.BlockSpec(w_out.shape, lambda i: (0, 0)),
            pl.BlockSpec((tm, d), lambda i: (i, 0)),
            pl.BlockSpec((1, 1, d), lambda i: (mod_row(i, tm), 0, 2)),
        ],
        out_specs=pl.BlockSpec((tm, d), lambda i: (i, 0)),
        out_shape=jax.ShapeDtypeStruct((rows, d), F32),
        compiler_params=_cparams(("arbitrary",)),
        name="out_projection",
    )(f, ym, w_out, x2, mod3)


def _pack_bf16_pair(lo, hi):
    def bf16_bits(x):
        u = lax.bitcast_convert_type(x, jnp.uint32)
        return (u + (jnp.uint32(0x7FFF) + ((u >> 16) & jnp.uint32(1)))) >> 16
    return bf16_bits(lo) | (bf16_bits(hi) << 16)


def _interleaved_key(q, n):
    half = n // 2
    return jnp.where(q < half, 2 * q, 2 * (q - half) + 1)


def _extract_top(s, k, exact, interleaved=False):
    n = s.shape[0]
    row = None
    if exact:
        row = lax.broadcasted_iota(jnp.int32, s.shape, 0)
        if interleaved:
            row = _interleaved_key(row, n)
    rank = jnp.full(s.shape, RANK_NONE, F32)
    vals = []
    for it in range(k):
        m = jnp.max(s, axis=0, keepdims=True)
        hit = s == m
        if exact:
            first = jnp.min(jnp.where(hit, row, n), axis=0, keepdims=True)
            hit = row == first
        rank = jnp.where(hit, float(it), rank)
        s = jnp.where(hit, NEG, s)
        vals.append(m)
    removed = jnp.sum(jnp.where(rank < RANK_NONE, 1.0, 0.0), axis=0, keepdims=True)
    return vals, rank, removed


def _route_chunk(s0, s1, exact):
    k = PEER_TOPK
    sub = SUBLANES
    v0, rank0, rem0 = _extract_top(s0, k, exact)
    v1, rank1, rem1 = _extract_top(s1, k, exact, interleaved=True)
    sv0 = jnp.concatenate(v0, axis=0)
    sv1 = jnp.concatenate(v1, axis=0)
    if exact:
        layout = [("row", a, b0) for a in range(k) for b0 in range(0, max(k // (a + 1), 1), sub)]
    else:
        layout = ([("row", 0, 0), ("row", 0, sub), ("row", 1, 0), ("col", 0, sub)]
                  + [("row", a, 0) for a in range(2, k // 3)]
                  + [("col", b, 0) for b in range(k // sub)])
    seen = set()
    blocks = []
    for kind, fixed, start in layout:
        cells = [(fixed, start + r) if kind == "row" else (start + r, fixed) for r in range(sub)]
        keep = [(a + 1) * (b + 1) <= k and (a, b) not in seen for a, b in cells]
        seen.update(cell for cell, kp in zip(cells, keep) if kp)
        blk = (v0[fixed] + sv1[start:start + sub]) if kind == "row" else (sv0[start:start + sub] + v1[fixed])
        if not all(keep):
            brow = lax.broadcasted_iota(jnp.int32, blk.shape, 0)
            mask = functools.reduce(jnp.logical_or, [brow == r for r, kp in enumerate(keep) if kp])
            blk = jnp.where(mask, blk, NEG)
        blocks.append(blk)
    assert len(seen) == sum(k // (a + 1) for a in range(k))
    cand = jnp.concatenate(blocks, axis=0)
    _, crank, rem2 = _extract_top(cand, k, exact)
    cnt = jnp.where(crank < RANK_NONE, 1.0, 0.0)
    z = jnp.sum(cnt * jnp.exp(cand - (v0[0] + v1[0])), axis=0, keepdims=True)
    arow = lax.broadcasted_iota(jnp.int32, sv0.shape, 0)
    n_rank = jnp.zeros(sv0.shape, F32)
    for idx, (kind, fixed, start) in enumerate(layout):
        c_blk = cnt[idx * sub:(idx + 1) * sub]
        if kind == "row":
            n_rank = n_rank + jnp.where(arow == fixed, jnp.sum(c_blk, axis=0, keepdims=True), 0.0)
        else:
            pieces = [c_blk if a0 == start else jnp.zeros_like(c_blk) for a0 in range(0, k, sub)]
            n_rank = n_rank + jnp.concatenate(pieces, axis=0)
    n_of_key = jnp.zeros(rank0.shape, F32)
    for a in range(k):
        n_of_key = n_of_key + jnp.where(rank0 == float(a), n_rank[a:a + 1], 0.0)
    tie = (jnp.where(rem0 == k, 0.0, 1.0) + jnp.where(rem1 == k, 0.0, 1.0)
           + jnp.where(rem2 == k, 0.0, 1.0))
    return jnp.exp(s0 - v0[0]) / z, n_of_key, jnp.exp(s1 - v1[0]), rank1, tie


def _peer_route_kernel(x_ref, g_ref, sh_ref, sc_ref, wq_ref, keys_ref,
                       ht_ref, hs_ref, e0_ref, nn_ref, e1_ref, r1_ref, s_scr):
    h2 = _rms_modulate(x_ref[...], g_ref[...], sh_ref[0], sc_ref[0])
    ht = h2.T
    amax = jnp.maximum(jnp.max(jnp.abs(ht), axis=0, keepdims=True), TINY)
    ht_ref[...] = (ht * (F8_TARGET / amax)).astype(F8)
    hs_ref[...] = amax * (1.0 / F8_TARGET)
    q = jnp.dot(h2.astype(BF16), wq_ref[...], preferred_element_type=F32)
    tb = q.shape[0]
    kd = keys_ref.shape[-1]
    for hp in range(2 * PEER_HEADS):
        s_scr[hp] = lax.dot_general(keys_ref[hp // 2, hp % 2], q[:, hp * kd:(hp + 1) * kd],
                                    (((1,), (1,)), ((), ())),
                                    precision=lax.Precision.HIGHEST, preferred_element_type=F32)

    def route_head(h, _):
        for c in range(tb // LANES):
            cols = slice(c * LANES, (c + 1) * LANES)

            def route(exact):
                e0, n_of_key, e1, rank1, tie = _route_chunk(s_scr[2 * h, :, cols],
                                                            s_scr[2 * h + 1, :, cols], exact)
                half = e1.shape[0] // 2
                e0_ref[h, :, cols] = _pack_bf16_pair(e0, e0)
                nn_ref[h, :, cols] = _pack_bf16_pair(n_of_key, n_of_key)
                e1_ref[h, :, cols] = _pack_bf16_pair(e1[:half], e1[half:])
                r1_ref[h, :, cols] = _pack_bf16_pair(rank1[:half], rank1[half:])
                return tie

            tie = route(False)

            @pl.when(jnp.max(tie) > 0.0)
            def _():
                route(True)
        return 0

    lax.fori_loop(0, PEER_HEADS, route_head, 0)


def _peer_route(x2, g, mod3, mod_row, wq, sub_keys):
    rows, d = x2.shape
    tb = min(256, rows)
    nk = sub_keys.shape[2]
    order = jnp.concatenate([jnp.arange(0, nk, 2), jnp.arange(1, nk, 2)])
    sub_keys = sub_keys.at[:, 1].set(sub_keys[:, 1][:, order])
    tab_spec = pl.BlockSpec((PEER_HEADS, nk, tb), lambda i: (0, 0, i))
    half_spec = pl.BlockSpec((PEER_HEADS, nk // 2, tb), lambda i: (0, 0, i))
    tab = jax.ShapeDtypeStruct((PEER_HEADS, nk, rows), jnp.uint32)
    half_tab = jax.ShapeDtypeStruct((PEER_HEADS, nk // 2, rows), jnp.uint32)
    return pl.pallas_call(
        _peer_route_kernel,
        grid=(rows // tb,),
        in_specs=[
            pl.BlockSpec((tb, d), lambda i: (i, 0)),
            pl.BlockSpec((1, d), lambda i: (0, 0)),
            pl.BlockSpec((1, 1, d), lambda i: (mod_row(i, tb), 0, 3)),
            pl.BlockSpec((1, 1, d), lambda i: (mod_row(i, tb), 0, 4)),
            pl.BlockSpec(wq.shape, lambda i: (0, 0)),
            pl.BlockSpec(sub_keys.shape, lambda i: (0, 0, 0, 0)),
        ],
        out_specs=[pl.BlockSpec((d, tb), lambda i: (0, i)), pl.BlockSpec((1, tb), lambda i: (0, i)),
                   tab_spec, tab_spec, half_spec, half_spec],
        out_shape=[jax.ShapeDtypeStruct((d, rows), F8), jax.ShapeDtypeStruct((1, rows), F32),
                   tab, tab, half_tab, half_tab],
        scratch_shapes=[pltpu.VMEM((2 * PEER_HEADS, nk, tb), F32)],
        compiler_params=_cparams(("arbitrary",)),
        name="peer_route",
    )(x2, g.reshape(1, d), mod3, mod3, wq, sub_keys)


def _gelu_tanh(x):
    return 0.5 * x * (1.0 + jnp.tanh(math.sqrt(2.0 / math.pi) * (x + 0.044715 * (x * x * x))))


def _peer_expert_kernel(ht_ref, hs_ref, u_ref, v_ref, ws_ref, e0_ref, nn_ref, e1_ref, r1_ref,
                        x_ref, ga_ref, gf_ref, o_ref, wa_scr, act_scr, *, final_norm):
    e = pl.program_id(1)
    nk = nn_ref.shape[1]
    tb = ht_ref.shape[1]
    ib = u_ref.shape[0] // nk
    sub = SUBLANES
    assert ib == SUBLANES
    i_rows = pl.ds(pl.multiple_of(e * ib, ib), ib)

    def packed(words):
        return pltpu.bitcast(words, BF16)

    @pl.when(e == 0)
    def _():
        o_ref[...] = jnp.zeros(o_ref.shape, F32)

    act = (jnp.dot(u_ref[...], ht_ref[...], preferred_element_type=F32)
           * (hs_ref[...] * ws_ref[0:1, 0:1]))
    act_scr[...] = act
    amax = jnp.max(jnp.max(jnp.abs(act), axis=0, keepdims=True), axis=1, keepdims=True)
    bound = PEER_HEADS * jnp.maximum(amax, TINY)
    to_f8 = F8_TARGET / bound
    n_words = e1_ref.shape[1]
    for il in range(ib):
        for c in range(tb // LANES):
            cols = slice(c * LANES, (c + 1) * LANES)
            w = [None] * (n_words // sub)
            for h in range(PEER_HEADS):
                n_i = packed(jnp.broadcast_to(nn_ref[h, i_rows, cols][il:il + 1], (sub, LANES)))
                e0_i = packed(jnp.broadcast_to(e0_ref[h, i_rows, cols][il:il + 1], (sub, LANES)))
                for s in range(n_words // sub):
                    rows = slice(s * sub, (s + 1) * sub)
                    e1 = packed(e1_ref[h, rows, cols])
                    t = jnp.where(packed(r1_ref[h, rows, cols]) < n_i, e1, jnp.zeros_like(e1)) * e0_i
                    w[s] = t if w[s] is None else w[s] + t
            rows = slice(il * nk, (il + 1) * nk)
            wa = jnp.concatenate(w, axis=0).astype(F32) * _gelu_tanh(act_scr[rows, cols])
            wa_scr[rows, cols] = (wa * to_f8).astype(F8)
    o_ref[...] += (pl.dot(wa_scr[...], v_ref[...], trans_a=True)
                   * (bound * (ws_ref[0:1, 1:2] * (1.0 / F8_TARGET))))

    @pl.when(e == pl.num_programs(1) - 1)
    def _():
        y = x_ref[...] + ga_ref[0] * o_ref[...]
        if final_norm:
            ms = jnp.mean(y * y, axis=-1, keepdims=True)
            y = y * lax.rsqrt(ms + EPS) * gf_ref[...]
        o_ref[...] = y


def _fp8_tensor(w):
    amax = jnp.maximum(jnp.max(jnp.abs(w)), TINY)
    return (w * (F8_TARGET / amax)).astype(F8), amax * (1.0 / F8_TARGET)


def _peer_experts(ht, hs, u8, v8, w_scales, tables, x2, mod3, mod_row, g_final, final_norm):
    rows, d = x2.shape
    n_exp = u8.shape[0]
    nk = tables[0].shape[1]
    tb = min(512, rows)
    eb = SUBLANES * nk
    tab_spec = pl.BlockSpec((PEER_HEADS, nk, tb), lambda i, e: (0, 0, i))
    half_spec = pl.BlockSpec((PEER_HEADS, nk // 2, tb), lambda i, e: (0, 0, i))
    return pl.pallas_call(
        functools.partial(_peer_expert_kernel, final_norm=final_norm),
        grid=(rows // tb, n_exp // eb),
        in_specs=[
            pl.BlockSpec((d, tb), lambda i, e: (0, i)),
            pl.BlockSpec((1, tb), lambda i, e: (0, i)),
            pl.BlockSpec((eb, d), lambda i, e: (e, 0)),
            pl.BlockSpec((eb, d), lambda i, e: (e, 0)),
            pl.BlockSpec((1, LANES), lambda i, e: (0, 0)),
            tab_spec, tab_spec, half_spec, half_spec,
            pl.BlockSpec((tb, d), lambda i, e: (i, 0)),
            pl.BlockSpec((1, 1, d), lambda i, e: (mod_row(i, tb), 0, 5)),
            pl.BlockSpec((1, d), lambda i, e: (0, 0)),
        ],
        out_specs=pl.BlockSpec((tb, d), lambda i, e: (i, 0)),
        out_shape=jax.ShapeDtypeStruct((rows, d), F32),
        scratch_shapes=[pltpu.VMEM((eb, tb), F8), pltpu.VMEM((eb, tb), F32)],
        compiler_params=_cparams(("arbitrary", "arbitrary")),
        name="peer_experts",
    )(ht, hs, u8, v8, w_scales, *tables, x2, mod3, g_final.reshape(1, d))


def kernel(x, c, ctx, c_ctx, w_mod, b_mod, g_norm_mix, g_norm_ffn, w_in, b_gate, conv_qk, g_fourier,
           g_mlstm, w_out, w_query, sub_keys, expert_u, expert_v, g_final):
    batch, seq, d = x.shape
    ctx_len = ctx.shape[1]
    depth = w_mod.shape[0]
    f_width = g_fourier.shape[1]
    m_width = g_mlstm.shape[1]
    qk_width = conv_qk.shape[-1]
    n_gates = b_gate.shape[1]
    n_main = f_width + qk_width + 2 * m_width
    assert w_in.shape[2] == n_main + n_gates and n_gates <= LANES
    assert f_width == qk_width == m_width and m_width == M_HEADS * LANES
    assert seq % MLSTM_CHUNK == 0 and ctx_len % MLSTM_CHUNK == 0 and seq % GRID_W == 0

    cond_rows = -(-(batch + 1) // SUBLANES) * SUBLANES
    cond = jnp.zeros((cond_rows, d), F32).at[:batch].set(c).at[batch].set(c_ctx)
    mod_all = _modulation(cond, w_mod, b_mod)

    latent_row = lambda i, tm: (i * tm) // seq
    context_row = lambda i, tm: batch

    x2 = x.reshape(batch * seq, d)
    c2 = ctx.reshape(batch * ctx_len, d)
    v_col0 = qk_width // LANES
    o_col0 = (qk_width + m_width) // LANES
    zero_state = jnp.zeros((batch, M_HEADS, 2, LANES, 2 * LANES), F32)
    zero_m = jnp.zeros((batch, M_HEADS, 2, SUBLANES, LANES), F32)

    for l in range(depth):
        last = l == depth - 1
        mod3 = mod_all[l].reshape(cond_rows, 1, N_MOD * d)
        w_main = w_in[l, :, :n_main].astype(BF16)
        w_gate = jnp.zeros((d, LANES), F32).at[:, :n_gates].set(w_in[l, :, n_main:]).astype(BF16)
        bg = jnp.zeros((1, LANES), F32).at[0, :n_gates].set(b_gate[l])
        w_out_bf = w_out[l].astype(BF16)
        wq_bf = w_query[l].astype(BF16)
        u8, u_inv = _fp8_tensor(expert_u[l])
        v8, v_inv = _fp8_tensor(expert_v[l])
        w_scales = jnp.zeros((1, LANES), F32).at[0, 0].set(u_inv).at[0, 1].set(v_inv)

        def mixer(tokens, mod_row, n_tok, vertical, state, m_state):
            four, p, gates = _in_projection(tokens, g_norm_mix[l], mod3, mod_row, w_main, w_gate, bg)
            qk = _qk_conv(p, conv_qk[l], qk_width, 0, n_tok, vertical)
            ym, state, m_state = _mlstm(qk, p, gates, g_mlstm[l], state, m_state, batch, n_tok,
                                        v_col0, o_col0)
            return four, ym, state, m_state

        def ffn_and_residuals(tokens, four, ym, mod_row, n_tok, final_norm):
            f = _fourier_mix(four, g_fourier[l], batch, n_tok)
            tokens = _out_projection(f, ym, w_out_bf, tokens, mod3, mod_row)
            routed = _peer_route(tokens, g_norm_ffn[l], mod3, mod_row, wq_bf, sub_keys[l])
            return _peer_experts(routed[0], routed[1], u8, v8, w_scales, routed[2:], tokens, mod3, mod_row,
                                 g_final, final_norm)

        pc, ymc, st, m_st = mixer(c2, context_row, ctx_len, False, zero_state, zero_m)
        p, ym, _, _ = mixer(x2, latent_row, seq, True, st, m_st)
        x2 = ffn_and_residuals(x2, p, ym, latent_row, seq, last)
        if not last:
            c2 = ffn_and_residuals(c2, pc, ymc, context_row, ctx_len, False)
    return x2.reshape(batch, seq, d)
```

```python
import functools
import math

import jax
import jax.numpy as jnp
from jax import lax
from jax.experimental import pallas as pl
from jax.experimental.pallas import tpu as pltpu

F32 = jnp.float32
BF16 = jnp.bfloat16

F_GROUPS = 4
M_HEADS = 8
GRID_W = 64
N_KEYS = 128
PEER_HEADS = 8
PEER_TOPK = 16
N_MOD = 6
EPS = 1e-6

LANES = 128
SUBLANES = 8
VMEM_LIMIT = 56 * 1024 * 1024

MLSTM_CHUNK = 128
DFT_N2 = 128
DFT_KB = 8
NEG = -(2.0 ** 110)
REMOVED = 2.0 ** 120
RANK_NONE = 99.0
F8 = jnp.float8_e4m3fn
F8_TARGET = 224.0
TINY = 1e-30


def _cparams(sem):
    return pltpu.CompilerParams(dimension_semantics=sem, vmem_limit_bytes=VMEM_LIMIT)


def _sigmoid(x):
    return 1.0 / (1.0 + jnp.exp(-x))


def _split3(x):
    p0 = x.astype(BF16)
    r1 = x - p0.astype(F32)
    p1 = r1.astype(BF16)
    p2 = (r1 - p1.astype(F32)).astype(BF16)
    return p0, p1, p2


def _dot01_left(a01, x):
    return sum(jnp.dot(a01, p, preferred_element_type=F32) for p in _split3(x))


def _dot01_right(x, b01):
    return sum(jnp.dot(p, b01, preferred_element_type=F32) for p in _split3(x))


def _mod_kernel(c_ref, w_ref, b_ref, o_ref):
    c = c_ref[...]
    s = (c * _sigmoid(c)).astype(BF16)
    o_ref[0] = jnp.dot(s, w_ref[0].astype(BF16), preferred_element_type=F32) + b_ref[0]


def _modulation(cond, w_mod, b_mod):
    n_layers, d, n_out = w_mod.shape
    rows = cond.shape[0]
    tn = 1024
    return pl.pallas_call(
        _mod_kernel,
        grid=(n_layers, n_out // tn),
        in_specs=[
            pl.BlockSpec((rows, d), lambda l, j: (0, 0)),
            pl.BlockSpec((1, d, tn), lambda l, j: (l, 0, j)),
            pl.BlockSpec((1, 1, tn), lambda l, j: (l, 0, j)),
        ],
        out_specs=pl.BlockSpec((1, rows, tn), lambda l, j: (l, 0, j)),
        out_shape=jax.ShapeDtypeStruct((n_layers, rows, n_out), F32),
        compiler_params=_cparams(("arbitrary", "arbitrary")),
        name="modulation",
    )(cond, w_mod, b_mod.reshape(n_layers, 1, n_out))


def _rms_modulate(x, g, shift, scale):
    ms = jnp.mean(x * x, axis=-1, keepdims=True)
    return (x * lax.rsqrt(ms + EPS) * g) * (1.0 + scale) + shift


def _inproj_kernel(x_ref, g_ref, sh_ref, sc_ref, w_ref, wg_ref, bg_ref, four_ref, p_ref, gate_ref, h_scr):
    j = pl.program_id(1)

    @pl.when(j == 0)
    def _():
        h = _rms_modulate(x_ref[...], g_ref[...], sh_ref[0], sc_ref[0]).astype(BF16)
        h_scr[...] = h
        gate_ref[...] = jnp.dot(h, wg_ref[...], preferred_element_type=F32) + bg_ref[...]

    acc = jnp.dot(h_scr[...], w_ref[...], preferred_element_type=F32)

    @pl.when(j == 0)
    def _():
        four_ref[...] = acc

    @pl.when(j > 0)
    def _():
        p_ref[...] = acc.astype(BF16)


def _in_projection(x2, g, mod3, mod_row, w_main, w_gate, b_gate):
    rows, d = x2.shape
    n_main = w_main.shape[1]
    tm = min(1024, rows)
    tn = 1024
    return pl.pallas_call(
        _inproj_kernel,
        grid=(rows // tm, n_main // tn),
        in_specs=[
            pl.BlockSpec((tm, d), lambda i, j: (i, 0)),
            pl.BlockSpec((1, d), lambda i, j: (0, 0)),
            pl.BlockSpec((1, 1, d), lambda i, j: (mod_row(i, tm), 0, 0)),
            pl.BlockSpec((1, 1, d), lambda i, j: (mod_row(i, tm), 0, 1)),
            pl.BlockSpec((d, tn), lambda i, j: (0, j)),
            pl.BlockSpec((d, LANES), lambda i, j: (0, 0)),
            pl.BlockSpec((1, LANES), lambda i, j: (0, 0)),
        ],
        out_specs=[
            pl.BlockSpec((tm, tn), lambda i, j: (i, 0)),
            pl.BlockSpec((tm, tn), lambda i, j: (i, jnp.maximum(j - 1, 0))),
            pl.BlockSpec((tm, LANES), lambda i, j: (i, 0)),
        ],
        out_shape=[
            jax.ShapeDtypeStruct((rows, tn), F32),
            jax.ShapeDtypeStruct((rows, n_main - tn), BF16),
            jax.ShapeDtypeStruct((rows, LANES), F32),
        ],
        scratch_shapes=[pltpu.VMEM((tm, d), BF16)],
        compiler_params=_cparams(("arbitrary", "arbitrary")),
        name="in_projection",
    )(x2, g.reshape(1, d), mod3, mod3, w_main, w_gate, b_gate)


def _conv_kernel(*refs, tb, width, vertical, blocks_per_image):
    if vertical:
        cur_ref, top_ref, bot_ref, w_ref, o_ref = refs
    else:
        cur_ref, w_ref, o_ref = refs
    cur = cur_ref[...].astype(F32)
    ch = cur.shape[1]
    wpos = lax.rem(lax.broadcasted_iota(jnp.int32, (tb, ch), 0), width)
    first_col = wpos == 0
    last_col = wpos == width - 1
    if vertical:
        r = lax.rem(pl.program_id(0), blocks_per_image)
        top = jnp.where(r == 0, 0.0, top_ref[...].astype(F32))
        bot = jnp.where(r == blocks_per_image - 1, 0.0, bot_ref[...].astype(F32))
        ext = jnp.concatenate([top, cur, bot], axis=0)
        bases = [(dr, ext[dr * width:dr * width + tb]) for dr in range(3)]
    else:
        bases = [(1, cur)]
    acc = jnp.zeros((tb, ch), F32)
    for dr, base in bases:
        left = jnp.where(first_col, 0.0, pltpu.roll(base, 1, axis=0))
        right = jnp.where(last_col, 0.0, pltpu.roll(base, tb - 1, axis=0))
        for dw, shifted in enumerate((left, base, right)):
            k = dr * 3 + dw
            acc = acc + shifted * w_ref[k:k + 1, :]
    o_ref[...] = (acc * _sigmoid(acc)).astype(BF16)


def _qk_conv(p, conv_w, qk_width, col_block, tokens_per_image, vertical):
    rows = p.shape[0]
    w9 = conv_w.reshape(9, qk_width)
    if vertical:
        width = GRID_W
        tb = min(512, tokens_per_image)
        bpi = tokens_per_image // tb
        halo = tb // width
        n_halo = rows // width
        in_specs = [
            pl.BlockSpec((tb, qk_width), lambda i: (i, col_block)),
            pl.BlockSpec((width, qk_width), lambda i: (jnp.maximum(i * halo - 1, 0), col_block)),
            pl.BlockSpec((width, qk_width), lambda i: (jnp.minimum((i + 1) * halo, n_halo - 1), col_block)),
            pl.BlockSpec((9, qk_width), lambda i: (0, 0)),
        ]
        args = (p, p, p, w9)
    else:
        width = tb = tokens_per_image
        bpi = 1
        in_specs = [
            pl.BlockSpec((tb, qk_width), lambda i: (i, col_block)),
            pl.BlockSpec((9, qk_width), lambda i: (0, 0)),
        ]
        args = (p, w9)
    return pl.pallas_call(
        functools.partial(_conv_kernel, tb=tb, width=width, vertical=vertical, blocks_per_image=bpi),
        grid=(rows // tb,),
        in_specs=in_specs,
        out_specs=pl.BlockSpec((tb, qk_width), lambda i: (i, 0)),
        out_shape=jax.ShapeDtypeStruct((rows, qk_width), BF16),
        compiler_params=_cparams(("arbitrary",)),
        name="qk_conv_latent" if vertical else "qk_conv_context",
    )(*args)


def _mlstm_chunks(chains):
    n = range(len(chains))
    ch = chains
    L = ch[0]["q"].shape[0]
    logsig = [jnp.minimum(c["gch"], 0.0) - jnp.log(1.0 + jnp.exp(-jnp.abs(c["gch"]))) for c in ch]
    ig_p = [_split3(c["gch"]) for c in ch]
    lf_p = [_split3(logsig[i]) for i in n]
    ig = [sum(jnp.dot(p, ch[i]["sel_i"], preferred_element_type=F32) for p in ig_p[i]) for i in n]
    lf = [sum(jnp.dot(p, ch[i]["sel_f"], preferred_element_type=F32) for p in lf_p[i]) for i in n]
    b_p = [_split3(lf[i]) for i in n]
    b = [sum(jnp.dot(ch[i]["cum"], p, preferred_element_type=F32) for p in b_p[i]) for i in n]
    b_end = [b[i][0:1, :] if ch[i]["reverse"] else b[i][L - 1:L, :] for i in n]
    a_t = [(ig[i] - b[i]).T for i in n]
    dmat = [jnp.where(ch[i]["causal"], b[i] + a_t[i], NEG) for i in n]
    g = [b[i] + ch[i]["m"] for i in n]
    m_j = [jnp.maximum(g[i], jnp.max(dmat[i], axis=-1, keepdims=True)) for i in n]
    pmat = [jnp.exp(dmat[i] - m_j[i]) for i in n]
    s_raw = [lax.dot_general(c["q"], c["k"], (((1,), (1,)), ((), ())), preferred_element_type=F32)
             for c in ch]
    s = [(s_raw[i] * pmat[i]).astype(BF16) for i in n]
    intra = [jnp.dot(s[i], ch[i]["vaug"], preferred_element_type=F32) for i in n]
    carried = [jnp.dot(c["q"], c["state"].astype(BF16), preferred_element_type=F32) for c in ch]
    inter = [jnp.exp(g[i] - m_j[i]) for i in n]
    tot = [intra[i] + jnp.concatenate([inter[i], inter[i]], axis=1) * carried[i] for i in n]
    h = [tot[i][:, :LANES] / jnp.maximum(jnp.abs(tot[i][:, LANES:]), jnp.exp(-m_j[i])) for i in n]
    a = [b_end[i] - b[i] + ig[i] for i in n]
    m_new = [jnp.maximum(b_end[i] + ch[i]["m"], jnp.max(a[i], axis=0, keepdims=True)) for i in n]
    kw = [(ch[i]["k"].astype(F32) * jnp.exp(a[i] - m_new[i])).astype(BF16) for i in n]
    f_old = [jnp.exp(b_end[i] + ch[i]["m"] - m_new[i]) for i in n]
    upd = [pl.dot(kw[i], ch[i]["vaug"], trans_a=True) for i in n]
    state_new = [jnp.concatenate([f_old[i], f_old[i]], axis=1) * ch[i]["state"] + upd[i] for i in n]
    return [(h[i], state_new[i], m_new[i]) for i in n]


def _mlstm_kernel(q_ref, k_ref, v_ref, o_ref, gt_ref, g_ref, sin_ref, min_ref,
                  y_ref, sout_ref, mout_ref, hf_scr, hb_scr, *, seq, k_scale):
    L = MLSTM_CHUNK
    nc = seq // L
    pair = pl.program_id(1)
    lane = lax.broadcasted_iota(jnp.int32, (1, LANES), 1)
    rr = lax.broadcasted_iota(jnp.int32, (L, L), 0)
    cc = lax.broadcasted_iota(jnp.int32, (L, L), 1)
    causal = (rr >= cc, rr <= cc)
    cum = tuple(jnp.where(m, 1.0, 0.0).astype(BF16) for m in causal)
    ones = jnp.ones((L, LANES), BF16)
    qmask, kmask, sel = [], [], []
    for j in range(2):
        own = (lane // (LANES // 2)) == j
        qmask.append(jnp.where(own, 1.0, 0.0).astype(BF16))
        kmask.append(jnp.where(own, k_scale, 0.0).astype(BF16))
        head = 2 * pair + j
        sel.append([jnp.where(rr == kind * M_HEADS + head, 1.0, 0.0).astype(BF16) for kind in range(4)])

    def body(c, carry):
        chains, dest = [], []
        for d in range(2):
            rows = pl.ds(pl.multiple_of((c if d == 0 else nc - 1 - c) * L, L), L)
            q_all, k_all, gch = q_ref[rows, :], k_ref[rows, :], gt_ref[rows, :]
            for j in range(2):
                hcols = slice(j * LANES, (j + 1) * LANES)
                idx = 2 * (2 * j + d)
                chains.append(dict(
                    q=q_all * qmask[j], k=k_all * kmask[j],
                    vaug=jnp.concatenate([v_ref[rows, hcols], ones], axis=1), gch=gch,
                    sel_i=sel[j][2 * d], sel_f=sel[j][2 * d + 1], cum=cum[d], causal=causal[d],
                    reverse=d == 1, state=carry[idx], m=carry[idx + 1]))
                dest.append((hf_scr if d == 0 else hb_scr, rows, hcols, idx))
        carry = list(carry)
        for (scr, rows, hcols, idx), (h, state, m) in zip(dest, _mlstm_chunks(chains)):
            scr[rows, hcols] = h.astype(BF16)
            carry[idx], carry[idx + 1] = state, m
        return tuple(carry)

    init = []
    for j in range(2):
        for d in range(2):
            init += [sin_ref[0, j, d], min_ref[0, j, d][0:1, :]]
    final = lax.fori_loop(0, nc, body, tuple(init))
    for j in range(2):
        for d in range(2):
            idx = 2 * (2 * j + d)
            sout_ref[0, j, d] = final[idx]
            mout_ref[0, j, d] = jnp.broadcast_to(final[idx + 1], (SUBLANES, LANES))

    def finish(c, _):
        rows = pl.ds(pl.multiple_of(c * L, L), L)
        for j in range(2):
            hcols = slice(j * LANES, (j + 1) * LANES)
            h = hf_scr[rows, hcols].astype(F32) + hb_scr[rows, hcols].astype(F32)
            ms = jnp.mean(h * h, axis=-1, keepdims=True)
            y = h * lax.rsqrt(ms + EPS) * g_ref[:, hcols]
            y_ref[rows, hcols] = (y * _sigmoid(o_ref[rows, hcols].astype(F32))).astype(BF16)
        return 0

    lax.fori_loop(0, nc, finish, 0)


def _mlstm(qk, p, gates, g_mlstm, state_in, m_in, batch, seq, v_col0, o_col0):
    rows = qk.shape[0]
    pw = 2 * LANES
    k_col0 = qk.shape[1] // (2 * LANES)
    dk = qk.shape[1] // (2 * M_HEADS)
    assert v_col0 % 2 == 0 and o_col0 % 2 == 0
    return pl.pallas_call(
        functools.partial(_mlstm_kernel, seq=seq, k_scale=dk ** -0.5),
        grid=(batch, M_HEADS // 2),
        in_specs=[
            pl.BlockSpec((seq, LANES), lambda b, h: (b, h)),
            pl.BlockSpec((seq, LANES), lambda b, h: (b, k_col0 + h)),
            pl.BlockSpec((seq, pw), lambda b, h: (b, v_col0 // 2 + h)),
            pl.BlockSpec((seq, pw), lambda b, h: (b, o_col0 // 2 + h)),
            pl.BlockSpec((seq, LANES), lambda b, h: (b, 0)),
            pl.BlockSpec((1, pw), lambda b, h: (0, h)),
            pl.BlockSpec((1, 2, 2, LANES, 2 * LANES), lambda b, h: (b, h, 0, 0, 0)),
            pl.BlockSpec((1, 2, 2, SUBLANES, LANES), lambda b, h: (b, h, 0, 0, 0)),
        ],
        out_specs=[
            pl.BlockSpec((seq, pw), lambda b, h: (b, h)),
            pl.BlockSpec((1, 2, 2, LANES, 2 * LANES), lambda b, h: (b, h, 0, 0, 0)),
            pl.BlockSpec((1, 2, 2, SUBLANES, LANES), lambda b, h: (b, h, 0, 0, 0)),
        ],
        out_shape=[
            jax.ShapeDtypeStruct((rows, M_HEADS * LANES), BF16),
            jax.ShapeDtypeStruct(state_in.shape, F32),
            jax.ShapeDtypeStruct(m_in.shape, F32),
        ],
        scratch_shapes=[pltpu.VMEM((seq, pw), BF16), pltpu.VMEM((seq, pw), BF16)],
        compiler_params=_cparams(("arbitrary", "arbitrary")),
        name="mlstm",
    )(qk, qk, p, p, gates, g_mlstm.reshape(1, -1), state_in, m_in)


def _dft_mats(n, scale):
    idx = jnp.arange(n, dtype=jnp.int32)
    ang = (2.0 * math.pi / n) * ((idx[:, None] * idx[None, :]) % n).astype(F32)
    return jnp.cos(ang) * scale, jnp.sin(ang) * scale


def _channel_dft(u, cc_ref, sc_ref):
    gw = cc_ref.shape[0]
    zr, zi = [], []
    for g in range(u.shape[1] // gw):
        ug = u[:, g * gw:(g + 1) * gw]
        zr.append(jnp.dot(ug, cc_ref[...], preferred_element_type=F32))
        zi.append(jnp.dot(ug, sc_ref[...], preferred_element_type=F32))
    return jnp.concatenate(zr, axis=1), jnp.concatenate(zi, axis=1)


def _group_rmsnorm(y, g, gw):
    outs = []
    for k in range(y.shape[1] // gw):
        yk = y[:, k * gw:(k + 1) * gw]
        ms = jnp.mean(yk * yk, axis=-1, keepdims=True)
        outs.append(yk * lax.rsqrt(ms + EPS) * g[:, k * gw:(k + 1) * gw])
    return jnp.concatenate(outs, axis=1)


def _fnet_a_kernel(u_ref, cc_ref, sc_ref, f_ref, cw_ref, sw_ref, yr_ref, yi_ref):
    fw = u_ref.shape[3]
    for r in range(u_ref.shape[2]):
        zr, zi = _channel_dft(u_ref[0, :, r, :].astype(BF16), cc_ref, sc_ref)
        z = jnp.concatenate([zr, zi], axis=0).astype(BF16)
        y = jnp.dot(f_ref[...], z, preferred_element_type=F32)
        n2 = y.shape[0] // 2
        yr, yi = y[:n2], y[n2:]
        cw, sw = cw_ref[r], sw_ref[r]
        for j in range(fw // LANES):
            cols = slice(j * LANES, (j + 1) * LANES)
            ocols = slice(r * fw + j * LANES, r * fw + (j + 1) * LANES)
            yr_ref[0, :, ocols] = (yr[:, cols] * cw + yi[:, cols] * sw).astype(BF16)
            yi_ref[0, :, ocols] = (yi[:, cols] * cw - yr[:, cols] * sw).astype(BF16)


def _fnet_b_kernel(yr_ref, yi_ref, gc_ref, gs_ref, g_ref, o_ref, *, gw):
    y = (jnp.dot(gc_ref[...], yr_ref[0], preferred_element_type=F32)
         + jnp.dot(gs_ref[...], yi_ref[0], preferred_element_type=F32))
    out = _group_rmsnorm(y, g_ref[...], gw).astype(BF16)
    o_ref[0] = out.reshape(o_ref.shape[1:])


def _fnet_direct_kernel(u_ref, cc_ref, sc_ref, ct_ref, st_ref, g_ref, o_ref, *, gw):
    zr, zi = _channel_dft(u_ref[...].astype(BF16), cc_ref, sc_ref)
    y = (jnp.dot(ct_ref[...], zr.astype(BF16), preferred_element_type=F32)
         + jnp.dot(st_ref[...], zi.astype(BF16), preferred_element_type=F32))
    o_ref[...] = _group_rmsnorm(y, g_ref[...], gw).astype(BF16)


def _fourier_mix(p, g_fourier, batch, seq):
    f_width = p.shape[1]
    gw = f_width // F_GROUPS
    cc, sc = _dft_mats(gw, gw ** -0.5)
    cc, msc = cc.astype(BF16), (-sc).astype(BF16)
    g2 = g_fourier.reshape(1, f_width)
    rows = batch * seq
    if seq <= 512:
        ct, st = _dft_mats(seq, seq ** -0.5)
        return pl.pallas_call(
            functools.partial(_fnet_direct_kernel, gw=gw),
            grid=(batch,),
            in_specs=[
                pl.BlockSpec((seq, f_width), lambda b: (b, 0)),
                pl.BlockSpec((gw, gw), lambda b: (0, 0)),
                pl.BlockSpec((gw, gw), lambda b: (0, 0)),
                pl.BlockSpec((seq, seq), lambda b: (0, 0)),
                pl.BlockSpec((seq, seq), lambda b: (0, 0)),
                pl.BlockSpec((1, f_width), lambda b: (0, 0)),
            ],
            out_specs=pl.BlockSpec((seq, f_width), lambda b: (b, 0)),
            out_shape=jax.ShapeDtypeStruct((rows, f_width), BF16),
            compiler_params=_cparams(("arbitrary",)),
            name="fourier_direct",
        )(p, cc, msc, ct.astype(BF16), st.astype(BF16), g2)

    n2 = DFT_N2
    n1 = seq // n2
    c2, s2 = _dft_mats(n2, n2 ** -0.5)
    fmat = jnp.concatenate([jnp.concatenate([c2, s2], axis=1),
                            jnp.concatenate([-s2, c2], axis=1)], axis=0).astype(BF16)
    i1 = jnp.arange(n1, dtype=jnp.int32)
    i2 = jnp.arange(n2, dtype=jnp.int32)
    tw = (2.0 * math.pi / seq) * (i1[:, None] * i2[None, :]).astype(F32)
    cw = jnp.broadcast_to(jnp.cos(tw)[:, :, None], (n1, n2, LANES))
    sw = jnp.broadcast_to(jnp.sin(tw)[:, :, None], (n1, n2, LANES))
    nr = SUBLANES
    yr, yi = pl.pallas_call(
        _fnet_a_kernel,
        grid=(batch, n1 // nr),
        in_specs=[
            pl.BlockSpec((1, n2, nr, f_width), lambda b, j: (b, 0, j, 0)),
            pl.BlockSpec((gw, gw), lambda b, j: (0, 0)),
            pl.BlockSpec((gw, gw), lambda b, j: (0, 0)),
            pl.BlockSpec((2 * n2, 2 * n2), lambda b, j: (0, 0)),
            pl.BlockSpec((nr, n2, LANES), lambda b, j: (j, 0, 0)),
            pl.BlockSpec((nr, n2, LANES), lambda b, j: (j, 0, 0)),
        ],
        out_specs=[
            pl.BlockSpec((1, n2, nr * f_width), lambda b, j: (b, 0, j)),
            pl.BlockSpec((1, n2, nr * f_width), lambda b, j: (b, 0, j)),
        ],
        out_shape=[jax.ShapeDtypeStruct((batch, n2, n1 * f_width), BF16)] * 2,
        compiler_params=_cparams(("arbitrary", "arbitrary")),
        name="fourier_stage_a",
    )(p.reshape(batch, n2, n1, f_width), cc, msc, fmat, cw, sw)

    kb = DFT_KB
    c1, s1 = _dft_mats(n1, n1 ** -0.5)
    eye = jnp.eye(kb, dtype=F32)
    gc = jnp.einsum("kn,ab->kabn", c1, eye).reshape(n1 * kb, kb * n1).astype(BF16)
    gs = jnp.einsum("kn,ab->kabn", s1, eye).reshape(n1 * kb, kb * n1).astype(BF16)
    out = pl.pallas_call(
        functools.partial(_fnet_b_kernel, gw=gw),
        grid=(batch, n2 // kb),
        in_specs=[
            pl.BlockSpec((1, kb * n1, f_width), lambda b, j: (b, j, 0)),
            pl.BlockSpec((1, kb * n1, f_width), lambda b, j: (b, j, 0)),
            pl.BlockSpec((n1 * kb, kb * n1), lambda b, j: (0, 0)),
            pl.BlockSpec((n1 * kb, kb * n1), lambda b, j: (0, 0)),
            pl.BlockSpec((1, f_width), lambda b, j: (0, 0)),
        ],
        out_specs=pl.BlockSpec((1, n1, kb, f_width), lambda b, j: (b, 0, j, 0)),
        out_shape=jax.ShapeDtypeStruct((batch, n1, n2, f_width), BF16),
        compiler_params=_cparams(("arbitrary", "arbitrary")),
        name="fourier_stage_b",
    )(yr.reshape(batch, n2 * n1, f_width), yi.reshape(batch, n2 * n1, f_width), gc, gs, g2)
    return out.reshape(rows, f_width)


def _outproj_kernel(f_ref, ym_ref, w_ref, x_ref, ga_ref, o_ref):
    fw = f_ref.shape[1]
    y = (jnp.dot(f_ref[...], w_ref[:fw, :], preferred_element_type=F32)
         + jnp.dot(ym_ref[...], w_ref[fw:, :], preferred_element_type=F32))
    o_ref[...] = x_ref[...] + ga_ref[0] * y


def _out_projection(f, ym, w_out, x2, mod3, mod_row):
    rows, d = x2.shape
    tm = min(512, rows)
    return pl.pallas_call(
        _outproj_kernel,
        grid=(rows // tm,),
        in_specs=[
            pl.BlockSpec((tm, f.shape[1]), lambda i: (i, 0)),
            pl.BlockSpec((tm, ym.shape[1]), lambda i: (i, 0)),
            pl.BlockSpec(w_out.shape, lambda i: (0, 0)),
            pl.BlockSpec((tm, d), lambda i: (i, 0)),
            pl.BlockSpec((1, 1, d), lambda i: (mod_row(i, tm), 0, 2)),
        ],
        out_specs=pl.BlockSpec((tm, d), lambda i: (i, 0)),
        out_shape=jax.ShapeDtypeStruct((rows, d), F32),
        compiler_params=_cparams(("arbitrary",)),
        name="out_projection",
    )(f, ym, w_out, x2, mod3)


def _pack_bf16_pair(lo, hi):
    def bf16_bits(x):
        u = lax.bitcast_convert_type(x, jnp.uint32)
        return (u + (jnp.uint32(0x7FFF) + ((u >> 16) & jnp.uint32(1)))) >> 16
    return bf16_bits(lo) | (bf16_bits(hi) << 16)


def _interleaved_key(q, n):
    half = n // 2
    return jnp.where(q < half, 2 * q, 2 * (q - half) + 1)


def _extract_top(arrays, k, exact, interleaved=()):
    idx = range(len(arrays))
    s = list(arrays)
    n = [x.shape[0] for x in s]
    row = [None] * len(s)
    if exact:
        for i in idx:
            row[i] = lax.broadcasted_iota(jnp.int32, s[i].shape, 0)
            if i < len(interleaved) and interleaved[i]:
                row[i] = _interleaved_key(row[i], n[i])
    vals = [[] for _ in s]
    for it in range(k):
        m = [jnp.max(s[i], axis=0, keepdims=True) for i in idx]
        hit = [s[i] == m[i] for i in idx]
        if exact:
            first = [jnp.min(jnp.where(hit[i], row[i], n[i]), axis=0, keepdims=True) for i in idx]
            hit = [row[i] == first[i] for i in idx]
        s = [jnp.where(hit[i], -(it + 1) * REMOVED, s[i]) for i in idx]
        for i in idx:
            vals[i].append(m[i])
    gone = [s[i] <= -REMOVED for i in idx]
    rank = [jnp.where(gone[i], s[i] * (-1.0 / REMOVED) - 1.0, RANK_NONE) for i in idx]
    removed = [jnp.sum(jnp.where(gone[i], 1.0, 0.0), axis=0, keepdims=True) for i in idx]
    return [(vals[i], rank[i], removed[i]) for i in idx]


def _route_chunk(s0, s1, exact):
    k = PEER_TOPK
    sub = SUBLANES
    (v0, rank0, rem0), (v1, rank1, rem1) = _extract_top([s0, s1], k, exact, interleaved=(False, True))
    sv0 = jnp.concatenate(v0, axis=0)
    sv1 = jnp.concatenate(v1, axis=0)
    if exact:
        layout = [("row", a, b0) for a in range(k) for b0 in range(0, max(k // (a + 1), 1), sub)]
    else:
        layout = ([("row", 0, 0), ("row", 0, sub), ("row", 1, 0), ("col", 0, sub)]
                  + [("row", a, 0) for a in range(2, k // 3)]
                  + [("col", b, 0) for b in range(k // sub)])
    seen = set()
    blocks = []
    for kind, fixed, start in layout:
        cells = [(fixed, start + r) if kind == "row" else (start + r, fixed) for r in range(sub)]
        keep = [(a + 1) * (b + 1) <= k and (a, b) not in seen for a, b in cells]
        seen.update(cell for cell, kp in zip(cells, keep) if kp)
        blk = (v0[fixed] + sv1[start:start + sub]) if kind == "row" else (sv0[start:start + sub] + v1[fixed])
        if not all(keep):
            brow = lax.broadcasted_iota(jnp.int32, blk.shape, 0)
            mask = functools.reduce(jnp.logical_or, [brow == r for r, kp in enumerate(keep) if kp])
            blk = jnp.where(mask, blk, NEG)
        blocks.append(blk)
    assert len(seen) == sum(k // (a + 1) for a in range(k))
    cand = jnp.concatenate(blocks, axis=0)
    (_, crank, rem2), = _extract_top([cand], k, exact)
    cnt = jnp.where(crank < RANK_NONE, 1.0, 0.0)
    z = jnp.sum(cnt * jnp.exp(cand - (v0[0] + v1[0])), axis=0, keepdims=True)
    arow = lax.broadcasted_iota(jnp.int32, sv0.shape, 0)
    n_rank = jnp.zeros(sv0.shape, F32)
    for idx, (kind, fixed, start) in enumerate(layout):
        c_blk = cnt[idx * sub:(idx + 1) * sub]
        if kind == "row":
            n_rank = n_rank + jnp.where(arow == fixed, jnp.sum(c_blk, axis=0, keepdims=True), 0.0)
        else:
            pieces = [c_blk if a0 == start else jnp.zeros_like(c_blk) for a0 in range(0, k, sub)]
            n_rank = n_rank + jnp.concatenate(pieces, axis=0)
    n_of_key = jnp.zeros(rank0.shape, F32)
    for a in range(k):
        n_of_key = jnp.where(rank0 == float(a), n_rank[a:a + 1], n_of_key)
    tie = (jnp.where(rem0 == k, 0.0, 1.0) + jnp.where(rem1 == k, 0.0, 1.0)
           + jnp.where(rem2 == k, 0.0, 1.0))
    return jnp.exp(s0 - v0[0]) / z, n_of_key, jnp.exp(s1 - v1[0]), rank1, tie


def _peer_route_kernel(x_ref, g_ref, sh_ref, sc_ref, wq_ref, keys_ref,
                       ht_ref, hs_ref, e0_ref, nn_ref, e1_ref, r1_ref, s_scr):
    h2 = _rms_modulate(x_ref[...], g_ref[...], sh_ref[0], sc_ref[0])
    ht = h2.T
    amax = jnp.maximum(jnp.max(jnp.abs(ht), axis=0, keepdims=True), TINY)
    ht_ref[...] = (ht * (F8_TARGET / amax)).astype(F8)
    hs_ref[...] = amax * (1.0 / F8_TARGET)
    q = jnp.dot(h2.astype(BF16), wq_ref[...], preferred_element_type=F32)
    tb = q.shape[0]
    kd = keys_ref.shape[-1]
    for hp in range(2 * PEER_HEADS):
        s_scr[hp] = lax.dot_general(keys_ref[hp // 2, hp % 2], q[:, hp * kd:(hp + 1) * kd],
                                    (((1,), (1,)), ((), ())),
                                    precision=lax.Precision.HIGHEST, preferred_element_type=F32)

    def route_head(h, _):
        for c in range(tb // LANES):
            cols = slice(c * LANES, (c + 1) * LANES)

            def route(exact):
                e0, n_of_key, e1, rank1, tie = _route_chunk(s_scr[2 * h, :, cols],
                                                            s_scr[2 * h + 1, :, cols], exact)
                half = e1.shape[0] // 2
                e0_ref[h, :, cols] = _pack_bf16_pair(e0, e0)
                nn_ref[h, :, cols] = _pack_bf16_pair(n_of_key, n_of_key)
                e1_ref[h, :, cols] = _pack_bf16_pair(e1[:half], e1[half:])
                r1_ref[h, :, cols] = _pack_bf16_pair(rank1[:half], rank1[half:])
                return tie

            tie = route(False)

            @pl.when(jnp.max(tie) > 0.0)
            def _():
                route(True)
        return 0

    lax.fori_loop(0, PEER_HEADS, route_head, 0)


def _peer_route(x2, g, mod3, mod_row, wq, sub_keys):
    rows, d = x2.shape
    tb = min(256, rows)
    nk = sub_keys.shape[2]
    order = jnp.concatenate([jnp.arange(0, nk, 2), jnp.arange(1, nk, 2)])
    sub_keys = sub_keys.at[:, 1].set(sub_keys[:, 1][:, order])
    tab_spec = pl.BlockSpec((PEER_HEADS, nk, tb), lambda i: (0, 0, i))
    half_spec = pl.BlockSpec((PEER_HEADS, nk // 2, tb), lambda i: (0, 0, i))
    tab = jax.ShapeDtypeStruct((PEER_HEADS, nk, rows), jnp.uint32)
    half_tab = jax.ShapeDtypeStruct((PEER_HEADS, nk // 2, rows), jnp.uint32)
    return pl.pallas_call(
        _peer_route_kernel,
        grid=(rows // tb,),
        in_specs=[
            pl.BlockSpec((tb, d), lambda i: (i, 0)),
            pl.BlockSpec((1, d), lambda i: (0, 0)),
            pl.BlockSpec((1, 1, d), lambda i: (mod_row(i, tb), 0, 3)),
            pl.BlockSpec((1, 1, d), lambda i: (mod_row(i, tb), 0, 4)),
            pl.BlockSpec(wq.shape, lambda i: (0, 0)),
            pl.BlockSpec(sub_keys.shape, lambda i: (0, 0, 0, 0)),
        ],
        out_specs=[pl.BlockSpec((d, tb), lambda i: (0, i)), pl.BlockSpec((1, tb), lambda i: (0, i)),
                   tab_spec, tab_spec, half_spec, half_spec],
        out_shape=[jax.ShapeDtypeStruct((d, rows), F8), jax.ShapeDtypeStruct((1, rows), F32),
                   tab, tab, half_tab, half_tab],
        scratch_shapes=[pltpu.VMEM((2 * PEER_HEADS, nk, tb), F32)],
        compiler_params=_cparams(("arbitrary",)),
        name="peer_route",
    )(x2, g.reshape(1, d), mod3, mod3, wq, sub_keys)


def _gelu_tanh(x):
    return 0.5 * x * (1.0 + jnp.tanh(math.sqrt(2.0 / math.pi) * (x + 0.044715 * (x * x * x))))


def _peer_expert_kernel(ht_ref, hs_ref, u_ref, v_ref, ws_ref, e0_ref, nn_ref, e1_ref, r1_ref,
                        x_ref, ga_ref, gf_ref, o_ref, wa_scr, act_scr, *, final_norm):
    e = pl.program_id(1)
    nk = nn_ref.shape[1]
    tb = ht_ref.shape[1]
    ib = u_ref.shape[0] // nk
    sub = SUBLANES
    assert ib == SUBLANES
    i_rows = pl.ds(pl.multiple_of(e * ib, ib), ib)

    def packed(words):
        return pltpu.bitcast(words, BF16)

    @pl.when(e == 0)
    def _():
        o_ref[...] = jnp.zeros(o_ref.shape, F32)

    act = (jnp.dot(u_ref[...], ht_ref[...], preferred_element_type=F32)
           * (hs_ref[...] * ws_ref[0:1, 0:1])).astype(BF16)
    act_scr[...] = pltpu.bitcast(act, jnp.uint32)
    amax = jnp.max(jnp.max(jnp.abs(act), axis=0, keepdims=True), axis=1, keepdims=True)
    bound = PEER_HEADS * jnp.maximum(amax.astype(F32), TINY)
    to_f8 = (F8_TARGET / bound).astype(BF16)
    from_f8 = 1.0 / to_f8.astype(F32)
    n_words = e1_ref.shape[1]
    for il in range(ib):
        for c in range(tb // LANES):
            cols = slice(c * LANES, (c + 1) * LANES)
            w = [None] * (n_words // sub)
            for h in range(PEER_HEADS):
                n_i = packed(jnp.broadcast_to(nn_ref[h, i_rows, cols][il:il + 1], (sub, LANES)))
                e0_i = packed(jnp.broadcast_to(e0_ref[h, i_rows, cols][il:il + 1], (sub, LANES)))
                for s in range(n_words // sub):
                    rows = slice(s * sub, (s + 1) * sub)
                    e1 = packed(e1_ref[h, rows, cols])
                    t = jnp.where(packed(r1_ref[h, rows, cols]) < n_i, e1, jnp.zeros_like(e1)) * e0_i
                    w[s] = t if w[s] is None else w[s] + t
            a = packed(act_scr[il * n_words:(il + 1) * n_words, cols])
            wa = jnp.concatenate(w, axis=0) * _gelu_tanh(a) * to_f8
            wa_scr[il * nk:(il + 1) * nk, cols] = wa.astype(F8)
    o_ref[...] += pl.dot(wa_scr[...], v_ref[...], trans_a=True) * (from_f8 * ws_ref[0:1, 1:2])

    @pl.when(e == pl.num_programs(1) - 1)
    def _():
        y = x_ref[...] + ga_ref[0] * o_ref[...]
        if final_norm:
            ms = jnp.mean(y * y, axis=-1, keepdims=True)
            y = y * lax.rsqrt(ms + EPS) * gf_ref[...]
        o_ref[...] = y


def _fp8_tensor(w):
    amax = jnp.maximum(jnp.max(jnp.abs(w)), TINY)
    return (w * (F8_TARGET / amax)).astype(F8), amax * (1.0 / F8_TARGET)


def _peer_experts(ht, hs, u8, v8, w_scales, tables, x2, mod3, mod_row, g_final, final_norm):
    rows, d = x2.shape
    n_exp = u8.shape[0]
    nk = tables[0].shape[1]
    tb = min(512, rows)
    eb = SUBLANES * nk
    tab_spec = pl.BlockSpec((PEER_HEADS, nk, tb), lambda i, e: (0, 0, i))
    half_spec = pl.BlockSpec((PEER_HEADS, nk // 2, tb), lambda i, e: (0, 0, i))
    return pl.pallas_call(
        functools.partial(_peer_expert_kernel, final_norm=final_norm),
        grid=(rows // tb, n_exp // eb),
        in_specs=[
            pl.BlockSpec((d, tb), lambda i, e: (0, i)),
            pl.BlockSpec((1, tb), lambda i, e: (0, i)),
            pl.BlockSpec((eb, d), lambda i, e: (e, 0)),
            pl.BlockSpec((eb, d), lambda i, e: (e, 0)),
            pl.BlockSpec((1, LANES), lambda i, e: (0, 0)),
            tab_spec, tab_spec, half_spec, half_spec,
            pl.BlockSpec((tb, d), lambda i, e: (i, 0)),
            pl.BlockSpec((1, 1, d), lambda i, e: (mod_row(i, tb), 0, 5)),
            pl.BlockSpec((1, d), lambda i, e: (0, 0)),
        ],
        out_specs=pl.BlockSpec((tb, d), lambda i, e: (i, 0)),
        out_shape=jax.ShapeDtypeStruct((rows, d), F32),
        scratch_shapes=[pltpu.VMEM((eb, tb), F8), pltpu.VMEM((eb // 2, tb), jnp.uint32)],
        compiler_params=_cparams(("arbitrary", "arbitrary")),
        name="peer_experts",
    )(ht, hs, u8, v8, w_scales, *tables, x2, mod3, g_final.reshape(1, d))


def kernel(x, c, ctx, c_ctx, w_mod, b_mod, g_norm_mix, g_norm_ffn, w_in, b_gate, conv_qk, g_fourier,
           g_mlstm, w_out, w_query, sub_keys, expert_u, expert_v, g_final):
    batch, seq, d = x.shape
    ctx_len = ctx.shape[1]
    depth = w_mod.shape[0]
    f_width = g_fourier.shape[1]
    m_width = g_mlstm.shape[1]
    qk_width = conv_qk.shape[-1]
    n_gates = b_gate.shape[1]
    n_main = f_width + qk_width + 2 * m_width
    assert w_in.shape[2] == n_main + n_gates and n_gates <= LANES
    assert f_width == qk_width == m_width and m_width == M_HEADS * LANES
    assert seq % MLSTM_CHUNK == 0 and ctx_len % MLSTM_CHUNK == 0 and seq % GRID_W == 0

    cond_rows = -(-(batch + 1) // SUBLANES) * SUBLANES
    cond = jnp.zeros((cond_rows, d), F32).at[:batch].set(c).at[batch].set(c_ctx)
    mod_all = _modulation(cond, w_mod, b_mod)

    latent_row = lambda i, tm: (i * tm) // seq
    context_row = lambda i, tm: batch

    x2 = x.reshape(batch * seq, d)
    c2 = ctx.reshape(batch * ctx_len, d)
    v_col0 = qk_width // LANES
    o_col0 = (qk_width + m_width) // LANES
    zero_state = jnp.zeros((batch, M_HEADS, 2, LANES, 2 * LANES), F32)
    zero_m = jnp.zeros((batch, M_HEADS, 2, SUBLANES, LANES), F32)

    for l in range(depth):
        last = l == depth - 1
        mod3 = mod_all[l].reshape(cond_rows, 1, N_MOD * d)
        w_main = w_in[l, :, :n_main].astype(BF16)
        w_gate = jnp.zeros((d, LANES), F32).at[:, :n_gates].set(w_in[l, :, n_main:]).astype(BF16)
        bg = jnp.zeros((1, LANES), F32).at[0, :n_gates].set(b_gate[l])
        w_out_bf = w_out[l].astype(BF16)
        wq_bf = w_query[l].astype(BF16)
        u8, u_inv = _fp8_tensor(expert_u[l])
        v8, v_inv = _fp8_tensor(expert_v[l])
        w_scales = jnp.zeros((1, LANES), F32).at[0, 0].set(u_inv).at[0, 1].set(v_inv)

        def mixer(tokens, mod_row, n_tok, vertical, state, m_state):
            four, p, gates = _in_projection(tokens, g_norm_mix[l], mod3, mod_row, w_main, w_gate, bg)
            qk = _qk_conv(p, conv_qk[l], qk_width, 0, n_tok, vertical)
            ym, state, m_state = _mlstm(qk, p, gates, g_mlstm[l], state, m_state, batch, n_tok,
                                        v_col0, o_col0)
            return four, ym, state, m_state

        def ffn_and_residuals(tokens, four, ym, mod_row, n_tok, final_norm):
            f = _fourier_mix(four, g_fourier[l], batch, n_tok)
            tokens = _out_projection(f, ym, w_out_bf, tokens, mod3, mod_row)
            routed = _peer_route(tokens, g_norm_ffn[l], mod3, mod_row, wq_bf, sub_keys[l])
            return _peer_experts(routed[0], routed[1], u8, v8, w_scales, routed[2:], tokens, mod3, mod_row,
                                 g_final, final_norm)

        pc, ymc, st, m_st = mixer(c2, context_row, ctx_len, False, zero_state, zero_m)
        p, ym, _, _ = mixer(x2, latent_row, seq, True, st, m_st)
        x2 = ffn_and_residuals(x2, p, ym, latent_row, seq, last)
        if not last:
            c2 = ffn_and_residuals(c2, pc, ymc, context_row, ctx_len, False)
    return x2.reshape(batch, seq, d)
```

```python
import functools
import math

import jax
import jax.numpy as jnp
from jax import lax
from jax.experimental import pallas as pl
from jax.experimental.pallas import tpu as pltpu

F32 = jnp.float32
BF16 = jnp.bfloat16

F_GROUPS = 4
M_HEADS = 8
GRID_W = 64
N_KEYS = 128
PEER_HEADS = 8
PEER_TOPK = 16
N_MOD = 6
EPS = 1e-6

LANES = 128
SUBLANES = 8
VMEM_LIMIT = 56 * 1024 * 1024

MLSTM_CHUNK = 128
DFT_N2 = 128
DFT_KB = 8
NEG = -(2.0 ** 110)
REMOVED = 2.0 ** 120
RANK_NONE = 99.0
F8 = jnp.float8_e4m3fn
F8_TARGET = 224.0
TINY = 1e-30


def _cparams(sem):
    return pltpu.CompilerParams(dimension_semantics=sem, vmem_limit_bytes=VMEM_LIMIT)


def _sigmoid(x):
    return 1.0 / (1.0 + jnp.exp(-x))


def _split3(x):
    p0 = x.astype(BF16)
    r1 = x - p0.astype(F32)
    p1 = r1.astype(BF16)
    p2 = (r1 - p1.astype(F32)).astype(BF16)
    return p0, p1, p2


def _dot01_left(a01, x):
    return sum(jnp.dot(a01, p, preferred_element_type=F32) for p in _split3(x))


def _dot01_right(x, b01):
    return sum(jnp.dot(p, b01, preferred_element_type=F32) for p in _split3(x))


def _mod_kernel(c_ref, w_ref, b_ref, o_ref):
    c = c_ref[...]
    s = (c * _sigmoid(c)).astype(BF16)
    o_ref[0] = jnp.dot(s, w_ref[0].astype(BF16), preferred_element_type=F32) + b_ref[0]


def _modulation(cond, w_mod, b_mod):
    n_layers, d, n_out = w_mod.shape
    rows = cond.shape[0]
    tn = 1024
    return pl.pallas_call(
        _mod_kernel,
        grid=(n_layers, n_out // tn),
        in_specs=[
            pl.BlockSpec((rows, d), lambda l, j: (0, 0)),
            pl.BlockSpec((1, d, tn), lambda l, j: (l, 0, j)),
            pl.BlockSpec((1, 1, tn), lambda l, j: (l, 0, j)),
        ],
        out_specs=pl.BlockSpec((1, rows, tn), lambda l, j: (l, 0, j)),
        out_shape=jax.ShapeDtypeStruct((n_layers, rows, n_out), F32),
        compiler_params=_cparams(("arbitrary", "arbitrary")),
        name="modulation",
    )(cond, w_mod, b_mod.reshape(n_layers, 1, n_out))


def _rms_modulate(x, g, shift, scale):
    ms = jnp.mean(x * x, axis=-1, keepdims=True)
    return (x * lax.rsqrt(ms + EPS) * g) * (1.0 + scale) + shift


def _inproj_kernel(x_ref, g_ref, sh_ref, sc_ref, w_ref, wg_ref, bg_ref, four_ref, p_ref, gate_ref, h_scr):
    j = pl.program_id(1)

    @pl.when(j == 0)
    def _():
        h = _rms_modulate(x_ref[...], g_ref[...], sh_ref[0], sc_ref[0]).astype(BF16)
        h_scr[...] = h
        gate_ref[...] = jnp.dot(h, wg_ref[...], preferred_element_type=F32) + bg_ref[...]

    acc = jnp.dot(h_scr[...], w_ref[...], preferred_element_type=F32)

    @pl.when(j == 0)
    def _():
        four_ref[...] = acc

    @pl.when(j > 0)
    def _():
        p_ref[...] = acc.astype(BF16)


def _in_projection(x2, g, mod3, mod_row, w_main, w_gate, b_gate):
    rows, d = x2.shape
    n_main = w_main.shape[1]
    tm = min(1024, rows)
    tn = 1024
    return pl.pallas_call(
        _inproj_kernel,
        grid=(rows // tm, n_main // tn),
        in_specs=[
            pl.BlockSpec((tm, d), lambda i, j: (i, 0)),
            pl.BlockSpec((1, d), lambda i, j: (0, 0)),
            pl.BlockSpec((1, 1, d), lambda i, j: (mod_row(i, tm), 0, 0)),
            pl.BlockSpec((1, 1, d), lambda i, j: (mod_row(i, tm), 0, 1)),
            pl.BlockSpec((d, tn), lambda i, j: (0, j)),
            pl.BlockSpec((d, LANES), lambda i, j: (0, 0)),
            pl.BlockSpec((1, LANES), lambda i, j: (0, 0)),
        ],
        out_specs=[
            pl.BlockSpec((tm, tn), lambda i, j: (i, 0)),
            pl.BlockSpec((tm, tn), lambda i, j: (i, jnp.maximum(j - 1, 0))),
            pl.BlockSpec((tm, LANES), lambda i, j: (i, 0)),
        ],
        out_shape=[
            jax.ShapeDtypeStruct((rows, tn), F32),
            jax.ShapeDtypeStruct((rows, n_main - tn), BF16),
            jax.ShapeDtypeStruct((rows, LANES), F32),
        ],
        scratch_shapes=[pltpu.VMEM((tm, d), BF16)],
        compiler_params=_cparams(("arbitrary", "arbitrary")),
        name="in_projection",
    )(x2, g.reshape(1, d), mod3, mod3, w_main, w_gate, b_gate)


def _conv_kernel(*refs, tb, width, vertical, blocks_per_image):
    if vertical:
        cur_ref, top_ref, bot_ref, w_ref, o_ref = refs
    else:
        cur_ref, w_ref, o_ref = refs
    cur = cur_ref[...].astype(F32)
    ch = cur.shape[1]
    wpos = lax.rem(lax.broadcasted_iota(jnp.int32, (tb, ch), 0), width)
    first_col = wpos == 0
    last_col = wpos == width - 1
    if vertical:
        r = lax.rem(pl.program_id(0), blocks_per_image)
        top = jnp.where(r == 0, 0.0, top_ref[...].astype(F32))
        bot = jnp.where(r == blocks_per_image - 1, 0.0, bot_ref[...].astype(F32))
        ext = jnp.concatenate([top, cur, bot], axis=0)
        bases = [(dr, ext[dr * width:dr * width + tb]) for dr in range(3)]
    else:
        bases = [(1, cur)]
    acc = jnp.zeros((tb, ch), F32)
    for dr, base in bases:
        left = jnp.where(first_col, 0.0, pltpu.roll(base, 1, axis=0))
        right = jnp.where(last_col, 0.0, pltpu.roll(base, tb - 1, axis=0))
        for dw, shifted in enumerate((left, base, right)):
            k = dr * 3 + dw
            acc = acc + shifted * w_ref[k:k + 1, :]
    o_ref[...] = (acc * _sigmoid(acc)).astype(BF16)


def _qk_conv(p, conv_w, qk_width, col_block, tokens_per_image, vertical):
    rows = p.shape[0]
    w9 = conv_w.reshape(9, qk_width)
    if vertical:
        width = GRID_W
        tb = min(512, tokens_per_image)
        bpi = tokens_per_image // tb
        halo = tb // width
        n_halo = rows // width
        in_specs = [
            pl.BlockSpec((tb, qk_width), lambda i: (i, col_block)),
            pl.BlockSpec((width, qk_width), lambda i: (jnp.maximum(i * halo - 1, 0), col_block)),
            pl.BlockSpec((width, qk_width), lambda i: (jnp.minimum((i + 1) * halo, n_halo - 1), col_block)),
            pl.BlockSpec((9, qk_width), lambda i: (0, 0)),
        ]
        args = (p, p, p, w9)
    else:
        width = tb = tokens_per_image
        bpi = 1
        in_specs = [
            pl.BlockSpec((tb, qk_width), lambda i: (i, col_block)),
            pl.BlockSpec((9, qk_width), lambda i: (0, 0)),
        ]
        args = (p, w9)
    return pl.pallas_call(
        functools.partial(_conv_kernel, tb=tb, width=width, vertical=vertical, blocks_per_image=bpi),
        grid=(rows // tb,),
        in_specs=in_specs,
        out_specs=pl.BlockSpec((tb, qk_width), lambda i: (i, 0)),
        out_shape=jax.ShapeDtypeStruct((rows, qk_width), BF16),
        compiler_params=_cparams(("arbitrary",)),
        name="qk_conv_latent" if vertical else "qk_conv_context",
    )(*args)


def _mlstm_chunks(chains):
    n = range(len(chains))
    ch = chains
    L = ch[0]["q"].shape[0]
    logsig = [jnp.minimum(c["gch"], 0.0) - jnp.log(1.0 + jnp.exp(-jnp.abs(c["gch"]))) for c in ch]
    ig_p = [_split3(c["gch"]) for c in ch]
    lf_p = [_split3(logsig[i]) for i in n]
    ig = [sum(jnp.dot(p, ch[i]["sel_i"], preferred_element_type=F32) for p in ig_p[i]) for i in n]
    lf = [sum(jnp.dot(p, ch[i]["sel_f"], preferred_element_type=F32) for p in lf_p[i]) for i in n]
    b_p = [_split3(lf[i]) for i in n]
    b = [sum(jnp.dot(ch[i]["cum"], p, preferred_element_type=F32) for p in b_p[i]) for i in n]
    b_end = [b[i][0:1, :] if ch[i]["reverse"] else b[i][L - 1:L, :] for i in n]
    a_t = [(ig[i] - b[i]).T for i in n]
    dmat = [jnp.where(ch[i]["causal"], b[i] + a_t[i], NEG) for i in n]
    g = [b[i] + ch[i]["m"] for i in n]
    m_j = [jnp.maximum(g[i], jnp.max(dmat[i], axis=-1, keepdims=True)) for i in n]
    pmat = [jnp.exp(dmat[i] - m_j[i]) for i in n]
    s_raw = [lax.dot_general(c["q"], c["k"], (((1,), (1,)), ((), ())), preferred_element_type=F32)
             for c in ch]
    s = [(s_raw[i] * pmat[i]).astype(BF16) for i in n]
    intra = [jnp.dot(s[i], ch[i]["vaug"], preferred_element_type=F32) for i in n]
    carried = [jnp.dot(c["q"], c["state"].astype(BF16), preferred_element_type=F32) for c in ch]
    inter = [jnp.exp(g[i] - m_j[i]) for i in n]
    tot = [intra[i] + jnp.concatenate([inter[i], inter[i]], axis=1) * carried[i] for i in n]
    h = [tot[i][:, :LANES] / jnp.maximum(jnp.abs(tot[i][:, LANES:]), jnp.exp(-m_j[i])) for i in n]
    a = [b_end[i] - b[i] + ig[i] for i in n]
    m_new = [jnp.maximum(b_end[i] + ch[i]["m"], jnp.max(a[i], axis=0, keepdims=True)) for i in n]
    kw = [(ch[i]["k"].astype(F32) * jnp.exp(a[i] - m_new[i])).astype(BF16) for i in n]
    f_old = [jnp.exp(b_end[i] + ch[i]["m"] - m_new[i]) for i in n]
    upd = [pl.dot(kw[i], ch[i]["vaug"], trans_a=True) for i in n]
    state_new = [jnp.concatenate([f_old[i], f_old[i]], axis=1) * ch[i]["state"] + upd[i] for i in n]
    return [(h[i], state_new[i], m_new[i]) for i in n]


def _mlstm_kernel(q_ref, k_ref, v_ref, o_ref, gt_ref, g_ref, sin_ref, min_ref,
                  y_ref, sout_ref, mout_ref, hf_scr, hb_scr, *, seq, k_scale):
    L = MLSTM_CHUNK
    nc = seq // L
    pair = pl.program_id(1)
    lane = lax.broadcasted_iota(jnp.int32, (1, LANES), 1)
    rr = lax.broadcasted_iota(jnp.int32, (L, L), 0)
    cc = lax.broadcasted_iota(jnp.int32, (L, L), 1)
    causal = (rr >= cc, rr <= cc)
    cum = tuple(jnp.where(m, 1.0, 0.0).astype(BF16) for m in causal)
    ones = jnp.ones((L, LANES), BF16)
    qmask, kmask, sel = [], [], []
    for j in range(2):
        own = (lane // (LANES // 2)) == j
        qmask.append(jnp.where(own, 1.0, 0.0).astype(BF16))
        kmask.append(jnp.where(own, k_scale, 0.0).astype(BF16))
        head = 2 * pair + j
        sel.append([jnp.where(rr == kind * M_HEADS + head, 1.0, 0.0).astype(BF16) for kind in range(4)])

    def body(c, carry):
        chains, dest = [], []
        for d in range(2):
            rows = pl.ds(pl.multiple_of((c if d == 0 else nc - 1 - c) * L, L), L)
            q_all, k_all, gch = q_ref[rows, :], k_ref[rows, :], gt_ref[rows, :]
            for j in range(2):
                hcols = slice(j * LANES, (j + 1) * LANES)
                idx = 2 * (2 * j + d)
                chains.append(dict(
                    q=q_all * qmask[j], k=k_all * kmask[j],
                    vaug=jnp.concatenate([v_ref[rows, hcols], ones], axis=1), gch=gch,
                    sel_i=sel[j][2 * d], sel_f=sel[j][2 * d + 1], cum=cum[d], causal=causal[d],
                    reverse=d == 1, state=carry[idx], m=carry[idx + 1]))
                dest.append((hf_scr if d == 0 else hb_scr, rows, hcols, idx))
        carry = list(carry)
        for (scr, rows, hcols, idx), (h, state, m) in zip(dest, _mlstm_chunks(chains)):
            scr[rows, hcols] = h.astype(BF16)
            carry[idx], carry[idx + 1] = state, m
        return tuple(carry)

    init = []
    for j in range(2):
        for d in range(2):
            init += [sin_ref[0, j, d], min_ref[0, j, d][0:1, :]]
    final = lax.fori_loop(0, nc, body, tuple(init))
    for j in range(2):
        for d in range(2):
            idx = 2 * (2 * j + d)
            sout_ref[0, j, d] = final[idx]
            mout_ref[0, j, d] = jnp.broadcast_to(final[idx + 1], (SUBLANES, LANES))

    def finish(c, _):
        rows = pl.ds(pl.multiple_of(c * L, L), L)
        for j in range(2):
            hcols = slice(j * LANES, (j + 1) * LANES)
            h = hf_scr[rows, hcols].astype(F32) + hb_scr[rows, hcols].astype(F32)
            ms = jnp.mean(h * h, axis=-1, keepdims=True)
            y = h * lax.rsqrt(ms + EPS) * g_ref[:, hcols]
            y_ref[rows, hcols] = (y * _sigmoid(o_ref[rows, hcols].astype(F32))).astype(BF16)
        return 0

    lax.fori_loop(0, nc, finish, 0)


def _mlstm(qk, p, gates, g_mlstm, state_in, m_in, batch, seq, v_col0, o_col0):
    rows = qk.shape[0]
    pw = 2 * LANES
    k_col0 = qk.shape[1] // (2 * LANES)
    dk = qk.shape[1] // (2 * M_HEADS)
    assert v_col0 % 2 == 0 and o_col0 % 2 == 0
    return pl.pallas_call(
        functools.partial(_mlstm_kernel, seq=seq, k_scale=dk ** -0.5),
        grid=(batch, M_HEADS // 2),
        in_specs=[
            pl.BlockSpec((seq, LANES), lambda b, h: (b, h)),
            pl.BlockSpec((seq, LANES), lambda b, h: (b, k_col0 + h)),
            pl.BlockSpec((seq, pw), lambda b, h: (b, v_col0 // 2 + h)),
            pl.BlockSpec((seq, pw), lambda b, h: (b, o_col0 // 2 + h)),
            pl.BlockSpec((seq, LANES), lambda b, h: (b, 0)),
            pl.BlockSpec((1, pw), lambda b, h: (0, h)),
            pl.BlockSpec((1, 2, 2, LANES, 2 * LANES), lambda b, h: (b, h, 0, 0, 0)),
            pl.BlockSpec((1, 2, 2, SUBLANES, LANES), lambda b, h: (b, h, 0, 0, 0)),
        ],
        out_specs=[
            pl.BlockSpec((seq, pw), lambda b, h: (b, h)),
            pl.BlockSpec((1, 2, 2, LANES, 2 * LANES), lambda b, h: (b, h, 0, 0, 0)),
            pl.BlockSpec((1, 2, 2, SUBLANES, LANES), lambda b, h: (b, h, 0, 0, 0)),
        ],
        out_shape=[
            jax.ShapeDtypeStruct((rows, M_HEADS * LANES), BF16),
            jax.ShapeDtypeStruct(state_in.shape, F32),
            jax.ShapeDtypeStruct(m_in.shape, F32),
        ],
        scratch_shapes=[pltpu.VMEM((seq, pw), BF16), pltpu.VMEM((seq, pw), BF16)],
        compiler_params=_cparams(("arbitrary", "arbitrary")),
        name="mlstm",
    )(qk, qk, p, p, gates, g_mlstm.reshape(1, -1), state_in, m_in)


def _dft_mats(n, scale):
    idx = jnp.arange(n, dtype=jnp.int32)
    ang = (2.0 * math.pi / n) * ((idx[:, None] * idx[None, :]) % n).astype(F32)
    return jnp.cos(ang) * scale, jnp.sin(ang) * scale


def _channel_dft(u, cc_ref, sc_ref):
    gw = cc_ref.shape[0]
    zr, zi = [], []
    for g in range(u.shape[1] // gw):
        ug = u[:, g * gw:(g + 1) * gw]
        zr.append(jnp.dot(ug, cc_ref[...], preferred_element_type=F32))
        zi.append(jnp.dot(ug, sc_ref[...], preferred_element_type=F32))
    return jnp.concatenate(zr, axis=1), jnp.concatenate(zi, axis=1)


def _group_rmsnorm(y, g, gw):
    outs = []
    for k in range(y.shape[1] // gw):
        yk = y[:, k * gw:(k + 1) * gw]
        ms = jnp.mean(yk * yk, axis=-1, keepdims=True)
        outs.append(yk * lax.rsqrt(ms + EPS) * g[:, k * gw:(k + 1) * gw])
    return jnp.concatenate(outs, axis=1)


def _fnet_a_kernel(u_ref, cc_ref, sc_ref, f_ref, cw_ref, sw_ref, yr_ref, yi_ref):
    fw = u_ref.shape[3]
    for r in range(u_ref.shape[2]):
        zr, zi = _channel_dft(u_ref[0, :, r, :].astype(BF16), cc_ref, sc_ref)
        z = jnp.concatenate([zr, zi], axis=0).astype(BF16)
        y = jnp.dot(f_ref[...], z, preferred_element_type=F32)
        n2 = y.shape[0] // 2
        yr, yi = y[:n2], y[n2:]
        cw, sw = cw_ref[r], sw_ref[r]
        for j in range(fw // LANES):
            cols = slice(j * LANES, (j + 1) * LANES)
            ocols = slice(r * fw + j * LANES, r * fw + (j + 1) * LANES)
            yr_ref[0, :, ocols] = (yr[:, cols] * cw + yi[:, cols] * sw).astype(BF16)
            yi_ref[0, :, ocols] = (yi[:, cols] * cw - yr[:, cols] * sw).astype(BF16)


def _fnet_b_kernel(yr_ref, yi_ref, gc_ref, gs_ref, g_ref, o_ref, *, gw):
    y = (jnp.dot(gc_ref[...], yr_ref[0], preferred_element_type=F32)
         + jnp.dot(gs_ref[...], yi_ref[0], preferred_element_type=F32))
    out = _group_rmsnorm(y, g_ref[...], gw).astype(BF16)
    o_ref[0] = out.reshape(o_ref.shape[1:])


def _fnet_direct_kernel(u_ref, cc_ref, sc_ref, ct_ref, st_ref, g_ref, o_ref, *, gw):
    zr, zi = _channel_dft(u_ref[...].astype(BF16), cc_ref, sc_ref)
    y = (jnp.dot(ct_ref[...], zr.astype(BF16), preferred_element_type=F32)
         + jnp.dot(st_ref[...], zi.astype(BF16), preferred_element_type=F32))
    o_ref[...] = _group_rmsnorm(y, g_ref[...], gw).astype(BF16)


def _fourier_mix(p, g_fourier, batch, seq):
    f_width = p.shape[1]
    gw = f_width // F_GROUPS
    cc, sc = _dft_mats(gw, gw ** -0.5)
    cc, msc = cc.astype(BF16), (-sc).astype(BF16)
    g2 = g_fourier.reshape(1, f_width)
    rows = batch * seq
    if seq <= 512:
        ct, st = _dft_mats(seq, seq ** -0.5)
        return pl.pallas_call(
            functools.partial(_fnet_direct_kernel, gw=gw),
            grid=(batch,),
            in_specs=[
                pl.BlockSpec((seq, f_width), lambda b: (b, 0)),
                pl.BlockSpec((gw, gw), lambda b: (0, 0)),
                pl.BlockSpec((gw, gw), lambda b: (0, 0)),
                pl.BlockSpec((seq, seq), lambda b: (0, 0)),
                pl.BlockSpec((seq, seq), lambda b: (0, 0)),
                pl.BlockSpec((1, f_width), lambda b: (0, 0)),
            ],
            out_specs=pl.BlockSpec((seq, f_width), lambda b: (b, 0)),
            out_shape=jax.ShapeDtypeStruct((rows, f_width), BF16),
            compiler_params=_cparams(("arbitrary",)),
            name="fourier_direct",
        )(p, cc, msc, ct.astype(BF16), st.astype(BF16), g2)

    n2 = DFT_N2
    n1 = seq // n2
    c2, s2 = _dft_mats(n2, n2 ** -0.5)
    fmat = jnp.concatenate([jnp.concatenate([c2, s2], axis=1),
                            jnp.concatenate([-s2, c2], axis=1)], axis=0).astype(BF16)
    i1 = jnp.arange(n1, dtype=jnp.int32)
    i2 = jnp.arange(n2, dtype=jnp.int32)
    tw = (2.0 * math.pi / seq) * (i1[:, None] * i2[None, :]).astype(F32)
    cw = jnp.broadcast_to(jnp.cos(tw)[:, :, None], (n1, n2, LANES))
    sw = jnp.broadcast_to(jnp.sin(tw)[:, :, None], (n1, n2, LANES))
    nr = SUBLANES
    yr, yi = pl.pallas_call(
        _fnet_a_kernel,
        grid=(batch, n1 // nr),
        in_specs=[
            pl.BlockSpec((1, n2, nr, f_width), lambda b, j: (b, 0, j, 0)),
            pl.BlockSpec((gw, gw), lambda b, j: (0, 0)),
            pl.BlockSpec((gw, gw), lambda b, j: (0, 0)),
            pl.BlockSpec((2 * n2, 2 * n2), lambda b, j: (0, 0)),
            pl.BlockSpec((nr, n2, LANES), lambda b, j: (j, 0, 0)),
            pl.BlockSpec((nr, n2, LANES), lambda b, j: (j, 0, 0)),
        ],
        out_specs=[
            pl.BlockSpec((1, n2, nr * f_width), lambda b, j: (b, 0, j)),
            pl.BlockSpec((1, n2, nr * f_width), lambda b, j: (b, 0, j)),
        ],
        out_shape=[jax.ShapeDtypeStruct((batch, n2, n1 * f_width), BF16)] * 2,
        compiler_params=_cparams(("arbitrary", "arbitrary")),
        name="fourier_stage_a",
    )(p.reshape(batch, n2, n1, f_width), cc, msc, fmat, cw, sw)

    kb = DFT_KB
    c1, s1 = _dft_mats(n1, n1 ** -0.5)
    eye = jnp.eye(kb, dtype=F32)
    gc = jnp.einsum("kn,ab->kabn", c1, eye).reshape(n1 * kb, kb * n1).astype(BF16)
    gs = jnp.einsum("kn,ab->kabn", s1, eye).reshape(n1 * kb, kb * n1).astype(BF16)
    out = pl.pallas_call(
        functools.partial(_fnet_b_kernel, gw=gw),
        grid=(batch, n2 // kb),
        in_specs=[
            pl.BlockSpec((1, kb * n1, f_width), lambda b, j: (b, j, 0)),
            pl.BlockSpec((1, kb * n1, f_width), lambda b, j: (b, j, 0)),
            pl.BlockSpec((n1 * kb, kb * n1), lambda b, j: (0, 0)),
            pl.BlockSpec((n1 * kb, kb * n1), lambda b, j: (0, 0)),
            pl.BlockSpec((1, f_width), lambda b, j: (0, 0)),
        ],
        out_specs=pl.BlockSpec((1, n1, kb, f_width), lambda b, j: (b, 0, j, 0)),
        out_shape=jax.ShapeDtypeStruct((batch, n1, n2, f_width), BF16),
        compiler_params=_cparams(("arbitrary", "arbitrary")),
        name="fourier_stage_b",
    )(yr.reshape(batch, n2 * n1, f_width), yi.reshape(batch, n2 * n1, f_width), gc, gs, g2)
    return out.reshape(rows, f_width)


def _outproj_kernel(f_ref, ym_ref, w_ref, x_ref, ga_ref, o_ref):
    fw = f_ref.shape[1]
    y = (jnp.dot(f_ref[...], w_ref[:fw, :], preferred_element_type=F32)
         + jnp.dot(ym_ref[...], w_ref[fw:, :], preferred_element_type=F32))
    o_ref[...] = x_ref[...] + ga_ref[0] * y


def _out_projection(f, ym, w_out, x2, mod3, mod_row):
    rows, d = x2.shape
    tm = min(512, rows)
    return pl.pallas_call(
        _outproj_kernel,
        grid=(rows // tm,),
        in_specs=[
            pl.BlockSpec((tm, f.shape[1]), lambda i: (i, 0)),
            pl.BlockSpec((tm, ym.shape[1]), lambda i: (i, 0)),
            pl.BlockSpec(w_out.shape, lambda i: (0, 0)),
            pl.BlockSpec((tm, d), lambda i: (i, 0)),
            pl.BlockSpec((1, 1, d), lambda i: (mod_row(i, tm), 0, 2)),
        ],
        out_specs=pl.BlockSpec((tm, d), lambda i: (i, 0)),
        out_shape=jax.ShapeDtypeStruct((rows, d), F32),
        compiler_params=_cparams(("arbitrary",)),
        name="out_projection",
    )(f, ym, w_out, x2, mod3)


def _pack_bf16_pair(lo, hi):
    def bf16_bits(x):
        u = lax.bitcast_convert_type(x, jnp.uint32)
        return (u + (jnp.uint32(0x7FFF) + ((u >> 16) & jnp.uint32(1)))) >> 16
    return bf16_bits(lo) | (bf16_bits(hi) << 16)


def _interleaved_key(q, n):
    half = n // 2
    return jnp.where(q < half, 2 * q, 2 * (q - half) + 1)


def _extract_top(arrays, k, exact, interleaved=()):
    idx = range(len(arrays))
    s = list(arrays)
    n = [x.shape[0] for x in s]
    row = [None] * len(s)
    if exact:
        for i in idx:
            row[i] = lax.broadcasted_iota(jnp.int32, s[i].shape, 0)
            if i < len(interleaved) and interleaved[i]:
                row[i] = _interleaved_key(row[i], n[i])
    vals = [[] for _ in s]
    for it in range(k):
        m = [jnp.max(s[i], axis=0, keepdims=True) for i in idx]
        hit = [s[i] == m[i] for i in idx]
        if exact:
            first = [jnp.min(jnp.where(hit[i], row[i], n[i]), axis=0, keepdims=True) for i in idx]
            hit = [row[i] == first[i] for i in idx]
        s = [jnp.where(hit[i], -(it + 1) * REMOVED, s[i]) for i in idx]
        for i in idx:
            vals[i].append(m[i])
    gone = [s[i] <= -REMOVED for i in idx]
    rank = [jnp.where(gone[i], s[i] * (-1.0 / REMOVED) - 1.0, RANK_NONE) for i in idx]
    removed = [jnp.sum(jnp.where(gone[i], 1.0, 0.0), axis=0, keepdims=True) for i in idx]
    return [(vals[i], rank[i], removed[i]) for i in idx]


def _route_chunk(s0, s1, exact):
    k = PEER_TOPK
    sub = SUBLANES
    (v0, rank0, rem0), (v1, rank1, rem1) = _extract_top([s0, s1], k, exact, interleaved=(False, True))
    sv0 = jnp.concatenate(v0, axis=0)
    sv1 = jnp.concatenate(v1, axis=0)
    if exact:
        layout = [("row", a, b0) for a in range(k) for b0 in range(0, max(k // (a + 1), 1), sub)]
    else:
        layout = ([("row", 0, 0), ("row", 0, sub), ("row", 1, 0), ("col", 0, sub)]
                  + [("row", a, 0) for a in range(2, k // 3)]
                  + [("col", b, 0) for b in range(k // sub)])
    seen = set()
    blocks = []
    for kind, fixed, start in layout:
        cells = [(fixed, start + r) if kind == "row" else (start + r, fixed) for r in range(sub)]
        keep = [(a + 1) * (b + 1) <= k and (a, b) not in seen for a, b in cells]
        seen.update(cell for cell, kp in zip(cells, keep) if kp)
        blk = (v0[fixed] + sv1[start:start + sub]) if kind == "row" else (sv0[start:start + sub] + v1[fixed])
        if not all(keep):
            brow = lax.broadcasted_iota(jnp.int32, blk.shape, 0)
            mask = functools.reduce(jnp.logical_or, [brow == r for r, kp in enumerate(keep) if kp])
            blk = jnp.where(mask, blk, NEG)
        blocks.append(blk)
    assert len(seen) == sum(k // (a + 1) for a in range(k))
    cand = jnp.concatenate(blocks, axis=0)
    (_, crank, rem2), = _extract_top([cand], k, exact)
    cnt = jnp.where(crank < RANK_NONE, 1.0, 0.0)
    z = jnp.sum(cnt * jnp.exp(cand - (v0[0] + v1[0])), axis=0, keepdims=True)
    arow = lax.broadcasted_iota(jnp.int32, sv0.shape, 0)
    n_rank = jnp.zeros(sv0.shape, F32)
    for idx, (kind, fixed, start) in enumerate(layout):
        c_blk = cnt[idx * sub:(idx + 1) * sub]
        if kind == "row":
            n_rank = n_rank + jnp.where(arow == fixed, jnp.sum(c_blk, axis=0, keepdims=True), 0.0)
        else:
            pieces = [c_blk if a0 == start else jnp.zeros_like(c_blk) for a0 in range(0, k, sub)]
            n_rank = n_rank + jnp.concatenate(pieces, axis=0)
    n_of_key = jnp.zeros(rank0.shape, F32)
    for a in range(k):
        n_of_key = jnp.where(rank0 == float(a), n_rank[a:a + 1], n_of_key)
    tie = (jnp.where(rem0 == k, 0.0, 1.0) + jnp.where(rem1 == k, 0.0, 1.0)
           + jnp.where(rem2 == k, 0.0, 1.0))
    return jnp.exp(s0 - v0[0]) / z, n_of_key, jnp.exp(s1 - v1[0]), rank1, tie


def _peer_route_kernel(x_ref, g_ref, sh_ref, sc_ref, wq_ref, keys_ref,
                       ht_ref, hs_ref, e0_ref, nn_ref, e1_ref, r1_ref, s_scr):
    h2 = _rms_modulate(x_ref[...], g_ref[...], sh_ref[0], sc_ref[0])
    ht = h2.T
    amax = jnp.maximum(jnp.max(jnp.abs(ht), axis=0, keepdims=True), TINY)
    ht_ref[...] = (ht * (F8_TARGET / amax)).astype(F8)
    hs_ref[...] = amax * (1.0 / F8_TARGET)
    q = jnp.dot(h2.astype(BF16), wq_ref[...], preferred_element_type=F32)
    tb = q.shape[0]
    kd = keys_ref.shape[-1]
    for hp in range(2 * PEER_HEADS):
        s_scr[hp] = lax.dot_general(keys_ref[hp // 2, hp % 2], q[:, hp * kd:(hp + 1) * kd],
                                    (((1,), (1,)), ((), ())),
                                    precision=lax.Precision.HIGHEST, preferred_element_type=F32)

    def route_head(h, _):
        for c in range(tb // LANES):
            cols = slice(c * LANES, (c + 1) * LANES)

            def route(exact):
                e0, n_of_key, e1, rank1, tie = _route_chunk(s_scr[2 * h, :, cols],
                                                            s_scr[2 * h + 1, :, cols], exact)
                half = e1.shape[0] // 2
                e0_ref[h, :, cols] = _pack_bf16_pair(e0, e0)
                nn_ref[h, :, cols] = _pack_bf16_pair(n_of_key, n_of_key)
                e1_ref[h, :, cols] = _pack_bf16_pair(e1[:half], e1[half:])
                r1_ref[h, :, cols] = _pack_bf16_pair(rank1[:half], rank1[half:])
                return tie

            tie = route(False)

            @pl.when(jnp.max(tie) > 0.0)
            def _():
                route(True)
        return 0

    lax.fori_loop(0, PEER_HEADS, route_head, 0)


def _peer_route(x2, g, mod3, mod_row, wq, sub_keys):
    rows, d = x2.shape
    tb = min(256, rows)
    nk = sub_keys.shape[2]
    order = jnp.concatenate([jnp.arange(0, nk, 2), jnp.arange(1, nk, 2)])
    sub_keys = sub_keys.at[:, 1].set(sub_keys[:, 1][:, order])
    tab_spec = pl.BlockSpec((PEER_HEADS, nk, tb), lambda i: (0, 0, i))
    half_spec = pl.BlockSpec((PEER_HEADS, nk // 2, tb), lambda i: (0, 0, i))
    tab = jax.ShapeDtypeStruct((PEER_HEADS, nk, rows), jnp.uint32)
    half_tab = jax.ShapeDtypeStruct((PEER_HEADS, nk // 2, rows), jnp.uint32)
    return pl.pallas_call(
        _peer_route_kernel,
        grid=(rows // tb,),
        in_specs=[
            pl.BlockSpec((tb, d), lambda i: (i, 0)),
            pl.BlockSpec((1, d), lambda i: (0, 0)),
            pl.BlockSpec((1, 1, d), lambda i: (mod_row(i, tb), 0, 3)),
            pl.BlockSpec((1, 1, d), lambda i: (mod_row(i, tb), 0, 4)),
            pl.BlockSpec(wq.shape, lambda i: (0, 0)),
            pl.BlockSpec(sub_keys.shape, lambda i: (0, 0, 0, 0)),
        ],
        out_specs=[pl.BlockSpec((d, tb), lambda i: (0, i)), pl.BlockSpec((1, tb), lambda i: (0, i)),
                   tab_spec, tab_spec, half_spec, half_spec],
        out_shape=[jax.ShapeDtypeStruct((d, rows), F8), jax.ShapeDtypeStruct((1, rows), F32),
                   tab, tab, half_tab, half_tab],
        scratch_shapes=[pltpu.VMEM((2 * PEER_HEADS, nk, tb), F32)],
        compiler_params=_cparams(("arbitrary",)),
        name="peer_route",
    )(x2, g.reshape(1, d), mod3, mod3, wq, sub_keys)


def _gelu_tanh(x):
    return 0.5 * x * (1.0 + jnp.tanh(math.sqrt(2.0 / math.pi) * (x + 0.044715 * (x * x * x))))


def _peer_expert_kernel(ht_ref, hs_ref, u_ref, v_ref, ws_ref, e0_ref, nn_ref, e1_ref, r1_ref,
                        x_ref, ga_ref, gf_ref, o_ref, wa_even, wa_odd, sc_even, sc_odd, act_scr,
                        *, final_norm, n_blocks):
    e = pl.program_id(1)
    nk = nn_ref.shape[1]
    d, tb = ht_ref.shape
    ib = u_ref.shape[0] // nk
    sub = SUBLANES
    tile = 2 * LANES
    assert ib == SUBLANES
    i_rows = pl.ds(pl.multiple_of(jnp.minimum(e, n_blocks - 1) * ib, ib), ib)
    n_words = e1_ref.shape[1]
    n_chunks = ib * (tb // LANES)
    n_pieces = d // tile

    def packed(words):
        return pltpu.bitcast(words, BF16)

    @pl.when(e == 0)
    def _():
        o_ref[...] = jnp.zeros(o_ref.shape, F32)
        wa_odd[...] = jnp.zeros(wa_odd.shape, F8)
        sc_odd[...] = jnp.zeros(sc_odd.shape, F32)

    def step(front, wa_cur, sc_cur, wa_prev, sc_prev):
        def back_piece(nt):
            ncols = slice(nt * tile, (nt + 1) * tile)
            o_ref[:, ncols] += (pl.dot(wa_prev[...], v_ref[:, ncols], trans_a=True)
                                * (sc_prev[0:1, 0:1] * ws_ref[0:1, 1:2]))

        done = 0
        if front:
            act = (jnp.dot(u_ref[...], ht_ref[...], preferred_element_type=F32)
                   * (hs_ref[...] * ws_ref[0:1, 0:1])).astype(BF16)
            act_scr[...] = pltpu.bitcast(act, jnp.uint32)
            amax = jnp.max(jnp.max(jnp.abs(act), axis=0, keepdims=True), axis=1, keepdims=True)
            bound = PEER_HEADS * jnp.maximum(amax.astype(F32), TINY)
            to_f8 = (F8_TARGET / bound).astype(BF16)
            sc_cur[...] = jnp.broadcast_to(1.0 / to_f8.astype(F32), sc_cur.shape)
            for il in range(ib):
                for c in range(tb // LANES):
                    cols = slice(c * LANES, (c + 1) * LANES)
                    w = [None] * (n_words // sub)
                    for h in range(PEER_HEADS):
                        n_i = packed(jnp.broadcast_to(nn_ref[h, i_rows, cols][il:il + 1], (sub, LANES)))
                        e0_i = packed(jnp.broadcast_to(e0_ref[h, i_rows, cols][il:il + 1], (sub, LANES)))
                        for s in range(n_words // sub):
                            rows = slice(s * sub, (s + 1) * sub)
                            e1 = packed(e1_ref[h, rows, cols])
                            t = jnp.where(packed(r1_ref[h, rows, cols]) < n_i, e1,
                                          jnp.zeros_like(e1)) * e0_i
                            w[s] = t if w[s] is None else w[s] + t
                    a = packed(act_scr[il * n_words:(il + 1) * n_words, cols])
                    wa = jnp.concatenate(w, axis=0) * _gelu_tanh(a) * to_f8
                    wa_cur[il * nk:(il + 1) * nk, cols] = wa.astype(F8)
                    due = ((il * (tb // LANES) + c + 1) * n_pieces) // n_chunks
                    while done < due:
                        back_piece(done)
                        done += 1
        while done < n_pieces:
            back_piece(done)
            done += 1

    even = lax.rem(e, 2) == 0

    @pl.when(jnp.logical_and(e < n_blocks, even))
    def _():
        step(True, wa_even, sc_even, wa_odd, sc_odd)

    @pl.when(jnp.logical_and(e < n_blocks, jnp.logical_not(even)))
    def _():
        step(True, wa_odd, sc_odd, wa_even, sc_even)

    @pl.when(e == n_blocks)
    def _():
        if n_blocks % 2 == 0:
            step(False, wa_even, sc_even, wa_odd, sc_odd)
        else:
            step(False, wa_odd, sc_odd, wa_even, sc_even)
        y = x_ref[...] + ga_ref[0] * o_ref[...]
        if final_norm:
            ms = jnp.mean(y * y, axis=-1, keepdims=True)
            y = y * lax.rsqrt(ms + EPS) * gf_ref[...]
        o_ref[...] = y


def _fp8_tensor(w):
    amax = jnp.maximum(jnp.max(jnp.abs(w)), TINY)
    return (w * (F8_TARGET / amax)).astype(F8), amax * (1.0 / F8_TARGET)


def _peer_experts(ht, hs, u8, v8, w_scales, tables, x2, mod3, mod_row, g_final, final_norm):
    rows, d = x2.shape
    n_exp = u8.shape[0]
    nk = tables[0].shape[1]
    tb = min(512, rows)
    eb = SUBLANES * nk
    nb = n_exp // eb
    tab_spec = pl.BlockSpec((PEER_HEADS, nk, tb), lambda i, e: (0, 0, i))
    half_spec = pl.BlockSpec((PEER_HEADS, nk // 2, tb), lambda i, e: (0, 0, i))
    return pl.pallas_call(
        functools.partial(_peer_expert_kernel, final_norm=final_norm, n_blocks=nb),
        grid=(rows // tb, nb + 1),
        in_specs=[
            pl.BlockSpec((d, tb), lambda i, e: (0, i)),
            pl.BlockSpec((1, tb), lambda i, e: (0, i)),
            pl.BlockSpec((eb, d), lambda i, e: (jnp.minimum(e, nb - 1), 0)),
            pl.BlockSpec((eb, d), lambda i, e: (jnp.maximum(e - 1, 0), 0)),
            pl.BlockSpec((1, LANES), lambda i, e: (0, 0)),
            tab_spec, tab_spec, half_spec, half_spec,
            pl.BlockSpec((tb, d), lambda i, e: (i, 0)),
            pl.BlockSpec((1, 1, d), lambda i, e: (mod_row(i, tb), 0, 5)),
            pl.BlockSpec((1, d), lambda i, e: (0, 0)),
        ],
        out_specs=pl.BlockSpec((tb, d), lambda i, e: (i, 0)),
        out_shape=jax.ShapeDtypeStruct((rows, d), F32),
        scratch_shapes=[pltpu.VMEM((eb, tb), F8), pltpu.VMEM((eb, tb), F8),
                        pltpu.VMEM((SUBLANES, LANES), F32), pltpu.VMEM((SUBLANES, LANES), F32),
                        pltpu.VMEM((eb // 2, tb), jnp.uint32)],
        compiler_params=_cparams(("arbitrary", "arbitrary")),
        name="peer_experts",
    )(ht, hs, u8, v8, w_scales, *tables, x2, mod3, g_final.reshape(1, d))


def kernel(x, c, ctx, c_ctx, w_mod, b_mod, g_norm_mix, g_norm_ffn, w_in, b_gate, conv_qk, g_fourier,
           g_mlstm, w_out, w_query, sub_keys, expert_u, expert_v, g_final):
    batch, seq, d = x.shape
    ctx_len = ctx.shape[1]
    depth = w_mod.shape[0]
    f_width = g_fourier.shape[1]
    m_width = g_mlstm.shape[1]
    qk_width = conv_qk.shape[-1]
    n_gates = b_gate.shape[1]
    n_main = f_width + qk_width + 2 * m_width
    assert w_in.shape[2] == n_main + n_gates and n_gates <= LANES
    assert f_width == qk_width == m_width and m_width == M_HEADS * LANES
    assert seq % MLSTM_CHUNK == 0 and ctx_len % MLSTM_CHUNK == 0 and seq % GRID_W == 0

    cond_rows = -(-(batch + 1) // SUBLANES) * SUBLANES
    cond = jnp.zeros((cond_rows, d), F32).at[:batch].set(c).at[batch].set(c_ctx)
    mod_all = _modulation(cond, w_mod, b_mod)

    latent_row = lambda i, tm: (i * tm) // seq
    context_row = lambda i, tm: batch

    x2 = x.reshape(batch * seq, d)
    c2 = ctx.reshape(batch * ctx_len, d)
    v_col0 = qk_width // LANES
    o_col0 = (qk_width + m_width) // LANES
    zero_state = jnp.zeros((batch, M_HEADS, 2, LANES, 2 * LANES), F32)
    zero_m = jnp.zeros((batch, M_HEADS, 2, SUBLANES, LANES), F32)

    for l in range(depth):
        last = l == depth - 1
        mod3 = mod_all[l].reshape(cond_rows, 1, N_MOD * d)
        w_main = w_in[l, :, :n_main].astype(BF16)
        w_gate = jnp.zeros((d, LANES), F32).at[:, :n_gates].set(w_in[l, :, n_main:]).astype(BF16)
        bg = jnp.zeros((1, LANES), F32).at[0, :n_gates].set(b_gate[l])
        w_out_bf = w_out[l].astype(BF16)
        wq_bf = w_query[l].astype(BF16)
        u8, u_inv = _fp8_tensor(expert_u[l])
        v8, v_inv = _fp8_tensor(expert_v[l])
        w_scales = jnp.zeros((1, LANES), F32).at[0, 0].set(u_inv).at[0, 1].set(v_inv)

        def mixer(tokens, mod_row, n_tok, vertical, state, m_state):
            four, p, gates = _in_projection(tokens, g_norm_mix[l], mod3, mod_row, w_main, w_gate, bg)
            qk = _qk_conv(p, conv_qk[l], qk_width, 0, n_tok, vertical)
            ym, state, m_state = _mlstm(qk, p, gates, g_mlstm[l], state, m_state, batch, n_tok,
                                        v_col0, o_col0)
            return four, ym, state, m_state

        def ffn_and_residuals(tokens, four, ym, mod_row, n_tok, final_norm):
            f = _fourier_mix(four, g_fourier[l], batch, n_tok)
            tokens = _out_projection(f, ym, w_out_bf, tokens, mod3, mod_row)
            routed = _peer_route(tokens, g_norm_ffn[l], mod3, mod_row, wq_bf, sub_keys[l])
            return _peer_experts(routed[0], routed[1], u8, v8, w_scales, routed[2:], tokens, mod3, mod_row,
                                 g_final, final_norm)

        pc, ymc, st, m_st = mixer(c2, context_row, ctx_len, False, zero_state, zero_m)
        p, ym, _, _ = mixer(x2, latent_row, seq, True, st, m_st)
        x2 = ffn_and_residuals(x2, p, ym, latent_row, seq, last)
        if not last:
            c2 = ffn_and_residuals(c2, pc, ymc, context_row, ctx_len, False)
    return x2.reshape(batch, seq, d)
```

```python
import functools
import math
from typing import Callable, NamedTuple

import jax
import jax.numpy as jnp
from jax import lax
from jax.experimental import pallas as pl
from jax.experimental.pallas import tpu as pltpu

F32 = jnp.float32
BF16 = jnp.bfloat16

F_GROUPS = 4
M_HEADS = 8
GRID_W = 64
N_KEYS = 128
PEER_HEADS = 8
PEER_TOPK = 16
N_MOD = 6
EPS = 1e-6

LANES = 128
SUBLANES = 8
VMEM_LIMIT = 56 * 1024 * 1024

MLSTM_CHUNK = 128
DFT_N2 = 128
DFT_KB = 8
NEG = -(2.0 ** 110)
REMOVED = 2.0 ** 120
RANK_NONE = 99.0
F8 = jnp.float8_e4m3fn
F8_TARGET = 224.0
TINY = 1e-30


def _cparams(sem):
    return pltpu.CompilerParams(dimension_semantics=sem, vmem_limit_bytes=VMEM_LIMIT)


class _ModRows(NamedTuple):
    row: Callable
    group: int


def _row_tile(limit, mod_row):
    tile = min(limit, mod_row.group)
    assert mod_row.group % tile == 0
    return tile


def _sigmoid(x):
    return 1.0 / (1.0 + jnp.exp(-x))


def _split3(x):
    p0 = x.astype(BF16)
    r1 = x - p0.astype(F32)
    p1 = r1.astype(BF16)
    p2 = (r1 - p1.astype(F32)).astype(BF16)
    return p0, p1, p2


def _dot01_left(a01, x):
    return sum(jnp.dot(a01, p, preferred_element_type=F32) for p in _split3(x))


def _dot01_right(x, b01):
    return sum(jnp.dot(p, b01, preferred_element_type=F32) for p in _split3(x))


def _mod_kernel(c_ref, w_ref, b_ref, o_ref):
    c = c_ref[...]
    s = (c * _sigmoid(c)).astype(BF16)
    o_ref[0] = jnp.dot(s, w_ref[0].astype(BF16), preferred_element_type=F32) + b_ref[0]


def _modulation(cond, w_mod, b_mod):
    n_layers, d, n_out = w_mod.shape
    rows = cond.shape[0]
    tn = 1024
    return pl.pallas_call(
        _mod_kernel,
        grid=(n_layers, n_out // tn),
        in_specs=[
            pl.BlockSpec((rows, d), lambda l, j: (0, 0)),
            pl.BlockSpec((1, d, tn), lambda l, j: (l, 0, j)),
            pl.BlockSpec((1, 1, tn), lambda l, j: (l, 0, j)),
        ],
        out_specs=pl.BlockSpec((1, rows, tn), lambda l, j: (l, 0, j)),
        out_shape=jax.ShapeDtypeStruct((n_layers, rows, n_out), F32),
        compiler_params=_cparams(("arbitrary", "arbitrary")),
        name="modulation",
    )(cond, w_mod, b_mod.reshape(n_layers, 1, n_out))


def _rms_modulate(x, g, shift, scale):
    ms = jnp.mean(x * x, axis=-1, keepdims=True)
    return (x * lax.rsqrt(ms + EPS) * g) * (1.0 + scale) + shift


def _inproj_kernel(x_ref, g_ref, sh_ref, sc_ref, w_ref, wg_ref, bg_ref, four_ref, p_ref, gate_ref, h_scr):
    j = pl.program_id(1)

    @pl.when(j == 0)
    def _():
        h = _rms_modulate(x_ref[...], g_ref[...], sh_ref[0], sc_ref[0]).astype(BF16)
        h_scr[...] = h
        gate_ref[...] = jnp.dot(h, wg_ref[...], preferred_element_type=F32) + bg_ref[...]

    acc = jnp.dot(h_scr[...], w_ref[...], preferred_element_type=F32)

    @pl.when(j == 0)
    def _():
        four_ref[...] = acc

    @pl.when(j > 0)
    def _():
        p_ref[...] = acc.astype(BF16)


def _in_projection(x2, g, mod3, mod_row, w_main, w_gate, b_gate):
    rows, d = x2.shape
    n_main = w_main.shape[1]
    tm = _row_tile(1024, mod_row)
    tn = 1024
    return pl.pallas_call(
        _inproj_kernel,
        grid=(rows // tm, n_main // tn),
        in_specs=[
            pl.BlockSpec((tm, d), lambda i, j: (i, 0)),
            pl.BlockSpec((1, d), lambda i, j: (0, 0)),
            pl.BlockSpec((1, 1, d), lambda i, j: (mod_row.row(i, tm), 0, 0)),
            pl.BlockSpec((1, 1, d), lambda i, j: (mod_row.row(i, tm), 0, 1)),
            pl.BlockSpec((d, tn), lambda i, j: (0, j)),
            pl.BlockSpec((d, LANES), lambda i, j: (0, 0)),
            pl.BlockSpec((1, LANES), lambda i, j: (0, 0)),
        ],
        out_specs=[
            pl.BlockSpec((tm, tn), lambda i, j: (i, 0)),
            pl.BlockSpec((tm, tn), lambda i, j: (i, jnp.maximum(j - 1, 0))),
            pl.BlockSpec((tm, LANES), lambda i, j: (i, 0)),
        ],
        out_shape=[
            jax.ShapeDtypeStruct((rows, tn), F32),
            jax.ShapeDtypeStruct((rows, n_main - tn), BF16),
            jax.ShapeDtypeStruct((rows, LANES), F32),
        ],
        scratch_shapes=[pltpu.VMEM((tm, d), BF16)],
        compiler_params=_cparams(("arbitrary", "arbitrary")),
        name="in_projection",
    )(x2, g.reshape(1, d), mod3, mod3, w_main, w_gate, b_gate)


def _conv_kernel(*refs, tb, width, vertical, blocks_per_image):
    if vertical:
        cur_ref, top_ref, bot_ref, w_ref, o_ref = refs
    else:
        cur_ref, w_ref, o_ref = refs
    cur = cur_ref[...].astype(F32)
    ch = cur.shape[1]
    wpos = lax.rem(lax.broadcasted_iota(jnp.int32, (tb, ch), 0), width)
    first_col = wpos == 0
    last_col = wpos == width - 1
    if vertical:
        r = lax.rem(pl.program_id(0), blocks_per_image)
        top = jnp.where(r == 0, 0.0, top_ref[...].astype(F32))
        bot = jnp.where(r == blocks_per_image - 1, 0.0, bot_ref[...].astype(F32))
        ext = jnp.concatenate([top, cur, bot], axis=0)
        bases = [(dr, ext[dr * width:dr * width + tb]) for dr in range(3)]
    else:
        bases = [(1, cur)]
    acc = jnp.zeros((tb, ch), F32)
    for dr, base in bases:
        left = jnp.where(first_col, 0.0, pltpu.roll(base, 1, axis=0))
        right = jnp.where(last_col, 0.0, pltpu.roll(base, tb - 1, axis=0))
        for dw, shifted in enumerate((left, base, right)):
            k = dr * 3 + dw
            acc = acc + shifted * w_ref[k:k + 1, :]
    o_ref[...] = (acc * _sigmoid(acc)).astype(BF16)


def _qk_conv(p, conv_w, qk_width, col_block, tokens_per_image, vertical):
    rows = p.shape[0]
    w9 = conv_w.reshape(9, qk_width)
    if vertical:
        width = GRID_W
        tb = min(512, tokens_per_image)
        bpi = tokens_per_image // tb
        halo = tb // width
        n_halo = rows // width
        in_specs = [
            pl.BlockSpec((tb, qk_width), lambda i: (i, col_block)),
            pl.BlockSpec((width, qk_width), lambda i: (jnp.maximum(i * halo - 1, 0), col_block)),
            pl.BlockSpec((width, qk_width), lambda i: (jnp.minimum((i + 1) * halo, n_halo - 1), col_block)),
            pl.BlockSpec((9, qk_width), lambda i: (0, 0)),
        ]
        args = (p, p, p, w9)
    else:
        width = tb = tokens_per_image
        bpi = 1
        in_specs = [
            pl.BlockSpec((tb, qk_width), lambda i: (i, col_block)),
            pl.BlockSpec((9, qk_width), lambda i: (0, 0)),
        ]
        args = (p, w9)
    return pl.pallas_call(
        functools.partial(_conv_kernel, tb=tb, width=width, vertical=vertical, blocks_per_image=bpi),
        grid=(rows // tb,),
        in_specs=in_specs,
        out_specs=pl.BlockSpec((tb, qk_width), lambda i: (i, 0)),
        out_shape=jax.ShapeDtypeStruct((rows, qk_width), BF16),
        compiler_params=_cparams(("arbitrary",)),
        name="qk_conv_latent" if vertical else "qk_conv_context",
    )(*args)


def _mlstm_chunks(chains):
    n = range(len(chains))
    ch = chains
    L = ch[0]["q"].shape[0]
    logsig = [jnp.minimum(c["gch"], 0.0) - jnp.log(1.0 + jnp.exp(-jnp.abs(c["gch"]))) for c in ch]
    shape = ch[0]["gch"].shape
    ig = [jnp.broadcast_to(jnp.sum(jnp.where(ch[i]["sel_i"], ch[i]["gch"], 0.0), axis=-1, keepdims=True),
                           shape) for i in n]
    lf = [jnp.broadcast_to(jnp.sum(jnp.where(ch[i]["sel_f"], logsig[i], 0.0), axis=-1, keepdims=True),
                           shape) for i in n]
    b_p = [_split3(lf[i]) for i in n]
    b = [sum(jnp.dot(ch[i]["cum"], p, preferred_element_type=F32) for p in b_p[i]) for i in n]
    b_end = [b[i][0:1, :] if ch[i]["reverse"] else b[i][L - 1:L, :] for i in n]
    a_t = [(ig[i] - b[i]).T for i in n]
    dmat = [jnp.where(ch[i]["causal"], b[i] + a_t[i], NEG) for i in n]
    g = [b[i] + ch[i]["m"] for i in n]
    m_j = [jnp.maximum(g[i], jnp.max(dmat[i], axis=-1, keepdims=True)) for i in n]
    pmat = [jnp.exp(dmat[i] - m_j[i]) for i in n]
    s_raw = [lax.dot_general(c["q"], c["k"], (((1,), (1,)), ((), ())), preferred_element_type=F32)
             for c in ch]
    s = [(s_raw[i] * pmat[i]).astype(BF16) for i in n]
    intra = [jnp.dot(s[i], ch[i]["vaug"], preferred_element_type=F32) for i in n]
    carried = [jnp.dot(c["q"], c["state"].astype(BF16), preferred_element_type=F32) for c in ch]
    inter = [jnp.exp(g[i] - m_j[i]) for i in n]
    tot = [intra[i] + jnp.concatenate([inter[i], inter[i]], axis=1) * carried[i] for i in n]
    h = [tot[i][:, :LANES] / jnp.maximum(jnp.abs(tot[i][:, LANES:]), jnp.exp(-m_j[i])) for i in n]
    a = [b_end[i] - b[i] + ig[i] for i in n]
    m_new = [jnp.maximum(b_end[i] + ch[i]["m"], jnp.max(a[i], axis=0, keepdims=True)) for i in n]
    kw = [(ch[i]["k"].astype(F32) * jnp.exp(a[i] - m_new[i])).astype(BF16) for i in n]
    f_old = [jnp.exp(b_end[i] + ch[i]["m"] - m_new[i]) for i in n]
    upd = [pl.dot(kw[i], ch[i]["vaug"], trans_a=True) for i in n]
    state_new = [jnp.concatenate([f_old[i], f_old[i]], axis=1) * ch[i]["state"] + upd[i] for i in n]
    return [(h[i], state_new[i], m_new[i]) for i in n]


def _mlstm_kernel(q_ref, k_ref, v_ref, o_ref, gt_ref, g_ref, sin_ref, min_ref,
                  y_ref, sout_ref, mout_ref, hf_scr, hb_scr, *, seq, k_scale):
    L = MLSTM_CHUNK
    nc = seq // L
    pair = pl.program_id(1)
    lane = lax.broadcasted_iota(jnp.int32, (1, LANES), 1)
    rr = lax.broadcasted_iota(jnp.int32, (L, L), 0)
    cc = lax.broadcasted_iota(jnp.int32, (L, L), 1)
    causal = (rr >= cc, rr <= cc)
    cum = tuple(jnp.where(m, 1.0, 0.0).astype(BF16) for m in causal)
    ones = jnp.ones((L, LANES), BF16)
    qmask, kmask, sel = [], [], []
    for j in range(2):
        own = (lane // (LANES // 2)) == j
        qmask.append(jnp.where(own, 1.0, 0.0).astype(BF16))
        kmask.append(jnp.where(own, k_scale, 0.0).astype(BF16))
        head = 2 * pair + j
        sel.append([lane == kind * M_HEADS + head for kind in range(4)])

    def body(c, carry):
        chains, dest = [], []
        for d in range(2):
            rows = pl.ds(pl.multiple_of((c if d == 0 else nc - 1 - c) * L, L), L)
            q_all, k_all, gch = q_ref[rows, :], k_ref[rows, :], gt_ref[rows, :]
            for j in range(2):
                hcols = slice(j * LANES, (j + 1) * LANES)
                idx = 2 * (2 * j + d)
                chains.append(dict(
                    q=q_all * qmask[j], k=k_all * kmask[j],
                    vaug=jnp.concatenate([v_ref[rows, hcols], ones], axis=1), gch=gch,
                    sel_i=sel[j][2 * d], sel_f=sel[j][2 * d + 1], cum=cum[d], causal=causal[d],
                    reverse=d == 1, state=carry[idx], m=carry[idx + 1]))
                dest.append((hf_scr if d == 0 else hb_scr, rows, hcols, idx))
        carry = list(carry)
        for (scr, rows, hcols, idx), (h, state, m) in zip(dest, _mlstm_chunks(chains)):
            scr[rows, hcols] = h.astype(BF16)
            carry[idx], carry[idx + 1] = state, m
        return tuple(carry)

    init = []
    for j in range(2):
        for d in range(2):
            init += [sin_ref[0, j, d], min_ref[0, j, d][0:1, :]]
    final = lax.fori_loop(0, nc, body, tuple(init))
    for j in range(2):
        for d in range(2):
            idx = 2 * (2 * j + d)
            sout_ref[0, j, d] = final[idx]
            mout_ref[0, j, d] = jnp.broadcast_to(final[idx + 1], (SUBLANES, LANES))

    def finish(c, _):
        rows = pl.ds(pl.multiple_of(c * L, L), L)
        for j in range(2):
            hcols = slice(j * LANES, (j + 1) * LANES)
            h = hf_scr[rows, hcols].astype(F32) + hb_scr[rows, hcols].astype(F32)
            ms = jnp.mean(h * h, axis=-1, keepdims=True)
            y = h * lax.rsqrt(ms + EPS) * g_ref[:, hcols]
            y_ref[rows, hcols] = (y * _sigmoid(o_ref[rows, hcols].astype(F32))).astype(BF16)
        return 0

    lax.fori_loop(0, nc, finish, 0)


def _mlstm(qk, p, gates, g_mlstm, state_in, m_in, batch, seq, v_col0, o_col0):
    rows = qk.shape[0]
    pw = 2 * LANES
    k_col0 = qk.shape[1] // (2 * LANES)
    dk = qk.shape[1] // (2 * M_HEADS)
    assert v_col0 % 2 == 0 and o_col0 % 2 == 0
    return pl.pallas_call(
        functools.partial(_mlstm_kernel, seq=seq, k_scale=dk ** -0.5),
        grid=(batch, M_HEADS // 2),
        in_specs=[
            pl.BlockSpec((seq, LANES), lambda b, h: (b, h)),
            pl.BlockSpec((seq, LANES), lambda b, h: (b, k_col0 + h)),
            pl.BlockSpec((seq, pw), lambda b, h: (b, v_col0 // 2 + h)),
            pl.BlockSpec((seq, pw), lambda b, h: (b, o_col0 // 2 + h)),
            pl.BlockSpec((seq, LANES), lambda b, h: (b, 0)),
            pl.BlockSpec((1, pw), lambda b, h: (0, h)),
            pl.BlockSpec((1, 2, 2, LANES, 2 * LANES), lambda b, h: (b, h, 0, 0, 0)),
            pl.BlockSpec((1, 2, 2, SUBLANES, LANES), lambda b, h: (b, h, 0, 0, 0)),
        ],
        out_specs=[
            pl.BlockSpec((seq, pw), lambda b, h: (b, h)),
            pl.BlockSpec((1, 2, 2, LANES, 2 * LANES), lambda b, h: (b, h, 0, 0, 0)),
            pl.BlockSpec((1, 2, 2, SUBLANES, LANES), lambda b, h: (b, h, 0, 0, 0)),
        ],
        out_shape=[
            jax.ShapeDtypeStruct((rows, M_HEADS * LANES), BF16),
            jax.ShapeDtypeStruct(state_in.shape, F32),
            jax.ShapeDtypeStruct(m_in.shape, F32),
        ],
        scratch_shapes=[pltpu.VMEM((seq, pw), BF16), pltpu.VMEM((seq, pw), BF16)],
        compiler_params=_cparams(("arbitrary", "arbitrary")),
        name="mlstm",
    )(qk, qk, p, p, gates, g_mlstm.reshape(1, -1), state_in, m_in)


def _dft_mats(n, scale):
    idx = jnp.arange(n, dtype=jnp.int32)
    ang = (2.0 * math.pi / n) * ((idx[:, None] * idx[None, :]) % n).astype(F32)
    return jnp.cos(ang) * scale, jnp.sin(ang) * scale


def _channel_dft(u, cc_ref, sc_ref):
    gw = cc_ref.shape[0]
    zr, zi = [], []
    for g in range(u.shape[1] // gw):
        ug = u[:, g * gw:(g + 1) * gw]
        zr.append(jnp.dot(ug, cc_ref[...], preferred_element_type=F32))
        zi.append(jnp.dot(ug, sc_ref[...], preferred_element_type=F32))
    return jnp.concatenate(zr, axis=1), jnp.concatenate(zi, axis=1)


def _group_rmsnorm(y, g, gw):
    outs = []
    for k in range(y.shape[1] // gw):
        yk = y[:, k * gw:(k + 1) * gw]
        ms = jnp.mean(yk * yk, axis=-1, keepdims=True)
        outs.append(yk * lax.rsqrt(ms + EPS) * g[:, k * gw:(k + 1) * gw])
    return jnp.concatenate(outs, axis=1)


def _fnet_a_kernel(u_ref, cc_ref, sc_ref, f_ref, cw_ref, sw_ref, yr_ref, yi_ref):
    fw = u_ref.shape[3]
    for r in range(u_ref.shape[2]):
        zr, zi = _channel_dft(u_ref[0, :, r, :].astype(BF16), cc_ref, sc_ref)
        z = jnp.concatenate([zr, zi], axis=0).astype(BF16)
        y = jnp.dot(f_ref[...], z, preferred_element_type=F32)
        n2 = y.shape[0] // 2
        yr, yi = y[:n2], y[n2:]
        cw, sw = cw_ref[r], sw_ref[r]
        for j in range(fw // LANES):
            cols = slice(j * LANES, (j + 1) * LANES)
            ocols = slice(r * fw + j * LANES, r * fw + (j + 1) * LANES)
            yr_ref[0, :, ocols] = (yr[:, cols] * cw + yi[:, cols] * sw).astype(BF16)
            yi_ref[0, :, ocols] = (yi[:, cols] * cw - yr[:, cols] * sw).astype(BF16)


def _fnet_b_kernel(yr_ref, yi_ref, gc_ref, gs_ref, g_ref, o_ref, *, gw):
    y = (jnp.dot(gc_ref[...], yr_ref[0], preferred_element_type=F32)
         + jnp.dot(gs_ref[...], yi_ref[0], preferred_element_type=F32))
    out = _group_rmsnorm(y, g_ref[...], gw).astype(BF16)
    o_ref[0] = out.reshape(o_ref.shape[1:])


def _fnet_direct_kernel(u_ref, cc_ref, sc_ref, ct_ref, st_ref, g_ref, o_ref, *, gw):
    zr, zi = _channel_dft(u_ref[...].astype(BF16), cc_ref, sc_ref)
    y = (jnp.dot(ct_ref[...], zr.astype(BF16), preferred_element_type=F32)
         + jnp.dot(st_ref[...], zi.astype(BF16), preferred_element_type=F32))
    o_ref[...] = _group_rmsnorm(y, g_ref[...], gw).astype(BF16)


def _fourier_mix(p, g_fourier, batch, seq):
    f_width = p.shape[1]
    gw = f_width // F_GROUPS
    cc, sc = _dft_mats(gw, gw ** -0.5)
    cc, msc = cc.astype(BF16), (-sc).astype(BF16)
    g2 = g_fourier.reshape(1, f_width)
    rows = batch * seq
    if seq <= 512:
        ct, st = _dft_mats(seq, seq ** -0.5)
        return pl.pallas_call(
            functools.partial(_fnet_direct_kernel, gw=gw),
            grid=(batch,),
            in_specs=[
                pl.BlockSpec((seq, f_width), lambda b: (b, 0)),
                pl.BlockSpec((gw, gw), lambda b: (0, 0)),
                pl.BlockSpec((gw, gw), lambda b: (0, 0)),
                pl.BlockSpec((seq, seq), lambda b: (0, 0)),
                pl.BlockSpec((seq, seq), lambda b: (0, 0)),
                pl.BlockSpec((1, f_width), lambda b: (0, 0)),
            ],
            out_specs=pl.BlockSpec((seq, f_width), lambda b: (b, 0)),
            out_shape=jax.ShapeDtypeStruct((rows, f_width), BF16),
            compiler_params=_cparams(("arbitrary",)),
            name="fourier_direct",
        )(p, cc, msc, ct.astype(BF16), st.astype(BF16), g2)

    n2 = DFT_N2
    n1 = seq // n2
    c2, s2 = _dft_mats(n2, n2 ** -0.5)
    fmat = jnp.concatenate([jnp.concatenate([c2, s2], axis=1),
                            jnp.concatenate([-s2, c2], axis=1)], axis=0).astype(BF16)
    i1 = jnp.arange(n1, dtype=jnp.int32)
    i2 = jnp.arange(n2, dtype=jnp.int32)
    tw = (2.0 * math.pi / seq) * (i1[:, None] * i2[None, :]).astype(F32)
    cw = jnp.broadcast_to(jnp.cos(tw)[:, :, None], (n1, n2, LANES))
    sw = jnp.broadcast_to(jnp.sin(tw)[:, :, None], (n1, n2, LANES))
    nr = SUBLANES
    yr, yi = pl.pallas_call(
        _fnet_a_kernel,
        grid=(batch, n1 // nr),
        in_specs=[
            pl.BlockSpec((1, n2, nr, f_width), lambda b, j: (b, 0, j, 0)),
            pl.BlockSpec((gw, gw), lambda b, j: (0, 0)),
            pl.BlockSpec((gw, gw), lambda b, j: (0, 0)),
            pl.BlockSpec((2 * n2, 2 * n2), lambda b, j: (0, 0)),
            pl.BlockSpec((nr, n2, LANES), lambda b, j: (j, 0, 0)),
            pl.BlockSpec((nr, n2, LANES), lambda b, j: (j, 0, 0)),
        ],
        out_specs=[
            pl.BlockSpec((1, n2, nr * f_width), lambda b, j: (b, 0, j)),
            pl.BlockSpec((1, n2, nr * f_width), lambda b, j: (b, 0, j)),
        ],
        out_shape=[jax.ShapeDtypeStruct((batch, n2, n1 * f_width), BF16)] * 2,
        compiler_params=_cparams(("arbitrary", "arbitrary")),
        name="fourier_stage_a",
    )(p.reshape(batch, n2, n1, f_width), cc, msc, fmat, cw, sw)

    kb = DFT_KB
    c1, s1 = _dft_mats(n1, n1 ** -0.5)
    eye = jnp.eye(kb, dtype=F32)
    gc = jnp.einsum("kn,ab->kabn", c1, eye).reshape(n1 * kb, kb * n1).astype(BF16)
    gs = jnp.einsum("kn,ab->kabn", s1, eye).reshape(n1 * kb, kb * n1).astype(BF16)
    out = pl.pallas_call(
        functools.partial(_fnet_b_kernel, gw=gw),
        grid=(batch, n2 // kb),
        in_specs=[
            pl.BlockSpec((1, kb * n1, f_width), lambda b, j: (b, j, 0)),
            pl.BlockSpec((1, kb * n1, f_width), lambda b, j: (b, j, 0)),
            pl.BlockSpec((n1 * kb, kb * n1), lambda b, j: (0, 0)),
            pl.BlockSpec((n1 * kb, kb * n1), lambda b, j: (0, 0)),
            pl.BlockSpec((1, f_width), lambda b, j: (0, 0)),
        ],
        out_specs=pl.BlockSpec((1, n1, kb, f_width), lambda b, j: (b, 0, j, 0)),
        out_shape=jax.ShapeDtypeStruct((batch, n1, n2, f_width), BF16),
        compiler_params=_cparams(("arbitrary", "arbitrary")),
        name="fourier_stage_b",
    )(yr.reshape(batch, n2 * n1, f_width), yi.reshape(batch, n2 * n1, f_width), gc, gs, g2)
    return out.reshape(rows, f_width)


def _outproj_kernel(f_ref, ym_ref, w_ref, x_ref, ga_ref, o_ref):
    fw = f_ref.shape[1]
    y = (jnp.dot(f_ref[...], w_ref[:fw, :], preferred_element_type=F32)
         + jnp.dot(ym_ref[...], w_ref[fw:, :], preferred_element_type=F32))
    o_ref[...] = x_ref[...] + ga_ref[0] * y


def _out_projection(f, ym, w_out, x2, mod3, mod_row):
    rows, d = x2.shape
    tm = _row_tile(512, mod_row)
    return pl.pallas_call(
        _outproj_kernel,
        grid=(rows // tm,),
        in_specs=[
            pl.BlockSpec((tm, f.shape[1]), lambda i: (i, 0)),
            pl.BlockSpec((tm, ym.shape[1]), lambda i: (i, 0)),
            pl.BlockSpec(w_out.shape, lambda i: (0, 0)),
            pl.BlockSpec((tm, d), lambda i: (i, 0)),
            pl.BlockSpec((1, 1, d), lambda i: (mod_row.row(i, tm), 0, 2)),
        ],
        out_specs=pl.BlockSpec((tm, d), lambda i: (i, 0)),
        out_shape=jax.ShapeDtypeStruct((rows, d), F32),
        compiler_params=_cparams(("arbitrary",)),
        name="out_projection",
    )(f, ym, w_out, x2, mod3)


def _pack_bf16_pair(lo, hi):
    def bf16_bits(x):
        u = lax.bitcast_convert_type(x, jnp.uint32)
        return (u + (jnp.uint32(0x7FFF) + ((u >> 16) & jnp.uint32(1)))) >> 16
    return bf16_bits(lo) | (bf16_bits(hi) << 16)


def _interleaved_key(q, n):
    half = n // 2
    return jnp.where(q < half, 2 * q, 2 * (q - half) + 1)


def _extract_top(arrays, k, exact, interleaved=()):
    idx = range(len(arrays))
    s = list(arrays)
    n = [x.shape[0] for x in s]
    row = [None] * len(s)
    if exact:
        for i in idx:
            row[i] = lax.broadcasted_iota(jnp.int32, s[i].shape, 0)
            if i < len(interleaved) and interleaved[i]:
                row[i] = _interleaved_key(row[i], n[i])
    vals = [[] for _ in s]
    for it in range(k):
        m = [jnp.max(s[i], axis=0, keepdims=True) for i in idx]
        hit = [s[i] == m[i] for i in idx]
        if exact:
            first = [jnp.min(jnp.where(hit[i], row[i], n[i]), axis=0, keepdims=True) for i in idx]
            hit = [row[i] == first[i] for i in idx]
        s = [jnp.where(hit[i], -(it + 1) * REMOVED, s[i]) for i in idx]
        for i in idx:
            vals[i].append(m[i])
    gone = [s[i] <= -REMOVED for i in idx]
    rank = [jnp.where(gone[i], s[i] * (-1.0 / REMOVED) - 1.0, RANK_NONE) for i in idx]
    removed = [jnp.sum(jnp.where(gone[i], 1.0, 0.0), axis=0, keepdims=True) for i in idx]
    return [(vals[i], rank[i], removed[i]) for i in idx]


def _route_chunk(s0, s1, exact):
    k = PEER_TOPK
    sub = SUBLANES
    (v0, rank0, rem0), (v1, rank1, rem1) = _extract_top([s0, s1], k, exact, interleaved=(False, True))
    sv0 = jnp.concatenate(v0, axis=0)
    sv1 = jnp.concatenate(v1, axis=0)
    if exact:
        layout = [("row", a, b0) for a in range(k) for b0 in range(0, max(k // (a + 1), 1), sub)]
    else:
        layout = ([("row", 0, 0), ("row", 0, sub), ("row", 1, 0), ("col", 0, sub)]
                  + [("row", a, 0) for a in range(2, k // 3)]
                  + [("col", b, 0) for b in range(k // sub)])
    seen = set()
    blocks = []
    for kind, fixed, start in layout:
        cells = [(fixed, start + r) if kind == "row" else (start + r, fixed) for r in range(sub)]
        keep = [(a + 1) * (b + 1) <= k and (a, b) not in seen for a, b in cells]
        seen.update(cell for cell, kp in zip(cells, keep) if kp)
        blk = (v0[fixed] + sv1[start:start + sub]) if kind == "row" else (sv0[start:start + sub] + v1[fixed])
        if not all(keep):
            brow = lax.broadcasted_iota(jnp.int32, blk.shape, 0)
            mask = functools.reduce(jnp.logical_or, [brow == r for r, kp in enumerate(keep) if kp])
            blk = jnp.where(mask, blk, NEG)
        blocks.append(blk)
    assert len(seen) == sum(k // (a + 1) for a in range(k))
    cand = jnp.concatenate(blocks, axis=0)
    (_, crank, rem2), = _extract_top([cand], k, exact)
    cnt = jnp.where(crank < RANK_NONE, 1.0, 0.0)
    z = jnp.sum(cnt * jnp.exp(cand - (v0[0] + v1[0])), axis=0, keepdims=True)
    arow = lax.broadcasted_iota(jnp.int32, sv0.shape, 0)
    n_rank = jnp.zeros(sv0.shape, F32)
    for idx, (kind, fixed, start) in enumerate(layout):
        c_blk = cnt[idx * sub:(idx + 1) * sub]
        if kind == "row":
            n_rank = n_rank + jnp.where(arow == fixed, jnp.sum(c_blk, axis=0, keepdims=True), 0.0)
        else:
            pieces = [c_blk if a0 == start else jnp.zeros_like(c_blk) for a0 in range(0, k, sub)]
            n_rank = n_rank + jnp.concatenate(pieces, axis=0)
    n_of_key = jnp.zeros(rank0.shape, F32)
    for a in range(k):
        n_of_key = jnp.where(rank0 == float(a), n_rank[a:a + 1], n_of_key)
    tie = (jnp.where(rem0 == k, 0.0, 1.0) + jnp.where(rem1 == k, 0.0, 1.0)
           + jnp.where(rem2 == k, 0.0, 1.0))
    return jnp.exp(s0 - v0[0]) / z, n_of_key, jnp.exp(s1 - v1[0]), rank1, tie


def _peer_route_kernel(x_ref, g_ref, sh_ref, sc_ref, wq_ref, keys_ref,
                       ht_ref, hs_ref, e0_ref, nn_ref, e1_ref, r1_ref, s_scr):
    h2 = _rms_modulate(x_ref[...], g_ref[...], sh_ref[0], sc_ref[0])
    ht = h2.T
    amax = jnp.maximum(jnp.max(jnp.abs(ht), axis=0, keepdims=True), TINY)
    ht_ref[...] = (ht * (F8_TARGET / amax)).astype(F8)
    hs_ref[...] = amax * (1.0 / F8_TARGET)
    q = jnp.dot(h2.astype(BF16), wq_ref[...], preferred_element_type=F32)
    tb = q.shape[0]
    kd = keys_ref.shape[-1]
    for hp in range(2 * PEER_HEADS):
        s_scr[hp] = lax.dot_general(keys_ref[hp // 2, hp % 2], q[:, hp * kd:(hp + 1) * kd],
                                    (((1,), (1,)), ((), ())),
                                    precision=lax.Precision.HIGHEST, preferred_element_type=F32)

    def route_head(h, _):
        for c in range(tb // LANES):
            cols = slice(c * LANES, (c + 1) * LANES)

            def route(exact):
                e0, n_of_key, e1, rank1, tie = _route_chunk(s_scr[2 * h, :, cols],
                                                            s_scr[2 * h + 1, :, cols], exact)
                half = e1.shape[0] // 2
                e0_ref[h, :, cols] = _pack_bf16_pair(e0, e0)
                nn_ref[h, :, cols] = _pack_bf16_pair(n_of_key, n_of_key)
                e1_ref[h, :, cols] = _pack_bf16_pair(e1[:half], e1[half:])
                r1_ref[h, :, cols] = _pack_bf16_pair(rank1[:half], rank1[half:])
                return tie

            tie = route(False)

            @pl.when(jnp.max(tie) > 0.0)
            def _():
                route(True)
        return 0

    lax.fori_loop(0, PEER_HEADS, route_head, 0)


def _peer_route(x2, g, mod3, mod_row, wq, sub_keys):
    rows, d = x2.shape
    tb = _row_tile(256, mod_row)
    nk = sub_keys.shape[2]
    order = jnp.concatenate([jnp.arange(0, nk, 2), jnp.arange(1, nk, 2)])
    sub_keys = sub_keys.at[:, 1].set(sub_keys[:, 1][:, order])
    tab_spec = pl.BlockSpec((PEER_HEADS, nk, tb), lambda i: (0, 0, i))
    half_spec = pl.BlockSpec((PEER_HEADS, nk // 2, tb), lambda i: (0, 0, i))
    tab = jax.ShapeDtypeStruct((PEER_HEADS, nk, rows), jnp.uint32)
    half_tab = jax.ShapeDtypeStruct((PEER_HEADS, nk // 2, rows), jnp.uint32)
    return pl.pallas_call(
        _peer_route_kernel,
        grid=(rows // tb,),
        in_specs=[
            pl.BlockSpec((tb, d), lambda i: (i, 0)),
            pl.BlockSpec((1, d), lambda i: (0, 0)),
            pl.BlockSpec((1, 1, d), lambda i: (mod_row.row(i, tb), 0, 3)),
            pl.BlockSpec((1, 1, d), lambda i: (mod_row.row(i, tb), 0, 4)),
            pl.BlockSpec(wq.shape, lambda i: (0, 0)),
            pl.BlockSpec(sub_keys.shape, lambda i: (0, 0, 0, 0)),
        ],
        out_specs=[pl.BlockSpec((d, tb), lambda i: (0, i)), pl.BlockSpec((1, tb), lambda i: (0, i)),
                   tab_spec, tab_spec, half_spec, half_spec],
        out_shape=[jax.ShapeDtypeStruct((d, rows), F8), jax.ShapeDtypeStruct((1, rows), F32),
                   tab, tab, half_tab, half_tab],
        scratch_shapes=[pltpu.VMEM((2 * PEER_HEADS, nk, tb), F32)],
        compiler_params=_cparams(("arbitrary",)),
        name="peer_route",
    )(x2, g.reshape(1, d), mod3, mod3, wq, sub_keys)


def _gelu_tanh(x):
    return 0.5 * x * (1.0 + jnp.tanh(math.sqrt(2.0 / math.pi) * (x + 0.044715 * (x * x * x))))


def _peer_expert_kernel(ht_ref, hs_ref, u_ref, v_ref, ws_ref, e0_ref, nn_ref, e1_ref, r1_ref,
                        x_ref, ga_ref, gf_ref, o_ref, wa_scr, act_scr, *, final_norm):
    e = pl.program_id(1)
    nk = nn_ref.shape[1]
    tb = ht_ref.shape[1]
    ib = u_ref.shape[0] // nk
    sub = SUBLANES
    assert ib == SUBLANES
    i_rows = pl.ds(pl.multiple_of(e * ib, ib), ib)

    def packed(words):
        return pltpu.bitcast(words, BF16)

    @pl.when(e == 0)
    def _():
        o_ref[...] = jnp.zeros(o_ref.shape, F32)

    act = (jnp.dot(u_ref[...], ht_ref[...], preferred_element_type=F32)
           * (hs_ref[...] * ws_ref[0:1, 0:1])).astype(BF16)
    act_scr[...] = pltpu.bitcast(act, jnp.uint32)
    amax = jnp.max(jnp.max(jnp.abs(act), axis=0, keepdims=True), axis=1, keepdims=True)
    bound = PEER_HEADS * jnp.maximum(amax.astype(F32), TINY)
    to_f8 = (F8_TARGET / bound).astype(BF16)
    from_f8 = 1.0 / to_f8.astype(F32)
    n_words = e1_ref.shape[1]
    for il in range(ib):
        for c in range(tb // LANES):
            cols = slice(c * LANES, (c + 1) * LANES)
            w = [None] * (n_words // sub)
            for h in range(PEER_HEADS):
                n_i = packed(jnp.broadcast_to(nn_ref[h, i_rows, cols][il:il + 1], (sub, LANES)))
                e0_i = packed(jnp.broadcast_to(e0_ref[h, i_rows, cols][il:il + 1], (sub, LANES)))
                for s in range(n_words // sub):
                    rows = slice(s * sub, (s + 1) * sub)
                    e1 = packed(e1_ref[h, rows, cols])
                    t = jnp.where(packed(r1_ref[h, rows, cols]) < n_i, e1, jnp.zeros_like(e1)) * e0_i
                    w[s] = t if w[s] is None else w[s] + t
            a = packed(act_scr[il * n_words:(il + 1) * n_words, cols])
            wa = jnp.concatenate(w, axis=0) * _gelu_tanh(a) * to_f8
            wa_scr[il * nk:(il + 1) * nk, cols] = wa.astype(F8)
    o_ref[...] += pl.dot(wa_scr[...], v_ref[...], trans_a=True) * (from_f8 * ws_ref[0:1, 1:2])

    @pl.when(e == pl.num_programs(1) - 1)
    def _():
        y = x_ref[...] + ga_ref[0] * o_ref[...]
        if final_norm:
            ms = jnp.mean(y * y, axis=-1, keepdims=True)
            y = y * lax.rsqrt(ms + EPS) * gf_ref[...]
        o_ref[...] = y


def _fp8_tensor(w):
    amax = jnp.maximum(jnp.max(jnp.abs(w)), TINY)
    return (w * (F8_TARGET / amax)).astype(F8), amax * (1.0 / F8_TARGET)


def _peer_experts(ht, hs, u8, v8, w_scales, tables, x2, mod3, mod_row, g_final, final_norm):
    rows, d = x2.shape
    n_exp = u8.shape[0]
    nk = tables[0].shape[1]
    tb = _row_tile(512, mod_row)
    eb = SUBLANES * nk
    tab_spec = pl.BlockSpec((PEER_HEADS, nk, tb), lambda i, e: (0, 0, i))
    half_spec = pl.BlockSpec((PEER_HEADS, nk // 2, tb), lambda i, e: (0, 0, i))
    return pl.pallas_call(
        functools.partial(_peer_expert_kernel, final_norm=final_norm),
        grid=(rows // tb, n_exp // eb),
        in_specs=[
            pl.BlockSpec((d, tb), lambda i, e: (0, i)),
            pl.BlockSpec((1, tb), lambda i, e: (0, i)),
            pl.BlockSpec((eb, d), lambda i, e: (e, 0)),
            pl.BlockSpec((eb, d), lambda i, e: (e, 0)),
            pl.BlockSpec((1, LANES), lambda i, e: (0, 0)),
            tab_spec, tab_spec, half_spec, half_spec,
            pl.BlockSpec((tb, d), lambda i, e: (i, 0)),
            pl.BlockSpec((1, 1, d), lambda i, e: (mod_row.row(i, tb), 0, 5)),
            pl.BlockSpec((1, d), lambda i, e: (0, 0)),
        ],
        out_specs=pl.BlockSpec((tb, d), lambda i, e: (i, 0)),
        out_shape=jax.ShapeDtypeStruct((rows, d), F32),
        scratch_shapes=[pltpu.VMEM((eb, tb), F8), pltpu.VMEM((eb // 2, tb), jnp.uint32)],
        compiler_params=_cparams(("arbitrary", "arbitrary")),
        name="peer_experts",
    )(ht, hs, u8, v8, w_scales, *tables, x2, mod3, g_final.reshape(1, d))


def kernel(x, c, ctx, c_ctx, w_mod, b_mod, g_norm_mix, g_norm_ffn, w_in, b_gate, conv_qk, g_fourier,
           g_mlstm, w_out, w_query, sub_keys, expert_u, expert_v, g_final):
    batch, seq, d = x.shape
    ctx_len = ctx.shape[1]
    depth = w_mod.shape[0]
    f_width = g_fourier.shape[1]
    m_width = g_mlstm.shape[1]
    qk_width = conv_qk.shape[-1]
    n_gates = b_gate.shape[1]
    n_main = f_width + qk_width + 2 * m_width
    assert w_in.shape[2] == n_main + n_gates and n_gates <= LANES
    assert f_width == qk_width == m_width and m_width == M_HEADS * LANES
    assert seq % MLSTM_CHUNK == 0 and ctx_len % MLSTM_CHUNK == 0 and seq % GRID_W == 0

    cond_rows = -(-(batch + 1) // SUBLANES) * SUBLANES
    cond = jnp.zeros((cond_rows, d), F32).at[:batch].set(c).at[batch].set(c_ctx)
    mod_all = _modulation(cond, w_mod, b_mod)

    latent_row = _ModRows(lambda i, tm: (i * tm) // seq, seq)
    context_row = _ModRows(lambda i, tm: batch, batch * ctx_len)

    x2 = x.reshape(batch * seq, d)
    c2 = ctx.reshape(batch * ctx_len, d)
    v_col0 = qk_width // LANES
    o_col0 = (qk_width + m_width) // LANES
    zero_state = jnp.zeros((batch, M_HEADS, 2, LANES, 2 * LANES), F32)
    zero_m = jnp.zeros((batch, M_HEADS, 2, SUBLANES, LANES), F32)

    for l in range(depth):
        last = l == depth - 1
        mod3 = mod_all[l].reshape(cond_rows, 1, N_MOD * d)
        w_main = w_in[l, :, :n_main].astype(BF16)
        w_gate = jnp.zeros((d, LANES), F32).at[:, :n_gates].set(w_in[l, :, n_main:]).astype(BF16)
        bg = jnp.zeros((1, LANES), F32).at[0, :n_gates].set(b_gate[l])
        w_out_bf = w_out[l].astype(BF16)
        wq_bf = w_query[l].astype(BF16)
        u8, u_inv = _fp8_tensor(expert_u[l])
        v8, v_inv = _fp8_tensor(expert_v[l])
        w_scales = jnp.zeros((1, LANES), F32).at[0, 0].set(u_inv).at[0, 1].set(v_inv)

        def mixer(tokens, mod_row, n_tok, vertical, state, m_state):
            four, p, gates = _in_projection(tokens, g_norm_mix[l], mod3, mod_row, w_main, w_gate, bg)
            qk = _qk_conv(p, conv_qk[l], qk_width, 0, n_tok, vertical)
            ym, state, m_state = _mlstm(qk, p, gates, g_mlstm[l], state, m_state, batch, n_tok,
                                        v_col0, o_col0)
            return four, ym, state, m_state

        def ffn_and_residuals(tokens, four, ym, mod_row, n_tok, final_norm):
            f = _fourier_mix(four, g_fourier[l], batch, n_tok)
            tokens = _out_projection(f, ym, w_out_bf, tokens, mod3, mod_row)
            routed = _peer_route(tokens, g_norm_ffn[l], mod3, mod_row, wq_bf, sub_keys[l])
            return _peer_experts(routed[0], routed[1], u8, v8, w_scales, routed[2:], tokens, mod3, mod_row,
                                 g_final, final_norm)

        pc, ymc, st, m_st = mixer(c2, context_row, ctx_len, False, zero_state, zero_m)
        p, ym, _, _ = mixer(x2, latent_row, seq, True, st, m_st)
        x2 = ffn_and_residuals(x2, p, ym, latent_row, seq, last)
        if not last:
            c2 = ffn_and_residuals(c2, pc, ymc, context_row, ctx_len, False)
    return x2.reshape(batch, seq, d)
```

```python
import functools
import math
from typing import Callable, NamedTuple

import jax
import jax.numpy as jnp
from jax import lax
from jax.experimental import pallas as pl
from jax.experimental.pallas import tpu as pltpu

F32 = jnp.float32
BF16 = jnp.bfloat16

F_GROUPS = 4
M_HEADS = 8
GRID_W = 64
N_KEYS = 128
PEER_HEADS = 8
PEER_TOPK = 16
N_MOD = 6
EPS = 1e-6

LANES = 128
SUBLANES = 8
VMEM_LIMIT = 56 * 1024 * 1024

MLSTM_CHUNK = 128
DFT_N2 = 128
DFT_KB = 8
NEG = -(2.0 ** 110)
REMOVED = 2.0 ** 120
RANK_NONE = 99.0
F8 = jnp.float8_e4m3fn
F8_TARGET = 224.0
TINY = 1e-30


def _cparams(sem):
    return pltpu.CompilerParams(dimension_semantics=sem, vmem_limit_bytes=VMEM_LIMIT)


class _ModRows(NamedTuple):
    row: Callable
    group: int


def _row_tile(limit, mod_row):
    tile = min(limit, mod_row.group)
    assert mod_row.group % tile == 0
    return tile


def _sigmoid(x):
    return 1.0 / (1.0 + jnp.exp(-x))


def _split3(x):
    p0 = x.astype(BF16)
    r1 = x - p0.astype(F32)
    p1 = r1.astype(BF16)
    p2 = (r1 - p1.astype(F32)).astype(BF16)
    return p0, p1, p2


def _dot01_left(a01, x):
    return sum(jnp.dot(a01, p, preferred_element_type=F32) for p in _split3(x))


def _dot01_right(x, b01):
    return sum(jnp.dot(p, b01, preferred_element_type=F32) for p in _split3(x))


def _mod_kernel(c_ref, w_ref, b_ref, o_ref):
    c = c_ref[...]
    s = (c * _sigmoid(c)).astype(BF16)
    o_ref[0] = jnp.dot(s, w_ref[0].astype(BF16), preferred_element_type=F32) + b_ref[0]


def _modulation(cond, w_mod, b_mod):
    n_layers, d, n_out = w_mod.shape
    rows = cond.shape[0]
    tn = 1024
    return pl.pallas_call(
        _mod_kernel,
        grid=(n_layers, n_out // tn),
        in_specs=[
            pl.BlockSpec((rows, d), lambda l, j: (0, 0)),
            pl.BlockSpec((1, d, tn), lambda l, j: (l, 0, j)),
            pl.BlockSpec((1, 1, tn), lambda l, j: (l, 0, j)),
        ],
        out_specs=pl.BlockSpec((1, rows, tn), lambda l, j: (l, 0, j)),
        out_shape=jax.ShapeDtypeStruct((n_layers, rows, n_out), F32),
        compiler_params=_cparams(("arbitrary", "arbitrary")),
        name="modulation",
    )(cond, w_mod, b_mod.reshape(n_layers, 1, n_out))


def _rms_modulate(x, g, shift, scale):
    ms = jnp.mean(x * x, axis=-1, keepdims=True)
    return (x * lax.rsqrt(ms + EPS) * g) * (1.0 + scale) + shift


def _inproj_kernel(x_ref, g_ref, sh_ref, sc_ref, w_ref, wg_ref, bg_ref, four_ref, p_ref, gate_ref, h_scr):
    j = pl.program_id(1)

    @pl.when(j == 0)
    def _():
        h = _rms_modulate(x_ref[...], g_ref[...], sh_ref[0], sc_ref[0]).astype(BF16)
        h_scr[...] = h
        gate_ref[...] = jnp.dot(h, wg_ref[...], preferred_element_type=F32) + bg_ref[...]

    acc = jnp.dot(h_scr[...], w_ref[...], preferred_element_type=F32)

    @pl.when(j == 0)
    def _():
        four_ref[...] = acc

    @pl.when(j > 0)
    def _():
        p_ref[...] = acc.astype(BF16)


def _in_projection(x2, g, mod3, mod_row, w_main, w_gate, b_gate):
    rows, d = x2.shape
    n_main = w_main.shape[1]
    tm = _row_tile(1024, mod_row)
    tn = 1024
    return pl.pallas_call(
        _inproj_kernel,
        grid=(rows // tm, n_main // tn),
        in_specs=[
            pl.BlockSpec((tm, d), lambda i, j: (i, 0)),
            pl.BlockSpec((1, d), lambda i, j: (0, 0)),
            pl.BlockSpec((1, 1, d), lambda i, j: (mod_row.row(i, tm), 0, 0)),
            pl.BlockSpec((1, 1, d), lambda i, j: (mod_row.row(i, tm), 0, 1)),
            pl.BlockSpec((d, tn), lambda i, j: (0, j)),
            pl.BlockSpec((d, LANES), lambda i, j: (0, 0)),
            pl.BlockSpec((1, LANES), lambda i, j: (0, 0)),
        ],
        out_specs=[
            pl.BlockSpec((tm, tn), lambda i, j: (i, 0)),
            pl.BlockSpec((tm, tn), lambda i, j: (i, jnp.maximum(j - 1, 0))),
            pl.BlockSpec((tm, LANES), lambda i, j: (i, 0)),
        ],
        out_shape=[
            jax.ShapeDtypeStruct((rows, tn), F32),
            jax.ShapeDtypeStruct((rows, n_main - tn), BF16),
            jax.ShapeDtypeStruct((rows, LANES), F32),
        ],
        scratch_shapes=[pltpu.VMEM((tm, d), BF16)],
        compiler_params=_cparams(("arbitrary", "arbitrary")),
        name="in_projection",
    )(x2, g.reshape(1, d), mod3, mod3, w_main, w_gate, b_gate)


def _conv_kernel(*refs, tb, width, vertical, blocks_per_image):
    if vertical:
        cur_ref, top_ref, bot_ref, w_ref, o_ref = refs
    else:
        cur_ref, w_ref, o_ref = refs
    cur = cur_ref[...].astype(F32)
    ch = cur.shape[1]
    wpos = lax.rem(lax.broadcasted_iota(jnp.int32, (tb, ch), 0), width)
    first_col = wpos == 0
    last_col = wpos == width - 1
    if vertical:
        r = lax.rem(pl.program_id(0), blocks_per_image)
        top = jnp.where(r == 0, 0.0, top_ref[...].astype(F32))
        bot = jnp.where(r == blocks_per_image - 1, 0.0, bot_ref[...].astype(F32))
        ext = jnp.concatenate([top, cur, bot], axis=0)
        bases = [(dr, ext[dr * width:dr * width + tb]) for dr in range(3)]
    else:
        bases = [(1, cur)]
    acc = jnp.zeros((tb, ch), F32)
    for dr, base in bases:
        left = jnp.where(first_col, 0.0, pltpu.roll(base, 1, axis=0))
        right = jnp.where(last_col, 0.0, pltpu.roll(base, tb - 1, axis=0))
        for dw, shifted in enumerate((left, base, right)):
            k = dr * 3 + dw
            acc = acc + shifted * w_ref[k:k + 1, :]
    o_ref[...] = (acc * _sigmoid(acc)).astype(BF16)


def _qk_conv(p, conv_w, qk_width, col_block, tokens_per_image, vertical):
    rows = p.shape[0]
    w9 = conv_w.reshape(9, qk_width)
    if vertical:
        width = GRID_W
        tb = min(512, tokens_per_image)
        bpi = tokens_per_image // tb
        halo = tb // width
        n_halo = rows // width
        in_specs = [
            pl.BlockSpec((tb, qk_width), lambda i: (i, col_block)),
            pl.BlockSpec((width, qk_width), lambda i: (jnp.maximum(i * halo - 1, 0), col_block)),
            pl.BlockSpec((width, qk_width), lambda i: (jnp.minimum((i + 1) * halo, n_halo - 1), col_block)),
            pl.BlockSpec((9, qk_width), lambda i: (0, 0)),
        ]
        args = (p, p, p, w9)
    else:
        width = tb = tokens_per_image
        bpi = 1
        in_specs = [
            pl.BlockSpec((tb, qk_width), lambda i: (i, col_block)),
            pl.BlockSpec((9, qk_width), lambda i: (0, 0)),
        ]
        args = (p, w9)
    return pl.pallas_call(
        functools.partial(_conv_kernel, tb=tb, width=width, vertical=vertical, blocks_per_image=bpi),
        grid=(rows // tb,),
        in_specs=in_specs,
        out_specs=pl.BlockSpec((tb, qk_width), lambda i: (i, 0)),
        out_shape=jax.ShapeDtypeStruct((rows, qk_width), BF16),
        compiler_params=_cparams(("arbitrary",)),
        name="qk_conv_latent" if vertical else "qk_conv_context",
    )(*args)


def _mlstm_chunks(chains):
    n = range(len(chains))
    ch = chains
    L = ch[0]["q"].shape[0]
    logsig = [jnp.minimum(c["gch"], 0.0) - jnp.log(1.0 + jnp.exp(-jnp.abs(c["gch"]))) for c in ch]
    shape = ch[0]["gch"].shape
    ig = [jnp.broadcast_to(jnp.sum(jnp.where(ch[i]["sel_i"], ch[i]["gch"], 0.0), axis=-1, keepdims=True),
                           shape) for i in n]
    lf = [jnp.broadcast_to(jnp.sum(jnp.where(ch[i]["sel_f"], logsig[i], 0.0), axis=-1, keepdims=True),
                           shape) for i in n]
    b_p = [_split3(lf[i]) for i in n]
    b = [sum(jnp.dot(ch[i]["cum"], p, preferred_element_type=F32) for p in b_p[i]) for i in n]
    b_end = [b[i][0:1, :] if ch[i]["reverse"] else b[i][L - 1:L, :] for i in n]
    a_t = [(ig[i] - b[i]).T for i in n]
    dmat = [jnp.where(ch[i]["causal"], b[i] + a_t[i], NEG) for i in n]
    g = [b[i] + ch[i]["m"] for i in n]
    m_j = [jnp.maximum(g[i], jnp.max(dmat[i], axis=-1, keepdims=True)) for i in n]
    pmat = [jnp.exp(dmat[i] - m_j[i]) for i in n]
    s_raw = [lax.dot_general(c["q"], c["k"], (((1,), (1,)), ((), ())), preferred_element_type=F32)
             for c in ch]
    s = [(s_raw[i] * pmat[i]).astype(BF16) for i in n]
    intra = [jnp.dot(s[i], ch[i]["vaug"], preferred_element_type=F32) for i in n]
    carried = [jnp.dot(c["q"], c["state"].astype(BF16), preferred_element_type=F32) for c in ch]
    inter = [jnp.exp(g[i] - m_j[i]) for i in n]
    tot = [intra[i] + jnp.concatenate([inter[i], inter[i]], axis=1) * carried[i] for i in n]
    h = [tot[i][:, :LANES] / jnp.maximum(jnp.abs(tot[i][:, LANES:]), jnp.exp(-m_j[i])) for i in n]
    a = [b_end[i] - b[i] + ig[i] for i in n]
    m_new = [jnp.maximum(b_end[i] + ch[i]["m"], jnp.max(a[i], axis=0, keepdims=True)) for i in n]
    kw = [(ch[i]["k"].astype(F32) * jnp.exp(a[i] - m_new[i])).astype(BF16) for i in n]
    f_old = [jnp.exp(b_end[i] + ch[i]["m"] - m_new[i]) for i in n]
    upd = [pl.dot(kw[i], ch[i]["vaug"], trans_a=True) for i in n]
    state_new = [jnp.concatenate([f_old[i], f_old[i]], axis=1) * ch[i]["state"] + upd[i] for i in n]
    return [(h[i], state_new[i], m_new[i]) for i in n]


def _mlstm_kernel(q_ref, k_ref, v_ref, o_ref, gt_ref, g_ref, sin_ref, min_ref,
                  y_ref, sout_ref, mout_ref, hf_scr, hb_scr, *, seq, k_scale):
    L = MLSTM_CHUNK
    nc = seq // L
    pair = pl.program_id(1)
    lane = lax.broadcasted_iota(jnp.int32, (1, LANES), 1)
    rr = lax.broadcasted_iota(jnp.int32, (L, L), 0)
    cc = lax.broadcasted_iota(jnp.int32, (L, L), 1)
    causal = (rr >= cc, rr <= cc)
    cum = tuple(jnp.where(m, 1.0, 0.0).astype(BF16) for m in causal)
    ones = jnp.ones((L, LANES), BF16)
    qmask, kmask, sel = [], [], []
    for j in range(2):
        own = (lane // (LANES // 2)) == j
        qmask.append(jnp.where(own, 1.0, 0.0).astype(BF16))
        kmask.append(jnp.where(own, k_scale, 0.0).astype(BF16))
        head = 2 * pair + j
        sel.append([lane == kind * M_HEADS + head for kind in range(4)])

    def body(c, carry):
        chains, dest = [], []
        for d in range(2):
            rows = pl.ds(pl.multiple_of((c if d == 0 else nc - 1 - c) * L, L), L)
            q_all, k_all, gch = q_ref[rows, :], k_ref[rows, :], gt_ref[rows, :]
            for j in range(2):
                hcols = slice(j * LANES, (j + 1) * LANES)
                idx = 2 * (2 * j + d)
                chains.append(dict(
                    q=q_all * qmask[j], k=k_all * kmask[j],
                    vaug=jnp.concatenate([v_ref[rows, hcols], ones], axis=1), gch=gch,
                    sel_i=sel[j][2 * d], sel_f=sel[j][2 * d + 1], cum=cum[d], causal=causal[d],
                    reverse=d == 1, state=carry[idx], m=carry[idx + 1]))
                dest.append((hf_scr if d == 0 else hb_scr, rows, hcols, idx))
        carry = list(carry)
        for (scr, rows, hcols, idx), (h, state, m) in zip(dest, _mlstm_chunks(chains)):
            scr[rows, hcols] = h.astype(BF16)
            carry[idx], carry[idx + 1] = state, m
        return tuple(carry)

    init = []
    for j in range(2):
        for d in range(2):
            init += [sin_ref[0, j, d], min_ref[0, j, d][0:1, :]]
    final = lax.fori_loop(0, nc, body, tuple(init))
    for j in range(2):
        for d in range(2):
            idx = 2 * (2 * j + d)
            sout_ref[0, j, d] = final[idx]
            mout_ref[0, j, d] = jnp.broadcast_to(final[idx + 1], (SUBLANES, LANES))

    def finish(c, _):
        rows = pl.ds(pl.multiple_of(c * L, L), L)
        for j in range(2):
            hcols = slice(j * LANES, (j + 1) * LANES)
            h = hf_scr[rows, hcols].astype(F32) + hb_scr[rows, hcols].astype(F32)
            ms = jnp.mean(h * h, axis=-1, keepdims=True)
            y = h * lax.rsqrt(ms + EPS) * g_ref[:, hcols]
            y_ref[rows, hcols] = (y * _sigmoid(o_ref[rows, hcols].astype(F32))).astype(BF16)
        return 0

    lax.fori_loop(0, nc, finish, 0)


def _mlstm(qk, p, gates, g_mlstm, state_in, m_in, batch, seq, v_col0, o_col0):
    rows = qk.shape[0]
    pw = 2 * LANES
    k_col0 = qk.shape[1] // (2 * LANES)
    dk = qk.shape[1] // (2 * M_HEADS)
    assert v_col0 % 2 == 0 and o_col0 % 2 == 0
    return pl.pallas_call(
        functools.partial(_mlstm_kernel, seq=seq, k_scale=dk ** -0.5),
        grid=(batch, M_HEADS // 2),
        in_specs=[
            pl.BlockSpec((seq, LANES), lambda b, h: (b, h)),
            pl.BlockSpec((seq, LANES), lambda b, h: (b, k_col0 + h)),
            pl.BlockSpec((seq, pw), lambda b, h: (b, v_col0 // 2 + h)),
            pl.BlockSpec((seq, pw), lambda b, h: (b, o_col0 // 2 + h)),
            pl.BlockSpec((seq, LANES), lambda b, h: (b, 0)),
            pl.BlockSpec((1, pw), lambda b, h: (0, h)),
            pl.BlockSpec((1, 2, 2, LANES, 2 * LANES), lambda b, h: (b, h, 0, 0, 0)),
            pl.BlockSpec((1, 2, 2, SUBLANES, LANES), lambda b, h: (b, h, 0, 0, 0)),
        ],
        out_specs=[
            pl.BlockSpec((seq, pw), lambda b, h: (b, h)),
            pl.BlockSpec((1, 2, 2, LANES, 2 * LANES), lambda b, h: (b, h, 0, 0, 0)),
            pl.BlockSpec((1, 2, 2, SUBLANES, LANES), lambda b, h: (b, h, 0, 0, 0)),
        ],
        out_shape=[
            jax.ShapeDtypeStruct((rows, M_HEADS * LANES), BF16),
            jax.ShapeDtypeStruct(state_in.shape, F32),
            jax.ShapeDtypeStruct(m_in.shape, F32),
        ],
        scratch_shapes=[pltpu.VMEM((seq, pw), BF16), pltpu.VMEM((seq, pw), BF16)],
        compiler_params=_cparams(("arbitrary", "arbitrary")),
        name="mlstm",
    )(qk, qk, p, p, gates, g_mlstm.reshape(1, -1), state_in, m_in)


def _dft_mats(n, scale):
    idx = jnp.arange(n, dtype=jnp.int32)
    ang = (2.0 * math.pi / n) * ((idx[:, None] * idx[None, :]) % n).astype(F32)
    return jnp.cos(ang) * scale, jnp.sin(ang) * scale


def _channel_dft(u, cc_ref, sc_ref):
    gw = cc_ref.shape[0]
    zr, zi = [], []
    for g in range(u.shape[1] // gw):
        ug = u[:, g * gw:(g + 1) * gw]
        zr.append(jnp.dot(ug, cc_ref[...], preferred_element_type=F32))
        zi.append(jnp.dot(ug, sc_ref[...], preferred_element_type=F32))
    return jnp.concatenate(zr, axis=1), jnp.concatenate(zi, axis=1)


def _group_rmsnorm(y, g, gw):
    outs = []
    for k in range(y.shape[1] // gw):
        yk = y[:, k * gw:(k + 1) * gw]
        ms = jnp.mean(yk * yk, axis=-1, keepdims=True)
        outs.append(yk * lax.rsqrt(ms + EPS) * g[:, k * gw:(k + 1) * gw])
    return jnp.concatenate(outs, axis=1)


def _fnet_a_kernel(u_ref, cc_ref, sc_ref, f_ref, cw_ref, sw_ref, yr_ref, yi_ref):
    fw = u_ref.shape[3]
    for r in range(u_ref.shape[2]):
        zr, zi = _channel_dft(u_ref[0, :, r, :].astype(BF16), cc_ref, sc_ref)
        z = jnp.concatenate([zr, zi], axis=0).astype(BF16)
        y = jnp.dot(f_ref[...], z, preferred_element_type=F32)
        n2 = y.shape[0] // 2
        yr, yi = y[:n2], y[n2:]
        cw, sw = cw_ref[r], sw_ref[r]
        for j in range(fw // LANES):
            cols = slice(j * LANES, (j + 1) * LANES)
            ocols = slice(r * fw + j * LANES, r * fw + (j + 1) * LANES)
            yr_ref[0, :, ocols] = (yr[:, cols] * cw + yi[:, cols] * sw).astype(BF16)
            yi_ref[0, :, ocols] = (yi[:, cols] * cw - yr[:, cols] * sw).astype(BF16)


def _fnet_b_kernel(yr_ref, yi_ref, gc_ref, gs_ref, g_ref, o_ref, *, gw):
    y = (jnp.dot(gc_ref[...], yr_ref[0], preferred_element_type=F32)
         + jnp.dot(gs_ref[...], yi_ref[0], preferred_element_type=F32))
    out = _group_rmsnorm(y, g_ref[...], gw).astype(BF16)
    o_ref[0] = out.reshape(o_ref.shape[1:])


def _fnet_direct_kernel(u_ref, cc_ref, sc_ref, ct_ref, st_ref, g_ref, o_ref, *, gw):
    zr, zi = _channel_dft(u_ref[...].astype(BF16), cc_ref, sc_ref)
    y = (jnp.dot(ct_ref[...], zr.astype(BF16), preferred_element_type=F32)
         + jnp.dot(st_ref[...], zi.astype(BF16), preferred_element_type=F32))
    o_ref[...] = _group_rmsnorm(y, g_ref[...], gw).astype(BF16)


def _fourier_mix(p, g_fourier, batch, seq):
    f_width = p.shape[1]
    gw = f_width // F_GROUPS
    cc, sc = _dft_mats(gw, gw ** -0.5)
    cc, msc = cc.astype(BF16), (-sc).astype(BF16)
    g2 = g_fourier.reshape(1, f_width)
    rows = batch * seq
    if seq <= 512:
        ct, st = _dft_mats(seq, seq ** -0.5)
        return pl.pallas_call(
            functools.partial(_fnet_direct_kernel, gw=gw),
            grid=(batch,),
            in_specs=[
                pl.BlockSpec((seq, f_width), lambda b: (b, 0)),
                pl.BlockSpec((gw, gw), lambda b: (0, 0)),
                pl.BlockSpec((gw, gw), lambda b: (0, 0)),
                pl.BlockSpec((seq, seq), lambda b: (0, 0)),
                pl.BlockSpec((seq, seq), lambda b: (0, 0)),
                pl.BlockSpec((1, f_width), lambda b: (0, 0)),
            ],
            out_specs=pl.BlockSpec((seq, f_width), lambda b: (b, 0)),
            out_shape=jax.ShapeDtypeStruct((rows, f_width), BF16),
            compiler_params=_cparams(("arbitrary",)),
            name="fourier_direct",
        )(p, cc, msc, ct.astype(BF16), st.astype(BF16), g2)

    n2 = DFT_N2
    n1 = seq // n2
    c2, s2 = _dft_mats(n2, n2 ** -0.5)
    fmat = jnp.concatenate([jnp.concatenate([c2, s2], axis=1),
                            jnp.concatenate([-s2, c2], axis=1)], axis=0).astype(BF16)
    i1 = jnp.arange(n1, dtype=jnp.int32)
    i2 = jnp.arange(n2, dtype=jnp.int32)
    tw = (2.0 * math.pi / seq) * (i1[:, None] * i2[None, :]).astype(F32)
    cw = jnp.broadcast_to(jnp.cos(tw)[:, :, None], (n1, n2, LANES))
    sw = jnp.broadcast_to(jnp.sin(tw)[:, :, None], (n1, n2, LANES))
    nr = SUBLANES
    yr, yi = pl.pallas_call(
        _fnet_a_kernel,
        grid=(batch, n1 // nr),
        in_specs=[
            pl.BlockSpec((1, n2, nr, f_width), lambda b, j: (b, 0, j, 0)),
            pl.BlockSpec((gw, gw), lambda b, j: (0, 0)),
            pl.BlockSpec((gw, gw), lambda b, j: (0, 0)),
            pl.BlockSpec((2 * n2, 2 * n2), lambda b, j: (0, 0)),
            pl.BlockSpec((nr, n2, LANES), lambda b, j: (j, 0, 0)),
            pl.BlockSpec((nr, n2, LANES), lambda b, j: (j, 0, 0)),
        ],
        out_specs=[
            pl.BlockSpec((1, n2, nr * f_width), lambda b, j: (b, 0, j)),
            pl.BlockSpec((1, n2, nr * f_width), lambda b, j: (b, 0, j)),
        ],
        out_shape=[jax.ShapeDtypeStruct((batch, n2, n1 * f_width), BF16)] * 2,
        compiler_params=_cparams(("arbitrary", "arbitrary")),
        name="fourier_stage_a",
    )(p.reshape(batch, n2, n1, f_width), cc, msc, fmat, cw, sw)

    kb = DFT_KB
    c1, s1 = _dft_mats(n1, n1 ** -0.5)
    eye = jnp.eye(kb, dtype=F32)
    gc = jnp.einsum("kn,ab->kabn", c1, eye).reshape(n1 * kb, kb * n1).astype(BF16)
    gs = jnp.einsum("kn,ab->kabn", s1, eye).reshape(n1 * kb, kb * n1).astype(BF16)
    out = pl.pallas_call(
        functools.partial(_fnet_b_kernel, gw=gw),
        grid=(batch, n2 // kb),
        in_specs=[
            pl.BlockSpec((1, kb * n1, f_width), lambda b, j: (b, j, 0)),
            pl.BlockSpec((1, kb * n1, f_width), lambda b, j: (b, j, 0)),
            pl.BlockSpec((n1 * kb, kb * n1), lambda b, j: (0, 0)),
            pl.BlockSpec((n1 * kb, kb * n1), lambda b, j: (0, 0)),
            pl.BlockSpec((1, f_width), lambda b, j: (0, 0)),
        ],
        out_specs=pl.BlockSpec((1, n1, kb, f_width), lambda b, j: (b, 0, j, 0)),
        out_shape=jax.ShapeDtypeStruct((batch, n1, n2, f_width), BF16),
        compiler_params=_cparams(("arbitrary", "arbitrary")),
        name="fourier_stage_b",
    )(yr.reshape(batch, n2 * n1, f_width), yi.reshape(batch, n2 * n1, f_width), gc, gs, g2)
    return out.reshape(rows, f_width)


def _outproj_kernel(f_ref, ym_ref, w_ref, x_ref, ga_ref, o_ref):
    fw = f_ref.shape[1]
    y = (jnp.dot(f_ref[...], w_ref[:fw, :], preferred_element_type=F32)
         + jnp.dot(ym_ref[...], w_ref[fw:, :], preferred_element_type=F32))
    o_ref[...] = x_ref[...] + ga_ref[0] * y


def _out_projection(f, ym, w_out, x2, mod3, mod_row):
    rows, d = x2.shape
    tm = _row_tile(512, mod_row)
    return pl.pallas_call(
        _outproj_kernel,
        grid=(rows // tm,),
        in_specs=[
            pl.BlockSpec((tm, f.shape[1]), lambda i: (i, 0)),
            pl.BlockSpec((tm, ym.shape[1]), lambda i: (i, 0)),
            pl.BlockSpec(w_out.shape, lambda i: (0, 0)),
            pl.BlockSpec((tm, d), lambda i: (i, 0)),
            pl.BlockSpec((1, 1, d), lambda i: (mod_row.row(i, tm), 0, 2)),
        ],
        out_specs=pl.BlockSpec((tm, d), lambda i: (i, 0)),
        out_shape=jax.ShapeDtypeStruct((rows, d), F32),
        compiler_params=_cparams(("arbitrary",)),
        name="out_projection",
    )(f, ym, w_out, x2, mod3)


def _pack_bf16_pair(lo, hi):
    def bf16_bits(x):
        u = lax.bitcast_convert_type(x, jnp.uint32)
        return (u + (jnp.uint32(0x7FFF) + ((u >> 16) & jnp.uint32(1)))) >> 16
    return bf16_bits(lo) | (bf16_bits(hi) << 16)


def _interleaved_key(q, n):
    half = n // 2
    return jnp.where(q < half, 2 * q, 2 * (q - half) + 1)


def _extract_top(arrays, k, exact, interleaved=()):
    idx = range(len(arrays))
    s = list(arrays)
    n = [x.shape[0] for x in s]
    row = [None] * len(s)
    if exact:
        for i in idx:
            row[i] = lax.broadcasted_iota(jnp.int32, s[i].shape, 0)
            if i < len(interleaved) and interleaved[i]:
                row[i] = _interleaved_key(row[i], n[i])
    vals = [[] for _ in s]
    for it in range(k):
        m = [jnp.max(s[i], axis=0, keepdims=True) for i in idx]
        hit = [s[i] == m[i] for i in idx]
        if exact:
            first = [jnp.min(jnp.where(hit[i], row[i], n[i]), axis=0, keepdims=True) for i in idx]
            hit = [row[i] == first[i] for i in idx]
        s = [jnp.where(hit[i], -(it + 1) * REMOVED, s[i]) for i in idx]
        for i in idx:
            vals[i].append(m[i])
    gone = [s[i] <= -REMOVED for i in idx]
    rank = [jnp.where(gone[i], s[i] * (-1.0 / REMOVED) - 1.0, RANK_NONE) for i in idx]
    removed = [jnp.sum(jnp.where(gone[i], 1.0, 0.0), axis=0, keepdims=True) for i in idx]
    return [(vals[i], rank[i], removed[i]) for i in idx]


def _route_chunks(score_pairs, exact):
    k = PEER_TOPK
    n_c = len(score_pairs)
    flat = [s for pair in score_pairs for s in pair]
    tops = _extract_top(flat, k, exact, interleaved=(False, True) * n_c)
    cands, layouts = [], None
    for c in range(n_c):
        cand, layouts = _candidate_sums(tops[2 * c][0], tops[2 * c + 1][0], exact)
        cands.append(cand)
    picks = _extract_top(cands, k, exact)
    return [_gate_factors(score_pairs[c], tops[2 * c], tops[2 * c + 1], cands[c], picks[c], layouts)
            for c in range(n_c)]


def _candidate_sums(v0, v1, exact):
    k = PEER_TOPK
    sub = SUBLANES
    sv0 = jnp.concatenate(v0, axis=0)
    sv1 = jnp.concatenate(v1, axis=0)
    if exact:
        layout = [("row", a, b0) for a in range(k) for b0 in range(0, max(k // (a + 1), 1), sub)]
    else:
        layout = ([("row", 0, 0), ("row", 0, sub), ("row", 1, 0), ("col", 0, sub)]
                  + [("row", a, 0) for a in range(2, k // 3)]
                  + [("col", b, 0) for b in range(k // sub)])
    seen = set()
    blocks = []
    for kind, fixed, start in layout:
        cells = [(fixed, start + r) if kind == "row" else (start + r, fixed) for r in range(sub)]
        keep = [(a + 1) * (b + 1) <= k and (a, b) not in seen for a, b in cells]
        seen.update(cell for cell, kp in zip(cells, keep) if kp)
        blk = (v0[fixed] + sv1[start:start + sub]) if kind == "row" else (sv0[start:start + sub] + v1[fixed])
        if not all(keep):
            brow = lax.broadcasted_iota(jnp.int32, blk.shape, 0)
            mask = functools.reduce(jnp.logical_or, [brow == r for r, kp in enumerate(keep) if kp])
            blk = jnp.where(mask, blk, NEG)
        blocks.append(blk)
    assert len(seen) == sum(k // (a + 1) for a in range(k))
    return jnp.concatenate(blocks, axis=0), layout


def _gate_factors(scores, top0, top1, cand, pick, layout):
    k = PEER_TOPK
    sub = SUBLANES
    s0, s1 = scores
    (v0, rank0, rem0), (v1, rank1, rem1) = top0, top1
    _, crank, rem2 = pick
    cnt = jnp.where(crank < RANK_NONE, 1.0, 0.0)
    z = jnp.sum(cnt * jnp.exp(cand - (v0[0] + v1[0])), axis=0, keepdims=True)
    arow = lax.broadcasted_iota(jnp.int32, (k,) + s0.shape[1:], 0)
    n_rank = jnp.zeros(arow.shape, F32)
    for idx, (kind, fixed, start) in enumerate(layout):
        c_blk = cnt[idx * sub:(idx + 1) * sub]
        if kind == "row":
            n_rank = n_rank + jnp.where(arow == fixed, jnp.sum(c_blk, axis=0, keepdims=True), 0.0)
        else:
            pieces = [c_blk if a0 == start else jnp.zeros_like(c_blk) for a0 in range(0, k, sub)]
            n_rank = n_rank + jnp.concatenate(pieces, axis=0)
    n_of_key = jnp.zeros(rank0.shape, F32)
    for a in range(k):
        n_of_key = jnp.where(rank0 == float(a), n_rank[a:a + 1], n_of_key)
    tie = (jnp.where(rem0 == k, 0.0, 1.0) + jnp.where(rem1 == k, 0.0, 1.0)
           + jnp.where(rem2 == k, 0.0, 1.0))
    return jnp.exp(s0 - v0[0]) / z, n_of_key, jnp.exp(s1 - v1[0]), rank1, tie


def _peer_route_kernel(x_ref, g_ref, sh_ref, sc_ref, wq_ref, keys_ref,
                       ht_ref, hs_ref, e0_ref, nn_ref, e1_ref, r1_ref, s_scr):
    h2 = _rms_modulate(x_ref[...], g_ref[...], sh_ref[0], sc_ref[0])
    ht = h2.T
    amax = jnp.maximum(jnp.max(jnp.abs(ht), axis=0, keepdims=True), TINY)
    ht_ref[...] = (ht * (F8_TARGET / amax)).astype(F8)
    hs_ref[...] = amax * (1.0 / F8_TARGET)
    q = jnp.dot(h2.astype(BF16), wq_ref[...], preferred_element_type=F32)
    tb = q.shape[0]
    kd = keys_ref.shape[-1]
    nt = (((1,), (1,)), ((), ()))
    for hp in range(2 * PEER_HEADS):
        k0, k1, _ = _split3(keys_ref[hp // 2, hp % 2])
        q0, q1, _ = _split3(q[:, hp * kd:(hp + 1) * kd])
        s_scr[hp] = (lax.dot_general(k0, q0, nt, preferred_element_type=F32)
                     + (lax.dot_general(k0, q1, nt, preferred_element_type=F32)
                        + lax.dot_general(k1, q0, nt, preferred_element_type=F32)))

    def route_head(h, _):
        chunks = [slice(c * LANES, (c + 1) * LANES) for c in range(tb // LANES)]

        def route(chunk_cols, exact):
            pairs = [(s_scr[2 * h, :, cols], s_scr[2 * h + 1, :, cols]) for cols in chunk_cols]
            ties = []
            for cols, (e0, n_of_key, e1, rank1, tie) in zip(chunk_cols, _route_chunks(pairs, exact)):
                half = e1.shape[0] // 2
                e0_ref[h, :, cols] = _pack_bf16_pair(e0, e0)
                nn_ref[h, :, cols] = _pack_bf16_pair(n_of_key, n_of_key)
                e1_ref[h, :, cols] = _pack_bf16_pair(e1[:half], e1[half:])
                r1_ref[h, :, cols] = _pack_bf16_pair(rank1[:half], rank1[half:])
                ties.append(tie)
            return ties

        for cols, tie in zip(chunks, route(chunks, False)):
            @pl.when(jnp.max(tie) > 0.0)
            def _(cols=cols):
                route([cols], True)
        return 0

    lax.fori_loop(0, PEER_HEADS, route_head, 0)


def _peer_route(x2, g, mod3, mod_row, wq, sub_keys):
    rows, d = x2.shape
    tb = _row_tile(256, mod_row)
    nk = sub_keys.shape[2]
    order = jnp.concatenate([jnp.arange(0, nk, 2), jnp.arange(1, nk, 2)])
    sub_keys = sub_keys.at[:, 1].set(sub_keys[:, 1][:, order])
    tab_spec = pl.BlockSpec((PEER_HEADS, nk, tb), lambda i: (0, 0, i))
    half_spec = pl.BlockSpec((PEER_HEADS, nk // 2, tb), lambda i: (0, 0, i))
    tab = jax.ShapeDtypeStruct((PEER_HEADS, nk, rows), jnp.uint32)
    half_tab = jax.ShapeDtypeStruct((PEER_HEADS, nk // 2, rows), jnp.uint32)
    return pl.pallas_call(
        _peer_route_kernel,
        grid=(rows // tb,),
        in_specs=[
            pl.BlockSpec((tb, d), lambda i: (i, 0)),
            pl.BlockSpec((1, d), lambda i: (0, 0)),
            pl.BlockSpec((1, 1, d), lambda i: (mod_row.row(i, tb), 0, 3)),
            pl.BlockSpec((1, 1, d), lambda i: (mod_row.row(i, tb), 0, 4)),
            pl.BlockSpec(wq.shape, lambda i: (0, 0)),
            pl.BlockSpec(sub_keys.shape, lambda i: (0, 0, 0, 0)),
        ],
        out_specs=[pl.BlockSpec((d, tb), lambda i: (0, i)), pl.BlockSpec((1, tb), lambda i: (0, i)),
                   tab_spec, tab_spec, half_spec, half_spec],
        out_shape=[jax.ShapeDtypeStruct((d, rows), F8), jax.ShapeDtypeStruct((1, rows), F32),
                   tab, tab, half_tab, half_tab],
        scratch_shapes=[pltpu.VMEM((2 * PEER_HEADS, nk, tb), F32)],
        compiler_params=_cparams(("arbitrary",)),
        name="peer_route",
    )(x2, g.reshape(1, d), mod3, mod3, wq, sub_keys)


def _gelu_tanh(x):
    return 0.5 * x * (1.0 + jnp.tanh(math.sqrt(2.0 / math.pi) * (x + 0.044715 * (x * x * x))))


def _peer_expert_kernel(ht_ref, hs_ref, u_ref, v_ref, ws_ref, e0_ref, nn_ref, e1_ref, r1_ref,
                        x_ref, ga_ref, gf_ref, o_ref, wa_scr, act_scr, *, final_norm):
    e = pl.program_id(1)
    nk = nn_ref.shape[1]
    tb = ht_ref.shape[1]
    ib = u_ref.shape[0] // nk
    sub = SUBLANES
    assert ib == SUBLANES
    i_rows = pl.ds(pl.multiple_of(e * ib, ib), ib)

    def packed(words):
        return pltpu.bitcast(words, BF16)

    @pl.when(e == 0)
    def _():
        o_ref[...] = jnp.zeros(o_ref.shape, F32)

    act = (jnp.dot(u_ref[...], ht_ref[...], preferred_element_type=F32)
           * (hs_ref[...] * ws_ref[0:1, 0:1])).astype(BF16)
    act_scr[...] = pltpu.bitcast(act, jnp.uint32)
    amax = jnp.max(jnp.max(jnp.abs(act), axis=0, keepdims=True), axis=1, keepdims=True)
    bound = PEER_HEADS * jnp.maximum(amax.astype(F32), TINY)
    to_f8 = (F8_TARGET / bound).astype(BF16)
    from_f8 = 1.0 / to_f8.astype(F32)
    n_words = e1_ref.shape[1]
    for il in range(ib):
        for c in range(tb // LANES):
            cols = slice(c * LANES, (c + 1) * LANES)
            w = [None] * (n_words // sub)
            for h in range(PEER_HEADS):
                n_i = packed(jnp.broadcast_to(nn_ref[h, i_rows, cols][il:il + 1], (sub, LANES)))
                e0_i = packed(jnp.broadcast_to(e0_ref[h, i_rows, cols][il:il + 1], (sub, LANES)))
                for s in range(n_words // sub):
                    rows = slice(s * sub, (s + 1) * sub)
                    e1 = packed(e1_ref[h, rows, cols])
                    t = jnp.where(packed(r1_ref[h, rows, cols]) < n_i, e1, jnp.zeros_like(e1)) * e0_i
                    w[s] = t if w[s] is None else w[s] + t
            a = packed(act_scr[il * n_words:(il + 1) * n_words, cols])
            wa = jnp.concatenate(w, axis=0) * _gelu_tanh(a) * to_f8
            wa_scr[il * nk:(il + 1) * nk, cols] = wa.astype(F8)
    o_ref[...] += pl.dot(wa_scr[...], v_ref[...], trans_a=True) * (from_f8 * ws_ref[0:1, 1:2])

    @pl.when(e == pl.num_programs(1) - 1)
    def _():
        y = x_ref[...] + ga_ref[0] * o_ref[...]
        if final_norm:
            ms = jnp.mean(y * y, axis=-1, keepdims=True)
            y = y * lax.rsqrt(ms + EPS) * gf_ref[...]
        o_ref[...] = y


def _fp8_tensor(w):
    amax = jnp.maximum(jnp.max(jnp.abs(w)), TINY)
    return (w * (F8_TARGET / amax)).astype(F8), amax * (1.0 / F8_TARGET)


def _peer_experts(ht, hs, u8, v8, w_scales, tables, x2, mod3, mod_row, g_final, final_norm):
    rows, d = x2.shape
    n_exp = u8.shape[0]
    nk = tables[0].shape[1]
    tb = _row_tile(512, mod_row)
    eb = SUBLANES * nk
    tab_spec = pl.BlockSpec((PEER_HEADS, nk, tb), lambda i, e: (0, 0, i))
    half_spec = pl.BlockSpec((PEER_HEADS, nk // 2, tb), lambda i, e: (0, 0, i))
    return pl.pallas_call(
        functools.partial(_peer_expert_kernel, final_norm=final_norm),
        grid=(rows // tb, n_exp // eb),
        in_specs=[
            pl.BlockSpec((d, tb), lambda i, e: (0, i)),
            pl.BlockSpec((1, tb), lambda i, e: (0, i)),
            pl.BlockSpec((eb, d), lambda i, e: (e, 0)),
            pl.BlockSpec((eb, d), lambda i, e: (e, 0)),
            pl.BlockSpec((1, LANES), lambda i, e: (0, 0)),
            tab_spec, tab_spec, half_spec, half_spec,
            pl.BlockSpec((tb, d), lambda i, e: (i, 0)),
            pl.BlockSpec((1, 1, d), lambda i, e: (mod_row.row(i, tb), 0, 5)),
            pl.BlockSpec((1, d), lambda i, e: (0, 0)),
        ],
        out_specs=pl.BlockSpec((tb, d), lambda i, e: (i, 0)),
        out_shape=jax.ShapeDtypeStruct((rows, d), F32),
        scratch_shapes=[pltpu.VMEM((eb, tb), F8), pltpu.VMEM((eb // 2, tb), jnp.uint32)],
        compiler_params=_cparams(("arbitrary", "arbitrary")),
        name="peer_experts",
    )(ht, hs, u8, v8, w_scales, *tables, x2, mod3, g_final.reshape(1, d))


def kernel(x, c, ctx, c_ctx, w_mod, b_mod, g_norm_mix, g_norm_ffn, w_in, b_gate, conv_qk, g_fourier,
           g_mlstm, w_out, w_query, sub_keys, expert_u, expert_v, g_final):
    batch, seq, d = x.shape
    ctx_len = ctx.shape[1]
    depth = w_mod.shape[0]
    f_width = g_fourier.shape[1]
    m_width = g_mlstm.shape[1]
    qk_width = conv_qk.shape[-1]
    n_gates = b_gate.shape[1]
    n_main = f_width + qk_width + 2 * m_width
    assert w_in.shape[2] == n_main + n_gates and n_gates <= LANES
    assert f_width == qk_width == m_width and m_width == M_HEADS * LANES
    assert seq % MLSTM_CHUNK == 0 and ctx_len % MLSTM_CHUNK == 0 and seq % GRID_W == 0

    cond_rows = -(-(batch + 1) // SUBLANES) * SUBLANES
    cond = jnp.zeros((cond_rows, d), F32).at[:batch].set(c).at[batch].set(c_ctx)
    mod_all = _modulation(cond, w_mod, b_mod)

    latent_row = _ModRows(lambda i, tm: (i * tm) // seq, seq)
    context_row = _ModRows(lambda i, tm: batch, batch * ctx_len)

    x2 = x.reshape(batch * seq, d)
    c2 = ctx.reshape(batch * ctx_len, d)
    v_col0 = qk_width // LANES
    o_col0 = (qk_width + m_width) // LANES
    zero_state = jnp.zeros((batch, M_HEADS, 2, LANES, 2 * LANES), F32)
    zero_m = jnp.zeros((batch, M_HEADS, 2, SUBLANES, LANES), F32)

    for l in range(depth):
        last = l == depth - 1
        mod3 = mod_all[l].reshape(cond_rows, 1, N_MOD * d)
        w_main = w_in[l, :, :n_main].astype(BF16)
        w_gate = jnp.zeros((d, LANES), F32).at[:, :n_gates].set(w_in[l, :, n_main:]).astype(BF16)
        bg = jnp.zeros((1, LANES), F32).at[0, :n_gates].set(b_gate[l])
        w_out_bf = w_out[l].astype(BF16)
        wq_bf = w_query[l].astype(BF16)
        u8, u_inv = _fp8_tensor(expert_u[l])
        v8, v_inv = _fp8_tensor(expert_v[l])
        w_scales = jnp.zeros((1, LANES), F32).at[0, 0].set(u_inv).at[0, 1].set(v_inv)

        def mixer(tokens, mod_row, n_tok, vertical, state, m_state):
            four, p, gates = _in_projection(tokens, g_norm_mix[l], mod3, mod_row, w_main, w_gate, bg)
            qk = _qk_conv(p, conv_qk[l], qk_width, 0, n_tok, vertical)
            ym, state, m_state = _mlstm(qk, p, gates, g_mlstm[l], state, m_state, batch, n_tok,
                                        v_col0, o_col0)
            return four, ym, state, m_state

        def ffn_and_residuals(tokens, four, ym, mod_row, n_tok, final_norm):
            f = _fourier_mix(four, g_fourier[l], batch, n_tok)
            tokens = _out_projection(f, ym, w_out_bf, tokens, mod3, mod_row)
            routed = _peer_route(tokens, g_norm_ffn[l], mod3, mod_row, wq_bf, sub_keys[l])
            return _peer_experts(routed[0], routed[1], u8, v8, w_scales, routed[2:], tokens, mod3, mod_row,
                                 g_final, final_norm)

        pc, ymc, st, m_st = mixer(c2, context_row, ctx_len, False, zero_state, zero_m)
        p, ym, _, _ = mixer(x2, latent_row, seq, True, st, m_st)
        x2 = ffn_and_residuals(x2, p, ym, latent_row, seq, last)
        if not last:
            c2 = ffn_and_residuals(c2, pc, ymc, context_row, ctx_len, False)
    return x2.reshape(batch, seq, d)
```

```python
import functools
import math
from typing import Callable, NamedTuple

import jax
import jax.numpy as jnp
from jax import lax
from jax.experimental import pallas as pl
from jax.experimental.pallas import tpu as pltpu

F32 = jnp.float32
BF16 = jnp.bfloat16

F_GROUPS = 4
M_HEADS = 8
GRID_W = 64
N_KEYS = 128
PEER_HEADS = 8
PEER_TOPK = 16
N_MOD = 6
EPS = 1e-6

LANES = 128
SUBLANES = 8
VMEM_LIMIT = 56 * 1024 * 1024

MLSTM_CHUNK = 128
DFT_N2 = 128
DFT_KB = 8
NEG = -(2.0 ** 110)
REMOVED = 2.0 ** 120
RANK_NONE = 99.0
F8 = jnp.float8_e4m3fn
F8_TARGET = 224.0
TINY = 1e-30


def _cparams(sem):
    return pltpu.CompilerParams(dimension_semantics=sem, vmem_limit_bytes=VMEM_LIMIT)


class _ModRows(NamedTuple):
    row: Callable
    group: int


def _row_tile(limit, mod_row):
    tile = min(limit, mod_row.group)
    assert mod_row.group % tile == 0
    return tile


def _sigmoid(x):
    return 1.0 / (1.0 + jnp.exp(-x))


def _split3(x):
    p0 = x.astype(BF16)
    r1 = x - p0.astype(F32)
    p1 = r1.astype(BF16)
    p2 = (r1 - p1.astype(F32)).astype(BF16)
    return p0, p1, p2


def _dot01_left(a01, x):
    return sum(jnp.dot(a01, p, preferred_element_type=F32) for p in _split3(x))


def _dot01_right(x, b01):
    return sum(jnp.dot(p, b01, preferred_element_type=F32) for p in _split3(x))


def _mod_kernel(c_ref, w_ref, b_ref, o_ref):
    c = c_ref[...]
    s = (c * _sigmoid(c)).astype(BF16)
    o_ref[0] = jnp.dot(s, w_ref[0].astype(BF16), preferred_element_type=F32) + b_ref[0]


def _modulation(cond, w_mod, b_mod):
    n_layers, d, n_out = w_mod.shape
    rows = cond.shape[0]
    tn = 1024
    return pl.pallas_call(
        _mod_kernel,
        grid=(n_layers, n_out // tn),
        in_specs=[
            pl.BlockSpec((rows, d), lambda l, j: (0, 0)),
            pl.BlockSpec((1, d, tn), lambda l, j: (l, 0, j)),
            pl.BlockSpec((1, 1, tn), lambda l, j: (l, 0, j)),
        ],
        out_specs=pl.BlockSpec((1, rows, tn), lambda l, j: (l, 0, j)),
        out_shape=jax.ShapeDtypeStruct((n_layers, rows, n_out), F32),
        compiler_params=_cparams(("arbitrary", "arbitrary")),
        name="modulation",
    )(cond, w_mod, b_mod.reshape(n_layers, 1, n_out))


def _rms_modulate(x, g, shift, scale):
    ms = jnp.mean(x * x, axis=-1, keepdims=True)
    return (x * lax.rsqrt(ms + EPS) * g) * (1.0 + scale) + shift


def _inproj_kernel(x_ref, g_ref, sh_ref, sc_ref, w_ref, wg_ref, bg_ref, four_ref, p_ref, gate_ref, h_scr):
    j = pl.program_id(1)

    @pl.when(j == 0)
    def _():
        h = _rms_modulate(x_ref[...], g_ref[...], sh_ref[0], sc_ref[0]).astype(BF16)
        h_scr[...] = h
        gate_ref[...] = jnp.dot(h, wg_ref[...], preferred_element_type=F32) + bg_ref[...]

    acc = jnp.dot(h_scr[...], w_ref[...], preferred_element_type=F32)

    @pl.when(j == 0)
    def _():
        four_ref[...] = acc

    @pl.when(j > 0)
    def _():
        p_ref[...] = acc.astype(BF16)


def _in_projection(x2, g, mod3, mod_row, w_main, w_gate, b_gate):
    rows, d = x2.shape
    n_main = w_main.shape[1]
    tm = _row_tile(1024, mod_row)
    tn = 1024
    return pl.pallas_call(
        _inproj_kernel,
        grid=(rows // tm, n_main // tn),
        in_specs=[
            pl.BlockSpec((tm, d), lambda i, j: (i, 0)),
            pl.BlockSpec((1, d), lambda i, j: (0, 0)),
            pl.BlockSpec((1, 1, d), lambda i, j: (mod_row.row(i, tm), 0, 0)),
            pl.BlockSpec((1, 1, d), lambda i, j: (mod_row.row(i, tm), 0, 1)),
            pl.BlockSpec((d, tn), lambda i, j: (0, j)),
            pl.BlockSpec((d, LANES), lambda i, j: (0, 0)),
            pl.BlockSpec((1, LANES), lambda i, j: (0, 0)),
        ],
        out_specs=[
            pl.BlockSpec((tm, tn), lambda i, j: (i, 0)),
            pl.BlockSpec((tm, tn), lambda i, j: (i, jnp.maximum(j - 1, 0))),
            pl.BlockSpec((tm, LANES), lambda i, j: (i, 0)),
        ],
        out_shape=[
            jax.ShapeDtypeStruct((rows, tn), F32),
            jax.ShapeDtypeStruct((rows, n_main - tn), BF16),
            jax.ShapeDtypeStruct((rows, LANES), F32),
        ],
        scratch_shapes=[pltpu.VMEM((tm, d), BF16)],
        compiler_params=_cparams(("arbitrary", "arbitrary")),
        name="in_projection",
    )(x2, g.reshape(1, d), mod3, mod3, w_main, w_gate, b_gate)


def _conv_kernel(*refs, tb, width, vertical, blocks_per_image):
    if vertical:
        cur_ref, top_ref, bot_ref, w_ref, o_ref = refs
    else:
        cur_ref, w_ref, o_ref = refs
    cur = cur_ref[...].astype(F32)
    ch = cur.shape[1]
    wpos = lax.rem(lax.broadcasted_iota(jnp.int32, (tb, ch), 0), width)
    first_col = wpos == 0
    last_col = wpos == width - 1
    if vertical:
        r = lax.rem(pl.program_id(0), blocks_per_image)
        top = jnp.where(r == 0, 0.0, top_ref[...].astype(F32))
        bot = jnp.where(r == blocks_per_image - 1, 0.0, bot_ref[...].astype(F32))
        ext = jnp.concatenate([top, cur, bot], axis=0)
        bases = [(dr, ext[dr * width:dr * width + tb]) for dr in range(3)]
    else:
        bases = [(1, cur)]
    acc = jnp.zeros((tb, ch), F32)
    for dr, base in bases:
        left = jnp.where(first_col, 0.0, pltpu.roll(base, 1, axis=0))
        right = jnp.where(last_col, 0.0, pltpu.roll(base, tb - 1, axis=0))
        for dw, shifted in enumerate((left, base, right)):
            k = dr * 3 + dw
            acc = acc + shifted * w_ref[k:k + 1, :]
    o_ref[...] = (acc * _sigmoid(acc)).astype(BF16)


def _qk_conv(p, conv_w, qk_width, col_block, tokens_per_image, vertical):
    rows = p.shape[0]
    w9 = conv_w.reshape(9, qk_width)
    if vertical:
        width = GRID_W
        tb = min(512, tokens_per_image)
        bpi = tokens_per_image // tb
        halo = tb // width
        n_halo = rows // width
        in_specs = [
            pl.BlockSpec((tb, qk_width), lambda i: (i, col_block)),
            pl.BlockSpec((width, qk_width), lambda i: (jnp.maximum(i * halo - 1, 0), col_block)),
            pl.BlockSpec((width, qk_width), lambda i: (jnp.minimum((i + 1) * halo, n_halo - 1), col_block)),
            pl.BlockSpec((9, qk_width), lambda i: (0, 0)),
        ]
        args = (p, p, p, w9)
    else:
        width = tb = tokens_per_image
        bpi = 1
        in_specs = [
            pl.BlockSpec((tb, qk_width), lambda i: (i, col_block)),
            pl.BlockSpec((9, qk_width), lambda i: (0, 0)),
        ]
        args = (p, w9)
    return pl.pallas_call(
        functools.partial(_conv_kernel, tb=tb, width=width, vertical=vertical, blocks_per_image=bpi),
        grid=(rows // tb,),
        in_specs=in_specs,
        out_specs=pl.BlockSpec((tb, qk_width), lambda i: (i, 0)),
        out_shape=jax.ShapeDtypeStruct((rows, qk_width), BF16),
        compiler_params=_cparams(("arbitrary",)),
        name="qk_conv_latent" if vertical else "qk_conv_context",
    )(*args)


def _mlstm_chunks(chains):
    n = range(len(chains))
    ch = chains
    L = ch[0]["q"].shape[0]
    logsig = [jnp.minimum(c["gch"], 0.0) - jnp.log(1.0 + jnp.exp(-jnp.abs(c["gch"]))) for c in ch]
    shape = ch[0]["gch"].shape
    ig = [jnp.broadcast_to(jnp.sum(jnp.where(ch[i]["sel_i"], ch[i]["gch"], 0.0), axis=-1, keepdims=True),
                           shape) for i in n]
    lf = [jnp.broadcast_to(jnp.sum(jnp.where(ch[i]["sel_f"], logsig[i], 0.0), axis=-1, keepdims=True),
                           shape) for i in n]
    b_p = [_split3(lf[i]) for i in n]
    b = [sum(jnp.dot(ch[i]["cum"], p, preferred_element_type=F32) for p in b_p[i]) for i in n]
    b_end = [b[i][0:1, :] if ch[i]["reverse"] else b[i][L - 1:L, :] for i in n]
    a_t = [(ig[i] - b[i]).T for i in n]
    dmat = [jnp.where(ch[i]["causal"], b[i] + a_t[i], NEG) for i in n]
    g = [b[i] + ch[i]["m"] for i in n]
    m_j = [jnp.maximum(g[i], jnp.max(dmat[i], axis=-1, keepdims=True)) for i in n]
    pmat = [jnp.exp(dmat[i] - m_j[i]) for i in n]
    s_raw = [lax.dot_general(c["q"], c["k"], (((1,), (1,)), ((), ())), preferred_element_type=F32)
             for c in ch]
    s = [(s_raw[i] * pmat[i]).astype(BF16) for i in n]
    intra = [jnp.dot(s[i], ch[i]["vaug"], preferred_element_type=F32) for i in n]
    carried = [jnp.dot(c["q"], c["state"].astype(BF16), preferred_element_type=F32) for c in ch]
    inter = [jnp.exp(g[i] - m_j[i]) for i in n]
    tot = [intra[i] + jnp.concatenate([inter[i], inter[i]], axis=1) * carried[i] for i in n]
    h = [tot[i][:, :LANES] / jnp.maximum(jnp.abs(tot[i][:, LANES:]), jnp.exp(-m_j[i])) for i in n]
    a = [b_end[i] - b[i] + ig[i] for i in n]
    m_new = [jnp.maximum(b_end[i] + ch[i]["m"], jnp.max(a[i], axis=0, keepdims=True)) for i in n]
    kw = [(ch[i]["k"].astype(F32) * jnp.exp(a[i] - m_new[i])).astype(BF16) for i in n]
    f_old = [jnp.exp(b_end[i] + ch[i]["m"] - m_new[i]) for i in n]
    upd = [pl.dot(kw[i], ch[i]["vaug"], trans_a=True) for i in n]
    state_new = [jnp.concatenate([f_old[i], f_old[i]], axis=1) * ch[i]["state"] + upd[i] for i in n]
    return [(h[i], state_new[i], m_new[i]) for i in n]


def _mlstm_kernel(q_ref, k_ref, v_ref, o_ref, gt_ref, g_ref, sin_ref, min_ref,
                  y_ref, sout_ref, mout_ref, hf_scr, hb_scr, *, seq, k_scale):
    L = MLSTM_CHUNK
    nc = seq // L
    pair = pl.program_id(1)
    lane = lax.broadcasted_iota(jnp.int32, (1, LANES), 1)
    rr = lax.broadcasted_iota(jnp.int32, (L, L), 0)
    cc = lax.broadcasted_iota(jnp.int32, (L, L), 1)
    causal = (rr >= cc, rr <= cc)
    cum = tuple(jnp.where(m, 1.0, 0.0).astype(BF16) for m in causal)
    ones = jnp.ones((L, LANES), BF16)
    qmask, kmask, sel = [], [], []
    for j in range(2):
        own = (lane // (LANES // 2)) == j
        qmask.append(jnp.where(own, 1.0, 0.0).astype(BF16))
        kmask.append(jnp.where(own, k_scale, 0.0).astype(BF16))
        head = 2 * pair + j
        sel.append([lane == kind * M_HEADS + head for kind in range(4)])

    def body(c, carry):
        chains, dest = [], []
        for d in range(2):
            rows = pl.ds(pl.multiple_of((c if d == 0 else nc - 1 - c) * L, L), L)
            q_all, k_all, gch = q_ref[rows, :], k_ref[rows, :], gt_ref[rows, :]
            for j in range(2):
                hcols = slice(j * LANES, (j + 1) * LANES)
                idx = 2 * (2 * j + d)
                chains.append(dict(
                    q=q_all * qmask[j], k=k_all * kmask[j],
                    vaug=jnp.concatenate([v_ref[rows, hcols], ones], axis=1), gch=gch,
                    sel_i=sel[j][2 * d], sel_f=sel[j][2 * d + 1], cum=cum[d], causal=causal[d],
                    reverse=d == 1, state=carry[idx], m=carry[idx + 1]))
                dest.append((hf_scr if d == 0 else hb_scr, rows, hcols, idx))
        carry = list(carry)
        for (scr, rows, hcols, idx), (h, state, m) in zip(dest, _mlstm_chunks(chains)):
            scr[rows, hcols] = h.astype(BF16)
            carry[idx], carry[idx + 1] = state, m
        return tuple(carry)

    init = []
    for j in range(2):
        for d in range(2):
            init += [sin_ref[0, j, d], min_ref[0, j, d][0:1, :]]
    final = lax.fori_loop(0, nc, body, tuple(init))
    for j in range(2):
        for d in range(2):
            idx = 2 * (2 * j + d)
            sout_ref[0, j, d] = final[idx]
            mout_ref[0, j, d] = jnp.broadcast_to(final[idx + 1], (SUBLANES, LANES))

    def finish(c, _):
        rows = pl.ds(pl.multiple_of(c * L, L), L)
        for j in range(2):
            hcols = slice(j * LANES, (j + 1) * LANES)
            h = hf_scr[rows, hcols].astype(F32) + hb_scr[rows, hcols].astype(F32)
            ms = jnp.mean(h * h, axis=-1, keepdims=True)
            y = h * lax.rsqrt(ms + EPS) * g_ref[:, hcols]
            y_ref[rows, hcols] = (y * _sigmoid(o_ref[rows, hcols].astype(F32))).astype(BF16)
        return 0

    lax.fori_loop(0, nc, finish, 0)


def _mlstm(qk, p, gates, g_mlstm, state_in, m_in, batch, seq, v_col0, o_col0):
    rows = qk.shape[0]
    pw = 2 * LANES
    k_col0 = qk.shape[1] // (2 * LANES)
    dk = qk.shape[1] // (2 * M_HEADS)
    assert v_col0 % 2 == 0 and o_col0 % 2 == 0
    return pl.pallas_call(
        functools.partial(_mlstm_kernel, seq=seq, k_scale=dk ** -0.5),
        grid=(batch, M_HEADS // 2),
        in_specs=[
            pl.BlockSpec((seq, LANES), lambda b, h: (b, h)),
            pl.BlockSpec((seq, LANES), lambda b, h: (b, k_col0 + h)),
            pl.BlockSpec((seq, pw), lambda b, h: (b, v_col0 // 2 + h)),
            pl.BlockSpec((seq, pw), lambda b, h: (b, o_col0 // 2 + h)),
            pl.BlockSpec((seq, LANES), lambda b, h: (b, 0)),
            pl.BlockSpec((1, pw), lambda b, h: (0, h)),
            pl.BlockSpec((1, 2, 2, LANES, 2 * LANES), lambda b, h: (b, h, 0, 0, 0)),
            pl.BlockSpec((1, 2, 2, SUBLANES, LANES), lambda b, h: (b, h, 0, 0, 0)),
        ],
        out_specs=[
            pl.BlockSpec((seq, pw), lambda b, h: (b, h)),
            pl.BlockSpec((1, 2, 2, LANES, 2 * LANES), lambda b, h: (b, h, 0, 0, 0)),
            pl.BlockSpec((1, 2, 2, SUBLANES, LANES), lambda b, h: (b, h, 0, 0, 0)),
        ],
        out_shape=[
            jax.ShapeDtypeStruct((rows, M_HEADS * LANES), BF16),
            jax.ShapeDtypeStruct(state_in.shape, F32),
            jax.ShapeDtypeStruct(m_in.shape, F32),
        ],
        scratch_shapes=[pltpu.VMEM((seq, pw), BF16), pltpu.VMEM((seq, pw), BF16)],
        compiler_params=_cparams(("arbitrary", "arbitrary")),
        name="mlstm",
    )(qk, qk, p, p, gates, g_mlstm.reshape(1, -1), state_in, m_in)


def _dft_mats(n, scale):
    idx = jnp.arange(n, dtype=jnp.int32)
    ang = (2.0 * math.pi / n) * ((idx[:, None] * idx[None, :]) % n).astype(F32)
    return jnp.cos(ang) * scale, jnp.sin(ang) * scale


def _channel_dft(u, cc_ref, sc_ref):
    gw = cc_ref.shape[0]
    zr, zi = [], []
    for g in range(u.shape[1] // gw):
        ug = u[:, g * gw:(g + 1) * gw]
        zr.append(jnp.dot(ug, cc_ref[...], preferred_element_type=F32))
        zi.append(jnp.dot(ug, sc_ref[...], preferred_element_type=F32))
    return jnp.concatenate(zr, axis=1), jnp.concatenate(zi, axis=1)


def _group_rmsnorm(y, g, gw):
    outs = []
    for k in range(y.shape[1] // gw):
        yk = y[:, k * gw:(k + 1) * gw]
        ms = jnp.mean(yk * yk, axis=-1, keepdims=True)
        outs.append(yk * lax.rsqrt(ms + EPS) * g[:, k * gw:(k + 1) * gw])
    return jnp.concatenate(outs, axis=1)


def _fnet_a_kernel(u_ref, cc_ref, sc_ref, f_ref, cw_ref, sw_ref, yr_ref, yi_ref):
    fw = u_ref.shape[3]
    for r in range(u_ref.shape[2]):
        zr, zi = _channel_dft(u_ref[0, :, r, :].astype(BF16), cc_ref, sc_ref)
        z = jnp.concatenate([zr, zi], axis=0).astype(BF16)
        y = jnp.dot(f_ref[...], z, preferred_element_type=F32)
        n2 = y.shape[0] // 2
        yr, yi = y[:n2], y[n2:]
        cw, sw = cw_ref[r], sw_ref[r]
        for j in range(fw // LANES):
            cols = slice(j * LANES, (j + 1) * LANES)
            ocols = slice(r * fw + j * LANES, r * fw + (j + 1) * LANES)
            yr_ref[0, :, ocols] = (yr[:, cols] * cw + yi[:, cols] * sw).astype(BF16)
            yi_ref[0, :, ocols] = (yi[:, cols] * cw - yr[:, cols] * sw).astype(BF16)


def _fnet_b_kernel(yr_ref, yi_ref, gc_ref, gs_ref, g_ref, o_ref, *, gw):
    y = (jnp.dot(gc_ref[...], yr_ref[0], preferred_element_type=F32)
         + jnp.dot(gs_ref[...], yi_ref[0], preferred_element_type=F32))
    out = _group_rmsnorm(y, g_ref[...], gw).astype(BF16)
    o_ref[0] = out.reshape(o_ref.shape[1:])


def _fnet_direct_kernel(u_ref, cc_ref, sc_ref, ct_ref, st_ref, g_ref, o_ref, *, gw):
    zr, zi = _channel_dft(u_ref[...].astype(BF16), cc_ref, sc_ref)
    y = (jnp.dot(ct_ref[...], zr.astype(BF16), preferred_element_type=F32)
         + jnp.dot(st_ref[...], zi.astype(BF16), preferred_element_type=F32))
    o_ref[...] = _group_rmsnorm(y, g_ref[...], gw).astype(BF16)


def _fourier_mix(p, g_fourier, batch, seq):
    f_width = p.shape[1]
    gw = f_width // F_GROUPS
    cc, sc = _dft_mats(gw, gw ** -0.5)
    cc, msc = cc.astype(BF16), (-sc).astype(BF16)
    g2 = g_fourier.reshape(1, f_width)
    rows = batch * seq
    if seq <= 512:
        ct, st = _dft_mats(seq, seq ** -0.5)
        return pl.pallas_call(
            functools.partial(_fnet_direct_kernel, gw=gw),
            grid=(batch,),
            in_specs=[
                pl.BlockSpec((seq, f_width), lambda b: (b, 0)),
                pl.BlockSpec((gw, gw), lambda b: (0, 0)),
                pl.BlockSpec((gw, gw), lambda b: (0, 0)),
                pl.BlockSpec((seq, seq), lambda b: (0, 0)),
                pl.BlockSpec((seq, seq), lambda b: (0, 0)),
                pl.BlockSpec((1, f_width), lambda b: (0, 0)),
            ],
            out_specs=pl.BlockSpec((seq, f_width), lambda b: (b, 0)),
            out_shape=jax.ShapeDtypeStruct((rows, f_width), BF16),
            compiler_params=_cparams(("arbitrary",)),
            name="fourier_direct",
        )(p, cc, msc, ct.astype(BF16), st.astype(BF16), g2)

    n2 = DFT_N2
    n1 = seq // n2
    c2, s2 = _dft_mats(n2, n2 ** -0.5)
    fmat = jnp.concatenate([jnp.concatenate([c2, s2], axis=1),
                            jnp.concatenate([-s2, c2], axis=1)], axis=0).astype(BF16)
    i1 = jnp.arange(n1, dtype=jnp.int32)
    i2 = jnp.arange(n2, dtype=jnp.int32)
    tw = (2.0 * math.pi / seq) * (i1[:, None] * i2[None, :]).astype(F32)
    cw = jnp.broadcast_to(jnp.cos(tw)[:, :, None], (n1, n2, LANES))
    sw = jnp.broadcast_to(jnp.sin(tw)[:, :, None], (n1, n2, LANES))
    nr = SUBLANES
    yr, yi = pl.pallas_call(
        _fnet_a_kernel,
        grid=(batch, n1 // nr),
        in_specs=[
            pl.BlockSpec((1, n2, nr, f_width), lambda b, j: (b, 0, j, 0)),
            pl.BlockSpec((gw, gw), lambda b, j: (0, 0)),
            pl.BlockSpec((gw, gw), lambda b, j: (0, 0)),
            pl.BlockSpec((2 * n2, 2 * n2), lambda b, j: (0, 0)),
            pl.BlockSpec((nr, n2, LANES), lambda b, j: (j, 0, 0)),
            pl.BlockSpec((nr, n2, LANES), lambda b, j: (j, 0, 0)),
        ],
        out_specs=[
            pl.BlockSpec((1, n2, nr * f_width), lambda b, j: (b, 0, j)),
            pl.BlockSpec((1, n2, nr * f_width), lambda b, j: (b, 0, j)),
        ],
        out_shape=[jax.ShapeDtypeStruct((batch, n2, n1 * f_width), BF16)] * 2,
        compiler_params=_cparams(("arbitrary", "arbitrary")),
        name="fourier_stage_a",
    )(p.reshape(batch, n2, n1, f_width), cc, msc, fmat, cw, sw)

    kb = DFT_KB
    c1, s1 = _dft_mats(n1, n1 ** -0.5)
    eye = jnp.eye(kb, dtype=F32)
    gc = jnp.einsum("kn,ab->kabn", c1, eye).reshape(n1 * kb, kb * n1).astype(BF16)
    gs = jnp.einsum("kn,ab->kabn", s1, eye).reshape(n1 * kb, kb * n1).astype(BF16)
    out = pl.pallas_call(
        functools.partial(_fnet_b_kernel, gw=gw),
        grid=(batch, n2 // kb),
        in_specs=[
            pl.BlockSpec((1, kb * n1, f_width), lambda b, j: (b, j, 0)),
            pl.BlockSpec((1, kb * n1, f_width), lambda b, j: (b, j, 0)),
            pl.BlockSpec((n1 * kb, kb * n1), lambda b, j: (0, 0)),
            pl.BlockSpec((n1 * kb, kb * n1), lambda b, j: (0, 0)),
            pl.BlockSpec((1, f_width), lambda b, j: (0, 0)),
        ],
        out_specs=pl.BlockSpec((1, n1, kb, f_width), lambda b, j: (b, 0, j, 0)),
        out_shape=jax.ShapeDtypeStruct((batch, n1, n2, f_width), BF16),
        compiler_params=_cparams(("arbitrary", "arbitrary")),
        name="fourier_stage_b",
    )(yr.reshape(batch, n2 * n1, f_width), yi.reshape(batch, n2 * n1, f_width), gc, gs, g2)
    return out.reshape(rows, f_width)


def _outproj_kernel(f_ref, ym_ref, w_ref, x_ref, ga_ref, o_ref):
    fw = f_ref.shape[1]
    y = (jnp.dot(f_ref[...], w_ref[:fw, :], preferred_element_type=F32)
         + jnp.dot(ym_ref[...], w_ref[fw:, :], preferred_element_type=F32))
    o_ref[...] = x_ref[...] + ga_ref[0] * y


def _out_projection(f, ym, w_out, x2, mod3, mod_row):
    rows, d = x2.shape
    tm = _row_tile(512, mod_row)
    return pl.pallas_call(
        _outproj_kernel,
        grid=(rows // tm,),
        in_specs=[
            pl.BlockSpec((tm, f.shape[1]), lambda i: (i, 0)),
            pl.BlockSpec((tm, ym.shape[1]), lambda i: (i, 0)),
            pl.BlockSpec(w_out.shape, lambda i: (0, 0)),
            pl.BlockSpec((tm, d), lambda i: (i, 0)),
            pl.BlockSpec((1, 1, d), lambda i: (mod_row.row(i, tm), 0, 2)),
        ],
        out_specs=pl.BlockSpec((tm, d), lambda i: (i, 0)),
        out_shape=jax.ShapeDtypeStruct((rows, d), F32),
        compiler_params=_cparams(("arbitrary",)),
        name="out_projection",
    )(f, ym, w_out, x2, mod3)


def _pack_bf16_pair(lo, hi):
    def bf16_bits(x):
        u = lax.bitcast_convert_type(x, jnp.uint32)
        return (u + (jnp.uint32(0x7FFF) + ((u >> 16) & jnp.uint32(1)))) >> 16
    return bf16_bits(lo) | (bf16_bits(hi) << 16)


def _interleaved_key(q, n):
    half = n // 2
    return jnp.where(q < half, 2 * q, 2 * (q - half) + 1)


def _extract_top(arrays, k, exact, interleaved=()):
    idx = range(len(arrays))
    s = list(arrays)
    n = [x.shape[0] for x in s]
    row = [None] * len(s)
    if exact:
        for i in idx:
            row[i] = lax.broadcasted_iota(jnp.int32, s[i].shape, 0)
            if i < len(interleaved) and interleaved[i]:
                row[i] = _interleaved_key(row[i], n[i])
    vals = [[] for _ in s]
    for it in range(k):
        m = [jnp.max(s[i], axis=0, keepdims=True) for i in idx]
        hit = [s[i] == m[i] for i in idx]
        if exact:
            first = [jnp.min(jnp.where(hit[i], row[i], n[i]), axis=0, keepdims=True) for i in idx]
            hit = [row[i] == first[i] for i in idx]
        s = [jnp.where(hit[i], -(it + 1) * REMOVED, s[i]) for i in idx]
        for i in idx:
            vals[i].append(m[i])
    gone = [s[i] <= -REMOVED for i in idx]
    rank = [jnp.where(gone[i], s[i] * (-1.0 / REMOVED) - 1.0, RANK_NONE) for i in idx]
    removed = [jnp.sum(jnp.where(gone[i], 1.0, 0.0), axis=0, keepdims=True) for i in idx]
    return [(vals[i], rank[i], removed[i]) for i in idx]


def _route_chunks(score_pairs, exact):
    k = PEER_TOPK
    n_c = len(score_pairs)
    flat = [s for pair in score_pairs for s in pair]
    tops = _extract_top(flat, k, exact, interleaved=(False, True) * n_c)
    cands, layouts = [], None
    for c in range(n_c):
        cand, layouts = _candidate_sums(tops[2 * c][0], tops[2 * c + 1][0], exact)
        cands.append(cand)
    picks = _extract_top(cands, k, exact)
    return [_gate_factors(score_pairs[c], tops[2 * c], tops[2 * c + 1], cands[c], picks[c], layouts)
            for c in range(n_c)]


def _candidate_sums(v0, v1, exact):
    k = PEER_TOPK
    sub = SUBLANES
    sv0 = jnp.concatenate(v0, axis=0)
    sv1 = jnp.concatenate(v1, axis=0)
    if exact:
        layout = [("row", a, b0) for a in range(k) for b0 in range(0, max(k // (a + 1), 1), sub)]
    else:
        layout = ([("row", 0, 0), ("row", 0, sub), ("row", 1, 0), ("col", 0, sub)]
                  + [("row", a, 0) for a in range(2, k // 3)]
                  + [("col", b, 0) for b in range(k // sub)])
    seen = set()
    blocks = []
    for kind, fixed, start in layout:
        cells = [(fixed, start + r) if kind == "row" else (start + r, fixed) for r in range(sub)]
        keep = [(a + 1) * (b + 1) <= k and (a, b) not in seen for a, b in cells]
        seen.update(cell for cell, kp in zip(cells, keep) if kp)
        blk = (v0[fixed] + sv1[start:start + sub]) if kind == "row" else (sv0[start:start + sub] + v1[fixed])
        if not all(keep):
            brow = lax.broadcasted_iota(jnp.int32, blk.shape, 0)
            mask = functools.reduce(jnp.logical_or, [brow == r for r, kp in enumerate(keep) if kp])
            blk = jnp.where(mask, blk, NEG)
        blocks.append(blk)
    assert len(seen) == sum(k // (a + 1) for a in range(k))
    return jnp.concatenate(blocks, axis=0), layout


def _gate_factors(scores, top0, top1, cand, pick, layout):
    k = PEER_TOPK
    sub = SUBLANES
    s0, s1 = scores
    (v0, rank0, rem0), (v1, rank1, rem1) = top0, top1
    _, crank, rem2 = pick
    cnt = jnp.where(crank < RANK_NONE, 1.0, 0.0)
    z = jnp.sum(cnt * jnp.exp(cand - (v0[0] + v1[0])), axis=0, keepdims=True)
    arow = lax.broadcasted_iota(jnp.int32, (k,) + s0.shape[1:], 0)
    n_rank = jnp.zeros(arow.shape, F32)
    for idx, (kind, fixed, start) in enumerate(layout):
        c_blk = cnt[idx * sub:(idx + 1) * sub]
        if kind == "row":
            n_rank = n_rank + jnp.where(arow == fixed, jnp.sum(c_blk, axis=0, keepdims=True), 0.0)
        else:
            pieces = [c_blk if a0 == start else jnp.zeros_like(c_blk) for a0 in range(0, k, sub)]
            n_rank = n_rank + jnp.concatenate(pieces, axis=0)
    n_of_key = jnp.zeros(rank0.shape, F32)
    for a in range(k):
        n_of_key = jnp.where(rank0 == float(a), n_rank[a:a + 1], n_of_key)
    tie = (jnp.where(rem0 == k, 0.0, 1.0) + jnp.where(rem1 == k, 0.0, 1.0)
           + jnp.where(rem2 == k, 0.0, 1.0))
    return jnp.exp(s0 - v0[0]) / z, n_of_key, jnp.exp(s1 - v1[0]), rank1, tie


def _peer_route_kernel(x_ref, g_ref, sh_ref, sc_ref, wq_ref, keys_ref,
                       ht_ref, hs_ref, e0_ref, nn_ref, e1_ref, r1_ref, s_scr):
    h2 = _rms_modulate(x_ref[...], g_ref[...], sh_ref[0], sc_ref[0])
    ht = h2.T
    amax = jnp.maximum(jnp.max(jnp.abs(ht), axis=0, keepdims=True), TINY)
    ht_ref[...] = (ht * (F8_TARGET / amax)).astype(F8)
    hs_ref[...] = amax * (1.0 / F8_TARGET)
    q = jnp.dot(h2.astype(BF16), wq_ref[...], preferred_element_type=F32)
    tb = q.shape[0]
    kd = keys_ref.shape[-1]
    nt = (((1,), (1,)), ((), ()))
    for hp in range(2 * PEER_HEADS):
        k0, k1, _ = _split3(keys_ref[hp // 2, hp % 2])
        q0, q1, _ = _split3(q[:, hp * kd:(hp + 1) * kd])
        s_scr[hp] = (lax.dot_general(k0, q0, nt, preferred_element_type=F32)
                     + (lax.dot_general(k0, q1, nt, preferred_element_type=F32)
                        + lax.dot_general(k1, q0, nt, preferred_element_type=F32)))

    def route_head(h, _):
        chunks = [slice(c * LANES, (c + 1) * LANES) for c in range(tb // LANES)]

        def route(chunk_cols, exact):
            pairs = [(s_scr[2 * h, :, cols], s_scr[2 * h + 1, :, cols]) for cols in chunk_cols]
            ties = []
            for cols, (e0, n_of_key, e1, rank1, tie) in zip(chunk_cols, _route_chunks(pairs, exact)):
                half = e1.shape[0] // 2
                e0_ref[h, :, cols] = _pack_bf16_pair(e0, e0)
                nn_ref[h, :, cols] = _pack_bf16_pair(n_of_key, n_of_key)
                e1_ref[h, :, cols] = _pack_bf16_pair(e1[:half], e1[half:])
                r1_ref[h, :, cols] = _pack_bf16_pair(rank1[:half], rank1[half:])
                ties.append(tie)
            return ties

        for cols, tie in zip(chunks, route(chunks, False)):
            @pl.when(jnp.max(tie) > 0.0)
            def _(cols=cols):
                route([cols], True)
        return 0

    lax.fori_loop(0, PEER_HEADS, route_head, 0)


def _peer_route(x2, g, mod3, mod_row, wq, sub_keys):
    rows, d = x2.shape
    tb = _row_tile(256, mod_row)
    nk = sub_keys.shape[2]
    order = jnp.concatenate([jnp.arange(0, nk, 2), jnp.arange(1, nk, 2)])
    sub_keys = sub_keys.at[:, 1].set(sub_keys[:, 1][:, order])
    tab_spec = pl.BlockSpec((PEER_HEADS, nk, tb), lambda i: (0, 0, i))
    half_spec = pl.BlockSpec((PEER_HEADS, nk // 2, tb), lambda i: (0, 0, i))
    tab = jax.ShapeDtypeStruct((PEER_HEADS, nk, rows), jnp.uint32)
    half_tab = jax.ShapeDtypeStruct((PEER_HEADS, nk // 2, rows), jnp.uint32)
    return pl.pallas_call(
        _peer_route_kernel,
        grid=(rows // tb,),
        in_specs=[
            pl.BlockSpec((tb, d), lambda i: (i, 0)),
            pl.BlockSpec((1, d), lambda i: (0, 0)),
            pl.BlockSpec((1, 1, d), lambda i: (mod_row.row(i, tb), 0, 3)),
            pl.BlockSpec((1, 1, d), lambda i: (mod_row.row(i, tb), 0, 4)),
            pl.BlockSpec(wq.shape, lambda i: (0, 0)),
            pl.BlockSpec(sub_keys.shape, lambda i: (0, 0, 0, 0)),
        ],
        out_specs=[pl.BlockSpec((d, tb), lambda i: (0, i)), pl.BlockSpec((1, tb), lambda i: (0, i)),
                   tab_spec, tab_spec, half_spec, half_spec],
        out_shape=[jax.ShapeDtypeStruct((d, rows), F8), jax.ShapeDtypeStruct((1, rows), F32),
                   tab, tab, half_tab, half_tab],
        scratch_shapes=[pltpu.VMEM((2 * PEER_HEADS, nk, tb), F32)],
        compiler_params=_cparams(("arbitrary",)),
        name="peer_route",
    )(x2, g.reshape(1, d), mod3, mod3, wq, sub_keys)


def _gelu_tanh(x):
    return 0.5 * x * (1.0 + jnp.tanh(math.sqrt(2.0 / math.pi) * (x + 0.044715 * (x * x * x))))


def _peer_expert_kernel(ht_ref, hs_ref, u_ref, v_ref, us_ref, vs_ref, e0_ref, nn_ref, e1_ref, r1_ref,
                        x_ref, ga_ref, gf_ref, o_ref, wa_scr, act_scr, *, final_norm):
    e = pl.program_id(1)
    nk = nn_ref.shape[1]
    tb = ht_ref.shape[1]
    ib = u_ref.shape[0] // nk
    sub = SUBLANES
    assert ib == SUBLANES
    i_rows = pl.ds(pl.multiple_of(e * ib, ib), ib)

    def packed(words):
        return pltpu.bitcast(words, BF16)

    @pl.when(e == 0)
    def _():
        o_ref[...] = jnp.zeros(o_ref.shape, F32)

    act = (jnp.dot(u_ref[...], ht_ref[...], preferred_element_type=F32)
           * (hs_ref[...] * us_ref[0:1, 0:1])).astype(BF16)
    act_scr[...] = pltpu.bitcast(act, jnp.uint32)
    amax = jnp.max(jnp.max(jnp.abs(act), axis=0, keepdims=True), axis=1, keepdims=True)
    bound = PEER_HEADS * jnp.maximum(amax.astype(F32), TINY)
    to_f8 = (F8_TARGET / bound).astype(BF16)
    from_f8 = 1.0 / to_f8.astype(F32)
    n_words = e1_ref.shape[1]
    for il in range(ib):
        for c in range(tb // LANES):
            cols = slice(c * LANES, (c + 1) * LANES)
            w = [None] * (n_words // sub)
            for h in range(PEER_HEADS):
                n_i = packed(jnp.broadcast_to(nn_ref[h, i_rows, cols][il:il + 1], (sub, LANES)))
                e0_i = packed(jnp.broadcast_to(e0_ref[h, i_rows, cols][il:il + 1], (sub, LANES)))
                for s in range(n_words // sub):
                    rows = slice(s * sub, (s + 1) * sub)
                    e1 = packed(e1_ref[h, rows, cols])
                    t = jnp.where(packed(r1_ref[h, rows, cols]) < n_i, e1, jnp.zeros_like(e1)) * e0_i
                    w[s] = t if w[s] is None else w[s] + t
            a = packed(act_scr[il * n_words:(il + 1) * n_words, cols])
            wa = jnp.concatenate(w, axis=0) * _gelu_tanh(a) * to_f8
            wa_scr[il * nk:(il + 1) * nk, cols] = wa.astype(F8)
    o_ref[...] += pl.dot(wa_scr[...], v_ref[...], trans_a=True) * (from_f8 * vs_ref[0:1, 0:1])

    @pl.when(e == pl.num_programs(1) - 1)
    def _():
        y = x_ref[...] + ga_ref[0] * o_ref[...]
        if final_norm:
            ms = jnp.mean(y * y, axis=-1, keepdims=True)
            y = y * lax.rsqrt(ms + EPS) * gf_ref[...]
        o_ref[...] = y


def _fp8_blocks_kernel(w_ref, o_ref, s_ref):
    w = w_ref[0]
    amax = jnp.max(jnp.max(jnp.abs(w), axis=0, keepdims=True), axis=1, keepdims=True)
    amax = jnp.maximum(amax, TINY)
    o_ref[0] = (w * (F8_TARGET / amax)).astype(F8)
    s_ref[0, 0] = jnp.broadcast_to(amax * (1.0 / F8_TARGET), s_ref.shape[2:])


def _fp8_blocks(w, eb):
    n_layers, n_exp, d = w.shape
    nb = n_exp // eb
    return pl.pallas_call(
        _fp8_blocks_kernel,
        grid=(n_layers, nb),
        in_specs=[pl.BlockSpec((1, eb, d), lambda l, e: (l, e, 0))],
        out_specs=[pl.BlockSpec((1, eb, d), lambda l, e: (l, e, 0)),
                   pl.BlockSpec((1, 1, SUBLANES, LANES), lambda l, e: (l, e, 0, 0))],
        out_shape=[jax.ShapeDtypeStruct(w.shape, F8),
                   jax.ShapeDtypeStruct((n_layers, nb, SUBLANES, LANES), F32)],
        compiler_params=_cparams(("arbitrary", "arbitrary")),
        name="fp8_expert_blocks",
    )(w)


def _expert_block(n_keys):
    return SUBLANES * n_keys


def _peer_experts(ht, hs, u8, v8, u_scales, v_scales, layer, tables, x2, mod3, mod_row, g_final,
                  final_norm):
    rows, d = x2.shape
    n_exp = u8.shape[1]
    nk = tables[0].shape[1]
    tb = _row_tile(512, mod_row)
    eb = _expert_block(nk)
    tab_spec = pl.BlockSpec((PEER_HEADS, nk, tb), lambda i, e: (0, 0, i))
    half_spec = pl.BlockSpec((PEER_HEADS, nk // 2, tb), lambda i, e: (0, 0, i))
    scale_spec = pl.BlockSpec((None, None, SUBLANES, LANES), lambda i, e: (layer, e, 0, 0))
    return pl.pallas_call(
        functools.partial(_peer_expert_kernel, final_norm=final_norm),
        grid=(rows // tb, n_exp // eb),
        in_specs=[
            pl.BlockSpec((d, tb), lambda i, e: (0, i)),
            pl.BlockSpec((1, tb), lambda i, e: (0, i)),
            pl.BlockSpec((None, eb, d), lambda i, e: (layer, e, 0)),
            pl.BlockSpec((None, eb, d), lambda i, e: (layer, e, 0)),
            scale_spec, scale_spec,
            tab_spec, tab_spec, half_spec, half_spec,
            pl.BlockSpec((tb, d), lambda i, e: (i, 0)),
            pl.BlockSpec((1, 1, d), lambda i, e: (mod_row.row(i, tb), 0, 5)),
            pl.BlockSpec((1, d), lambda i, e: (0, 0)),
        ],
        out_specs=pl.BlockSpec((tb, d), lambda i, e: (i, 0)),
        out_shape=jax.ShapeDtypeStruct((rows, d), F32),
        scratch_shapes=[pltpu.VMEM((eb, tb), F8), pltpu.VMEM((eb // 2, tb), jnp.uint32)],
        compiler_params=_cparams(("arbitrary", "arbitrary")),
        name="peer_experts",
    )(ht, hs, u8, v8, u_scales, v_scales, *tables, x2, mod3, g_final.reshape(1, d))


def kernel(x, c, ctx, c_ctx, w_mod, b_mod, g_norm_mix, g_norm_ffn, w_in, b_gate, conv_qk, g_fourier,
           g_mlstm, w_out, w_query, sub_keys, expert_u, expert_v, g_final):
    batch, seq, d = x.shape
    ctx_len = ctx.shape[1]
    depth = w_mod.shape[0]
    f_width = g_fourier.shape[1]
    m_width = g_mlstm.shape[1]
    qk_width = conv_qk.shape[-1]
    n_gates = b_gate.shape[1]
    n_main = f_width + qk_width + 2 * m_width
    assert w_in.shape[2] == n_main + n_gates and n_gates <= LANES
    assert f_width == qk_width == m_width and m_width == M_HEADS * LANES
    assert seq % MLSTM_CHUNK == 0 and ctx_len % MLSTM_CHUNK == 0 and seq % GRID_W == 0

    cond_rows = -(-(batch + 1) // SUBLANES) * SUBLANES
    cond = jnp.zeros((cond_rows, d), F32).at[:batch].set(c).at[batch].set(c_ctx)
    mod_all = _modulation(cond, w_mod, b_mod)

    latent_row = _ModRows(lambda i, tm: (i * tm) // seq, seq)
    context_row = _ModRows(lambda i, tm: batch, batch * ctx_len)

    x2 = x.reshape(batch * seq, d)
    c2 = ctx.reshape(batch * ctx_len, d)
    v_col0 = qk_width // LANES
    o_col0 = (qk_width + m_width) // LANES
    zero_state = jnp.zeros((batch, M_HEADS, 2, LANES, 2 * LANES), F32)
    zero_m = jnp.zeros((batch, M_HEADS, 2, SUBLANES, LANES), F32)
    u8, u_scales = _fp8_blocks(expert_u, _expert_block(sub_keys.shape[3]))
    v8, v_scales = _fp8_blocks(expert_v, _expert_block(sub_keys.shape[3]))

    for l in range(depth):
        last = l == depth - 1
        mod3 = mod_all[l].reshape(cond_rows, 1, N_MOD * d)
        w_main = w_in[l, :, :n_main].astype(BF16)
        w_gate = jnp.zeros((d, LANES), F32).at[:, :n_gates].set(w_in[l, :, n_main:]).astype(BF16)
        bg = jnp.zeros((1, LANES), F32).at[0, :n_gates].set(b_gate[l])
        w_out_bf = w_out[l].astype(BF16)
        wq_bf = w_query[l].astype(BF16)

        def mixer(tokens, mod_row, n_tok, vertical, state, m_state):
            four, p, gates = _in_projection(tokens, g_norm_mix[l], mod3, mod_row, w_main, w_gate, bg)
            qk = _qk_conv(p, conv_qk[l], qk_width, 0, n_tok, vertical)
            ym, state, m_state = _mlstm(qk, p, gates, g_mlstm[l], state, m_state, batch, n_tok,
                                        v_col0, o_col0)
            return four, ym, state, m_state

        def ffn_and_residuals(tokens, four, ym, mod_row, n_tok, final_norm):
            f = _fourier_mix(four, g_fourier[l], batch, n_tok)
            tokens = _out_projection(f, ym, w_out_bf, tokens, mod3, mod_row)
            routed = _peer_route(tokens, g_norm_ffn[l], mod3, mod_row, wq_bf, sub_keys[l])
            return _peer_experts(routed[0], routed[1], u8, v8, u_scales, v_scales, l, routed[2:], tokens,
                                 mod3, mod_row, g_final, final_norm)

        pc, ymc, st, m_st = mixer(c2, context_row, ctx_len, False, zero_state, zero_m)
        p, ym, _, _ = mixer(x2, latent_row, seq, True, st, m_st)
        x2 = ffn_and_residuals(x2, p, ym, latent_row, seq, last)
        if not last:
            c2 = ffn_and_residuals(c2, pc, ymc, context_row, ctx_len, False)
    return x2.reshape(batch, seq, d)
```

```python
import functools
import math
from typing import Callable, NamedTuple

import jax
import jax.numpy as jnp
from jax import lax
from jax.experimental import pallas as pl
from jax.experimental.pallas import tpu as pltpu

F32 = jnp.float32
BF16 = jnp.bfloat16

F_GROUPS = 4
M_HEADS = 8
GRID_W = 64
N_KEYS = 128
PEER_HEADS = 8
PEER_TOPK = 16
N_MOD = 6
EPS = 1e-6

LANES = 128
SUBLANES = 8
VMEM_LIMIT = 56 * 1024 * 1024

MLSTM_CHUNK = 128
DFT_N2 = 128
DFT_KB = 8
NEG = -(2.0 ** 110)
REMOVED = 2.0 ** 120
RANK_NONE = 99.0
F8 = jnp.float8_e4m3fn
F8_TARGET = 224.0
TINY = 1e-30


def _cparams(sem):
    return pltpu.CompilerParams(dimension_semantics=sem, vmem_limit_bytes=VMEM_LIMIT)


class _ModRows(NamedTuple):
    row: Callable
    group: int


def _row_tile(limit, mod_row):
    tile = min(limit, mod_row.group)
    assert mod_row.group % tile == 0
    return tile


def _sigmoid(x):
    return 1.0 / (1.0 + jnp.exp(-x))


def _split3(x):
    p0 = x.astype(BF16)
    r1 = x - p0.astype(F32)
    p1 = r1.astype(BF16)
    p2 = (r1 - p1.astype(F32)).astype(BF16)
    return p0, p1, p2


def _dot01_left(a01, x):
    return sum(jnp.dot(a01, p, preferred_element_type=F32) for p in _split3(x))


def _dot01_right(x, b01):
    return sum(jnp.dot(p, b01, preferred_element_type=F32) for p in _split3(x))


def _mod_kernel(c_ref, w_ref, b_ref, o_ref):
    c = c_ref[...]
    s = (c * _sigmoid(c)).astype(BF16)
    o_ref[0] = jnp.dot(s, w_ref[0].astype(BF16), preferred_element_type=F32) + b_ref[0]


def _modulation(cond, w_mod, b_mod):
    n_layers, d, n_out = w_mod.shape
    rows = cond.shape[0]
    tn = 1024
    return pl.pallas_call(
        _mod_kernel,
        grid=(n_layers, n_out // tn),
        in_specs=[
            pl.BlockSpec((rows, d), lambda l, j: (0, 0)),
            pl.BlockSpec((1, d, tn), lambda l, j: (l, 0, j)),
            pl.BlockSpec((1, 1, tn), lambda l, j: (l, 0, j)),
        ],
        out_specs=pl.BlockSpec((1, rows, tn), lambda l, j: (l, 0, j)),
        out_shape=jax.ShapeDtypeStruct((n_layers, rows, n_out), F32),
        compiler_params=_cparams(("arbitrary", "arbitrary")),
        name="modulation",
    )(cond, w_mod, b_mod.reshape(n_layers, 1, n_out))


def _rms_modulate(x, g, shift, scale):
    ms = jnp.mean(x * x, axis=-1, keepdims=True)
    return (x * lax.rsqrt(ms + EPS) * g) * (1.0 + scale) + shift


def _inproj_kernel(x_ref, g_ref, sh_ref, sc_ref, w_ref, wg_ref, bg_ref, four_ref, p_ref, gate_ref, h_scr):
    j = pl.program_id(1)

    @pl.when(j == 0)
    def _():
        h = _rms_modulate(x_ref[...], g_ref[...], sh_ref[0], sc_ref[0]).astype(BF16)
        h_scr[...] = h
        gate_ref[...] = jnp.dot(h, wg_ref[...], preferred_element_type=F32) + bg_ref[...]

    acc = jnp.dot(h_scr[...], w_ref[...], preferred_element_type=F32)

    @pl.when(j == 0)
    def _():
        four_ref[...] = acc

    @pl.when(j > 0)
    def _():
        p_ref[...] = acc.astype(BF16)


def _in_projection(x2, g, mod3, mod_row, w_main, w_gate, b_gate):
    rows, d = x2.shape
    n_main = w_main.shape[1]
    tm = _row_tile(1024, mod_row)
    tn = 1024
    return pl.pallas_call(
        _inproj_kernel,
        grid=(rows // tm, n_main // tn),
        in_specs=[
            pl.BlockSpec((tm, d), lambda i, j: (i, 0)),
            pl.BlockSpec((1, d), lambda i, j: (0, 0)),
            pl.BlockSpec((1, 1, d), lambda i, j: (mod_row.row(i, tm), 0, 0)),
            pl.BlockSpec((1, 1, d), lambda i, j: (mod_row.row(i, tm), 0, 1)),
            pl.BlockSpec((d, tn), lambda i, j: (0, j)),
            pl.BlockSpec((d, LANES), lambda i, j: (0, 0)),
            pl.BlockSpec((1, LANES), lambda i, j: (0, 0)),
        ],
        out_specs=[
            pl.BlockSpec((tm, tn), lambda i, j: (i, 0)),
            pl.BlockSpec((tm, tn), lambda i, j: (i, jnp.maximum(j - 1, 0))),
            pl.BlockSpec((tm, LANES), lambda i, j: (i, 0)),
        ],
        out_shape=[
            jax.ShapeDtypeStruct((rows, tn), F32),
            jax.ShapeDtypeStruct((rows, n_main - tn), BF16),
            jax.ShapeDtypeStruct((rows, LANES), F32),
        ],
        scratch_shapes=[pltpu.VMEM((tm, d), BF16)],
        compiler_params=_cparams(("arbitrary", "arbitrary")),
        name="in_projection",
    )(x2, g.reshape(1, d), mod3, mod3, w_main, w_gate, b_gate)


def _conv_kernel(*refs, tb, width, vertical, blocks_per_image):
    if vertical:
        cur_ref, top_ref, bot_ref, w_ref, o_ref = refs
    else:
        cur_ref, w_ref, o_ref = refs
    cur = cur_ref[...].astype(F32)
    ch = cur.shape[1]
    wpos = lax.rem(lax.broadcasted_iota(jnp.int32, (tb, ch), 0), width)
    first_col = wpos == 0
    last_col = wpos == width - 1
    if vertical:
        r = lax.rem(pl.program_id(0), blocks_per_image)
        top = jnp.where(r == 0, 0.0, top_ref[...].astype(F32))
        bot = jnp.where(r == blocks_per_image - 1, 0.0, bot_ref[...].astype(F32))
        ext = jnp.concatenate([top, cur, bot], axis=0)
        bases = [(dr, ext[dr * width:dr * width + tb]) for dr in range(3)]
    else:
        bases = [(1, cur)]
    acc = jnp.zeros((tb, ch), F32)
    for dr, base in bases:
        left = jnp.where(first_col, 0.0, pltpu.roll(base, 1, axis=0))
        right = jnp.where(last_col, 0.0, pltpu.roll(base, tb - 1, axis=0))
        for dw, shifted in enumerate((left, base, right)):
            k = dr * 3 + dw
            acc = acc + shifted * w_ref[k:k + 1, :]
    o_ref[...] = (acc * _sigmoid(acc)).astype(BF16)


def _qk_conv(p, conv_w, qk_width, col_block, tokens_per_image, vertical):
    rows = p.shape[0]
    w9 = conv_w.reshape(9, qk_width)
    if vertical:
        width = GRID_W
        tb = min(512, tokens_per_image)
        bpi = tokens_per_image // tb
        halo = tb // width
        n_halo = rows // width
        in_specs = [
            pl.BlockSpec((tb, qk_width), lambda i: (i, col_block)),
            pl.BlockSpec((width, qk_width), lambda i: (jnp.maximum(i * halo - 1, 0), col_block)),
            pl.BlockSpec((width, qk_width), lambda i: (jnp.minimum((i + 1) * halo, n_halo - 1), col_block)),
            pl.BlockSpec((9, qk_width), lambda i: (0, 0)),
        ]
        args = (p, p, p, w9)
    else:
        width = tb = tokens_per_image
        bpi = 1
        in_specs = [
            pl.BlockSpec((tb, qk_width), lambda i: (i, col_block)),
            pl.BlockSpec((9, qk_width), lambda i: (0, 0)),
        ]
        args = (p, w9)
    return pl.pallas_call(
        functools.partial(_conv_kernel, tb=tb, width=width, vertical=vertical, blocks_per_image=bpi),
        grid=(rows // tb,),
        in_specs=in_specs,
        out_specs=pl.BlockSpec((tb, qk_width), lambda i: (i, 0)),
        out_shape=jax.ShapeDtypeStruct((rows, qk_width), BF16),
        compiler_params=_cparams(("arbitrary",)),
        name="qk_conv_latent" if vertical else "qk_conv_context",
    )(*args)


def _mlstm_chunks(chains):
    n = range(len(chains))
    ch = chains
    L = ch[0]["q"].shape[0]
    logsig = [jnp.minimum(c["gch"], 0.0) - jnp.log(1.0 + jnp.exp(-jnp.abs(c["gch"]))) for c in ch]
    shape = ch[0]["gch"].shape
    ig = [jnp.broadcast_to(jnp.sum(jnp.where(ch[i]["sel_i"], ch[i]["gch"], 0.0), axis=-1, keepdims=True),
                           shape) for i in n]
    lf = [jnp.broadcast_to(jnp.sum(jnp.where(ch[i]["sel_f"], logsig[i], 0.0), axis=-1, keepdims=True),
                           shape) for i in n]
    b_p = [_split3(lf[i]) for i in n]
    b = [sum(jnp.dot(ch[i]["cum"], p, preferred_element_type=F32) for p in b_p[i]) for i in n]
    b_end = [b[i][0:1, :] if ch[i]["reverse"] else b[i][L - 1:L, :] for i in n]
    a_t = [(ig[i] - b[i]).T for i in n]
    dmat = [jnp.where(ch[i]["causal"], b[i] + a_t[i], NEG) for i in n]
    g = [b[i] + ch[i]["m"] for i in n]
    m_j = [jnp.maximum(g[i], jnp.max(dmat[i], axis=-1, keepdims=True)) for i in n]
    pmat = [jnp.exp(dmat[i] - m_j[i]) for i in n]
    s_raw = [lax.dot_general(c["q"], c["k"], (((1,), (1,)), ((), ())), preferred_element_type=F32)
             for c in ch]
    s = [(s_raw[i] * pmat[i]).astype(BF16) for i in n]
    intra = [jnp.dot(s[i], ch[i]["vaug"], preferred_element_type=F32) for i in n]
    carried = [jnp.dot(c["q"], c["state"].astype(BF16), preferred_element_type=F32) for c in ch]
    inter = [jnp.exp(g[i] - m_j[i]) for i in n]
    tot = [intra[i] + jnp.concatenate([inter[i], inter[i]], axis=1) * carried[i] for i in n]
    h = [tot[i][:, :LANES] / jnp.maximum(jnp.abs(tot[i][:, LANES:]), jnp.exp(-m_j[i])) for i in n]
    a = [b_end[i] - b[i] + ig[i] for i in n]
    m_new = [jnp.maximum(b_end[i] + ch[i]["m"], jnp.max(a[i], axis=0, keepdims=True)) for i in n]
    kw = [(ch[i]["k"].astype(F32) * jnp.exp(a[i] - m_new[i])).astype(BF16) for i in n]
    f_old = [jnp.exp(b_end[i] + ch[i]["m"] - m_new[i]) for i in n]
    upd = [pl.dot(kw[i], ch[i]["vaug"], trans_a=True) for i in n]
    state_new = [jnp.concatenate([f_old[i], f_old[i]], axis=1) * ch[i]["state"] + upd[i] for i in n]
    return [(h[i], state_new[i], m_new[i]) for i in n]


def _mlstm_kernel(q_ref, k_ref, v_ref, o_ref, gt_ref, g_ref, sin_ref, min_ref,
                  y_ref, sout_ref, mout_ref, hf_scr, hb_scr, *, seq, k_scale):
    L = MLSTM_CHUNK
    nc = seq // L
    pair = pl.program_id(1)
    lane = lax.broadcasted_iota(jnp.int32, (1, LANES), 1)
    rr = lax.broadcasted_iota(jnp.int32, (L, L), 0)
    cc = lax.broadcasted_iota(jnp.int32, (L, L), 1)
    causal = (rr >= cc, rr <= cc)
    cum = tuple(jnp.where(m, 1.0, 0.0).astype(BF16) for m in causal)
    ones = jnp.ones((L, LANES), BF16)
    qmask, kmask, sel = [], [], []
    for j in range(2):
        own = (lane // (LANES // 2)) == j
        qmask.append(jnp.where(own, 1.0, 0.0).astype(BF16))
        kmask.append(jnp.where(own, k_scale, 0.0).astype(BF16))
        head = 2 * pair + j
        sel.append([lane == kind * M_HEADS + head for kind in range(4)])

    def body(c, carry):
        chains, dest = [], []
        for d in range(2):
            rows = pl.ds(pl.multiple_of((c if d == 0 else nc - 1 - c) * L, L), L)
            q_all, k_all, gch = q_ref[rows, :], k_ref[rows, :], gt_ref[rows, :]
            for j in range(2):
                hcols = slice(j * LANES, (j + 1) * LANES)
                idx = 2 * (2 * j + d)
                chains.append(dict(
                    q=q_all * qmask[j], k=k_all * kmask[j],
                    vaug=jnp.concatenate([v_ref[rows, hcols], ones], axis=1), gch=gch,
                    sel_i=sel[j][2 * d], sel_f=sel[j][2 * d + 1], cum=cum[d], causal=causal[d],
                    reverse=d == 1, state=carry[idx], m=carry[idx + 1]))
                dest.append((hf_scr if d == 0 else hb_scr, rows, hcols, idx))
        carry = list(carry)
        for (scr, rows, hcols, idx), (h, state, m) in zip(dest, _mlstm_chunks(chains)):
            scr[rows, hcols] = h.astype(BF16)
            carry[idx], carry[idx + 1] = state, m
        return tuple(carry)

    init = []
    for j in range(2):
        for d in range(2):
            init += [sin_ref[0, j, d], min_ref[0, j, d][0:1, :]]
    final = lax.fori_loop(0, nc, body, tuple(init))
    for j in range(2):
        for d in range(2):
            idx = 2 * (2 * j + d)
            sout_ref[0, j, d] = final[idx]
            mout_ref[0, j, d] = jnp.broadcast_to(final[idx + 1], (SUBLANES, LANES))

    def finish(c, _):
        rows = pl.ds(pl.multiple_of(c * L, L), L)
        for j in range(2):
            hcols = slice(j * LANES, (j + 1) * LANES)
            h = hf_scr[rows, hcols].astype(F32) + hb_scr[rows, hcols].astype(F32)
            ms = jnp.mean(h * h, axis=-1, keepdims=True)
            y = h * lax.rsqrt(ms + EPS) * g_ref[:, hcols]
            y_ref[rows, hcols] = (y * _sigmoid(o_ref[rows, hcols].astype(F32))).astype(BF16)
        return 0

    lax.fori_loop(0, nc, finish, 0)


def _mlstm(qk, p, gates, g_mlstm, state_in, m_in, batch, seq, v_col0, o_col0):
    rows = qk.shape[0]
    pw = 2 * LANES
    k_col0 = qk.shape[1] // (2 * LANES)
    dk = qk.shape[1] // (2 * M_HEADS)
    assert v_col0 % 2 == 0 and o_col0 % 2 == 0
    return pl.pallas_call(
        functools.partial(_mlstm_kernel, seq=seq, k_scale=dk ** -0.5),
        grid=(batch, M_HEADS // 2),
        in_specs=[
            pl.BlockSpec((seq, LANES), lambda b, h: (b, h)),
            pl.BlockSpec((seq, LANES), lambda b, h: (b, k_col0 + h)),
            pl.BlockSpec((seq, pw), lambda b, h: (b, v_col0 // 2 + h)),
            pl.BlockSpec((seq, pw), lambda b, h: (b, o_col0 // 2 + h)),
            pl.BlockSpec((seq, LANES), lambda b, h: (b, 0)),
            pl.BlockSpec((1, pw), lambda b, h: (0, h)),
            pl.BlockSpec((1, 2, 2, LANES, 2 * LANES), lambda b, h: (b, h, 0, 0, 0)),
            pl.BlockSpec((1, 2, 2, SUBLANES, LANES), lambda b, h: (b, h, 0, 0, 0)),
        ],
        out_specs=[
            pl.BlockSpec((seq, pw), lambda b, h: (b, h)),
            pl.BlockSpec((1, 2, 2, LANES, 2 * LANES), lambda b, h: (b, h, 0, 0, 0)),
            pl.BlockSpec((1, 2, 2, SUBLANES, LANES), lambda b, h: (b, h, 0, 0, 0)),
        ],
        out_shape=[
            jax.ShapeDtypeStruct((rows, M_HEADS * LANES), BF16),
            jax.ShapeDtypeStruct(state_in.shape, F32),
            jax.ShapeDtypeStruct(m_in.shape, F32),
        ],
        scratch_shapes=[pltpu.VMEM((seq, pw), BF16), pltpu.VMEM((seq, pw), BF16)],
        compiler_params=_cparams(("arbitrary", "arbitrary")),
        name="mlstm",
    )(qk, qk, p, p, gates, g_mlstm.reshape(1, -1), state_in, m_in)


def _dft_mats(n, scale):
    idx = jnp.arange(n, dtype=jnp.int32)
    ang = (2.0 * math.pi / n) * ((idx[:, None] * idx[None, :]) % n).astype(F32)
    return jnp.cos(ang) * scale, jnp.sin(ang) * scale


def _channel_dft(u, cc_ref, sc_ref):
    gw = cc_ref.shape[0]
    zr, zi = [], []
    for g in range(u.shape[1] // gw):
        ug = u[:, g * gw:(g + 1) * gw]
        zr.append(jnp.dot(ug, cc_ref[...], preferred_element_type=F32))
        zi.append(jnp.dot(ug, sc_ref[...], preferred_element_type=F32))
    return jnp.concatenate(zr, axis=1), jnp.concatenate(zi, axis=1)


def _group_rmsnorm(y, g, gw):
    outs = []
    for k in range(y.shape[1] // gw):
        yk = y[:, k * gw:(k + 1) * gw]
        ms = jnp.mean(yk * yk, axis=-1, keepdims=True)
        outs.append(yk * lax.rsqrt(ms + EPS) * g[:, k * gw:(k + 1) * gw])
    return jnp.concatenate(outs, axis=1)


def _fnet_a_kernel(u_ref, cc_ref, sc_ref, f_ref, cw_ref, sw_ref, yr_ref, yi_ref):
    fw = u_ref.shape[3]
    for r in range(u_ref.shape[2]):
        zr, zi = _channel_dft(u_ref[0, :, r, :].astype(BF16), cc_ref, sc_ref)
        z = jnp.concatenate([zr, zi], axis=0).astype(BF16)
        y = jnp.dot(f_ref[...], z, preferred_element_type=F32)
        n2 = y.shape[0] // 2
        yr, yi = y[:n2], y[n2:]
        cw, sw = cw_ref[r], sw_ref[r]
        for j in range(fw // LANES):
            cols = slice(j * LANES, (j + 1) * LANES)
            ocols = slice(r * fw + j * LANES, r * fw + (j + 1) * LANES)
            yr_ref[0, :, ocols] = (yr[:, cols] * cw + yi[:, cols] * sw).astype(BF16)
            yi_ref[0, :, ocols] = (yi[:, cols] * cw - yr[:, cols] * sw).astype(BF16)


def _fnet_b_kernel(yr_ref, yi_ref, gc_ref, gs_ref, g_ref, o_ref, *, gw):
    y = (jnp.dot(gc_ref[...], yr_ref[0], preferred_element_type=F32)
         + jnp.dot(gs_ref[...], yi_ref[0], preferred_element_type=F32))
    out = _group_rmsnorm(y, g_ref[...], gw).astype(BF16)
    o_ref[0] = out.reshape(o_ref.shape[1:])


def _fnet_direct_kernel(u_ref, cc_ref, sc_ref, ct_ref, st_ref, g_ref, o_ref, *, gw):
    zr, zi = _channel_dft(u_ref[...].astype(BF16), cc_ref, sc_ref)
    y = (jnp.dot(ct_ref[...], zr.astype(BF16), preferred_element_type=F32)
         + jnp.dot(st_ref[...], zi.astype(BF16), preferred_element_type=F32))
    o_ref[...] = _group_rmsnorm(y, g_ref[...], gw).astype(BF16)


def _fourier_mix(p, g_fourier, batch, seq):
    f_width = p.shape[1]
    gw = f_width // F_GROUPS
    cc, sc = _dft_mats(gw, gw ** -0.5)
    cc, msc = cc.astype(BF16), (-sc).astype(BF16)
    g2 = g_fourier.reshape(1, f_width)
    rows = batch * seq
    if seq <= 512:
        ct, st = _dft_mats(seq, seq ** -0.5)
        return pl.pallas_call(
            functools.partial(_fnet_direct_kernel, gw=gw),
            grid=(batch,),
            in_specs=[
                pl.BlockSpec((seq, f_width), lambda b: (b, 0)),
                pl.BlockSpec((gw, gw), lambda b: (0, 0)),
                pl.BlockSpec((gw, gw), lambda b: (0, 0)),
                pl.BlockSpec((seq, seq), lambda b: (0, 0)),
                pl.BlockSpec((seq, seq), lambda b: (0, 0)),
                pl.BlockSpec((1, f_width), lambda b: (0, 0)),
            ],
            out_specs=pl.BlockSpec((seq, f_width), lambda b: (b, 0)),
            out_shape=jax.ShapeDtypeStruct((rows, f_width), BF16),
            compiler_params=_cparams(("arbitrary",)),
            name="fourier_direct",
        )(p, cc, msc, ct.astype(BF16), st.astype(BF16), g2)

    n2 = DFT_N2
    n1 = seq // n2
    c2, s2 = _dft_mats(n2, n2 ** -0.5)
    fmat = jnp.concatenate([jnp.concatenate([c2, s2], axis=1),
                            jnp.concatenate([-s2, c2], axis=1)], axis=0).astype(BF16)
    i1 = jnp.arange(n1, dtype=jnp.int32)
    i2 = jnp.arange(n2, dtype=jnp.int32)
    tw = (2.0 * math.pi / seq) * (i1[:, None] * i2[None, :]).astype(F32)
    cw = jnp.broadcast_to(jnp.cos(tw)[:, :, None], (n1, n2, LANES))
    sw = jnp.broadcast_to(jnp.sin(tw)[:, :, None], (n1, n2, LANES))
    nr = SUBLANES
    yr, yi = pl.pallas_call(
        _fnet_a_kernel,
        grid=(batch, n1 // nr),
        in_specs=[
            pl.BlockSpec((1, n2, nr, f_width), lambda b, j: (b, 0, j, 0)),
            pl.BlockSpec((gw, gw), lambda b, j: (0, 0)),
            pl.BlockSpec((gw, gw), lambda b, j: (0, 0)),
            pl.BlockSpec((2 * n2, 2 * n2), lambda b, j: (0, 0)),
            pl.BlockSpec((nr, n2, LANES), lambda b, j: (j, 0, 0)),
            pl.BlockSpec((nr, n2, LANES), lambda b, j: (j, 0, 0)),
        ],
        out_specs=[
            pl.BlockSpec((1, n2, nr * f_width), lambda b, j: (b, 0, j)),
            pl.BlockSpec((1, n2, nr * f_width), lambda b, j: (b, 0, j)),
        ],
        out_shape=[jax.ShapeDtypeStruct((batch, n2, n1 * f_width), BF16)] * 2,
        compiler_params=_cparams(("arbitrary", "arbitrary")),
        name="fourier_stage_a",
    )(p.reshape(batch, n2, n1, f_width), cc, msc, fmat, cw, sw)

    kb = DFT_KB
    c1, s1 = _dft_mats(n1, n1 ** -0.5)
    eye = jnp.eye(kb, dtype=F32)
    gc = jnp.einsum("kn,ab->kabn", c1, eye).reshape(n1 * kb, kb * n1).astype(BF16)
    gs = jnp.einsum("kn,ab->kabn", s1, eye).reshape(n1 * kb, kb * n1).astype(BF16)
    out = pl.pallas_call(
        functools.partial(_fnet_b_kernel, gw=gw),
        grid=(batch, n2 // kb),
        in_specs=[
            pl.BlockSpec((1, kb * n1, f_width), lambda b, j: (b, j, 0)),
            pl.BlockSpec((1, kb * n1, f_width), lambda b, j: (b, j, 0)),
            pl.BlockSpec((n1 * kb, kb * n1), lambda b, j: (0, 0)),
            pl.BlockSpec((n1 * kb, kb * n1), lambda b, j: (0, 0)),
            pl.BlockSpec((1, f_width), lambda b, j: (0, 0)),
        ],
        out_specs=pl.BlockSpec((1, n1, kb, f_width), lambda b, j: (b, 0, j, 0)),
        out_shape=jax.ShapeDtypeStruct((batch, n1, n2, f_width), BF16),
        compiler_params=_cparams(("arbitrary", "arbitrary")),
        name="fourier_stage_b",
    )(yr.reshape(batch, n2 * n1, f_width), yi.reshape(batch, n2 * n1, f_width), gc, gs, g2)
    return out.reshape(rows, f_width)


def _outproj_kernel(f_ref, ym_ref, w_ref, x_ref, ga_ref, o_ref):
    fw = f_ref.shape[1]
    y = (jnp.dot(f_ref[...], w_ref[:fw, :], preferred_element_type=F32)
         + jnp.dot(ym_ref[...], w_ref[fw:, :], preferred_element_type=F32))
    o_ref[...] = x_ref[...] + ga_ref[0] * y


def _out_projection(f, ym, w_out, x2, mod3, mod_row):
    rows, d = x2.shape
    tm = _row_tile(512, mod_row)
    return pl.pallas_call(
        _outproj_kernel,
        grid=(rows // tm,),
        in_specs=[
            pl.BlockSpec((tm, f.shape[1]), lambda i: (i, 0)),
            pl.BlockSpec((tm, ym.shape[1]), lambda i: (i, 0)),
            pl.BlockSpec(w_out.shape, lambda i: (0, 0)),
            pl.BlockSpec((tm, d), lambda i: (i, 0)),
            pl.BlockSpec((1, 1, d), lambda i: (mod_row.row(i, tm), 0, 2)),
        ],
        out_specs=pl.BlockSpec((tm, d), lambda i: (i, 0)),
        out_shape=jax.ShapeDtypeStruct((rows, d), F32),
        compiler_params=_cparams(("arbitrary",)),
        name="out_projection",
    )(f, ym, w_out, x2, mod3)


def _pack_bf16_pair(lo, hi):
    def bf16_bits(x):
        u = lax.bitcast_convert_type(x, jnp.uint32)
        return (u + (jnp.uint32(0x7FFF) + ((u >> 16) & jnp.uint32(1)))) >> 16
    return bf16_bits(lo) | (bf16_bits(hi) << 16)


def _interleaved_key(q, n):
    half = n // 2
    return jnp.where(q < half, 2 * q, 2 * (q - half) + 1)


def _extract_top(arrays, k, exact, interleaved=()):
    idx = range(len(arrays))
    s = list(arrays)
    n = [x.shape[0] for x in s]
    row = [None] * len(s)
    if exact:
        for i in idx:
            row[i] = lax.broadcasted_iota(jnp.int32, s[i].shape, 0)
            if i < len(interleaved) and interleaved[i]:
                row[i] = _interleaved_key(row[i], n[i])
    vals = [[] for _ in s]
    for it in range(k):
        m = [jnp.max(s[i], axis=0, keepdims=True) for i in idx]
        hit = [s[i] == m[i] for i in idx]
        if exact:
            first = [jnp.min(jnp.where(hit[i], row[i], n[i]), axis=0, keepdims=True) for i in idx]
            hit = [row[i] == first[i] for i in idx]
        s = [jnp.where(hit[i], -(it + 1) * REMOVED, s[i]) for i in idx]
        for i in idx:
            vals[i].append(m[i])
    gone = [s[i] <= -REMOVED for i in idx]
    rank = [jnp.where(gone[i], s[i] * (-1.0 / REMOVED) - 1.0, RANK_NONE) for i in idx]
    removed = [jnp.sum(jnp.where(gone[i], 1.0, 0.0), axis=0, keepdims=True) for i in idx]
    return [(vals[i], rank[i], removed[i]) for i in idx]


def _route_chunks(score_pairs, exact):
    k = PEER_TOPK
    n_c = len(score_pairs)
    flat = [s for pair in score_pairs for s in pair]
    tops = _extract_top(flat, k, exact, interleaved=(False, True) * n_c)
    cands, layouts = [], None
    for c in range(n_c):
        cand, layouts = _candidate_sums(tops[2 * c][0], tops[2 * c + 1][0], exact)
        cands.append(cand)
    picks = _extract_top(cands, k, exact)
    return [_gate_factors(score_pairs[c], tops[2 * c], tops[2 * c + 1], cands[c], picks[c], layouts)
            for c in range(n_c)]


def _candidate_sums(v0, v1, exact):
    k = PEER_TOPK
    sub = SUBLANES
    sv0 = jnp.concatenate(v0, axis=0)
    sv1 = jnp.concatenate(v1, axis=0)
    if exact:
        layout = [("row", a, b0) for a in range(k) for b0 in range(0, max(k // (a + 1), 1), sub)]
    else:
        layout = ([("row", 0, 0), ("row", 0, sub), ("row", 1, 0), ("col", 0, sub)]
                  + [("row", a, 0) for a in range(2, k // 3)]
                  + [("col", b, 0) for b in range(k // sub)])
    seen = set()
    blocks = []
    for kind, fixed, start in layout:
        cells = [(fixed, start + r) if kind == "row" else (start + r, fixed) for r in range(sub)]
        keep = [(a + 1) * (b + 1) <= k and (a, b) not in seen for a, b in cells]
        seen.update(cell for cell, kp in zip(cells, keep) if kp)
        blk = (v0[fixed] + sv1[start:start + sub]) if kind == "row" else (sv0[start:start + sub] + v1[fixed])
        if not all(keep):
            brow = lax.broadcasted_iota(jnp.int32, blk.shape, 0)
            mask = functools.reduce(jnp.logical_or, [brow == r for r, kp in enumerate(keep) if kp])
            blk = jnp.where(mask, blk, NEG)
        blocks.append(blk)
    assert len(seen) == sum(k // (a + 1) for a in range(k))
    return jnp.concatenate(blocks, axis=0), layout


def _gate_factors(scores, top0, top1, cand, pick, layout):
    k = PEER_TOPK
    sub = SUBLANES
    s0, s1 = scores
    (v0, rank0, rem0), (v1, rank1, rem1) = top0, top1
    _, crank, rem2 = pick
    cnt = jnp.where(crank < RANK_NONE, 1.0, 0.0)
    z = jnp.sum(cnt * jnp.exp(cand - (v0[0] + v1[0])), axis=0, keepdims=True)
    arow = lax.broadcasted_iota(jnp.int32, (k,) + s0.shape[1:], 0)
    n_rank = jnp.zeros(arow.shape, F32)
    for idx, (kind, fixed, start) in enumerate(layout):
        c_blk = cnt[idx * sub:(idx + 1) * sub]
        if kind == "row":
            n_rank = n_rank + jnp.where(arow == fixed, jnp.sum(c_blk, axis=0, keepdims=True), 0.0)
        else:
            pieces = [c_blk if a0 == start else jnp.zeros_like(c_blk) for a0 in range(0, k, sub)]
            n_rank = n_rank + jnp.concatenate(pieces, axis=0)
    n_of_key = jnp.zeros(rank0.shape, F32)
    for a in range(k):
        n_of_key = jnp.where(rank0 == float(a), n_rank[a:a + 1], n_of_key)
    tie = (jnp.where(rem0 == k, 0.0, 1.0) + jnp.where(rem1 == k, 0.0, 1.0)
           + jnp.where(rem2 == k, 0.0, 1.0))
    return jnp.exp(s0 - v0[0]) / z, n_of_key, jnp.exp(s1 - v1[0]), rank1, tie


def _peer_route_kernel(x_ref, g_ref, sh_ref, sc_ref, wq_ref, keys_ref,
                       ht_ref, hs_ref, e0_ref, nn_ref, e1_ref, r1_ref, s_scr):
    h2 = _rms_modulate(x_ref[...], g_ref[...], sh_ref[0], sc_ref[0])
    ht = h2.T
    amax = jnp.maximum(jnp.max(jnp.abs(ht), axis=0, keepdims=True), TINY)
    ht_ref[...] = (ht * (F8_TARGET / amax)).astype(F8)
    hs_ref[...] = amax * (1.0 / F8_TARGET)
    q = jnp.dot(h2.astype(BF16), wq_ref[...], preferred_element_type=F32)
    tb = q.shape[0]
    kd = keys_ref.shape[-1]
    nt = (((1,), (1,)), ((), ()))
    for hp in range(2 * PEER_HEADS):
        k0, k1, _ = _split3(keys_ref[hp // 2, hp % 2])
        q0, q1, _ = _split3(q[:, hp * kd:(hp + 1) * kd])
        s_scr[hp] = (lax.dot_general(k0, q0, nt, preferred_element_type=F32)
                     + (lax.dot_general(k0, q1, nt, preferred_element_type=F32)
                        + lax.dot_general(k1, q0, nt, preferred_element_type=F32)))

    def route_head(h, _):
        chunks = [slice(c * LANES, (c + 1) * LANES) for c in range(tb // LANES)]

        def route(chunk_cols, exact):
            pairs = [(s_scr[2 * h, :, cols], s_scr[2 * h + 1, :, cols]) for cols in chunk_cols]
            ties = []
            for cols, (e0, n_of_key, e1, rank1, tie) in zip(chunk_cols, _route_chunks(pairs, exact)):
                half = e1.shape[0] // 2
                e0_ref[h, :, cols] = _pack_bf16_pair(e0, e0)
                nn_ref[h, :, cols] = _pack_bf16_pair(n_of_key, n_of_key)
                e1_ref[h, :, cols] = _pack_bf16_pair(e1[:half], e1[half:])
                r1_ref[h, :, cols] = _pack_bf16_pair(rank1[:half], rank1[half:])
                ties.append(tie)
            return ties

        for cols, tie in zip(chunks, route(chunks, False)):
            @pl.when(jnp.max(tie) > 0.0)
            def _(cols=cols):
                route([cols], True)
        return 0

    lax.fori_loop(0, PEER_HEADS, route_head, 0)


def _peer_route(x2, g, mod3, mod_row, wq, sub_keys):
    rows, d = x2.shape
    tb = _row_tile(256, mod_row)
    nk = sub_keys.shape[2]
    order = jnp.concatenate([jnp.arange(0, nk, 2), jnp.arange(1, nk, 2)])
    sub_keys = sub_keys.at[:, 1].set(sub_keys[:, 1][:, order])
    tab_spec = pl.BlockSpec((PEER_HEADS, nk, tb), lambda i: (0, 0, i))
    half_spec = pl.BlockSpec((PEER_HEADS, nk // 2, tb), lambda i: (0, 0, i))
    tab = jax.ShapeDtypeStruct((PEER_HEADS, nk, rows), jnp.uint32)
    half_tab = jax.ShapeDtypeStruct((PEER_HEADS, nk // 2, rows), jnp.uint32)
    return pl.pallas_call(
        _peer_route_kernel,
        grid=(rows // tb,),
        in_specs=[
            pl.BlockSpec((tb, d), lambda i: (i, 0)),
            pl.BlockSpec((1, d), lambda i: (0, 0)),
            pl.BlockSpec((1, 1, d), lambda i: (mod_row.row(i, tb), 0, 3)),
            pl.BlockSpec((1, 1, d), lambda i: (mod_row.row(i, tb), 0, 4)),
            pl.BlockSpec(wq.shape, lambda i: (0, 0)),
            pl.BlockSpec(sub_keys.shape, lambda i: (0, 0, 0, 0)),
        ],
        out_specs=[pl.BlockSpec((d, tb), lambda i: (0, i)), pl.BlockSpec((1, tb), lambda i: (0, i)),
                   tab_spec, tab_spec, half_spec, half_spec],
        out_shape=[jax.ShapeDtypeStruct((d, rows), F8), jax.ShapeDtypeStruct((1, rows), F32),
                   tab, tab, half_tab, half_tab],
        scratch_shapes=[pltpu.VMEM((2 * PEER_HEADS, nk, tb), F32)],
        compiler_params=_cparams(("arbitrary",)),
        name="peer_route",
    )(x2, g.reshape(1, d), mod3, mod3, wq, sub_keys)


def _gelu_tanh(x):
    return 0.5 * x * (1.0 + jnp.tanh(math.sqrt(2.0 / math.pi) * (x + 0.044715 * (x * x * x))))


def _peer_expert_kernel(ht_ref, hs_ref, u_ref, v_ref, us_ref, vs_ref, e0_ref, nn_ref, e1_ref, r1_ref,
                        x_ref, ga_ref, gf_ref, o_ref, wa_scr, act_scr, *, final_norm):
    e = pl.program_id(1)
    nk = nn_ref.shape[1]
    tb = ht_ref.shape[1]
    ib = u_ref.shape[0] // nk
    sub = SUBLANES
    assert ib % SUBLANES == 0
    i_rows = pl.ds(pl.multiple_of(e * ib, ib), ib)

    def packed(words):
        return pltpu.bitcast(words, BF16)

    @pl.when(e == 0)
    def _():
        o_ref[...] = jnp.zeros(o_ref.shape, F32)

    act = (jnp.dot(u_ref[...], ht_ref[...], preferred_element_type=F32)
           * (hs_ref[...] * us_ref[0:1, 0:1])).astype(BF16)
    act_scr[...] = pltpu.bitcast(act, jnp.uint32)
    amax = jnp.max(jnp.max(jnp.abs(act), axis=0, keepdims=True), axis=1, keepdims=True)
    bound = PEER_HEADS * jnp.maximum(amax.astype(F32), TINY)
    to_f8 = (F8_TARGET / bound).astype(BF16)
    from_f8 = 1.0 / to_f8.astype(F32)
    n_words = e1_ref.shape[1]
    for il in range(ib):
        for c in range(tb // LANES):
            cols = slice(c * LANES, (c + 1) * LANES)
            w = [None] * (n_words // sub)
            for h in range(PEER_HEADS):
                n_i = packed(jnp.broadcast_to(nn_ref[h, i_rows, cols][il:il + 1], (sub, LANES)))
                e0_i = packed(jnp.broadcast_to(e0_ref[h, i_rows, cols][il:il + 1], (sub, LANES)))
                for s in range(n_words // sub):
                    rows = slice(s * sub, (s + 1) * sub)
                    e1 = packed(e1_ref[h, rows, cols])
                    t = jnp.where(packed(r1_ref[h, rows, cols]) < n_i, e1, jnp.zeros_like(e1)) * e0_i
                    w[s] = t if w[s] is None else w[s] + t
            a = packed(act_scr[il * n_words:(il + 1) * n_words, cols])
            wa = jnp.concatenate(w, axis=0) * _gelu_tanh(a) * to_f8
            wa_scr[il * nk:(il + 1) * nk, cols] = wa.astype(F8)
    o_ref[...] += pl.dot(wa_scr[...], v_ref[...], trans_a=True) * (from_f8 * vs_ref[0:1, 0:1])

    @pl.when(e == pl.num_programs(1) - 1)
    def _():
        y = x_ref[...] + ga_ref[0] * o_ref[...]
        if final_norm:
            ms = jnp.mean(y * y, axis=-1, keepdims=True)
            y = y * lax.rsqrt(ms + EPS) * gf_ref[...]
        o_ref[...] = y


def _fp8_blocks_kernel(w_ref, o_ref, s_ref):
    w = w_ref[0]
    amax = jnp.max(jnp.max(jnp.abs(w), axis=0, keepdims=True), axis=1, keepdims=True)
    amax = jnp.maximum(amax, TINY)
    o_ref[0] = (w * (F8_TARGET / amax)).astype(F8)
    s_ref[0, 0] = jnp.broadcast_to(amax * (1.0 / F8_TARGET), s_ref.shape[2:])


def _fp8_blocks(w, eb):
    n_layers, n_exp, d = w.shape
    nb = n_exp // eb
    return pl.pallas_call(
        _fp8_blocks_kernel,
        grid=(n_layers, nb),
        in_specs=[pl.BlockSpec((1, eb, d), lambda l, e: (l, e, 0))],
        out_specs=[pl.BlockSpec((1, eb, d), lambda l, e: (l, e, 0)),
                   pl.BlockSpec((1, 1, SUBLANES, LANES), lambda l, e: (l, e, 0, 0))],
        out_shape=[jax.ShapeDtypeStruct(w.shape, F8),
                   jax.ShapeDtypeStruct((n_layers, nb, SUBLANES, LANES), F32)],
        compiler_params=_cparams(("arbitrary", "arbitrary")),
        name="fp8_expert_blocks",
    )(w)


def _expert_block(n_keys):
    return 2 * SUBLANES * n_keys


def _peer_experts(ht, hs, u8, v8, u_scales, v_scales, layer, tables, x2, mod3, mod_row, g_final,
                  final_norm):
    rows, d = x2.shape
    n_exp = u8.shape[1]
    nk = tables[0].shape[1]
    tb = _row_tile(512, mod_row)
    eb = _expert_block(nk)
    tab_spec = pl.BlockSpec((PEER_HEADS, nk, tb), lambda i, e: (0, 0, i))
    half_spec = pl.BlockSpec((PEER_HEADS, nk // 2, tb), lambda i, e: (0, 0, i))
    scale_spec = pl.BlockSpec((None, None, SUBLANES, LANES), lambda i, e: (layer, e, 0, 0))
    return pl.pallas_call(
        functools.partial(_peer_expert_kernel, final_norm=final_norm),
        grid=(rows // tb, n_exp // eb),
        in_specs=[
            pl.BlockSpec((d, tb), lambda i, e: (0, i)),
            pl.BlockSpec((1, tb), lambda i, e: (0, i)),
            pl.BlockSpec((None, eb, d), lambda i, e: (layer, e, 0)),
            pl.BlockSpec((None, eb, d), lambda i, e: (layer, e, 0)),
            scale_spec, scale_spec,
            tab_spec, tab_spec, half_spec, half_spec,
            pl.BlockSpec((tb, d), lambda i, e: (i, 0)),
            pl.BlockSpec((1, 1, d), lambda i, e: (mod_row.row(i, tb), 0, 5)),
            pl.BlockSpec((1, d), lambda i, e: (0, 0)),
        ],
        out_specs=pl.BlockSpec((tb, d), lambda i, e: (i, 0)),
        out_shape=jax.ShapeDtypeStruct((rows, d), F32),
        scratch_shapes=[pltpu.VMEM((eb, tb), F8), pltpu.VMEM((eb // 2, tb), jnp.uint32)],
        compiler_params=_cparams(("arbitrary", "arbitrary")),
        name="peer_experts",
    )(ht, hs, u8, v8, u_scales, v_scales, *tables, x2, mod3, g_final.reshape(1, d))


def kernel(x, c, ctx, c_ctx, w_mod, b_mod, g_norm_mix, g_norm_ffn, w_in, b_gate, conv_qk, g_fourier,
           g_mlstm, w_out, w_query, sub_keys, expert_u, expert_v, g_final):
    batch, seq, d = x.shape
    ctx_len = ctx.shape[1]
    depth = w_mod.shape[0]
    f_width = g_fourier.shape[1]
    m_width = g_mlstm.shape[1]
    qk_width = conv_qk.shape[-1]
    n_gates = b_gate.shape[1]
    n_main = f_width + qk_width + 2 * m_width
    assert w_in.shape[2] == n_main + n_gates and n_gates <= LANES
    assert f_width == qk_width == m_width and m_width == M_HEADS * LANES
    assert seq % MLSTM_CHUNK == 0 and ctx_len % MLSTM_CHUNK == 0 and seq % GRID_W == 0

    cond_rows = -(-(batch + 1) // SUBLANES) * SUBLANES
    cond = jnp.zeros((cond_rows, d), F32).at[:batch].set(c).at[batch].set(c_ctx)
    mod_all = _modulation(cond, w_mod, b_mod)

    latent_row = _ModRows(lambda i, tm: (i * tm) // seq, seq)
    context_row = _ModRows(lambda i, tm: batch, batch * ctx_len)

    x2 = x.reshape(batch * seq, d)
    c2 = ctx.reshape(batch * ctx_len, d)
    v_col0 = qk_width // LANES
    o_col0 = (qk_width + m_width) // LANES
    zero_state = jnp.zeros((batch, M_HEADS, 2, LANES, 2 * LANES), F32)
    zero_m = jnp.zeros((batch, M_HEADS, 2, SUBLANES, LANES), F32)
    u8, u_scales = _fp8_blocks(expert_u, _expert_block(sub_keys.shape[3]))
    v8, v_scales = _fp8_blocks(expert_v, _expert_block(sub_keys.shape[3]))

    for l in range(depth):
        last = l == depth - 1
        mod3 = mod_all[l].reshape(cond_rows, 1, N_MOD * d)
        w_main = w_in[l, :, :n_main].astype(BF16)
        w_gate = jnp.zeros((d, LANES), F32).at[:, :n_gates].set(w_in[l, :, n_main:]).astype(BF16)
        bg = jnp.zeros((1, LANES), F32).at[0, :n_gates].set(b_gate[l])
        w_out_bf = w_out[l].astype(BF16)
        wq_bf = w_query[l].astype(BF16)

        def mixer(tokens, mod_row, n_tok, vertical, state, m_state):
            four, p, gates = _in_projection(tokens, g_norm_mix[l], mod3, mod_row, w_main, w_gate, bg)
            qk = _qk_conv(p, conv_qk[l], qk_width, 0, n_tok, vertical)
            ym, state, m_state = _mlstm(qk, p, gates, g_mlstm[l], state, m_state, batch, n_tok,
                                        v_col0, o_col0)
            return four, ym, state, m_state

        def ffn_and_residuals(tokens, four, ym, mod_row, n_tok, final_norm):
            f = _fourier_mix(four, g_fourier[l], batch, n_tok)
            tokens = _out_projection(f, ym, w_out_bf, tokens, mod3, mod_row)
            routed = _peer_route(tokens, g_norm_ffn[l], mod3, mod_row, wq_bf, sub_keys[l])
            return _peer_experts(routed[0], routed[1], u8, v8, u_scales, v_scales, l, routed[2:], tokens,
                                 mod3, mod_row, g_final, final_norm)

        pc, ymc, st, m_st = mixer(c2, context_row, ctx_len, False, zero_state, zero_m)
        p, ym, _, _ = mixer(x2, latent_row, seq, True, st, m_st)
        x2 = ffn_and_residuals(x2, p, ym, latent_row, seq, last)
        if not last:
            c2 = ffn_and_residuals(c2, pc, ymc, context_row, ctx_len, False)
    return x2.reshape(batch, seq, d)
```

```python
import functools
import math
from typing import Callable, NamedTuple

import jax
import jax.numpy as jnp
from jax import lax
from jax.experimental import pallas as pl
from jax.experimental.pallas import tpu as pltpu

F32 = jnp.float32
BF16 = jnp.bfloat16

F_GROUPS = 4
M_HEADS = 8
GRID_W = 64
PEER_HEADS = 8
PEER_TOPK = 16
N_MOD = 6
EPS = 1e-6

LANES = 128
SUBLANES = 8
VMEM_LIMIT = 56 * 1024 * 1024

MLSTM_CHUNK = 128
DFT_N2 = 128
DFT_KB = 8
NEG = -(2.0 ** 110)
REMOVED = 2.0 ** 120
RANK_NONE = 99.0
F8 = jnp.float8_e4m3fn
F8_TARGET = 224.0
TINY = 1e-30


def _cparams(sem):
    return pltpu.CompilerParams(dimension_semantics=sem, vmem_limit_bytes=VMEM_LIMIT)


class _ModRows(NamedTuple):
    row: Callable
    group: int


def _row_tile(limit, mod_row):
    tile = min(limit, mod_row.group)
    assert mod_row.group % tile == 0
    return tile


def _sigmoid(x):
    return 1.0 / (1.0 + jnp.exp(-x))


def _split3(x):
    p0 = x.astype(BF16)
    r1 = x - p0.astype(F32)
    p1 = r1.astype(BF16)
    p2 = (r1 - p1.astype(F32)).astype(BF16)
    return p0, p1, p2


def _mod_kernel(c_ref, w_ref, b_ref, o_ref):
    c = c_ref[...]
    s = (c * _sigmoid(c)).astype(BF16)
    o_ref[0] = jnp.dot(s, w_ref[0].astype(BF16), preferred_element_type=F32) + b_ref[0]


def _modulation(cond, w_mod, b_mod):
    n_layers, d, n_out = w_mod.shape
    rows = cond.shape[0]
    tn = 1024
    return pl.pallas_call(
        _mod_kernel,
        grid=(n_layers, n_out // tn),
        in_specs=[
            pl.BlockSpec((rows, d), lambda l, j: (0, 0)),
            pl.BlockSpec((1, d, tn), lambda l, j: (l, 0, j)),
            pl.BlockSpec((1, 1, tn), lambda l, j: (l, 0, j)),
        ],
        out_specs=pl.BlockSpec((1, rows, tn), lambda l, j: (l, 0, j)),
        out_shape=jax.ShapeDtypeStruct((n_layers, rows, n_out), F32),
        compiler_params=_cparams(("arbitrary", "arbitrary")),
        name="modulation",
    )(cond, w_mod, b_mod.reshape(n_layers, 1, n_out))


def _rms_modulate(x, g, shift, scale):
    ms = jnp.mean(x * x, axis=-1, keepdims=True)
    return (x * lax.rsqrt(ms + EPS) * g) * (1.0 + scale) + shift


def _inproj_kernel(x_ref, g_ref, sh_ref, sc_ref, w_ref, wg_ref, bg_ref, four_ref, p_ref, gate_ref, h_scr):
    j = pl.program_id(1)

    @pl.when(j == 0)
    def _():
        h = _rms_modulate(x_ref[...], g_ref[...], sh_ref[0], sc_ref[0]).astype(BF16)
        h_scr[...] = h
        gate_ref[...] = jnp.dot(h, wg_ref[...], preferred_element_type=F32) + bg_ref[...]

    acc = jnp.dot(h_scr[...], w_ref[...], preferred_element_type=F32)

    @pl.when(j == 0)
    def _():
        four_ref[...] = acc

    @pl.when(j > 0)
    def _():
        p_ref[...] = acc.astype(BF16)


def _in_projection(x2, g, mod3, mod_row, w_main, w_gate, b_gate):
    rows, d = x2.shape
    n_main = w_main.shape[1]
    tm = _row_tile(1024, mod_row)
    tn = 1024
    return pl.pallas_call(
        _inproj_kernel,
        grid=(rows // tm, n_main // tn),
        in_specs=[
            pl.BlockSpec((tm, d), lambda i, j: (i, 0)),
            pl.BlockSpec((1, d), lambda i, j: (0, 0)),
            pl.BlockSpec((1, 1, d), lambda i, j: (mod_row.row(i, tm), 0, 0)),
            pl.BlockSpec((1, 1, d), lambda i, j: (mod_row.row(i, tm), 0, 1)),
            pl.BlockSpec((d, tn), lambda i, j: (0, j)),
            pl.BlockSpec((d, LANES), lambda i, j: (0, 0)),
            pl.BlockSpec((1, LANES), lambda i, j: (0, 0)),
        ],
        out_specs=[
            pl.BlockSpec((tm, tn), lambda i, j: (i, 0)),
            pl.BlockSpec((tm, tn), lambda i, j: (i, jnp.maximum(j - 1, 0))),
            pl.BlockSpec((tm, LANES), lambda i, j: (i, 0)),
        ],
        out_shape=[
            jax.ShapeDtypeStruct((rows, tn), F32),
            jax.ShapeDtypeStruct((rows, n_main - tn), BF16),
            jax.ShapeDtypeStruct((rows, LANES), F32),
        ],
        scratch_shapes=[pltpu.VMEM((tm, d), BF16)],
        compiler_params=_cparams(("arbitrary", "arbitrary")),
        name="in_projection",
    )(x2, g.reshape(1, d), mod3, mod3, w_main, w_gate, b_gate)


def _conv_kernel(*refs, tb, width, vertical, blocks_per_image):
    if vertical:
        cur_ref, top_ref, bot_ref, w_ref, o_ref = refs
    else:
        cur_ref, w_ref, o_ref = refs
    cur = cur_ref[...].astype(F32)
    ch = cur.shape[1]
    wpos = lax.rem(lax.broadcasted_iota(jnp.int32, (tb, ch), 0), width)
    first_col = wpos == 0
    last_col = wpos == width - 1
    if vertical:
        r = lax.rem(pl.program_id(0), blocks_per_image)
        top = jnp.where(r == 0, 0.0, top_ref[...].astype(F32))
        bot = jnp.where(r == blocks_per_image - 1, 0.0, bot_ref[...].astype(F32))
        ext = jnp.concatenate([top, cur, bot], axis=0)
        bases = [(dr, ext[dr * width:dr * width + tb]) for dr in range(3)]
    else:
        bases = [(1, cur)]
    acc = jnp.zeros((tb, ch), F32)
    for dr, base in bases:
        left = jnp.where(first_col, 0.0, pltpu.roll(base, 1, axis=0))
        right = jnp.where(last_col, 0.0, pltpu.roll(base, tb - 1, axis=0))
        for dw, shifted in enumerate((left, base, right)):
            k = dr * 3 + dw
            acc = acc + shifted * w_ref[k:k + 1, :]
    o_ref[...] = (acc * _sigmoid(acc)).astype(BF16)


def _qk_conv(p, conv_w, qk_width, col_block, tokens_per_image, vertical):
    rows = p.shape[0]
    w9 = conv_w.reshape(9, qk_width)
    if vertical:
        width = GRID_W
        tb = min(512, tokens_per_image)
        bpi = tokens_per_image // tb
        halo = tb // width
        n_halo = rows // width
        in_specs = [
            pl.BlockSpec((tb, qk_width), lambda i: (i, col_block)),
            pl.BlockSpec((width, qk_width), lambda i: (jnp.maximum(i * halo - 1, 0), col_block)),
            pl.BlockSpec((width, qk_width), lambda i: (jnp.minimum((i + 1) * halo, n_halo - 1), col_block)),
            pl.BlockSpec((9, qk_width), lambda i: (0, 0)),
        ]
        args = (p, p, p, w9)
    else:
        width = tb = tokens_per_image
        bpi = 1
        in_specs = [
            pl.BlockSpec((tb, qk_width), lambda i: (i, col_block)),
            pl.BlockSpec((9, qk_width), lambda i: (0, 0)),
        ]
        args = (p, w9)
    return pl.pallas_call(
        functools.partial(_conv_kernel, tb=tb, width=width, vertical=vertical, blocks_per_image=bpi),
        grid=(rows // tb,),
        in_specs=in_specs,
        out_specs=pl.BlockSpec((tb, qk_width), lambda i: (i, 0)),
        out_shape=jax.ShapeDtypeStruct((rows, qk_width), BF16),
        compiler_params=_cparams(("arbitrary",)),
        name="qk_conv_latent" if vertical else "qk_conv_context",
    )(*args)


def _mlstm_chunks(chains):
    n = range(len(chains))
    ch = chains
    L = ch[0]["q"].shape[0]
    logsig = [jnp.minimum(c["gch"], 0.0) - jnp.log(1.0 + jnp.exp(-jnp.abs(c["gch"]))) for c in ch]
    shape = ch[0]["gch"].shape
    ig = [jnp.broadcast_to(jnp.sum(jnp.where(ch[i]["sel_i"], ch[i]["gch"], 0.0), axis=-1, keepdims=True),
                           shape) for i in n]
    lf = [jnp.broadcast_to(jnp.sum(jnp.where(ch[i]["sel_f"], logsig[i], 0.0), axis=-1, keepdims=True),
                           shape) for i in n]
    b_p = [_split3(lf[i]) for i in n]
    b = [sum(jnp.dot(ch[i]["cum"], p, preferred_element_type=F32) for p in b_p[i]) for i in n]
    b_end = [b[i][0:1, :] if ch[i]["reverse"] else b[i][L - 1:L, :] for i in n]
    a_t = [(ig[i] - b[i]).T for i in n]
    dmat = [jnp.where(ch[i]["causal"], b[i] + a_t[i], NEG) for i in n]
    g = [b[i] + ch[i]["m"] for i in n]
    m_j = [jnp.maximum(g[i], jnp.max(dmat[i], axis=-1, keepdims=True)) for i in n]
    pmat = [jnp.exp(dmat[i] - m_j[i]) for i in n]
    s_raw = [lax.dot_general(c["q"], c["k"], (((1,), (1,)), ((), ())), preferred_element_type=F32)
             for c in ch]
    s = [(s_raw[i] * pmat[i]).astype(BF16) for i in n]
    intra = [jnp.dot(s[i], ch[i]["vaug"], preferred_element_type=F32) for i in n]
    carried = [jnp.dot(c["q"], c["state"].astype(BF16), preferred_element_type=F32) for c in ch]
    inter = [jnp.exp(g[i] - m_j[i]) for i in n]
    tot = [intra[i] + jnp.concatenate([inter[i], inter[i]], axis=1) * carried[i] for i in n]
    h = [tot[i][:, :LANES] / jnp.maximum(jnp.abs(tot[i][:, LANES:]), jnp.exp(-m_j[i])) for i in n]
    a = [b_end[i] - b[i] + ig[i] for i in n]
    m_new = [jnp.maximum(b_end[i] + ch[i]["m"], jnp.max(a[i], axis=0, keepdims=True)) for i in n]
    kw = [(ch[i]["k"].astype(F32) * jnp.exp(a[i] - m_new[i])).astype(BF16) for i in n]
    f_old = [jnp.exp(b_end[i] + ch[i]["m"] - m_new[i]) for i in n]
    upd = [pl.dot(kw[i], ch[i]["vaug"], trans_a=True) for i in n]
    state_new = [jnp.concatenate([f_old[i], f_old[i]], axis=1) * ch[i]["state"] + upd[i] for i in n]
    return [(h[i], state_new[i], m_new[i]) for i in n]


def _mlstm_kernel(q_ref, k_ref, v_ref, o_ref, gt_ref, g_ref, sin_ref, min_ref,
                  y_ref, sout_ref, mout_ref, hf_scr, hb_scr, *, seq, k_scale):
    L = MLSTM_CHUNK
    nc = seq // L
    pair = pl.program_id(1)
    lane = lax.broadcasted_iota(jnp.int32, (1, LANES), 1)
    rr = lax.broadcasted_iota(jnp.int32, (L, L), 0)
    cc = lax.broadcasted_iota(jnp.int32, (L, L), 1)
    causal = (rr >= cc, rr <= cc)
    cum = tuple(jnp.where(m, 1.0, 0.0).astype(BF16) for m in causal)
    ones = jnp.ones((L, LANES), BF16)
    qmask, kmask, sel = [], [], []
    for j in range(2):
        own = (lane // (LANES // 2)) == j
        qmask.append(jnp.where(own, 1.0, 0.0).astype(BF16))
        kmask.append(jnp.where(own, k_scale, 0.0).astype(BF16))
        head = 2 * pair + j
        sel.append([lane == kind * M_HEADS + head for kind in range(4)])

    def body(c, carry):
        chains, dest = [], []
        for d in range(2):
            rows = pl.ds(pl.multiple_of((c if d == 0 else nc - 1 - c) * L, L), L)
            q_all, k_all, gch = q_ref[rows, :], k_ref[rows, :], gt_ref[rows, :]
            for j in range(2):
                hcols = slice(j * LANES, (j + 1) * LANES)
                idx = 2 * (2 * j + d)
                chains.append(dict(
                    q=q_all * qmask[j], k=k_all * kmask[j],
                    vaug=jnp.concatenate([v_ref[rows, hcols], ones], axis=1), gch=gch,
                    sel_i=sel[j][2 * d], sel_f=sel[j][2 * d + 1], cum=cum[d], causal=causal[d],
                    reverse=d == 1, state=carry[idx], m=carry[idx + 1]))
                dest.append((hf_scr if d == 0 else hb_scr, rows, hcols, idx))
        carry = list(carry)
        for (scr, rows, hcols, idx), (h, state, m) in zip(dest, _mlstm_chunks(chains)):
            scr[rows, hcols] = h.astype(BF16)
            carry[idx], carry[idx + 1] = state, m
        return tuple(carry)

    init = []
    for j in range(2):
        for d in range(2):
            init += [sin_ref[0, j, d], min_ref[0, j, d][0:1, :]]
    final = lax.fori_loop(0, nc, body, tuple(init))
    for j in range(2):
        for d in range(2):
            idx = 2 * (2 * j + d)
            sout_ref[0, j, d] = final[idx]
            mout_ref[0, j, d] = jnp.broadcast_to(final[idx + 1], (SUBLANES, LANES))

    def finish(c, _):
        rows = pl.ds(pl.multiple_of(c * L, L), L)
        for j in range(2):
            hcols = slice(j * LANES, (j + 1) * LANES)
            h = hf_scr[rows, hcols].astype(F32) + hb_scr[rows, hcols].astype(F32)
            ms = jnp.mean(h * h, axis=-1, keepdims=True)
            y = h * lax.rsqrt(ms + EPS) * g_ref[:, hcols]
            y_ref[rows, hcols] = (y * _sigmoid(o_ref[rows, hcols].astype(F32))).astype(BF16)
        return 0

    lax.fori_loop(0, nc, finish, 0)


def _mlstm(qk, p, gates, g_mlstm, state_in, m_in, batch, seq, v_col0, o_col0):
    rows = qk.shape[0]
    pw = 2 * LANES
    k_col0 = qk.shape[1] // (2 * LANES)
    dk = qk.shape[1] // (2 * M_HEADS)
    assert v_col0 % 2 == 0 and o_col0 % 2 == 0
    return pl.pallas_call(
        functools.partial(_mlstm_kernel, seq=seq, k_scale=dk ** -0.5),
        grid=(batch, M_HEADS // 2),
        in_specs=[
            pl.BlockSpec((seq, LANES), lambda b, h: (b, h)),
            pl.BlockSpec((seq, LANES), lambda b, h: (b, k_col0 + h)),
            pl.BlockSpec((seq, pw), lambda b, h: (b, v_col0 // 2 + h)),
            pl.BlockSpec((seq, pw), lambda b, h: (b, o_col0 // 2 + h)),
            pl.BlockSpec((seq, LANES), lambda b, h: (b, 0)),
            pl.BlockSpec((1, pw), lambda b, h: (0, h)),
            pl.BlockSpec((1, 2, 2, LANES, 2 * LANES), lambda b, h: (b, h, 0, 0, 0)),
            pl.BlockSpec((1, 2, 2, SUBLANES, LANES), lambda b, h: (b, h, 0, 0, 0)),
        ],
        out_specs=[
            pl.BlockSpec((seq, pw), lambda b, h: (b, h)),
            pl.BlockSpec((1, 2, 2, LANES, 2 * LANES), lambda b, h: (b, h, 0, 0, 0)),
            pl.BlockSpec((1, 2, 2, SUBLANES, LANES), lambda b, h: (b, h, 0, 0, 0)),
        ],
        out_shape=[
            jax.ShapeDtypeStruct((rows, M_HEADS * LANES), BF16),
            jax.ShapeDtypeStruct(state_in.shape, F32),
            jax.ShapeDtypeStruct(m_in.shape, F32),
        ],
        scratch_shapes=[pltpu.VMEM((seq, pw), BF16), pltpu.VMEM((seq, pw), BF16)],
        compiler_params=_cparams(("arbitrary", "arbitrary")),
        name="mlstm",
    )(qk, qk, p, p, gates, g_mlstm.reshape(1, -1), state_in, m_in)


def _dft_mats(n, scale):
    idx = jnp.arange(n, dtype=jnp.int32)
    ang = (2.0 * math.pi / n) * ((idx[:, None] * idx[None, :]) % n).astype(F32)
    return jnp.cos(ang) * scale, jnp.sin(ang) * scale


def _channel_dft(u, cc_ref, sc_ref):
    gw = cc_ref.shape[0]
    zr, zi = [], []
    for g in range(u.shape[1] // gw):
        ug = u[:, g * gw:(g + 1) * gw]
        zr.append(jnp.dot(ug, cc_ref[...], preferred_element_type=F32))
        zi.append(jnp.dot(ug, sc_ref[...], preferred_element_type=F32))
    return jnp.concatenate(zr, axis=1), jnp.concatenate(zi, axis=1)


def _group_rmsnorm(y, g, gw):
    outs = []
    for k in range(y.shape[1] // gw):
        yk = y[:, k * gw:(k + 1) * gw]
        ms = jnp.mean(yk * yk, axis=-1, keepdims=True)
        outs.append(yk * lax.rsqrt(ms + EPS) * g[:, k * gw:(k + 1) * gw])
    return jnp.concatenate(outs, axis=1)


def _fnet_a_kernel(u_ref, cc_ref, sc_ref, f_ref, cw_ref, sw_ref, yr_ref, yi_ref):
    fw = u_ref.shape[3]
    for r in range(u_ref.shape[2]):
        zr, zi = _channel_dft(u_ref[0, :, r, :].astype(BF16), cc_ref, sc_ref)
        z = jnp.concatenate([zr, zi], axis=0).astype(BF16)
        y = jnp.dot(f_ref[...], z, preferred_element_type=F32)
        n2 = y.shape[0] // 2
        yr, yi = y[:n2], y[n2:]
        cw, sw = cw_ref[r], sw_ref[r]
        for j in range(fw // LANES):
            cols = slice(j * LANES, (j + 1) * LANES)
            ocols = slice(r * fw + j * LANES, r * fw + (j + 1) * LANES)
            yr_ref[0, :, ocols] = (yr[:, cols] * cw + yi[:, cols] * sw).astype(BF16)
            yi_ref[0, :, ocols] = (yi[:, cols] * cw - yr[:, cols] * sw).astype(BF16)


def _fnet_b_kernel(yr_ref, yi_ref, gc_ref, gs_ref, g_ref, o_ref, *, gw):
    y = (jnp.dot(gc_ref[...], yr_ref[0], preferred_element_type=F32)
         + jnp.dot(gs_ref[...], yi_ref[0], preferred_element_type=F32))
    out = _group_rmsnorm(y, g_ref[...], gw).astype(BF16)
    o_ref[0] = out.reshape(o_ref.shape[1:])


def _fnet_direct_kernel(u_ref, cc_ref, sc_ref, ct_ref, st_ref, g_ref, o_ref, *, gw):
    zr, zi = _channel_dft(u_ref[...].astype(BF16), cc_ref, sc_ref)
    y = (jnp.dot(ct_ref[...], zr.astype(BF16), preferred_element_type=F32)
         + jnp.dot(st_ref[...], zi.astype(BF16), preferred_element_type=F32))
    o_ref[...] = _group_rmsnorm(y, g_ref[...], gw).astype(BF16)


def _fourier_mix(p, g_fourier, batch, seq):
    f_width = p.shape[1]
    gw = f_width // F_GROUPS
    cc, sc = _dft_mats(gw, gw ** -0.5)
    cc, msc = cc.astype(BF16), (-sc).astype(BF16)
    g2 = g_fourier.reshape(1, f_width)
    rows = batch * seq
    if seq <= 512:
        ct, st = _dft_mats(seq, seq ** -0.5)
        return pl.pallas_call(
            functools.partial(_fnet_direct_kernel, gw=gw),
            grid=(batch,),
            in_specs=[
                pl.BlockSpec((seq, f_width), lambda b: (b, 0)),
                pl.BlockSpec((gw, gw), lambda b: (0, 0)),
                pl.BlockSpec((gw, gw), lambda b: (0, 0)),
                pl.BlockSpec((seq, seq), lambda b: (0, 0)),
                pl.BlockSpec((seq, seq), lambda b: (0, 0)),
                pl.BlockSpec((1, f_width), lambda b: (0, 0)),
            ],
            out_specs=pl.BlockSpec((seq, f_width), lambda b: (b, 0)),
            out_shape=jax.ShapeDtypeStruct((rows, f_width), BF16),
            compiler_params=_cparams(("arbitrary",)),
            name="fourier_direct",
        )(p, cc, msc, ct.astype(BF16), st.astype(BF16), g2)

    n2 = DFT_N2
    n1 = seq // n2
    c2, s2 = _dft_mats(n2, n2 ** -0.5)
    fmat = jnp.concatenate([jnp.concatenate([c2, s2], axis=1),
                            jnp.concatenate([-s2, c2], axis=1)], axis=0).astype(BF16)
    i1 = jnp.arange(n1, dtype=jnp.int32)
    i2 = jnp.arange(n2, dtype=jnp.int32)
    tw = (2.0 * math.pi / seq) * (i1[:, None] * i2[None, :]).astype(F32)
    cw = jnp.broadcast_to(jnp.cos(tw)[:, :, None], (n1, n2, LANES))
    sw = jnp.broadcast_to(jnp.sin(tw)[:, :, None], (n1, n2, LANES))
    nr = SUBLANES
    yr, yi = pl.pallas_call(
        _fnet_a_kernel,
        grid=(batch, n1 // nr),
        in_specs=[
            pl.BlockSpec((1, n2, nr, f_width), lambda b, j: (b, 0, j, 0)),
            pl.BlockSpec((gw, gw), lambda b, j: (0, 0)),
            pl.BlockSpec((gw, gw), lambda b, j: (0, 0)),
            pl.BlockSpec((2 * n2, 2 * n2), lambda b, j: (0, 0)),
            pl.BlockSpec((nr, n2, LANES), lambda b, j: (j, 0, 0)),
            pl.BlockSpec((nr, n2, LANES), lambda b, j: (j, 0, 0)),
        ],
        out_specs=[
            pl.BlockSpec((1, n2, nr * f_width), lambda b, j: (b, 0, j)),
            pl.BlockSpec((1, n2, nr * f_width), lambda b, j: (b, 0, j)),
        ],
        out_shape=[jax.ShapeDtypeStruct((batch, n2, n1 * f_width), BF16)] * 2,
        compiler_params=_cparams(("arbitrary", "arbitrary")),
        name="fourier_stage_a",
    )(p.reshape(batch, n2, n1, f_width), cc, msc, fmat, cw, sw)

    kb = DFT_KB
    c1, s1 = _dft_mats(n1, n1 ** -0.5)
    eye = jnp.eye(kb, dtype=F32)
    gc = jnp.einsum("kn,ab->kabn", c1, eye).reshape(n1 * kb, kb * n1).astype(BF16)
    gs = jnp.einsum("kn,ab->kabn", s1, eye).reshape(n1 * kb, kb * n1).astype(BF16)
    out = pl.pallas_call(
        functools.partial(_fnet_b_kernel, gw=gw),
        grid=(batch, n2 // kb),
        in_specs=[
            pl.BlockSpec((1, kb * n1, f_width), lambda b, j: (b, j, 0)),
            pl.BlockSpec((1, kb * n1, f_width), lambda b, j: (b, j, 0)),
            pl.BlockSpec((n1 * kb, kb * n1), lambda b, j: (0, 0)),
            pl.BlockSpec((n1 * kb, kb * n1), lambda b, j: (0, 0)),
            pl.BlockSpec((1, f_width), lambda b, j: (0, 0)),
        ],
        out_specs=pl.BlockSpec((1, n1, kb, f_width), lambda b, j: (b, 0, j, 0)),
        out_shape=jax.ShapeDtypeStruct((batch, n1, n2, f_width), BF16),
        compiler_params=_cparams(("arbitrary", "arbitrary")),
        name="fourier_stage_b",
    )(yr.reshape(batch, n2 * n1, f_width), yi.reshape(batch, n2 * n1, f_width), gc, gs, g2)
    return out.reshape(rows, f_width)


def _outproj_kernel(f_ref, ym_ref, w_ref, x_ref, ga_ref, o_ref):
    fw = f_ref.shape[1]
    y = (jnp.dot(f_ref[...], w_ref[:fw, :], preferred_element_type=F32)
         + jnp.dot(ym_ref[...], w_ref[fw:, :], preferred_element_type=F32))
    o_ref[...] = x_ref[...] + ga_ref[0] * y


def _out_projection(f, ym, w_out, x2, mod3, mod_row):
    rows, d = x2.shape
    tm = _row_tile(512, mod_row)
    return pl.pallas_call(
        _outproj_kernel,
        grid=(rows // tm,),
        in_specs=[
            pl.BlockSpec((tm, f.shape[1]), lambda i: (i, 0)),
            pl.BlockSpec((tm, ym.shape[1]), lambda i: (i, 0)),
            pl.BlockSpec(w_out.shape, lambda i: (0, 0)),
            pl.BlockSpec((tm, d), lambda i: (i, 0)),
            pl.BlockSpec((1, 1, d), lambda i: (mod_row.row(i, tm), 0, 2)),
        ],
        out_specs=pl.BlockSpec((tm, d), lambda i: (i, 0)),
        out_shape=jax.ShapeDtypeStruct((rows, d), F32),
        compiler_params=_cparams(("arbitrary",)),
        name="out_projection",
    )(f, ym, w_out, x2, mod3)


def _pack_bf16_pair(lo, hi):
    def bf16_bits(x):
        u = lax.bitcast_convert_type(x, jnp.uint32)
        return (u + (jnp.uint32(0x7FFF) + ((u >> 16) & jnp.uint32(1)))) >> 16
    return bf16_bits(lo) | (bf16_bits(hi) << 16)


def _interleaved_key(q, n):
    half = n // 2
    return jnp.where(q < half, 2 * q, 2 * (q - half) + 1)


def _extract_top(arrays, k, exact, interleaved=()):
    idx = range(len(arrays))
    s = list(arrays)
    n = [x.shape[0] for x in s]
    row = [None] * len(s)
    if exact:
        for i in idx:
            row[i] = lax.broadcasted_iota(jnp.int32, s[i].shape, 0)
            if i < len(interleaved) and interleaved[i]:
                row[i] = _interleaved_key(row[i], n[i])
    vals = [[] for _ in s]
    for it in range(k):
        m = [jnp.max(s[i], axis=0, keepdims=True) for i in idx]
        hit = [s[i] == m[i] for i in idx]
        if exact:
            first = [jnp.min(jnp.where(hit[i], row[i], n[i]), axis=0, keepdims=True) for i in idx]
            hit = [row[i] == first[i] for i in idx]
        s = [jnp.where(hit[i], -(it + 1) * REMOVED, s[i]) for i in idx]
        for i in idx:
            vals[i].append(m[i])
    gone = [s[i] <= -REMOVED for i in idx]
    rank = [jnp.where(gone[i], s[i] * (-1.0 / REMOVED) - 1.0, RANK_NONE) for i in idx]
    removed = [jnp.sum(jnp.where(gone[i], 1.0, 0.0), axis=0, keepdims=True) for i in idx]
    return [(vals[i], rank[i], removed[i]) for i in idx]


def _route_chunks(score_pairs, exact):
    k = PEER_TOPK
    n_c = len(score_pairs)
    flat = [s for pair in score_pairs for s in pair]
    tops = _extract_top(flat, k, exact, interleaved=(False, True) * n_c)
    cands, layouts = [], None
    for c in range(n_c):
        cand, layouts = _candidate_sums(tops[2 * c][0], tops[2 * c + 1][0], exact)
        cands.append(cand)
    picks = _extract_top(cands, k, exact)
    return [_gate_factors(score_pairs[c], tops[2 * c], tops[2 * c + 1], cands[c], picks[c], layouts)
            for c in range(n_c)]


def _candidate_sums(v0, v1, exact):
    k = PEER_TOPK
    sub = SUBLANES
    sv0 = jnp.concatenate(v0, axis=0)
    sv1 = jnp.concatenate(v1, axis=0)
    if exact:
        layout = [("row", a, b0) for a in range(k) for b0 in range(0, max(k // (a + 1), 1), sub)]
    else:
        layout = ([("row", 0, 0), ("row", 0, sub), ("row", 1, 0), ("col", 0, sub)]
                  + [("row", a, 0) for a in range(2, k // 3)]
                  + [("col", b, 0) for b in range(k // sub)])
    seen = set()
    blocks = []
    for kind, fixed, start in layout:
        cells = [(fixed, start + r) if kind == "row" else (start + r, fixed) for r in range(sub)]
        keep = [(a + 1) * (b + 1) <= k and (a, b) not in seen for a, b in cells]
        seen.update(cell for cell, kp in zip(cells, keep) if kp)
        blk = (v0[fixed] + sv1[start:start + sub]) if kind == "row" else (sv0[start:start + sub] + v1[fixed])
        if not all(keep):
            brow = lax.broadcasted_iota(jnp.int32, blk.shape, 0)
            mask = functools.reduce(jnp.logical_or, [brow == r for r, kp in enumerate(keep) if kp])
            blk = jnp.where(mask, blk, NEG)
        blocks.append(blk)
    assert len(seen) == sum(k // (a + 1) for a in range(k))
    return jnp.concatenate(blocks, axis=0), layout


def _gate_factors(scores, top0, top1, cand, pick, layout):
    k = PEER_TOPK
    sub = SUBLANES
    s0, s1 = scores
    (v0, rank0, rem0), (v1, rank1, rem1) = top0, top1
    _, crank, rem2 = pick
    cnt = jnp.where(crank < RANK_NONE, 1.0, 0.0)
    z = jnp.sum(cnt * jnp.exp(cand - (v0[0] + v1[0])), axis=0, keepdims=True)
    arow = lax.broadcasted_iota(jnp.int32, (k,) + s0.shape[1:], 0)
    n_rank = jnp.zeros(arow.shape, F32)
    for idx, (kind, fixed, start) in enumerate(layout):
        c_blk = cnt[idx * sub:(idx + 1) * sub]
        if kind == "row":
            n_rank = n_rank + jnp.where(arow == fixed, jnp.sum(c_blk, axis=0, keepdims=True), 0.0)
        else:
            pieces = [c_blk if a0 == start else jnp.zeros_like(c_blk) for a0 in range(0, k, sub)]
            n_rank = n_rank + jnp.concatenate(pieces, axis=0)
    n_of_key = jnp.zeros(rank0.shape, F32)
    for a in range(k):
        n_of_key = jnp.where(rank0 == float(a), n_rank[a:a + 1], n_of_key)
    tie = (jnp.where(rem0 == k, 0.0, 1.0) + jnp.where(rem1 == k, 0.0, 1.0)
           + jnp.where(rem2 == k, 0.0, 1.0))
    return jnp.exp(s0 - v0[0]) / z, n_of_key, jnp.exp(s1 - v1[0]), rank1, tie


def _peer_route_kernel(x_ref, g_ref, sh_ref, sc_ref, wq_ref, keys_ref,
                       ht_ref, hs_ref, e0_ref, nn_ref, e1_ref, r1_ref, s_scr):
    h2 = _rms_modulate(x_ref[...], g_ref[...], sh_ref[0], sc_ref[0])
    ht = h2.T
    amax = jnp.maximum(jnp.max(jnp.abs(ht), axis=0, keepdims=True), TINY)
    ht_ref[...] = (ht * (F8_TARGET / amax)).astype(F8)
    hs_ref[...] = amax * (1.0 / F8_TARGET)
    q = jnp.dot(h2.astype(BF16), wq_ref[...], preferred_element_type=F32)
    tb = q.shape[0]
    kd = keys_ref.shape[-1]
    nt = (((1,), (1,)), ((), ()))
    for hp in range(2 * PEER_HEADS):
        k0, k1, _ = _split3(keys_ref[hp // 2, hp % 2])
        q0, q1, _ = _split3(q[:, hp * kd:(hp + 1) * kd])
        s_scr[hp] = (lax.dot_general(k0, q0, nt, preferred_element_type=F32)
                     + (lax.dot_general(k0, q1, nt, preferred_element_type=F32)
                        + lax.dot_general(k1, q0, nt, preferred_element_type=F32)))

    def route_head(h, _):
        chunks = [slice(c * LANES, (c + 1) * LANES) for c in range(tb // LANES)]

        def route(chunk_cols, exact):
            pairs = [(s_scr[2 * h, :, cols], s_scr[2 * h + 1, :, cols]) for cols in chunk_cols]
            ties = []
            for cols, (e0, n_of_key, e1, rank1, tie) in zip(chunk_cols, _route_chunks(pairs, exact)):
                half = e1.shape[0] // 2
                e0_ref[h, :, cols] = _pack_bf16_pair(e0, e0)
                nn_ref[h, :, cols] = _pack_bf16_pair(n_of_key, n_of_key)
                e1_ref[h, :, cols] = _pack_bf16_pair(e1[:half], e1[half:])
                r1_ref[h, :, cols] = _pack_bf16_pair(rank1[:half], rank1[half:])
                ties.append(tie)
            return ties

        for cols, tie in zip(chunks, route(chunks, False)):
            @pl.when(jnp.max(tie) > 0.0)
            def _(cols=cols):
                route([cols], True)
        return 0

    lax.fori_loop(0, PEER_HEADS, route_head, 0)


def _peer_route(x2, g, mod3, mod_row, wq, sub_keys):
    rows, d = x2.shape
    tb = _row_tile(256, mod_row)
    nk = sub_keys.shape[2]
    order = jnp.concatenate([jnp.arange(0, nk, 2), jnp.arange(1, nk, 2)])
    sub_keys = sub_keys.at[:, 1].set(sub_keys[:, 1][:, order])
    tab_spec = pl.BlockSpec((PEER_HEADS, nk, tb), lambda i: (0, 0, i))
    half_spec = pl.BlockSpec((PEER_HEADS, nk // 2, tb), lambda i: (0, 0, i))
    tab = jax.ShapeDtypeStruct((PEER_HEADS, nk, rows), jnp.uint32)
    half_tab = jax.ShapeDtypeStruct((PEER_HEADS, nk // 2, rows), jnp.uint32)
    return pl.pallas_call(
        _peer_route_kernel,
        grid=(rows // tb,),
        in_specs=[
            pl.BlockSpec((tb, d), lambda i: (i, 0)),
            pl.BlockSpec((1, d), lambda i: (0, 0)),
            pl.BlockSpec((1, 1, d), lambda i: (mod_row.row(i, tb), 0, 3)),
            pl.BlockSpec((1, 1, d), lambda i: (mod_row.row(i, tb), 0, 4)),
            pl.BlockSpec(wq.shape, lambda i: (0, 0)),
            pl.BlockSpec(sub_keys.shape, lambda i: (0, 0, 0, 0)),
        ],
        out_specs=[pl.BlockSpec((d, tb), lambda i: (0, i)), pl.BlockSpec((1, tb), lambda i: (0, i)),
                   tab_spec, tab_spec, half_spec, half_spec],
        out_shape=[jax.ShapeDtypeStruct((d, rows), F8), jax.ShapeDtypeStruct((1, rows), F32),
                   tab, tab, half_tab, half_tab],
        scratch_shapes=[pltpu.VMEM((2 * PEER_HEADS, nk, tb), F32)],
        compiler_params=_cparams(("arbitrary",)),
        name="peer_route",
    )(x2, g.reshape(1, d), mod3, mod3, wq, sub_keys)


def _gelu_tanh(x):
    return 0.5 * x * (1.0 + jnp.tanh(math.sqrt(2.0 / math.pi) * (x + 0.044715 * (x * x * x))))


def _peer_expert_kernel(ht_ref, hs_ref, u_ref, v_ref, us_ref, vs_ref, e0_ref, nn_ref, e1_ref, r1_ref,
                        x_ref, ga_ref, gf_ref, o_ref, wa_scr, act_scr, *, final_norm):
    e = pl.program_id(1)
    nk = nn_ref.shape[1]
    tb = ht_ref.shape[1]
    ib = u_ref.shape[0] // nk
    sub = SUBLANES
    assert ib % SUBLANES == 0
    i_rows = pl.ds(pl.multiple_of(e * ib, ib), ib)

    def packed(words):
        return pltpu.bitcast(words, BF16)

    @pl.when(e == 0)
    def _():
        o_ref[...] = jnp.zeros(o_ref.shape, F32)

    act = (jnp.dot(u_ref[...], ht_ref[...], preferred_element_type=F32)
           * (hs_ref[...] * us_ref[0:1, 0:1])).astype(BF16)
    act_scr[...] = pltpu.bitcast(act, jnp.uint32)
    amax = jnp.max(jnp.max(jnp.abs(act), axis=0, keepdims=True), axis=1, keepdims=True)
    bound = PEER_HEADS * jnp.maximum(amax.astype(F32), TINY)
    to_f8 = (F8_TARGET / bound).astype(BF16)
    from_f8 = 1.0 / to_f8.astype(F32)
    n_words = e1_ref.shape[1]
    for il in range(ib):
        for c in range(tb // LANES):
            cols = slice(c * LANES, (c + 1) * LANES)
            w = [None] * (n_words // sub)
            for h in range(PEER_HEADS):
                n_i = packed(jnp.broadcast_to(nn_ref[h, i_rows, cols][il:il + 1], (sub, LANES)))
                e0_i = packed(jnp.broadcast_to(e0_ref[h, i_rows, cols][il:il + 1], (sub, LANES)))
                for s in range(n_words // sub):
                    rows = slice(s * sub, (s + 1) * sub)
                    e1 = packed(e1_ref[h, rows, cols])
                    t = jnp.where(packed(r1_ref[h, rows, cols]) < n_i, e1, jnp.zeros_like(e1)) * e0_i
                    w[s] = t if w[s] is None else w[s] + t
            a = packed(act_scr[il * n_words:(il + 1) * n_words, cols])
            wa = jnp.concatenate(w, axis=0) * _gelu_tanh(a) * to_f8
            wa_scr[il * nk:(il + 1) * nk, cols] = wa.astype(F8)
    o_ref[...] += pl.dot(wa_scr[...], v_ref[...], trans_a=True) * (from_f8 * vs_ref[0:1, 0:1])

    @pl.when(e == pl.num_programs(1) - 1)
    def _():
        y = x_ref[...] + ga_ref[0] * o_ref[...]
        if final_norm:
            ms = jnp.mean(y * y, axis=-1, keepdims=True)
            y = y * lax.rsqrt(ms + EPS) * gf_ref[...]
        o_ref[...] = y


def _fp8_blocks_kernel(w_ref, o_ref, s_ref):
    w = w_ref[0]
    amax = jnp.max(jnp.max(jnp.abs(w), axis=0, keepdims=True), axis=1, keepdims=True)
    amax = jnp.maximum(amax, TINY)
    o_ref[0] = (w * (F8_TARGET / amax)).astype(F8)
    s_ref[0, 0] = jnp.broadcast_to(amax * (1.0 / F8_TARGET), s_ref.shape[2:])


def _fp8_blocks(w, eb):
    n_layers, n_exp, d = w.shape
    nb = n_exp // eb
    return pl.pallas_call(
        _fp8_blocks_kernel,
        grid=(n_layers, nb),
        in_specs=[pl.BlockSpec((1, eb, d), lambda l, e: (l, e, 0))],
        out_specs=[pl.BlockSpec((1, eb, d), lambda l, e: (l, e, 0)),
                   pl.BlockSpec((1, 1, SUBLANES, LANES), lambda l, e: (l, e, 0, 0))],
        out_shape=[jax.ShapeDtypeStruct(w.shape, F8),
                   jax.ShapeDtypeStruct((n_layers, nb, SUBLANES, LANES), F32)],
        compiler_params=_cparams(("arbitrary", "arbitrary")),
        name="fp8_expert_blocks",
    )(w)


def _expert_block(n_keys):
    return 2 * SUBLANES * n_keys


def _peer_experts(ht, hs, u8, v8, u_scales, v_scales, layer, tables, x2, mod3, mod_row, g_final,
                  final_norm):
    rows, d = x2.shape
    n_exp = u8.shape[1]
    nk = tables[0].shape[1]
    tb = _row_tile(512, mod_row)
    eb = _expert_block(nk)
    tab_spec = pl.BlockSpec((PEER_HEADS, nk, tb), lambda i, e: (0, 0, i))
    half_spec = pl.BlockSpec((PEER_HEADS, nk // 2, tb), lambda i, e: (0, 0, i))
    scale_spec = pl.BlockSpec((None, None, SUBLANES, LANES), lambda i, e: (layer, e, 0, 0))
    return pl.pallas_call(
        functools.partial(_peer_expert_kernel, final_norm=final_norm),
        grid=(rows // tb, n_exp // eb),
        in_specs=[
            pl.BlockSpec((d, tb), lambda i, e: (0, i)),
            pl.BlockSpec((1, tb), lambda i, e: (0, i)),
            pl.BlockSpec((None, eb, d), lambda i, e: (layer, e, 0)),
            pl.BlockSpec((None, eb, d), lambda i, e: (layer, e, 0)),
            scale_spec, scale_spec,
            tab_spec, tab_spec, half_spec, half_spec,
            pl.BlockSpec((tb, d), lambda i, e: (i, 0)),
            pl.BlockSpec((1, 1, d), lambda i, e: (mod_row.row(i, tb), 0, 5)),
            pl.BlockSpec((1, d), lambda i, e: (0, 0)),
        ],
        out_specs=pl.BlockSpec((tb, d), lambda i, e: (i, 0)),
        out_shape=jax.ShapeDtypeStruct((rows, d), F32),
        scratch_shapes=[pltpu.VMEM((eb, tb), F8), pltpu.VMEM((eb // 2, tb), jnp.uint32)],
        compiler_params=_cparams(("arbitrary", "arbitrary")),
        name="peer_experts",
    )(ht, hs, u8, v8, u_scales, v_scales, *tables, x2, mod3, g_final.reshape(1, d))


def kernel(x, c, ctx, c_ctx, w_mod, b_mod, g_norm_mix, g_norm_ffn, w_in, b_gate, conv_qk, g_fourier,
           g_mlstm, w_out, w_query, sub_keys, expert_u, expert_v, g_final):
    batch, seq, d = x.shape
    ctx_len = ctx.shape[1]
    depth = w_mod.shape[0]
    f_width = g_fourier.shape[1]
    m_width = g_mlstm.shape[1]
    qk_width = conv_qk.shape[-1]
    n_gates = b_gate.shape[1]
    n_main = f_width + qk_width + 2 * m_width
    assert w_in.shape[2] == n_main + n_gates and n_gates <= LANES
    assert f_width == qk_width == m_width and m_width == M_HEADS * LANES
    assert seq % MLSTM_CHUNK == 0 and ctx_len % MLSTM_CHUNK == 0 and seq % GRID_W == 0

    cond_rows = -(-(batch + 1) // SUBLANES) * SUBLANES
    cond = jnp.zeros((cond_rows, d), F32).at[:batch].set(c).at[batch].set(c_ctx)
    mod_all = _modulation(cond, w_mod, b_mod)

    latent_row = _ModRows(lambda i, tm: (i * tm) // seq, seq)
    context_row = _ModRows(lambda i, tm: batch, batch * ctx_len)

    x2 = x.reshape(batch * seq, d)
    c2 = ctx.reshape(batch * ctx_len, d)
    v_col0 = qk_width // LANES
    o_col0 = (qk_width + m_width) // LANES
    zero_state = jnp.zeros((batch, M_HEADS, 2, LANES, 2 * LANES), F32)
    zero_m = jnp.zeros((batch, M_HEADS, 2, SUBLANES, LANES), F32)
    u8, u_scales = _fp8_blocks(expert_u, _expert_block(sub_keys.shape[3]))
    v8, v_scales = _fp8_blocks(expert_v, _expert_block(sub_keys.shape[3]))

    for l in range(depth):
        last = l == depth - 1
        mod3 = mod_all[l].reshape(cond_rows, 1, N_MOD * d)
        w_main = w_in[l, :, :n_main].astype(BF16)
        w_gate = jnp.zeros((d, LANES), F32).at[:, :n_gates].set(w_in[l, :, n_main:]).astype(BF16)
        bg = jnp.zeros((1, LANES), F32).at[0, :n_gates].set(b_gate[l])
        w_out_bf = w_out[l].astype(BF16)
        wq_bf = w_query[l].astype(BF16)

        def mixer(tokens, mod_row, n_tok, vertical, state, m_state):
            four, p, gates = _in_projection(tokens, g_norm_mix[l], mod3, mod_row, w_main, w_gate, bg)
            qk = _qk_conv(p, conv_qk[l], qk_width, 0, n_tok, vertical)
            ym, state, m_state = _mlstm(qk, p, gates, g_mlstm[l], state, m_state, batch, n_tok,
                                        v_col0, o_col0)
            return four, ym, state, m_state

        def ffn_and_residuals(tokens, four, ym, mod_row, n_tok, final_norm):
            f = _fourier_mix(four, g_fourier[l], batch, n_tok)
            tokens = _out_projection(f, ym, w_out_bf, tokens, mod3, mod_row)
            routed = _peer_route(tokens, g_norm_ffn[l], mod3, mod_row, wq_bf, sub_keys[l])
            return _peer_experts(routed[0], routed[1], u8, v8, u_scales, v_scales, l, routed[2:], tokens,
                                 mod3, mod_row, g_final, final_norm)

        pc, ymc, st, m_st = mixer(c2, context_row, ctx_len, False, zero_state, zero_m)
        p, ym, _, _ = mixer(x2, latent_row, seq, True, st, m_st)
        x2 = ffn_and_residuals(x2, p, ym, latent_row, seq, last)
        if not last:
            c2 = ffn_and_residuals(c2, pc, ymc, context_row, ctx_len, False)
    return x2.reshape(batch, seq, d)
```

```python
import functools
import math
from typing import Callable, NamedTuple

import jax
import jax.numpy as jnp
from jax import lax
from jax.experimental import pallas as pl
from jax.experimental.pallas import tpu as pltpu

F32 = jnp.float32
BF16 = jnp.bfloat16

F_GROUPS = 4
M_HEADS = 8
GRID_W = 64
PEER_HEADS = 8
PEER_TOPK = 16
N_MOD = 6
EPS = 1e-6

LANES = 128
SUBLANES = 8
VMEM_LIMIT = 56 * 1024 * 1024

MLSTM_CHUNK = 128
DFT_N2 = 128
DFT_KB = 8
NEG = -(2.0 ** 110)
REMOVED = 2.0 ** 120
RANK_NONE = 99.0
F8 = jnp.float8_e4m3fn
F8_TARGET = 224.0
TINY = 1e-30


def _cparams(sem):
    return pltpu.CompilerParams(dimension_semantics=sem, vmem_limit_bytes=VMEM_LIMIT)


class _ModRows(NamedTuple):
    row: Callable
    group: int


def _row_tile(limit, mod_row):
    tile = min(limit, mod_row.group)
    assert mod_row.group % tile == 0
    return tile


def _sigmoid(x):
    return 1.0 / (1.0 + jnp.exp(-x))


def _split3(x):
    p0 = x.astype(BF16)
    r1 = x - p0.astype(F32)
    p1 = r1.astype(BF16)
    p2 = (r1 - p1.astype(F32)).astype(BF16)
    return p0, p1, p2


def _mod_kernel(c_ref, w_ref, b_ref, o_ref):
    c = c_ref[...]
    s = (c * _sigmoid(c)).astype(BF16)
    o_ref[0] = jnp.dot(s, w_ref[0].astype(BF16), preferred_element_type=F32) + b_ref[0]


def _modulation(cond, w_mod, b_mod):
    n_layers, d, n_out = w_mod.shape
    rows = cond.shape[0]
    tn = 1024
    return pl.pallas_call(
        _mod_kernel,
        grid=(n_layers, n_out // tn),
        in_specs=[
            pl.BlockSpec((rows, d), lambda l, j: (0, 0)),
            pl.BlockSpec((1, d, tn), lambda l, j: (l, 0, j)),
            pl.BlockSpec((1, 1, tn), lambda l, j: (l, 0, j)),
        ],
        out_specs=pl.BlockSpec((1, rows, tn), lambda l, j: (l, 0, j)),
        out_shape=jax.ShapeDtypeStruct((n_layers, rows, n_out), F32),
        compiler_params=_cparams(("arbitrary", "arbitrary")),
        name="modulation",
    )(cond, w_mod, b_mod.reshape(n_layers, 1, n_out))


def _rms_modulate(x, g, shift, scale):
    ms = jnp.mean(x * x, axis=-1, keepdims=True)
    return (x * lax.rsqrt(ms + EPS) * g) * (1.0 + scale) + shift


def _inproj_kernel(x_ref, g_ref, sh_ref, sc_ref, w_ref, wg_ref, bg_ref, four_ref, p_ref, gate_ref, h_scr):
    j = pl.program_id(1)

    @pl.when(j == 0)
    def _():
        h = _rms_modulate(x_ref[...], g_ref[...], sh_ref[0], sc_ref[0]).astype(BF16)
        h_scr[...] = h
        gate_ref[...] = jnp.dot(h, wg_ref[...], preferred_element_type=F32) + bg_ref[...]

    acc = jnp.dot(h_scr[...], w_ref[...], preferred_element_type=F32)

    @pl.when(j == 0)
    def _():
        four_ref[...] = acc

    @pl.when(j > 0)
    def _():
        p_ref[...] = acc.astype(BF16)


def _in_projection(x2, g, mod3, mod_row, w_main, w_gate, b_gate):
    rows, d = x2.shape
    n_main = w_main.shape[1]
    tm = _row_tile(1024, mod_row)
    tn = 1024
    return pl.pallas_call(
        _inproj_kernel,
        grid=(rows // tm, n_main // tn),
        in_specs=[
            pl.BlockSpec((tm, d), lambda i, j: (i, 0)),
            pl.BlockSpec((1, d), lambda i, j: (0, 0)),
            pl.BlockSpec((1, 1, d), lambda i, j: (mod_row.row(i, tm), 0, 0)),
            pl.BlockSpec((1, 1, d), lambda i, j: (mod_row.row(i, tm), 0, 1)),
            pl.BlockSpec((d, tn), lambda i, j: (0, j)),
            pl.BlockSpec((d, LANES), lambda i, j: (0, 0)),
            pl.BlockSpec((1, LANES), lambda i, j: (0, 0)),
        ],
        out_specs=[
            pl.BlockSpec((tm, tn), lambda i, j: (i, 0)),
            pl.BlockSpec((tm, tn), lambda i, j: (i, jnp.maximum(j - 1, 0))),
            pl.BlockSpec((tm, LANES), lambda i, j: (i, 0)),
        ],
        out_shape=[
            jax.ShapeDtypeStruct((rows, tn), F32),
            jax.ShapeDtypeStruct((rows, n_main - tn), BF16),
            jax.ShapeDtypeStruct((rows, LANES), F32),
        ],
        scratch_shapes=[pltpu.VMEM((tm, d), BF16)],
        compiler_params=_cparams(("arbitrary", "arbitrary")),
        name="in_projection",
    )(x2, g.reshape(1, d), mod3, mod3, w_main, w_gate, b_gate)


def _conv_kernel(*refs, tb, width, vertical, blocks_per_image):
    if vertical:
        cur_ref, top_ref, bot_ref, w_ref, o_ref = refs
    else:
        cur_ref, w_ref, o_ref = refs
    cur = cur_ref[...].astype(F32)
    ch = cur.shape[1]
    wpos = lax.rem(lax.broadcasted_iota(jnp.int32, (tb, ch), 0), width)
    first_col = wpos == 0
    last_col = wpos == width - 1
    if vertical:
        r = lax.rem(pl.program_id(0), blocks_per_image)
        top = jnp.where(r == 0, 0.0, top_ref[...].astype(F32))
        bot = jnp.where(r == blocks_per_image - 1, 0.0, bot_ref[...].astype(F32))
        ext = jnp.concatenate([top, cur, bot], axis=0)
        bases = [(dr, ext[dr * width:dr * width + tb]) for dr in range(3)]
    else:
        bases = [(1, cur)]
    acc = jnp.zeros((tb, ch), F32)
    for dr, base in bases:
        left = jnp.where(first_col, 0.0, pltpu.roll(base, 1, axis=0))
        right = jnp.where(last_col, 0.0, pltpu.roll(base, tb - 1, axis=0))
        for dw, shifted in enumerate((left, base, right)):
            k = dr * 3 + dw
            acc = acc + shifted * w_ref[k:k + 1, :]
    o_ref[...] = (acc * _sigmoid(acc)).astype(BF16)


def _qk_conv(p, conv_w, qk_width, col_block, tokens_per_image, vertical):
    rows = p.shape[0]
    w9 = conv_w.reshape(9, qk_width)
    if vertical:
        width = GRID_W
        tb = min(512, tokens_per_image)
        bpi = tokens_per_image // tb
        halo = tb // width
        n_halo = rows // width
        in_specs = [
            pl.BlockSpec((tb, qk_width), lambda i: (i, col_block)),
            pl.BlockSpec((width, qk_width), lambda i: (jnp.maximum(i * halo - 1, 0), col_block)),
            pl.BlockSpec((width, qk_width), lambda i: (jnp.minimum((i + 1) * halo, n_halo - 1), col_block)),
            pl.BlockSpec((9, qk_width), lambda i: (0, 0)),
        ]
        args = (p, p, p, w9)
    else:
        width = tb = tokens_per_image
        bpi = 1
        in_specs = [
            pl.BlockSpec((tb, qk_width), lambda i: (i, col_block)),
            pl.BlockSpec((9, qk_width), lambda i: (0, 0)),
        ]
        args = (p, w9)
    return pl.pallas_call(
        functools.partial(_conv_kernel, tb=tb, width=width, vertical=vertical, blocks_per_image=bpi),
        grid=(rows // tb,),
        in_specs=in_specs,
        out_specs=pl.BlockSpec((tb, qk_width), lambda i: (i, 0)),
        out_shape=jax.ShapeDtypeStruct((rows, qk_width), BF16),
        compiler_params=_cparams(("arbitrary",)),
        name="qk_conv_latent" if vertical else "qk_conv_context",
    )(*args)


def _mlstm_chunks(chains):
    n = range(len(chains))
    ch = chains
    L = ch[0]["q"].shape[0]
    logsig = [jnp.minimum(c["gch"], 0.0) - jnp.log(1.0 + jnp.exp(-jnp.abs(c["gch"]))) for c in ch]
    shape = ch[0]["gch"].shape
    ig = [jnp.broadcast_to(jnp.sum(jnp.where(ch[i]["sel_i"], ch[i]["gch"], 0.0), axis=-1, keepdims=True),
                           shape) for i in n]
    lf = [jnp.broadcast_to(jnp.sum(jnp.where(ch[i]["sel_f"], logsig[i], 0.0), axis=-1, keepdims=True),
                           shape) for i in n]
    b_p = [_split3(lf[i]) for i in n]
    b = [sum(jnp.dot(ch[i]["cum"], p, preferred_element_type=F32) for p in b_p[i]) for i in n]
    b_end = [b[i][0:1, :] if ch[i]["reverse"] else b[i][L - 1:L, :] for i in n]
    a_t = [(ig[i] - b[i]).T for i in n]
    dmat = [jnp.where(ch[i]["causal"], b[i] + a_t[i], NEG) for i in n]
    g = [b[i] + ch[i]["m"] for i in n]
    m_j = [jnp.maximum(g[i], jnp.max(dmat[i], axis=-1, keepdims=True)) for i in n]
    pmat = [jnp.exp(dmat[i] - m_j[i]) for i in n]
    s_raw = [lax.dot_general(c["q"], c["k"], (((1,), (1,)), ((), ())), preferred_element_type=F32)
             for c in ch]
    s = [(s_raw[i] * pmat[i]).astype(BF16) for i in n]
    intra = [jnp.dot(s[i], ch[i]["vaug"], preferred_element_type=F32) for i in n]
    carried = [jnp.dot(c["q"], c["state"].astype(BF16), preferred_element_type=F32) for c in ch]
    inter = [jnp.exp(g[i] - m_j[i]) for i in n]
    tot = [intra[i] + jnp.concatenate([inter[i], inter[i]], axis=1) * carried[i] for i in n]
    h = [tot[i][:, :LANES] / jnp.maximum(jnp.abs(tot[i][:, LANES:]), jnp.exp(-m_j[i])) for i in n]
    a = [b_end[i] - b[i] + ig[i] for i in n]
    m_new = [jnp.maximum(b_end[i] + ch[i]["m"], jnp.max(a[i], axis=0, keepdims=True)) for i in n]
    kw = [(ch[i]["k"].astype(F32) * jnp.exp(a[i] - m_new[i])).astype(BF16) for i in n]
    f_old = [jnp.exp(b_end[i] + ch[i]["m"] - m_new[i]) for i in n]
    upd = [pl.dot(kw[i], ch[i]["vaug"], trans_a=True) for i in n]
    state_new = [jnp.concatenate([f_old[i], f_old[i]], axis=1) * ch[i]["state"] + upd[i] for i in n]
    return [(h[i], state_new[i], m_new[i]) for i in n]


def _mlstm_kernel(q_ref, k_ref, v_ref, o_ref, gt_ref, g_ref, sin_ref, min_ref,
                  y_ref, sout_ref, mout_ref, hf_scr, hb_scr, *, seq, k_scale):
    L = MLSTM_CHUNK
    nc = seq // L
    pair = pl.program_id(1)
    lane = lax.broadcasted_iota(jnp.int32, (1, LANES), 1)
    rr = lax.broadcasted_iota(jnp.int32, (L, L), 0)
    cc = lax.broadcasted_iota(jnp.int32, (L, L), 1)
    causal = (rr >= cc, rr <= cc)
    cum = tuple(jnp.where(m, 1.0, 0.0).astype(BF16) for m in causal)
    ones = jnp.ones((L, LANES), BF16)
    qmask, kmask, sel = [], [], []
    for j in range(2):
        own = (lane // (LANES // 2)) == j
        qmask.append(jnp.where(own, 1.0, 0.0).astype(BF16))
        kmask.append(jnp.where(own, k_scale, 0.0).astype(BF16))
        head = 2 * pair + j
        sel.append([lane == kind * M_HEADS + head for kind in range(4)])

    def body(c, carry):
        chains, dest = [], []
        for d in range(2):
            rows = pl.ds(pl.multiple_of((c if d == 0 else nc - 1 - c) * L, L), L)
            q_all, k_all, gch = q_ref[rows, :], k_ref[rows, :], gt_ref[rows, :]
            for j in range(2):
                hcols = slice(j * LANES, (j + 1) * LANES)
                idx = 2 * (2 * j + d)
                chains.append(dict(
                    q=q_all * qmask[j], k=k_all * kmask[j],
                    vaug=jnp.concatenate([v_ref[rows, hcols], ones], axis=1), gch=gch,
                    sel_i=sel[j][2 * d], sel_f=sel[j][2 * d + 1], cum=cum[d], causal=causal[d],
                    reverse=d == 1, state=carry[idx], m=carry[idx + 1]))
                dest.append((hf_scr if d == 0 else hb_scr, rows, hcols, idx))
        carry = list(carry)
        for (scr, rows, hcols, idx), (h, state, m) in zip(dest, _mlstm_chunks(chains)):
            scr[rows, hcols] = h.astype(BF16)
            carry[idx], carry[idx + 1] = state, m
        return tuple(carry)

    init = []
    for j in range(2):
        for d in range(2):
            init += [sin_ref[0, j, d], min_ref[0, j, d][0:1, :]]
    final = lax.fori_loop(0, nc, body, tuple(init))
    for j in range(2):
        for d in range(2):
            idx = 2 * (2 * j + d)
            sout_ref[0, j, d] = final[idx]
            mout_ref[0, j, d] = jnp.broadcast_to(final[idx + 1], (SUBLANES, LANES))

    def finish(c, _):
        rows = pl.ds(pl.multiple_of(c * L, L), L)
        for j in range(2):
            hcols = slice(j * LANES, (j + 1) * LANES)
            h = hf_scr[rows, hcols].astype(F32) + hb_scr[rows, hcols].astype(F32)
            ms = jnp.mean(h * h, axis=-1, keepdims=True)
            y = h * lax.rsqrt(ms + EPS) * g_ref[:, hcols]
            y_ref[rows, hcols] = (y * _sigmoid(o_ref[rows, hcols].astype(F32))).astype(BF16)
        return 0

    lax.fori_loop(0, nc, finish, 0)


def _mlstm(qk, p, gates, g_mlstm, state_in, m_in, batch, seq, v_col0, o_col0):
    rows = qk.shape[0]
    pw = 2 * LANES
    k_col0 = qk.shape[1] // (2 * LANES)
    dk = qk.shape[1] // (2 * M_HEADS)
    assert v_col0 % 2 == 0 and o_col0 % 2 == 0
    return pl.pallas_call(
        functools.partial(_mlstm_kernel, seq=seq, k_scale=dk ** -0.5),
        grid=(batch, M_HEADS // 2),
        in_specs=[
            pl.BlockSpec((seq, LANES), lambda b, h: (b, h)),
            pl.BlockSpec((seq, LANES), lambda b, h: (b, k_col0 + h)),
            pl.BlockSpec((seq, pw), lambda b, h: (b, v_col0 // 2 + h)),
            pl.BlockSpec((seq, pw), lambda b, h: (b, o_col0 // 2 + h)),
            pl.BlockSpec((seq, LANES), lambda b, h: (b, 0)),
            pl.BlockSpec((1, pw), lambda b, h: (0, h)),
            pl.BlockSpec((1, 2, 2, LANES, 2 * LANES), lambda b, h: (b, h, 0, 0, 0)),
            pl.BlockSpec((1, 2, 2, SUBLANES, LANES), lambda b, h: (b, h, 0, 0, 0)),
        ],
        out_specs=[
            pl.BlockSpec((seq, pw), lambda b, h: (b, h)),
            pl.BlockSpec((1, 2, 2, LANES, 2 * LANES), lambda b, h: (b, h, 0, 0, 0)),
            pl.BlockSpec((1, 2, 2, SUBLANES, LANES), lambda b, h: (b, h, 0, 0, 0)),
        ],
        out_shape=[
            jax.ShapeDtypeStruct((rows, M_HEADS * LANES), BF16),
            jax.ShapeDtypeStruct(state_in.shape, F32),
            jax.ShapeDtypeStruct(m_in.shape, F32),
        ],
        scratch_shapes=[pltpu.VMEM((seq, pw), BF16), pltpu.VMEM((seq, pw), BF16)],
        compiler_params=_cparams(("arbitrary", "arbitrary")),
        name="mlstm",
    )(qk, qk, p, p, gates, g_mlstm.reshape(1, -1), state_in, m_in)


def _dft_mats(n, scale):
    idx = jnp.arange(n, dtype=jnp.int32)
    ang = (2.0 * math.pi / n) * ((idx[:, None] * idx[None, :]) % n).astype(F32)
    return jnp.cos(ang) * scale, jnp.sin(ang) * scale


def _channel_dft(u, cc_ref, sc_ref):
    gw = cc_ref.shape[0]
    zr, zi = [], []
    for g in range(u.shape[1] // gw):
        ug = u[:, g * gw:(g + 1) * gw]
        zr.append(jnp.dot(ug, cc_ref[...], preferred_element_type=F32))
        zi.append(jnp.dot(ug, sc_ref[...], preferred_element_type=F32))
    return jnp.concatenate(zr, axis=1), jnp.concatenate(zi, axis=1)


def _group_rmsnorm(y, g, gw):
    outs = []
    for k in range(y.shape[1] // gw):
        yk = y[:, k * gw:(k + 1) * gw]
        ms = jnp.mean(yk * yk, axis=-1, keepdims=True)
        outs.append(yk * lax.rsqrt(ms + EPS) * g[:, k * gw:(k + 1) * gw])
    return jnp.concatenate(outs, axis=1)


def _fnet_a_kernel(u_ref, cc_ref, sc_ref, f_ref, cw_ref, sw_ref, yr_ref, yi_ref):
    fw = u_ref.shape[3]
    for r in range(u_ref.shape[2]):
        zr, zi = _channel_dft(u_ref[0, :, r, :].astype(BF16), cc_ref, sc_ref)
        z = jnp.concatenate([zr, zi], axis=0).astype(BF16)
        y = jnp.dot(f_ref[...], z, preferred_element_type=F32)
        n2 = y.shape[0] // 2
        yr, yi = y[:n2], y[n2:]
        cw, sw = cw_ref[r], sw_ref[r]
        for j in range(fw // LANES):
            cols = slice(j * LANES, (j + 1) * LANES)
            ocols = slice(r * fw + j * LANES, r * fw + (j + 1) * LANES)
            yr_ref[0, :, ocols] = (yr[:, cols] * cw + yi[:, cols] * sw).astype(BF16)
            yi_ref[0, :, ocols] = (yi[:, cols] * cw - yr[:, cols] * sw).astype(BF16)


def _fnet_b_kernel(yr_ref, yi_ref, gc_ref, gs_ref, g_ref, o_ref, *, gw):
    y = (jnp.dot(gc_ref[...], yr_ref[0], preferred_element_type=F32)
         + jnp.dot(gs_ref[...], yi_ref[0], preferred_element_type=F32))
    out = _group_rmsnorm(y, g_ref[...], gw).astype(BF16)
    o_ref[0] = out.reshape(o_ref.shape[1:])


def _fnet_direct_kernel(u_ref, cc_ref, sc_ref, ct_ref, st_ref, g_ref, o_ref, *, gw):
    zr, zi = _channel_dft(u_ref[...].astype(BF16), cc_ref, sc_ref)
    y = (jnp.dot(ct_ref[...], zr.astype(BF16), preferred_element_type=F32)
         + jnp.dot(st_ref[...], zi.astype(BF16), preferred_element_type=F32))
    o_ref[...] = _group_rmsnorm(y, g_ref[...], gw).astype(BF16)


def _fourier_mix(p, g_fourier, batch, seq):
    f_width = p.shape[1]
    gw = f_width // F_GROUPS
    cc, sc = _dft_mats(gw, gw ** -0.5)
    cc, msc = cc.astype(BF16), (-sc).astype(BF16)
    g2 = g_fourier.reshape(1, f_width)
    rows = batch * seq
    if seq <= 512:
        ct, st = _dft_mats(seq, seq ** -0.5)
        return pl.pallas_call(
            functools.partial(_fnet_direct_kernel, gw=gw),
            grid=(batch,),
            in_specs=[
                pl.BlockSpec((seq, f_width), lambda b: (b, 0)),
                pl.BlockSpec((gw, gw), lambda b: (0, 0)),
                pl.BlockSpec((gw, gw), lambda b: (0, 0)),
                pl.BlockSpec((seq, seq), lambda b: (0, 0)),
                pl.BlockSpec((seq, seq), lambda b: (0, 0)),
                pl.BlockSpec((1, f_width), lambda b: (0, 0)),
            ],
            out_specs=pl.BlockSpec((seq, f_width), lambda b: (b, 0)),
            out_shape=jax.ShapeDtypeStruct((rows, f_width), BF16),
            compiler_params=_cparams(("arbitrary",)),
            name="fourier_direct",
        )(p, cc, msc, ct.astype(BF16), st.astype(BF16), g2)

    n2 = DFT_N2
    n1 = seq // n2
    c2, s2 = _dft_mats(n2, n2 ** -0.5)
    fmat = jnp.concatenate([jnp.concatenate([c2, s2], axis=1),
                            jnp.concatenate([-s2, c2], axis=1)], axis=0).astype(BF16)
    i1 = jnp.arange(n1, dtype=jnp.int32)
    i2 = jnp.arange(n2, dtype=jnp.int32)
    tw = (2.0 * math.pi / seq) * (i1[:, None] * i2[None, :]).astype(F32)
    cw = jnp.broadcast_to(jnp.cos(tw)[:, :, None], (n1, n2, LANES))
    sw = jnp.broadcast_to(jnp.sin(tw)[:, :, None], (n1, n2, LANES))
    nr = SUBLANES
    yr, yi = pl.pallas_call(
        _fnet_a_kernel,
        grid=(batch, n1 // nr),
        in_specs=[
            pl.BlockSpec((1, n2, nr, f_width), lambda b, j: (b, 0, j, 0)),
            pl.BlockSpec((gw, gw), lambda b, j: (0, 0)),
            pl.BlockSpec((gw, gw), lambda b, j: (0, 0)),
            pl.BlockSpec((2 * n2, 2 * n2), lambda b, j: (0, 0)),
            pl.BlockSpec((nr, n2, LANES), lambda b, j: (j, 0, 0)),
            pl.BlockSpec((nr, n2, LANES), lambda b, j: (j, 0, 0)),
        ],
        out_specs=[
            pl.BlockSpec((1, n2, nr * f_width), lambda b, j: (b, 0, j)),
            pl.BlockSpec((1, n2, nr * f_width), lambda b, j: (b, 0, j)),
        ],
        out_shape=[jax.ShapeDtypeStruct((batch, n2, n1 * f_width), BF16)] * 2,
        compiler_params=_cparams(("arbitrary", "arbitrary")),
        name="fourier_stage_a",
    )(p.reshape(batch, n2, n1, f_width), cc, msc, fmat, cw, sw)

    kb = DFT_KB
    c1, s1 = _dft_mats(n1, n1 ** -0.5)
    eye = jnp.eye(kb, dtype=F32)
    gc = jnp.einsum("kn,ab->kabn", c1, eye).reshape(n1 * kb, kb * n1).astype(BF16)
    gs = jnp.einsum("kn,ab->kabn", s1, eye).reshape(n1 * kb, kb * n1).astype(BF16)
    out = pl.pallas_call(
        functools.partial(_fnet_b_kernel, gw=gw),
        grid=(batch, n2 // kb),
        in_specs=[
            pl.BlockSpec((1, kb * n1, f_width), lambda b, j: (b, j, 0)),
            pl.BlockSpec((1, kb * n1, f_width), lambda b, j: (b, j, 0)),
            pl.BlockSpec((n1 * kb, kb * n1), lambda b, j: (0, 0)),
            pl.BlockSpec((n1 * kb, kb * n1), lambda b, j: (0, 0)),
            pl.BlockSpec((1, f_width), lambda b, j: (0, 0)),
        ],
        out_specs=pl.BlockSpec((1, n1, kb, f_width), lambda b, j: (b, 0, j, 0)),
        out_shape=jax.ShapeDtypeStruct((batch, n1, n2, f_width), BF16),
        compiler_params=_cparams(("arbitrary", "arbitrary")),
        name="fourier_stage_b",
    )(yr.reshape(batch, n2 * n1, f_width), yi.reshape(batch, n2 * n1, f_width), gc, gs, g2)
    return out.reshape(rows, f_width)


def _outproj_kernel(f_ref, ym_ref, w_ref, x_ref, ga_ref, o_ref):
    fw = f_ref.shape[1]
    y = (jnp.dot(f_ref[...], w_ref[:fw, :], preferred_element_type=F32)
         + jnp.dot(ym_ref[...], w_ref[fw:, :], preferred_element_type=F32))
    o_ref[...] = x_ref[...] + ga_ref[0] * y


def _out_projection(f, ym, w_out, x2, mod3, mod_row):
    rows, d = x2.shape
    tm = _row_tile(512, mod_row)
    return pl.pallas_call(
        _outproj_kernel,
        grid=(rows // tm,),
        in_specs=[
            pl.BlockSpec((tm, f.shape[1]), lambda i: (i, 0)),
            pl.BlockSpec((tm, ym.shape[1]), lambda i: (i, 0)),
            pl.BlockSpec(w_out.shape, lambda i: (0, 0)),
            pl.BlockSpec((tm, d), lambda i: (i, 0)),
            pl.BlockSpec((1, 1, d), lambda i: (mod_row.row(i, tm), 0, 2)),
        ],
        out_specs=pl.BlockSpec((tm, d), lambda i: (i, 0)),
        out_shape=jax.ShapeDtypeStruct((rows, d), F32),
        compiler_params=_cparams(("arbitrary",)),
        name="out_projection",
    )(f, ym, w_out, x2, mod3)


def _pack_bf16_pair(lo, hi):
    def bf16_bits(x):
        u = lax.bitcast_convert_type(x, jnp.uint32)
        return (u + (jnp.uint32(0x7FFF) + ((u >> 16) & jnp.uint32(1)))) >> 16
    return bf16_bits(lo) | (bf16_bits(hi) << 16)


def _interleaved_key(q, n):
    half = n // 2
    return jnp.where(q < half, 2 * q, 2 * (q - half) + 1)


def _extract_top(arrays, k, exact, interleaved=()):
    idx = range(len(arrays))
    s = list(arrays)
    n = [x.shape[0] for x in s]
    row = [None] * len(s)
    if exact:
        for i in idx:
            row[i] = lax.broadcasted_iota(jnp.int32, s[i].shape, 0)
            if i < len(interleaved) and interleaved[i]:
                row[i] = _interleaved_key(row[i], n[i])
    vals = [[] for _ in s]
    for it in range(k):
        m = [jnp.max(s[i], axis=0, keepdims=True) for i in idx]
        hit = [s[i] == m[i] for i in idx]
        if exact:
            first = [jnp.min(jnp.where(hit[i], row[i], n[i]), axis=0, keepdims=True) for i in idx]
            hit = [row[i] == first[i] for i in idx]
        s = [jnp.where(hit[i], -(it + 1) * REMOVED, s[i]) for i in idx]
        for i in idx:
            vals[i].append(m[i])
    gone = [s[i] <= -REMOVED for i in idx]
    rank = [jnp.where(gone[i], s[i] * (-1.0 / REMOVED) - 1.0, RANK_NONE) for i in idx]
    removed = [jnp.sum(jnp.where(gone[i], 1.0, 0.0), axis=0, keepdims=True) for i in idx]
    return [(vals[i], rank[i], removed[i]) for i in idx]


def _route_chunks(score_pairs, exact):
    k = PEER_TOPK
    n_c = len(score_pairs)
    flat = [s for pair in score_pairs for s in pair]
    tops = _extract_top(flat, k, exact, interleaved=(False, True) * n_c)
    cands, layouts = [], None
    for c in range(n_c):
        cand, layouts = _candidate_sums(tops[2 * c][0], tops[2 * c + 1][0], exact)
        cands.append(cand)
    picks = _extract_top(cands, k, exact)
    return [_gate_factors(score_pairs[c], tops[2 * c], tops[2 * c + 1], cands[c], picks[c], layouts)
            for c in range(n_c)]


def _candidate_sums(v0, v1, exact):
    k = PEER_TOPK
    sub = SUBLANES
    sv0 = jnp.concatenate(v0, axis=0)
    sv1 = jnp.concatenate(v1, axis=0)
    if exact:
        layout = [("row", a, b0) for a in range(k) for b0 in range(0, max(k // (a + 1), 1), sub)]
    else:
        layout = ([("row", 0, 0), ("row", 0, sub), ("row", 1, 0), ("col", 0, sub)]
                  + [("row", a, 0) for a in range(2, k // 3)]
                  + [("col", b, 0) for b in range(k // sub)])
    seen = set()
    blocks = []
    for kind, fixed, start in layout:
        cells = [(fixed, start + r) if kind == "row" else (start + r, fixed) for r in range(sub)]
        keep = [(a + 1) * (b + 1) <= k and (a, b) not in seen for a, b in cells]
        seen.update(cell for cell, kp in zip(cells, keep) if kp)
        blk = (v0[fixed] + sv1[start:start + sub]) if kind == "row" else (sv0[start:start + sub] + v1[fixed])
        if not all(keep):
            brow = lax.broadcasted_iota(jnp.int32, blk.shape, 0)
            mask = functools.reduce(jnp.logical_or, [brow == r for r, kp in enumerate(keep) if kp])
            blk = jnp.where(mask, blk, NEG)
        blocks.append(blk)
    assert len(seen) == sum(k // (a + 1) for a in range(k))
    return jnp.concatenate(blocks, axis=0), layout


def _gate_factors(scores, top0, top1, cand, pick, layout):
    k = PEER_TOPK
    sub = SUBLANES
    s0, s1 = scores
    (v0, rank0, rem0), (v1, rank1, rem1) = top0, top1
    _, crank, rem2 = pick
    cnt = jnp.where(crank < RANK_NONE, 1.0, 0.0)
    z = jnp.sum(cnt * jnp.exp(cand - (v0[0] + v1[0])), axis=0, keepdims=True)
    arow = lax.broadcasted_iota(jnp.int32, (k,) + s0.shape[1:], 0)
    n_rank = jnp.zeros(arow.shape, F32)
    for idx, (kind, fixed, start) in enumerate(layout):
        c_blk = cnt[idx * sub:(idx + 1) * sub]
        if kind == "row":
            n_rank = n_rank + jnp.where(arow == fixed, jnp.sum(c_blk, axis=0, keepdims=True), 0.0)
        else:
            pieces = [c_blk if a0 == start else jnp.zeros_like(c_blk) for a0 in range(0, k, sub)]
            n_rank = n_rank + jnp.concatenate(pieces, axis=0)
    n_of_key = jnp.zeros(rank0.shape, F32)
    for a in range(k):
        n_of_key = jnp.where(rank0 == float(a), n_rank[a:a + 1], n_of_key)
    tie = (jnp.where(rem0 == k, 0.0, 1.0) + jnp.where(rem1 == k, 0.0, 1.0)
           + jnp.where(rem2 == k, 0.0, 1.0))
    return jnp.exp(s0 - v0[0]) / z, n_of_key, jnp.exp(s1 - v1[0]), rank1, tie


def _peer_route_kernel(x_ref, g_ref, sh_ref, sc_ref, wq_ref, keys_ref,
                       ht_ref, hs_ref, e0_ref, nn_ref, e1_ref, r1_ref, s_scr):
    h2 = _rms_modulate(x_ref[...], g_ref[...], sh_ref[0], sc_ref[0])
    ht = h2.T
    amax = jnp.maximum(jnp.max(jnp.abs(ht), axis=0, keepdims=True), TINY)
    ht_ref[...] = (ht * (F8_TARGET / amax)).astype(F8)
    hs_ref[...] = amax * (1.0 / F8_TARGET)
    q = jnp.dot(h2.astype(BF16), wq_ref[...], preferred_element_type=F32)
    tb = q.shape[0]
    kd = keys_ref.shape[-1]
    nt = (((1,), (1,)), ((), ()))
    for hp in range(2 * PEER_HEADS):
        k0, k1, _ = _split3(keys_ref[hp // 2, hp % 2])
        q0, q1, _ = _split3(q[:, hp * kd:(hp + 1) * kd])
        s_scr[hp] = (lax.dot_general(k0, q0, nt, preferred_element_type=F32)
                     + (lax.dot_general(k0, q1, nt, preferred_element_type=F32)
                        + lax.dot_general(k1, q0, nt, preferred_element_type=F32)))

    def route_head(h, _):
        chunks = [slice(c * LANES, (c + 1) * LANES) for c in range(tb // LANES)]

        def route(chunk_cols, exact):
            pairs = [(s_scr[2 * h, :, cols], s_scr[2 * h + 1, :, cols]) for cols in chunk_cols]
            ties = []
            for cols, (e0, n_of_key, e1, rank1, tie) in zip(chunk_cols, _route_chunks(pairs, exact)):
                half = e1.shape[0] // 2
                e0_ref[h, :, cols] = _pack_bf16_pair(e0, e0)
                nn_ref[h, :, cols] = _pack_bf16_pair(n_of_key, n_of_key)
                e1_ref[h, :, cols] = _pack_bf16_pair(e1[:half], e1[half:])
                r1_ref[h, :, cols] = _pack_bf16_pair(rank1[:half], rank1[half:])
                ties.append(tie)
            return ties

        for cols, tie in zip(chunks, route(chunks, False)):
            @pl.when(jnp.max(tie) > 0.0)
            def _(cols=cols):
                route([cols], True)
        return 0

    lax.fori_loop(0, PEER_HEADS, route_head, 0)


def _peer_route(x2, g, mod3, mod_row, wq, sub_keys):
    rows, d = x2.shape
    tb = _row_tile(256, mod_row)
    nk = sub_keys.shape[2]
    order = jnp.concatenate([jnp.arange(0, nk, 2), jnp.arange(1, nk, 2)])
    sub_keys = sub_keys.at[:, 1].set(sub_keys[:, 1][:, order])
    tab_spec = pl.BlockSpec((PEER_HEADS, nk, tb), lambda i: (0, 0, i))
    half_spec = pl.BlockSpec((PEER_HEADS, nk // 2, tb), lambda i: (0, 0, i))
    tab = jax.ShapeDtypeStruct((PEER_HEADS, nk, rows), jnp.uint32)
    half_tab = jax.ShapeDtypeStruct((PEER_HEADS, nk // 2, rows), jnp.uint32)
    return pl.pallas_call(
        _peer_route_kernel,
        grid=(rows // tb,),
        in_specs=[
            pl.BlockSpec((tb, d), lambda i: (i, 0)),
            pl.BlockSpec((1, d), lambda i: (0, 0)),
            pl.BlockSpec((1, 1, d), lambda i: (mod_row.row(i, tb), 0, 3)),
            pl.BlockSpec((1, 1, d), lambda i: (mod_row.row(i, tb), 0, 4)),
            pl.BlockSpec(wq.shape, lambda i: (0, 0)),
            pl.BlockSpec(sub_keys.shape, lambda i: (0, 0, 0, 0)),
        ],
        out_specs=[pl.BlockSpec((d, tb), lambda i: (0, i)), pl.BlockSpec((1, tb), lambda i: (0, i)),
                   tab_spec, tab_spec, half_spec, half_spec],
        out_shape=[jax.ShapeDtypeStruct((d, rows), F8), jax.ShapeDtypeStruct((1, rows), F32),
                   tab, tab, half_tab, half_tab],
        scratch_shapes=[pltpu.VMEM((2 * PEER_HEADS, nk, tb), F32)],
        compiler_params=_cparams(("arbitrary",)),
        name="peer_route",
    )(x2, g.reshape(1, d), mod3, mod3, wq, sub_keys)


def _gelu_tanh(x):
    return 0.5 * x * (1.0 + jnp.tanh(math.sqrt(2.0 / math.pi) * (x + 0.044715 * (x * x * x))))


def _peer_expert_kernel(ht_ref, hs_ref, u_ref, v_ref, us_ref, vs_ref, e0_ref, nn_ref, e1_ref, r1_ref,
                        x_ref, ga_ref, gf_ref, o_ref, wa_scr, act_scr, *, final_norm):
    e = pl.program_id(1)
    nk = nn_ref.shape[1]
    tb = ht_ref.shape[1]
    ib = u_ref.shape[0] // nk
    sub = SUBLANES
    assert ib % SUBLANES == 0
    i_rows = pl.ds(pl.multiple_of(e * ib, ib), ib)

    def packed(words):
        return pltpu.bitcast(words, BF16)

    @pl.when(e == 0)
    def _():
        o_ref[...] = jnp.zeros(o_ref.shape, F32)

    unscale = hs_ref[...] * us_ref[0:1, 0:1]
    amax = None
    for part in range(ib // SUBLANES):
        rows = slice(part * SUBLANES * nk, (part + 1) * SUBLANES * nk)
        act = (jnp.dot(u_ref[rows, :], ht_ref[...], preferred_element_type=F32)
               * unscale).astype(BF16)
        act_scr[part * SUBLANES * nk // 2:(part + 1) * SUBLANES * nk // 2, :] = pltpu.bitcast(act, jnp.uint32)
        part_max = jnp.max(jnp.max(jnp.abs(act), axis=0, keepdims=True), axis=1, keepdims=True)
        amax = part_max if amax is None else jnp.maximum(amax, part_max)
    bound = PEER_HEADS * jnp.maximum(amax.astype(F32), TINY)
    to_f8 = (F8_TARGET / bound).astype(BF16)
    from_f8 = 1.0 / to_f8.astype(F32)
    n_words = e1_ref.shape[1]
    for il in range(ib):
        for c in range(tb // LANES):
            cols = slice(c * LANES, (c + 1) * LANES)
            w = [None] * (n_words // sub)
            for h in range(PEER_HEADS):
                n_i = packed(jnp.broadcast_to(nn_ref[h, i_rows, cols][il:il + 1], (sub, LANES)))
                e0_i = packed(jnp.broadcast_to(e0_ref[h, i_rows, cols][il:il + 1], (sub, LANES)))
                for s in range(n_words // sub):
                    rows = slice(s * sub, (s + 1) * sub)
                    e1 = packed(e1_ref[h, rows, cols])
                    t = jnp.where(packed(r1_ref[h, rows, cols]) < n_i, e1, jnp.zeros_like(e1)) * e0_i
                    w[s] = t if w[s] is None else w[s] + t
            a = packed(act_scr[il * n_words:(il + 1) * n_words, cols])
            wa = jnp.concatenate(w, axis=0) * _gelu_tanh(a) * to_f8
            wa_scr[il * nk:(il + 1) * nk, cols] = wa.astype(F8)
    o_ref[...] += pl.dot(wa_scr[...], v_ref[...], trans_a=True) * (from_f8 * vs_ref[0:1, 0:1])

    @pl.when(e == pl.num_programs(1) - 1)
    def _():
        y = x_ref[...] + ga_ref[0] * o_ref[...]
        if final_norm:
            ms = jnp.mean(y * y, axis=-1, keepdims=True)
            y = y * lax.rsqrt(ms + EPS) * gf_ref[...]
        o_ref[...] = y


def _fp8_blocks_kernel(w_ref, o_ref, s_ref):
    w = w_ref[0]
    amax = jnp.max(jnp.max(jnp.abs(w), axis=0, keepdims=True), axis=1, keepdims=True)
    amax = jnp.maximum(amax, TINY)
    o_ref[0] = (w * (F8_TARGET / amax)).astype(F8)
    s_ref[0, 0] = jnp.broadcast_to(amax * (1.0 / F8_TARGET), s_ref.shape[2:])


def _fp8_blocks(w, eb):
    n_layers, n_exp, d = w.shape
    nb = n_exp // eb
    return pl.pallas_call(
        _fp8_blocks_kernel,
        grid=(n_layers, nb),
        in_specs=[pl.BlockSpec((1, eb, d), lambda l, e: (l, e, 0))],
        out_specs=[pl.BlockSpec((1, eb, d), lambda l, e: (l, e, 0)),
                   pl.BlockSpec((1, 1, SUBLANES, LANES), lambda l, e: (l, e, 0, 0))],
        out_shape=[jax.ShapeDtypeStruct(w.shape, F8),
                   jax.ShapeDtypeStruct((n_layers, nb, SUBLANES, LANES), F32)],
        compiler_params=_cparams(("arbitrary", "arbitrary")),
        name="fp8_expert_blocks",
    )(w)


def _expert_block(n_keys):
    return 2 * SUBLANES * n_keys


def _peer_experts(ht, hs, u8, v8, u_scales, v_scales, layer, tables, x2, mod3, mod_row, g_final,
                  final_norm):
    rows, d = x2.shape
    n_exp = u8.shape[1]
    nk = tables[0].shape[1]
    tb = _row_tile(512, mod_row)
    eb = _expert_block(nk)
    tab_spec = pl.BlockSpec((PEER_HEADS, nk, tb), lambda i, e: (0, 0, i))
    half_spec = pl.BlockSpec((PEER_HEADS, nk // 2, tb), lambda i, e: (0, 0, i))
    scale_spec = pl.BlockSpec((None, None, SUBLANES, LANES), lambda i, e: (layer, e, 0, 0))
    return pl.pallas_call(
        functools.partial(_peer_expert_kernel, final_norm=final_norm),
        grid=(rows // tb, n_exp // eb),
        in_specs=[
            pl.BlockSpec((d, tb), lambda i, e: (0, i)),
            pl.BlockSpec((1, tb), lambda i, e: (0, i)),
            pl.BlockSpec((None, eb, d), lambda i, e: (layer, e, 0)),
            pl.BlockSpec((None, eb, d), lambda i, e: (layer, e, 0)),
            scale_spec, scale_spec,
            tab_spec, tab_spec, half_spec, half_spec,
            pl.BlockSpec((tb, d), lambda i, e: (i, 0)),
            pl.BlockSpec((1, 1, d), lambda i, e: (mod_row.row(i, tb), 0, 5)),
            pl.BlockSpec((1, d), lambda i, e: (0, 0)),
        ],
        out_specs=pl.BlockSpec((tb, d), lambda i, e: (i, 0)),
        out_shape=jax.ShapeDtypeStruct((rows, d), F32),
        scratch_shapes=[pltpu.VMEM((eb, tb), F8), pltpu.VMEM((eb // 2, tb), jnp.uint32)],
        compiler_params=_cparams(("arbitrary", "arbitrary")),
        name="peer_experts",
    )(ht, hs, u8, v8, u_scales, v_scales, *tables, x2, mod3, g_final.reshape(1, d))


def kernel(x, c, ctx, c_ctx, w_mod, b_mod, g_norm_mix, g_norm_ffn, w_in, b_gate, conv_qk, g_fourier,
           g_mlstm, w_out, w_query, sub_keys, expert_u, expert_v, g_final):
    batch, seq, d = x.shape
    ctx_len = ctx.shape[1]
    depth = w_mod.shape[0]
    f_width = g_fourier.shape[1]
    m_width = g_mlstm.shape[1]
    qk_width = conv_qk.shape[-1]
    n_gates = b_gate.shape[1]
    n_main = f_width + qk_width + 2 * m_width
    assert w_in.shape[2] == n_main + n_gates and n_gates <= LANES
    assert f_width == qk_width == m_width and m_width == M_HEADS * LANES
    assert seq % MLSTM_CHUNK == 0 and ctx_len % MLSTM_CHUNK == 0 and seq % GRID_W == 0

    cond_rows = -(-(batch + 1) // SUBLANES) * SUBLANES
    cond = jnp.zeros((cond_rows, d), F32).at[:batch].set(c).at[batch].set(c_ctx)
    mod_all = _modulation(cond, w_mod, b_mod)

    latent_row = _ModRows(lambda i, tm: (i * tm) // seq, seq)
    context_row = _ModRows(lambda i, tm: batch, batch * ctx_len)

    x2 = x.reshape(batch * seq, d)
    c2 = ctx.reshape(batch * ctx_len, d)
    v_col0 = qk_width // LANES
    o_col0 = (qk_width + m_width) // LANES
    zero_state = jnp.zeros((batch, M_HEADS, 2, LANES, 2 * LANES), F32)
    zero_m = jnp.zeros((batch, M_HEADS, 2, SUBLANES, LANES), F32)
    u8, u_scales = _fp8_blocks(expert_u, _expert_block(sub_keys.shape[3]))
    v8, v_scales = _fp8_blocks(expert_v, _expert_block(sub_keys.shape[3]))

    for l in range(depth):
        last = l == depth - 1
        mod3 = mod_all[l].reshape(cond_rows, 1, N_MOD * d)
        w_main = w_in[l, :, :n_main].astype(BF16)
        w_gate = jnp.zeros((d, LANES), F32).at[:, :n_gates].set(w_in[l, :, n_main:]).astype(BF16)
        bg = jnp.zeros((1, LANES), F32).at[0, :n_gates].set(b_gate[l])
        w_out_bf = w_out[l].astype(BF16)
        wq_bf = w_query[l].astype(BF16)

        def mixer(tokens, mod_row, n_tok, vertical, state, m_state):
            four, p, gates = _in_projection(tokens, g_norm_mix[l], mod3, mod_row, w_main, w_gate, bg)
            qk = _qk_conv(p, conv_qk[l], qk_width, 0, n_tok, vertical)
            ym, state, m_state = _mlstm(qk, p, gates, g_mlstm[l], state, m_state, batch, n_tok,
                                        v_col0, o_col0)
            return four, ym, state, m_state

        def ffn_and_residuals(tokens, four, ym, mod_row, n_tok, final_norm):
            f = _fourier_mix(four, g_fourier[l], batch, n_tok)
            tokens = _out_projection(f, ym, w_out_bf, tokens, mod3, mod_row)
            routed = _peer_route(tokens, g_norm_ffn[l], mod3, mod_row, wq_bf, sub_keys[l])
            return _peer_experts(routed[0], routed[1], u8, v8, u_scales, v_scales, l, routed[2:], tokens,
                                 mod3, mod_row, g_final, final_norm)

        pc, ymc, st, m_st = mixer(c2, context_row, ctx_len, False, zero_state, zero_m)
        p, ym, _, _ = mixer(x2, latent_row, seq, True, st, m_st)
        x2 = ffn_and_residuals(x2, p, ym, latent_row, seq, last)
        if not last:
            c2 = ffn_and_residuals(c2, pc, ymc, context_row, ctx_len, False)
    return x2.reshape(batch, seq, d)
```
